```python
import jax, jax.numpy as jnp
from jax import lax
import numpy as np

D_MODEL = 2048
BATCH = 8
SEQ = 8192
DEPTH = 1

N_SUB = 3
D_FF = 5632
POOL_WINDOWS = (2, 4, 8, 16)
POOL_GROUPS = len(POOL_WINDOWS)
POOL_GROUP_W = D_MODEL // 8
POOL_W = POOL_GROUPS * POOL_GROUP_W
HEAD_DIM = 64
N_HEADS = 16
N_KV_HEADS = 2
GQA_GROUP = N_HEADS // N_KV_HEADS
WINDOW = 128
BLK = 128
NUM_BUCKETS = 32
MAX_EXACT = NUM_BUCKETS // 2
REL_MAX_DIST = 128
EPS = 1e-6
NEG_INF = -1e30
IN_SPLITS = (POOL_W, N_HEADS * HEAD_DIM, N_KV_HEADS * HEAD_DIM, N_KV_HEADS * HEAD_DIM, D_MODEL, D_MODEL)
IN_W = sum(IN_SPLITS)

kernel_name = "hybrid_pool_swa_gated_macaron_block"


def rms_norm(x, g):
    xf = x.astype(jnp.float32)
    y = xf * lax.rsqrt(jnp.mean(xf * xf, axis=-1, keepdims=True) + EPS)
    return (y * g.astype(jnp.float32)).astype(x.dtype)


def modulate(h, shift, scale):
    return h * (1 + scale) + shift


def swiglu(h, w_gu, w_down):
    g, u = jnp.split(h @ w_gu, 2, axis=-1)
    return (jax.nn.silu(g) * u) @ w_down


def multiscale_pool(u, pool_mix, pool_scale):
    B, S, _ = u.shape
    uf = u.astype(jnp.float32).reshape(B, S, POOL_GROUPS, POOL_GROUP_W)
    cs = jnp.pad(jnp.cumsum(uf, axis=1), ((0, 0), (1, 0), (0, 0), (0, 0)))
    t1 = np.arange(1, S + 1)
    outs = []
    for gi, w in enumerate(POOL_WINDOWS):
        lo = np.maximum(t1 - w, 0)
        cnt = np.minimum(t1, w).astype(np.float32)[None, :, None]
        win_sum = cs[:, 1:, gi] - cs[:, lo, gi]
        outs.append(win_sum / cnt - uf[:, :, gi])
    pooled = jnp.stack(outs, axis=2).astype(u.dtype)
    mixed = jnp.einsum('bsgc,gcd->bsgd', pooled, pool_mix)
    return mixed.reshape(B, S, POOL_W) * pool_scale


def rel_bucket_band():
    ql = np.arange(BLK)[:, None]
    j = np.arange(2 * BLK)[None, :]
    n = np.clip(BLK + ql - j, 0, None)
    nf = np.maximum(n, 1).astype(np.float32)
    large = MAX_EXACT + (np.log(nf / MAX_EXACT) / np.log(REL_MAX_DIST / MAX_EXACT)
                         * (NUM_BUCKETS - MAX_EXACT)).astype(np.int32)
    large = np.minimum(large, NUM_BUCKETS - 1)
    return np.where(n < MAX_EXACT, n, large).astype(np.int32)


def band_mask(nb):
    qpos = (np.arange(nb)[:, None, None] * BLK + np.arange(BLK)[None, :, None])
    kpos = ((np.arange(nb)[:, None, None] - 1) * BLK + np.arange(2 * BLK)[None, None, :])
    dist = qpos - kpos
    return (dist >= 0) & (dist < WINDOW) & (kpos >= 0)


def swa_sink_attention(q, k, v, q_gain, k_gain, sinks, rel_bias):
    B, S = q.shape[:2]
    nb = S // BLK
    q = rms_norm(q, q_gain)
    k = rms_norm(k, k_gain)
    qb = q.reshape(B, nb, BLK, N_KV_HEADS, GQA_GROUP, HEAD_DIM)

    def band(t):
        tb = t.reshape(B, nb, BLK, N_KV_HEADS, HEAD_DIM)
        prev = jnp.pad(tb, ((0, 0), (1, 0), (0, 0), (0, 0), (0, 0)))[:, :nb]
        return jnp.concatenate([prev, tb], axis=2)

    kband, vband = band(k), band(v)
    logits = jnp.einsum('bnqkgd,bnjkd->bnkgqj', qb, kband).astype(jnp.float32) * (HEAD_DIM ** -0.5)
    bias = rel_bias.astype(jnp.float32)[rel_bucket_band()]
    bias = jnp.transpose(bias, (2, 0, 1)).reshape(N_KV_HEADS, GQA_GROUP, BLK, 2 * BLK)
    logits = logits + bias
    mask = band_mask(nb)[None, :, None, None]
    logits = jnp.where(mask, logits, NEG_INF)
    sink = jnp.broadcast_to(sinks.astype(jnp.float32).reshape(1, 1, N_KV_HEADS, GQA_GROUP, 1, 1),
                            logits.shape[:-1] + (1,))
    p = jax.nn.softmax(jnp.concatenate([logits, sink], axis=-1), axis=-1)[..., :-1]
    out = jnp.einsum('bnkgqj,bnjkd->bnqkgd', p.astype(v.dtype), vband)
    return out.reshape(B, S, N_HEADS * HEAD_DIM)


def _fwd_setup_inputs(seed: int = 0) -> dict:
    key = jax.random.key(seed)
    ks = jax.random.split(key, 24)
    L = DEPTH

    def dense(k, shape, fan_in):
        return jax.random.normal(k, shape, jnp.float32) * (fan_in ** -0.5)

    def gain(k, shape, s=0.05):
        return 1.0 + s * jax.random.normal(k, shape, jnp.float32)

    return {
        "x": jax.random.normal(ks[0], (BATCH, SEQ, D_MODEL), jnp.float32),
        "c": jax.random.normal(ks[1], (BATCH, D_MODEL), jnp.float32),
        "w_ada": dense(ks[2], (L, D_MODEL, 3 * N_SUB * D_MODEL), D_MODEL),
        "b_ada": 0.02 * jax.random.normal(ks[3], (L, 3 * N_SUB * D_MODEL), jnp.float32),
        "g_ffn1": gain(ks[4], (L, D_MODEL)),
        "w_ffn1_gu": dense(ks[5], (L, D_MODEL, 2 * D_FF), D_MODEL),
        "w_ffn1_down": dense(ks[6], (L, D_FF, D_MODEL), D_FF),
        "g_mix": gain(ks[7], (L, D_MODEL)),
        "w_in": dense(ks[8], (L, D_MODEL, IN_W), D_MODEL),
        "pool_mix": dense(ks[9], (L, POOL_GROUPS, POOL_GROUP_W, POOL_GROUP_W), POOL_GROUP_W),
        "pool_scale": gain(ks[10], (L, POOL_W), 0.1),
        "w_pool_up": dense(ks[11], (L, POOL_W, D_MODEL), POOL_W),
        "q_gain": gain(ks[12], (L, HEAD_DIM)),
        "k_gain": gain(ks[13], (L, HEAD_DIM)),
        "sinks": jax.random.normal(ks[14], (L, N_HEADS), jnp.float32),
        "rel_bias": 0.5 * jax.random.normal(ks[15], (NUM_BUCKETS, N_HEADS), jnp.float32),
        "w_attn_up": dense(ks[16], (L, N_HEADS * HEAD_DIM, D_MODEL), N_HEADS * HEAD_DIM),
        "w_o": dense(ks[17], (L, D_MODEL, D_MODEL), D_MODEL),
        "g_ffn2": gain(ks[18], (L, D_MODEL)),
        "w_ffn2_gu": dense(ks[19], (L, D_MODEL, 2 * D_FF), D_MODEL),
        "w_ffn2_down": dense(ks[20], (L, D_FF, D_MODEL), D_FF),
    }


def _fwd_reference(x, c, w_ada, b_ada, g_ffn1, w_ffn1_gu, w_ffn1_down, g_mix, w_in, pool_mix,
              pool_scale, w_pool_up, q_gain, k_gain, sinks, rel_bias, w_attn_up, w_o,
              g_ffn2, w_ffn2_gu, w_ffn2_down):
    B, S, D = x.shape
    split_idx = [int(s) for s in np.cumsum(IN_SPLITS)[:-1]]
    for l in range(DEPTH):
        mod = (jax.nn.silu(c) @ w_ada[l] + b_ada[l]).reshape(B, 3 * N_SUB, 1, D)

        h = modulate(rms_norm(x, g_ffn1[l]), mod[:, 0], mod[:, 1])
        x = x + 0.5 * mod[:, 2] * swiglu(h, w_ffn1_gu[l], w_ffn1_down[l])

        h = modulate(rms_norm(x, g_mix[l]), mod[:, 3], mod[:, 4])
        z = h @ w_in[l]
        u_pool, q, k, v, ga, gb = jnp.split(z, split_idx, axis=-1)
        y_pool = multiscale_pool(u_pool, pool_mix[l], pool_scale[l]) @ w_pool_up[l]
        y_attn = swa_sink_attention(
            q.reshape(B, S, N_HEADS, HEAD_DIM),
            k.reshape(B, S, N_KV_HEADS, HEAD_DIM),
            v.reshape(B, S, N_KV_HEADS, HEAD_DIM),
            q_gain[l], k_gain[l], sinks[l], rel_bias) @ w_attn_up[l]
        merged = jax.nn.sigmoid(ga) * y_pool + jax.nn.sigmoid(gb) * y_attn
        x = x + mod[:, 5] * (merged @ w_o[l])

        h = modulate(rms_norm(x, g_ffn2[l]), mod[:, 6], mod[:, 7])
        x = x + 0.5 * mod[:, 8] * swiglu(h, w_ffn2_gu[l], w_ffn2_down[l])
    return x


import jax as _jax
import jax.numpy as _jnp

TWIN_FORMAT = 'train_step'
FWD_PARAMS = ['x', 'c', 'w_ada', 'b_ada', 'g_ffn1', 'w_ffn1_gu', 'w_ffn1_down', 'g_mix', 'w_in', 'pool_mix', 'pool_scale', 'w_pool_up', 'q_gain', 'k_gain', 'sinks', 'rel_bias', 'w_attn_up', 'w_o', 'g_ffn2', 'w_ffn2_gu', 'w_ffn2_down']
TWIN_WEIGHTS = ['w_ada', 'b_ada', 'g_ffn1', 'w_ffn1_gu', 'w_ffn1_down', 'g_mix', 'w_in', 'pool_mix', 'pool_scale', 'w_pool_up', 'q_gain', 'k_gain', 'sinks', 'rel_bias', 'w_attn_up', 'w_o', 'g_ffn2', 'w_ffn2_gu', 'w_ffn2_down']
TWIN_DIFF_INPUT = 'x'
TWIN_INPUTS = ['x', 'c', 'w_ada', 'b_ada', 'g_ffn1', 'w_ffn1_gu', 'w_ffn1_down', 'g_mix', 'w_in', 'pool_mix', 'pool_scale', 'w_pool_up', 'q_gain', 'k_gain', 'sinks', 'rel_bias', 'w_attn_up', 'w_o', 'g_ffn2', 'w_ffn2_gu', 'w_ffn2_down', 'loss_target', 'm_w_ada', 'm_b_ada', 'm_g_ffn1', 'm_w_ffn1_gu', 'm_w_ffn1_down', 'm_g_mix', 'm_w_in', 'm_pool_mix', 'm_pool_scale', 'm_w_pool_up', 'm_q_gain', 'm_k_gain', 'm_sinks', 'm_rel_bias', 'm_w_attn_up', 'm_w_o', 'm_g_ffn2', 'm_w_ffn2_gu', 'm_w_ffn2_down', 'v_w_ada', 'v_b_ada', 'v_g_ffn1', 'v_w_ffn1_gu', 'v_w_ffn1_down', 'v_g_mix', 'v_w_in', 'v_pool_mix', 'v_pool_scale', 'v_w_pool_up', 'v_q_gain', 'v_k_gain', 'v_sinks', 'v_rel_bias', 'v_w_attn_up', 'v_w_o', 'v_g_ffn2', 'v_w_ffn2_gu', 'v_w_ffn2_down']
TWIN_OUTPUTS = ['loss', 'grad_x', 'grad_w_ada', 'grad_b_ada', 'grad_g_ffn1', 'grad_w_ffn1_gu', 'grad_w_ffn1_down', 'grad_g_mix', 'grad_w_in', 'grad_pool_mix', 'grad_pool_scale', 'grad_w_pool_up', 'grad_q_gain', 'grad_k_gain', 'grad_sinks', 'grad_rel_bias', 'grad_w_attn_up', 'grad_w_o', 'grad_g_ffn2', 'grad_w_ffn2_gu', 'grad_w_ffn2_down', 'delta_w_ada', 'delta_b_ada', 'delta_g_ffn1', 'delta_w_ffn1_gu', 'delta_w_ffn1_down', 'delta_g_mix', 'delta_w_in', 'delta_pool_mix', 'delta_pool_scale', 'delta_w_pool_up', 'delta_q_gain', 'delta_k_gain', 'delta_sinks', 'delta_rel_bias', 'delta_w_attn_up', 'delta_w_o', 'delta_g_ffn2', 'delta_w_ffn2_gu', 'delta_w_ffn2_down', 'new_m_w_ada', 'new_m_b_ada', 'new_m_g_ffn1', 'new_m_w_ffn1_gu', 'new_m_w_ffn1_down', 'new_m_g_mix', 'new_m_w_in', 'new_m_pool_mix', 'new_m_pool_scale', 'new_m_w_pool_up', 'new_m_q_gain', 'new_m_k_gain', 'new_m_sinks', 'new_m_rel_bias', 'new_m_w_attn_up', 'new_m_w_o', 'new_m_g_ffn2', 'new_m_w_ffn2_gu', 'new_m_w_ffn2_down', 'new_v_w_ada', 'new_v_b_ada', 'new_v_g_ffn1', 'new_v_w_ffn1_gu', 'new_v_w_ffn1_down', 'new_v_g_mix', 'new_v_w_in', 'new_v_pool_mix', 'new_v_pool_scale', 'new_v_w_pool_up', 'new_v_q_gain', 'new_v_k_gain', 'new_v_sinks', 'new_v_rel_bias', 'new_v_w_attn_up', 'new_v_w_o', 'new_v_g_ffn2', 'new_v_w_ffn2_gu', 'new_v_w_ffn2_down']
TWIN_LEAF_KINDS = {'loss': 'loss', 'grad_x': 'grad_x', 'grad_w_ada': 'grad_w', 'grad_b_ada': 'grad_w', 'grad_g_ffn1': 'grad_w', 'grad_w_ffn1_gu': 'grad_w', 'grad_w_ffn1_down': 'grad_w', 'grad_g_mix': 'grad_w', 'grad_w_in': 'grad_w', 'grad_pool_mix': 'grad_w', 'grad_pool_scale': 'grad_w', 'grad_w_pool_up': 'grad_w', 'grad_q_gain': 'grad_w', 'grad_k_gain': 'grad_w', 'grad_sinks': 'grad_w', 'grad_rel_bias': 'grad_w', 'grad_w_attn_up': 'grad_w', 'grad_w_o': 'grad_w', 'grad_g_ffn2': 'grad_w', 'grad_w_ffn2_gu': 'grad_w', 'grad_w_ffn2_down': 'grad_w', 'delta_w_ada': 'delta_w', 'delta_b_ada': 'delta_w', 'delta_g_ffn1': 'delta_w', 'delta_w_ffn1_gu': 'delta_w', 'delta_w_ffn1_down': 'delta_w', 'delta_g_mix': 'delta_w', 'delta_w_in': 'delta_w', 'delta_pool_mix': 'delta_w', 'delta_pool_scale': 'delta_w', 'delta_w_pool_up': 'delta_w', 'delta_q_gain': 'delta_w', 'delta_k_gain': 'delta_w', 'delta_sinks': 'delta_w', 'delta_rel_bias': 'delta_w', 'delta_w_attn_up': 'delta_w', 'delta_w_o': 'delta_w', 'delta_g_ffn2': 'delta_w', 'delta_w_ffn2_gu': 'delta_w', 'delta_w_ffn2_down': 'delta_w', 'new_m_w_ada': 'new_m', 'new_m_b_ada': 'new_m', 'new_m_g_ffn1': 'new_m', 'new_m_w_ffn1_gu': 'new_m', 'new_m_w_ffn1_down': 'new_m', 'new_m_g_mix': 'new_m', 'new_m_w_in': 'new_m', 'new_m_pool_mix': 'new_m', 'new_m_pool_scale': 'new_m', 'new_m_w_pool_up': 'new_m', 'new_m_q_gain': 'new_m', 'new_m_k_gain': 'new_m', 'new_m_sinks': 'new_m', 'new_m_rel_bias': 'new_m', 'new_m_w_attn_up': 'new_m', 'new_m_w_o': 'new_m', 'new_m_g_ffn2': 'new_m', 'new_m_w_ffn2_gu': 'new_m', 'new_m_w_ffn2_down': 'new_m', 'new_v_w_ada': 'new_v', 'new_v_b_ada': 'new_v', 'new_v_g_ffn1': 'new_v', 'new_v_w_ffn1_gu': 'new_v', 'new_v_w_ffn1_down': 'new_v', 'new_v_g_mix': 'new_v', 'new_v_w_in': 'new_v', 'new_v_pool_mix': 'new_v', 'new_v_pool_scale': 'new_v', 'new_v_w_pool_up': 'new_v', 'new_v_q_gain': 'new_v', 'new_v_k_gain': 'new_v', 'new_v_sinks': 'new_v', 'new_v_rel_bias': 'new_v', 'new_v_w_attn_up': 'new_v', 'new_v_w_o': 'new_v', 'new_v_g_ffn2': 'new_v', 'new_v_w_ffn2_gu': 'new_v', 'new_v_w_ffn2_down': 'new_v'}


def _forward(args):
    return _fwd_reference(*[args[k] for k in FWD_PARAMS])


def _output_shape():
    def fwd():
        inp = _fwd_setup_inputs(0)
        return _fwd_reference(*[inp[k] for k in FWD_PARAMS])
    out = _jax.eval_shape(fwd)
    return out.shape, out.dtype

N_MICROBATCH = 1
ADAM_LR = 0.001
ADAM_B1 = 0.9
ADAM_B2 = 0.999
ADAM_EPS = 1e-08
ADAM_WD = 0.01
ADAM_STEP = 10
PER_EXAMPLE_BATCH_AXIS = {'x': 0, 'c': 0, 'loss_target': 0}
SHARED_INPUTS = []
_WEIGHT_DTYPES = {'w_ada': _jnp.float32, 'b_ada': _jnp.float32, 'g_ffn1': _jnp.float32, 'w_ffn1_gu': _jnp.float32, 'w_ffn1_down': _jnp.float32, 'g_mix': _jnp.float32, 'w_in': _jnp.float32, 'pool_mix': _jnp.float32, 'pool_scale': _jnp.float32, 'w_pool_up': _jnp.float32, 'q_gain': _jnp.float32, 'k_gain': _jnp.float32, 'sinks': _jnp.float32, 'rel_bias': _jnp.float32, 'w_attn_up': _jnp.float32, 'w_o': _jnp.float32, 'g_ffn2': _jnp.float32, 'w_ffn2_gu': _jnp.float32, 'w_ffn2_down': _jnp.float32}
MOMENT_SCALE = {'w_ada': 1.453100e+00, 'b_ada': 3.641262e+00, 'g_ffn1': 6.255108e+00, 'w_ffn1_gu': 2.500244e-01, 'w_ffn1_down': 3.530399e-01, 'g_mix': 4.818767e+00, 'w_in': 1.264271e+00, 'pool_mix': 4.943694e-01, 'pool_scale': 7.799063e+00, 'w_pool_up': 2.099525e-01, 'q_gain': 1.803775e+00, 'k_gain': 1.783675e+00, 'sinks': 1.678473e+00, 'rel_bias': 1.978492e-01, 'w_attn_up': 9.977437e-01, 'w_o': 7.861021e-01, 'g_ffn2': 6.339514e+00, 'w_ffn2_gu': 2.134967e-01, 'w_ffn2_down': 2.099547e-01}


def _to_microbatches(a, axis):
    t = _jnp.moveaxis(a, axis, 0)
    t = t.reshape((N_MICROBATCH, t.shape[0] // N_MICROBATCH) + t.shape[1:])
    return _jnp.moveaxis(t, 1, axis + 1)


def setup_inputs(seed: int = 0) -> dict:
    inp = _fwd_setup_inputs(seed)
    key = _jax.random.fold_in(_jax.random.key(seed), 7919)
    shape, _ = _output_shape()
    out = dict(inp)
    out["loss_target"] = _jax.random.normal(_jax.random.fold_in(key, 0), shape, _jnp.float32)
    for i, name in enumerate(TWIN_WEIGHTS):
        w = inp[name].astype(_jnp.float32)
        if MOMENT_SCALE is None:
            s = _jnp.sqrt(_jnp.mean(_jnp.square(w)) + 1e-30)
        else:
            s = MOMENT_SCALE[name]
        km, kv = _jax.random.split(_jax.random.fold_in(key, i + 1))
        out[name] = w
        out["m_" + name] = s * _jax.random.normal(km, w.shape, _jnp.float32)
        out["v_" + name] = (s * s) * _jax.random.uniform(kv, w.shape, _jnp.float32, 0.5, 1.5)
    if N_MICROBATCH > 1:
        for name, axis in PER_EXAMPLE_BATCH_AXIS.items():
            out[name] = _to_microbatches(out[name], axis)
    return {'x': out['x'], 'c': out['c'], 'w_ada': out['w_ada'], 'b_ada': out['b_ada'], 'g_ffn1': out['g_ffn1'], 'w_ffn1_gu': out['w_ffn1_gu'], 'w_ffn1_down': out['w_ffn1_down'], 'g_mix': out['g_mix'], 'w_in': out['w_in'], 'pool_mix': out['pool_mix'], 'pool_scale': out['pool_scale'], 'w_pool_up': out['w_pool_up'], 'q_gain': out['q_gain'], 'k_gain': out['k_gain'], 'sinks': out['sinks'], 'rel_bias': out['rel_bias'], 'w_attn_up': out['w_attn_up'], 'w_o': out['w_o'], 'g_ffn2': out['g_ffn2'], 'w_ffn2_gu': out['w_ffn2_gu'], 'w_ffn2_down': out['w_ffn2_down'], 'loss_target': out['loss_target'], 'm_w_ada': out['m_w_ada'], 'm_b_ada': out['m_b_ada'], 'm_g_ffn1': out['m_g_ffn1'], 'm_w_ffn1_gu': out['m_w_ffn1_gu'], 'm_w_ffn1_down': out['m_w_ffn1_down'], 'm_g_mix': out['m_g_mix'], 'm_w_in': out['m_w_in'], 'm_pool_mix': out['m_pool_mix'], 'm_pool_scale': out['m_pool_scale'], 'm_w_pool_up': out['m_w_pool_up'], 'm_q_gain': out['m_q_gain'], 'm_k_gain': out['m_k_gain'], 'm_sinks': out['m_sinks'], 'm_rel_bias': out['m_rel_bias'], 'm_w_attn_up': out['m_w_attn_up'], 'm_w_o': out['m_w_o'], 'm_g_ffn2': out['m_g_ffn2'], 'm_w_ffn2_gu': out['m_w_ffn2_gu'], 'm_w_ffn2_down': out['m_w_ffn2_down'], 'v_w_ada': out['v_w_ada'], 'v_b_ada': out['v_b_ada'], 'v_g_ffn1': out['v_g_ffn1'], 'v_w_ffn1_gu': out['v_w_ffn1_gu'], 'v_w_ffn1_down': out['v_w_ffn1_down'], 'v_g_mix': out['v_g_mix'], 'v_w_in': out['v_w_in'], 'v_pool_mix': out['v_pool_mix'], 'v_pool_scale': out['v_pool_scale'], 'v_w_pool_up': out['v_w_pool_up'], 'v_q_gain': out['v_q_gain'], 'v_k_gain': out['v_k_gain'], 'v_sinks': out['v_sinks'], 'v_rel_bias': out['v_rel_bias'], 'v_w_attn_up': out['v_w_attn_up'], 'v_w_o': out['v_w_o'], 'v_g_ffn2': out['v_g_ffn2'], 'v_w_ffn2_gu': out['v_w_ffn2_gu'], 'v_w_ffn2_down': out['v_w_ffn2_down']}


def _loss(weights, diff, rest, loss_target):
    with _jax.named_scope("forward"):
        args = {**rest, TWIN_DIFF_INPUT: diff, **{k: w.astype(_WEIGHT_DTYPES[k]) for k, w in weights.items()}}
        y = _forward(args)
    with _jax.named_scope("loss_head"):
        err = _jnp.square(y.astype(_jnp.float32) - loss_target)
        return 0.5 * _jnp.sum(_jnp.mean(err, axis=-1)) if err.ndim else 0.5 * err


def _adamw(w, g, m, v):
    m = ADAM_B1 * m + (1.0 - ADAM_B1) * g
    v = ADAM_B2 * v + (1.0 - ADAM_B2) * _jnp.square(g)
    m_hat = m / (1.0 - ADAM_B1 ** ADAM_STEP)
    v_hat = v / (1.0 - ADAM_B2 ** ADAM_STEP)
    delta = -ADAM_LR * (m_hat / (_jnp.sqrt(v_hat) + ADAM_EPS) + ADAM_WD * w)
    return delta, m, v


def reference(x, c, w_ada, b_ada, g_ffn1, w_ffn1_gu, w_ffn1_down, g_mix, w_in, pool_mix, pool_scale, w_pool_up, q_gain, k_gain, sinks, rel_bias, w_attn_up, w_o, g_ffn2, w_ffn2_gu, w_ffn2_down, loss_target, m_w_ada, m_b_ada, m_g_ffn1, m_w_ffn1_gu, m_w_ffn1_down, m_g_mix, m_w_in, m_pool_mix, m_pool_scale, m_w_pool_up, m_q_gain, m_k_gain, m_sinks, m_rel_bias, m_w_attn_up, m_w_o, m_g_ffn2, m_w_ffn2_gu, m_w_ffn2_down, v_w_ada, v_b_ada, v_g_ffn1, v_w_ffn1_gu, v_w_ffn1_down, v_g_mix, v_w_in, v_pool_mix, v_pool_scale, v_w_pool_up, v_q_gain, v_k_gain, v_sinks, v_rel_bias, v_w_attn_up, v_w_o, v_g_ffn2, v_w_ffn2_gu, v_w_ffn2_down):
    given = dict(x=x, c=c, w_ada=w_ada, b_ada=b_ada, g_ffn1=g_ffn1, w_ffn1_gu=w_ffn1_gu, w_ffn1_down=w_ffn1_down, g_mix=g_mix, w_in=w_in, pool_mix=pool_mix, pool_scale=pool_scale, w_pool_up=w_pool_up, q_gain=q_gain, k_gain=k_gain, sinks=sinks, rel_bias=rel_bias, w_attn_up=w_attn_up, w_o=w_o, g_ffn2=g_ffn2, w_ffn2_gu=w_ffn2_gu, w_ffn2_down=w_ffn2_down, loss_target=loss_target, m_w_ada=m_w_ada, m_b_ada=m_b_ada, m_g_ffn1=m_g_ffn1, m_w_ffn1_gu=m_w_ffn1_gu, m_w_ffn1_down=m_w_ffn1_down, m_g_mix=m_g_mix, m_w_in=m_w_in, m_pool_mix=m_pool_mix, m_pool_scale=m_pool_scale, m_w_pool_up=m_w_pool_up, m_q_gain=m_q_gain, m_k_gain=m_k_gain, m_sinks=m_sinks, m_rel_bias=m_rel_bias, m_w_attn_up=m_w_attn_up, m_w_o=m_w_o, m_g_ffn2=m_g_ffn2, m_w_ffn2_gu=m_w_ffn2_gu, m_w_ffn2_down=m_w_ffn2_down, v_w_ada=v_w_ada, v_b_ada=v_b_ada, v_g_ffn1=v_g_ffn1, v_w_ffn1_gu=v_w_ffn1_gu, v_w_ffn1_down=v_w_ffn1_down, v_g_mix=v_g_mix, v_w_in=v_w_in, v_pool_mix=v_pool_mix, v_pool_scale=v_pool_scale, v_w_pool_up=v_w_pool_up, v_q_gain=v_q_gain, v_k_gain=v_k_gain, v_sinks=v_sinks, v_rel_bias=v_rel_bias, v_w_attn_up=v_w_attn_up, v_w_o=v_w_o, v_g_ffn2=v_g_ffn2, v_w_ffn2_gu=v_w_ffn2_gu, v_w_ffn2_down=v_w_ffn2_down)
    weights = {n: given[n] for n in TWIN_WEIGHTS}
    shared = {n: given[n] for n in SHARED_INPUTS}
    per_example = {n: given[n] for n in ['x', 'c']}
    grad_fn = _jax.value_and_grad(_loss, argnums=(0, 1))

    def one_microbatch(ex, loss_target):
        ex = dict(ex)
        diff = ex.pop(TWIN_DIFF_INPUT)
        return grad_fn(weights, diff, {**shared, **ex}, loss_target)

    if N_MICROBATCH == 1:
        loss, (grad_w, grad_x) = one_microbatch(per_example, given["loss_target"])
    else:
        def body(carry, xs):
            loss_sum, grad_sum = carry
            l_k, (gw_k, gx_k) = one_microbatch(xs[0], xs[1])
            with _jax.named_scope("update"):
                return (loss_sum + l_k, _jax.tree.map(_jnp.add, grad_sum, gw_k)), gx_k

        init = (_jnp.zeros((), _jnp.float32), _jax.tree.map(_jnp.zeros_like, weights))
        (loss, grad_w), grad_x = _jax.lax.scan(body, init, (per_example, given["loss_target"]))
    with _jax.named_scope("update"):
        delta_w, new_m, new_v = {}, {}, {}
        for n in TWIN_WEIGHTS:
            delta_w[n], new_m[n], new_v[n] = _adamw(weights[n], grad_w[n], given["m_" + n], given["v_" + n])
    return (loss, grad_x, *[grad_w[n] for n in TWIN_WEIGHTS], *[delta_w[n] for n in TWIN_WEIGHTS],
            *[new_m[n] for n in TWIN_WEIGHTS], *[new_v[n] for n in TWIN_WEIGHTS])
```

```python
import functools

import numpy as np
import jax
import jax.numpy as jnp
from jax import lax
from jax.experimental import pallas as pl
from jax.experimental.pallas import tpu as pltpu

F32, BF16 = jnp.float32, jnp.bfloat16
MESH_ID = pl.DeviceIdType.MESH

N_DEV = 8
D_MODEL = 2048
N_MOD = 9
POOL_WINDOWS = (2, 4, 8, 16)
POOL_GROUPS = 4
POOL_GROUP_W = D_MODEL // 8
POOL_W = POOL_GROUPS * POOL_GROUP_W
POOL_HALO = 16
HEAD_DIM = 64
N_HEADS = 16
N_KV = 2
GQA = N_HEADS // N_KV
BLK = 128
NUM_BUCKETS = 32
MAX_EXACT = 16
REL_MAX_DIST = 128
EPS = 1e-6
NEG_INF = -1e30
ATT_W = N_HEADS * HEAD_DIM
KV_W = N_KV * HEAD_DIM
IN_W = POOL_W + ATT_W + 2 * KV_W + 2 * D_MODEL
COL_Q, COL_K, COL_V, COL_GA, COL_GB = POOL_W, POOL_W + ATT_W, POOL_W + ATT_W + KV_W, POOL_W + ATT_W + 2 * KV_W, POOL_W + ATT_W + 2 * KV_W + D_MODEL
LANE = 128

ADAM_LR = 0.001
ADAM_B1 = 0.9
ADAM_B2 = 0.999
ADAM_EPS = 1e-08
ADAM_WD = 0.01
ADAM_STEP = 10

NN = ((1,), (0,))
NT = ((1,), (1,))
TN = ((0,), (0,))


def _dot(a, b, dims, precision=None):
    return lax.dot_general(a, b, (dims, ((), ())), preferred_element_type=F32, precision=precision)


def _tile(n, pref, unit):
    t = (min(pref, n) // unit) * unit
    while t >= unit:
        if n % t == 0:
            return t
        t -= unit
    return n


def _params(*sem):
    return pltpu.CompilerParams(dimension_semantics=sem)


def _sigmoid(x):
    return 1.0 / (1.0 + jnp.exp(-x))


def _mesh_pos():
    return lax.axis_index("x"), lax.axis_index("y"), lax.axis_index("c")


def _slot(p):
    return 4 * p[0] + 2 * p[1] + p[2]


def all_gather_small(x_shard, name):
    m_per, n = x_shard.shape

    def body(x_ref, out_ref, send_sems, recv_sems, local_sem):
        x, y, c = _mesh_pos()
        me, sibling = (x, y, c), (x, y, 1 - c)
        chips = [(1 - x, y), (x, 1 - y), (1 - x, 1 - y)]

        def rows(p):
            return out_ref.at[pl.ds(_slot(p) * m_per, m_per), :]

        def copy(k, block, to, src=None):
            return pltpu.make_async_remote_copy(
                src_ref=rows(block) if src is None else src, dst_ref=rows(block),
                send_sem=send_sems.at[k], recv_sem=recv_sems.at[k], device_id=to, device_id_type=MESH_ID)

        mine = pltpu.make_async_copy(x_ref, rows(me), local_sem)
        mine.start()
        first = [copy(0, me, sibling, src=x_ref)]
        first += [copy(1 + j, me, (*chip, c), src=x_ref) for j, chip in enumerate(chips)]
        for cp in first:
            cp.start()
        passed = [copy(4 + j, (*chip, c), sibling) for j, chip in enumerate(chips)]
        for j, chip in enumerate(chips):
            copy(1 + j, (*chip, c), me).wait_recv()
            passed[j].start()
        copy(0, sibling, me).wait_recv()
        for j, chip in enumerate(chips):
            copy(4 + j, (*chip, 1 - c), me).wait_recv()
        for cp in first + passed:
            cp.wait_send()
        mine.wait()

    return pl.pallas_call(
        body, name=name,
        out_shape=jax.ShapeDtypeStruct((N_DEV * m_per, n), x_shard.dtype),
        in_specs=[pl.BlockSpec(memory_space=pltpu.VMEM)],
        out_specs=pl.BlockSpec(memory_space=pltpu.VMEM),
        scratch_shapes=[pltpu.SemaphoreType.DMA((7,)), pltpu.SemaphoreType.DMA((7,)), pltpu.SemaphoreType.DMA],
    )(x_shard)


def all_gather_hbm(shards, name):
    n_arr = len(shards)

    def body(*refs):
        ins, outs = refs[:n_arr], refs[n_arr:2 * n_arr]
        send_sems, recv_sems, local_sems = refs[2 * n_arr:]
        x, y, c = _mesh_pos()
        me, sibling = (x, y, c), (x, y, 1 - c)
        chips = [(1 - x, y), (x, 1 - y), (1 - x, 1 - y)]

        def copy(a, k, block, to, src=None):
            dst = outs[a].at[_slot(block)]
            return pltpu.make_async_remote_copy(
                src_ref=dst if src is None else src, dst_ref=dst,
                send_sem=send_sems.at[7 * a + k], recv_sem=recv_sems.at[7 * a + k],
                device_id=to, device_id_type=MESH_ID)

        mine = [pltpu.make_async_copy(ins[a], outs[a].at[_slot(me)], local_sems.at[a]) for a in range(n_arr)]
        for cp in mine:
            cp.start()
        first = []
        for a in range(n_arr):
            first.append(copy(a, 0, me, sibling, src=ins[a]))
            first += [copy(a, 1 + j, me, (*chip, c), src=ins[a]) for j, chip in enumerate(chips)]
        for cp in first:
            cp.start()
        passed = []
        for a in range(n_arr):
            for j, chip in enumerate(chips):
                copy(a, 1 + j, (*chip, c), me).wait_recv()
                fwd = copy(a, 4 + j, (*chip, c), sibling)
                fwd.start()
                passed.append(fwd)
        for a in range(n_arr):
            copy(a, 0, sibling, me).wait_recv()
            for j, chip in enumerate(chips):
                copy(a, 4 + j, (*chip, 1 - c), me).wait_recv()
        for cp in first + passed:
            cp.wait_send()
        for cp in mine:
            cp.wait()

    any_spec = pl.BlockSpec(memory_space=pl.ANY)
    return pl.pallas_call(
        body, name=name,
        out_shape=[jax.ShapeDtypeStruct((N_DEV,) + s.shape, s.dtype) for s in shards],
        in_specs=[any_spec] * n_arr, out_specs=[any_spec] * n_arr,
        scratch_shapes=[pltpu.SemaphoreType.DMA((7 * n_arr,)), pltpu.SemaphoreType.DMA((7 * n_arr,)),
                        pltpu.SemaphoreType.DMA((n_arr,))],
    )(*shards)


def all_to_all_hbm(blocks, name):
    n_arr = len(blocks)

    def body(*refs):
        ins, outs = refs[:n_arr], refs[n_arr:2 * n_arr]
        send_sems, recv_sems, local_sems = refs[2 * n_arr:]
        x, y, c = _mesh_pos()
        me = _slot((x, y, c))
        mine = [pltpu.make_async_copy(ins[a].at[me], outs[a].at[me], local_sems.at[a]) for a in range(n_arr)]
        for cp in mine:
            cp.start()
        peers = []
        for k in range(1, N_DEV):
            peers.append(((1 - x) if k & 4 else x, (1 - y) if k & 2 else y, (1 - c) if k & 1 else c))
        sends = []
        for a in range(n_arr):
            for k, peer in enumerate(peers):
                sends.append(pltpu.make_async_remote_copy(
                    src_ref=ins[a].at[_slot(peer)], dst_ref=outs[a].at[me],
                    send_sem=send_sems.at[7 * a + k], recv_sem=recv_sems.at[7 * a + k],
                    device_id=peer, device_id_type=MESH_ID))
        for cp in sends:
            cp.start()
        for a in range(n_arr):
            for k, peer in enumerate(peers):
                pltpu.make_async_remote_copy(
                    src_ref=ins[a].at[_slot(peer)], dst_ref=outs[a].at[_slot(peer)],
                    send_sem=send_sems.at[7 * a + k], recv_sem=recv_sems.at[7 * a + k],
                    device_id=peer, device_id_type=MESH_ID).wait_recv()
        for cp in sends:
            cp.wait_send()
        for cp in mine:
            cp.wait()

    any_spec = pl.BlockSpec(memory_space=pl.ANY)
    return pl.pallas_call(
        body, name=name,
        out_shape=[jax.ShapeDtypeStruct(b.shape, b.dtype) for b in blocks],
        in_specs=[any_spec] * n_arr, out_specs=[any_spec] * n_arr,
        scratch_shapes=[pltpu.SemaphoreType.DMA((7 * n_arr,)), pltpu.SemaphoreType.DMA((7 * n_arr,)),
                        pltpu.SemaphoreType.DMA((n_arr,))],
    )(*blocks)


def mm_nn(a, w, *, out_dtype, name, tm=512, tn=512):
    m, k = a.shape
    tm = _tile(m, tm, 16)
    if w.ndim == 3:
        tn = w.shape[2]
        n = w.shape[0] * tn
        w_spec = pl.BlockSpec((None, k, tn), lambda j, i: (j, 0, 0))
    else:
        n = w.shape[1]
        tn = _tile(n, tn, LANE)
        w_spec = pl.BlockSpec((k, tn), lambda j, i: (0, j))

    def body(a_ref, w_ref, o_ref):
        o_ref[...] = _dot(a_ref[...], w_ref[...], NN).astype(o_ref.dtype)

    return pl.pallas_call(
        body, name=name, grid=(n // tn, m // tm),
        in_specs=[pl.BlockSpec((tm, k), lambda j, i: (i, 0)), w_spec],
        out_specs=pl.BlockSpec((tm, tn), lambda j, i: (i, j)),
        out_shape=jax.ShapeDtypeStruct((m, n), out_dtype),
        compiler_params=_params("parallel", "parallel"),
    )(a, w)


def mm_nn_residual(a, w, x_in, gate, coef, *, name, tm=512, tn=512):
    m, k = a.shape
    n = w.shape[1]
    tm, tn = _tile(m, tm, 16), _tile(n, tn, LANE)

    def body(a_ref, w_ref, x_ref, g_ref, o_ref, f_ref):
        f = _dot(a_ref[...], w_ref[...], NN)
        f_ref[...] = f.astype(BF16)
        o_ref[...] = x_ref[...] + (coef * g_ref[...]) * f

    return pl.pallas_call(
        body, name=name, grid=(n // tn, m // tm),
        in_specs=[pl.BlockSpec((tm, k), lambda j, i: (i, 0)), pl.BlockSpec((k, tn), lambda j, i: (0, j)),
                  pl.BlockSpec((tm, tn), lambda j, i: (i, j)), pl.BlockSpec((1, tn), lambda j, i: (0, j))],
        out_specs=[pl.BlockSpec((tm, tn), lambda j, i: (i, j)), pl.BlockSpec((tm, tn), lambda j, i: (i, j))],
        out_shape=[jax.ShapeDtypeStruct((m, n), F32), jax.ShapeDtypeStruct((m, n), BF16)],
        compiler_params=_params("parallel", "parallel"),
    )(a, w, x_in, gate)


def mm_nt(a, w, *, out_dtype, name, tm=512, tn=512):
    m, k = a.shape
    n = w.shape[0]
    tm, tn = _tile(m, tm, 16), _tile(n, tn, LANE)

    def body(a_ref, w_ref, o_ref):
        o_ref[...] = _dot(a_ref[...], w_ref[...], NT).astype(o_ref.dtype)

    return pl.pallas_call(
        body, name=name, grid=(n // tn, m // tm),
        in_specs=[pl.BlockSpec((tm, k), lambda j, i: (i, 0)), pl.BlockSpec((tn, k), lambda j, i: (j, 0))],
        out_specs=pl.BlockSpec((tm, tn), lambda j, i: (i, j)),
        out_shape=jax.ShapeDtypeStruct((m, n), out_dtype),
        compiler_params=_params("parallel", "parallel"),
    )(a, w)


def mm_nt_blocks(a, w3, *, out_dtype, name, tm=1024):
    m = a.shape[0]
    n_blk, n, tn = w3.shape
    tm = _tile(m, tm, 16)

    def body(a_ref, w_ref, o_ref, acc_ref):
        j = pl.program_id(1)
        p = _dot(a_ref[...], w_ref[...], NT)

        @pl.when(j == 0)
        def _():
            acc_ref[...] = p

        @pl.when(j > 0)
        def _():
            acc_ref[...] += p

        @pl.when(j == n_blk - 1)
        def _():
            o_ref[...] = acc_ref[...].astype(o_ref.dtype)

    return pl.pallas_call(
        body, name=name, grid=(m // tm, n_blk),
        in_specs=[pl.BlockSpec((tm, tn), lambda i, j: (i, j)), pl.BlockSpec((None, n, tn), lambda i, j: (j, 0, 0))],
        out_specs=pl.BlockSpec((tm, n), lambda i, j: (i, 0)),
        out_shape=jax.ShapeDtypeStruct((m, n), out_dtype),
        scratch_shapes=[pltpu.VMEM((tm, n), F32)],
        compiler_params=_params("parallel", "arbitrary"),
    )(a, w3)


def mm_tn(a, dy, *, name, n_blocks=None, tk=2048, tn=1408, ts=512):
    s, k = a.shape
    n = dy.shape[1]
    tk, ts = _tile(k, tk, LANE), _tile(s, ts, 16)
    if n_blocks is None:
        tn = _tile(n, tn, LANE)
        out_spec = pl.BlockSpec((tk, tn), lambda kk, j, t: (kk, j))
        out_shape = jax.ShapeDtypeStruct((k, n), BF16)
    else:
        tn = n // n_blocks
        out_spec = pl.BlockSpec((None, tk, tn), lambda kk, j, t: (j, kk, 0))
        out_shape = jax.ShapeDtypeStruct((n_blocks, k, tn), BF16)
    n_steps = s // ts

    def body(a_ref, dy_ref, o_ref, acc_ref):
        t = pl.program_id(2)
        p = _dot(a_ref[...], dy_ref[...], TN)

        @pl.when(t == 0)
        def _():
            acc_ref[...] = p

        @pl.when(t > 0)
        def _():
            acc_ref[...] += p

        @pl.when(t == n_steps - 1)
        def _():
            o_ref[...] = acc_ref[...].astype(BF16)

    return pl.pallas_call(
        body, name=name, grid=(k // tk, n // tn, n_steps),
        in_specs=[pl.BlockSpec((ts, tk), lambda kk, j, t: (t, kk)), pl.BlockSpec((ts, tn), lambda kk, j, t: (t, j))],
        out_specs=out_spec, out_shape=out_shape,
        scratch_shapes=[pltpu.VMEM((tk, tn), F32)],
        compiler_params=_params("parallel", "parallel", "arbitrary"),
    )(a, dy)


def ffn_up(h, wgu3, *, name, tm=512):
    s, k = h.shape
    n = wgu3.shape[2]
    half = wgu3.shape[0] // 2
    tm = _tile(s, tm, 16)

    def body(h_ref, wg_ref, wu_ref, g_ref, u_ref, a_ref):
        hh = h_ref[...]
        g = _dot(hh, wg_ref[...], NN)
        u = _dot(hh, wu_ref[...], NN)
        g_ref[...] = g.astype(BF16)
        u_ref[...] = u.astype(BF16)
        a_ref[...] = (g * _sigmoid(g) * u).astype(BF16)

    out = jax.ShapeDtypeStruct((s, half * n), BF16)
    blk = pl.BlockSpec((tm, n), lambda j, i: (i, j))
    return pl.pallas_call(
        body, name=name, grid=(half, s // tm),
        in_specs=[pl.BlockSpec((tm, k), lambda j, i: (i, 0)),
                  pl.BlockSpec((None, k, n), lambda j, i: (j, 0, 0)),
                  pl.BlockSpec((None, k, n), lambda j, i: (j + half, 0, 0))],
        out_specs=[blk, blk, blk], out_shape=[out, out, out],
        compiler_params=_params("parallel", "parallel"),
    )(h, wgu3, wgu3)


def _row_spec(ts, width, col=0):
    return pl.BlockSpec((ts, width), lambda i: (i, col))


def _vec_spec(width):
    return pl.BlockSpec((1, width), lambda i: (0, 0))


def _accumulate(ref, value):
    i = pl.program_id(0)

    @pl.when(i == 0)
    def _():
        ref[...] = value

    @pl.when(i > 0)
    def _():
        ref[...] += value


def norm_modulate(x, g, shift, scale, *, name, ts=256):
    s, d = x.shape
    ts = _tile(s, ts, 16)

    def body(x_ref, g_ref, sh_ref, sc_ref, h_ref):
        xx = x_ref[...]
        r = lax.rsqrt(jnp.mean(xx * xx, axis=-1, keepdims=True) + EPS)
        h_ref[...] = ((xx * r) * g_ref[...] * (1 + sc_ref[...]) + sh_ref[...]).astype(BF16)

    return pl.pallas_call(
        body, name=name, grid=(s // ts,),
        in_specs=[_row_spec(ts, d), _vec_spec(d), _vec_spec(d), _vec_spec(d)],
        out_specs=_row_spec(ts, d), out_shape=jax.ShapeDtypeStruct((s, d), BF16),
        compiler_params=_params("parallel"),
    )(x, g, shift, scale)


def norm_modulate_bwd(x, g, scale, dh, dx_out, *, name, ts=256):
    s, d = x.shape
    ts = _tile(s, ts, 16)

    def body(x_ref, g_ref, sc_ref, dh_ref, dxo_ref, dx_ref, dsh_ref, dsc_ref, dg_ref):
        xx, dh_ = x_ref[...], dh_ref[...]
        r = lax.rsqrt(jnp.mean(xx * xx, axis=-1, keepdims=True) + EPS)
        xh = xx * r
        dn = dh_ * (1 + sc_ref[...])
        dxh = dn * g_ref[...]
        dx_ref[...] = dxo_ref[...] + r * (dxh - xh * jnp.mean(dxh * xh, axis=-1, keepdims=True))
        _accumulate(dsh_ref, jnp.sum(dh_, axis=0, keepdims=True))
        _accumulate(dsc_ref, jnp.sum(dh_ * (xh * g_ref[...]), axis=0, keepdims=True))
        _accumulate(dg_ref, jnp.sum(dn * xh, axis=0, keepdims=True))

    vec = jax.ShapeDtypeStruct((1, d), F32)
    return pl.pallas_call(
        body, name=name, grid=(s // ts,),
        in_specs=[_row_spec(ts, d), _vec_spec(d), _vec_spec(d), _row_spec(ts, d), _row_spec(ts, d)],
        out_specs=[_row_spec(ts, d), _vec_spec(d), _vec_spec(d), _vec_spec(d)],
        out_shape=[jax.ShapeDtypeStruct((s, d), F32), vec, vec, vec],
        compiler_params=_params("arbitrary"),
    )(x, g, scale, dh, dx_out)


def gate_bwd(dx_out, f, gate, coef, *, name, ts=256):
    s, d = dx_out.shape
    ts = _tile(s, ts, 16)

    def body(dx_ref, f_ref, g_ref, df_ref, dg_ref):
        dx = dx_ref[...]
        df_ref[...] = ((coef * g_ref[...]) * dx).astype(BF16)
        _accumulate(dg_ref, coef * jnp.sum(dx * f_ref[...].astype(F32), axis=0, keepdims=True))

    return pl.pallas_call(
        body, name=name, grid=(s // ts,),
        in_specs=[_row_spec(ts, d), _row_spec(ts, d), _vec_spec(d)],
        out_specs=[_row_spec(ts, d), _vec_spec(d)],
        out_shape=[jax.ShapeDtypeStruct((s, d), BF16), jax.ShapeDtypeStruct((1, d), F32)],
        compiler_params=_params("arbitrary"),
    )(dx_out, f, gate)


def swiglu_bwd(da, g, u, *, name, ts=256):
    s, f = da.shape
    ts = _tile(s, ts, 16)

    def body(da_ref, g_ref, u_ref, o_ref):
        da_, gg, uu = da_ref[...].astype(F32), g_ref[...].astype(F32), u_ref[...].astype(F32)
        sg = _sigmoid(gg)
        o_ref[:, :f] = (da_ * uu * (sg * (1 + gg * (1 - sg)))).astype(BF16)
        o_ref[:, f:] = (da_ * (gg * sg)).astype(BF16)

    return pl.pallas_call(
        body, name=name, grid=(s // ts,),
        in_specs=[_row_spec(ts, f)] * 3,
        out_specs=_row_spec(ts, 2 * f), out_shape=jax.ShapeDtypeStruct((s, 2 * f), BF16),
        compiler_params=_params("parallel"),
    )(da, g, u)


def loss_head(y, target, *, name, ts=256):
    s, d = y.shape
    ts = _tile(s, ts, 16)

    def body(y_ref, t_ref, dy_ref, l_ref):
        err = y_ref[...] - t_ref[...]
        dy_ref[...] = err * (1.0 / d)
        part = jnp.sum(jnp.sum(err * err, axis=0, keepdims=True), axis=1, keepdims=True) * (0.5 / d)
        _accumulate(l_ref, jnp.broadcast_to(part, (1, LANE)))

    return pl.pallas_call(
        body, name=name, grid=(s // ts,),
        in_specs=[_row_spec(ts, d), _row_spec(ts, d)],
        out_specs=[_row_spec(ts, d), _vec_spec(LANE)],
        out_shape=[jax.ShapeDtypeStruct((s, d), F32), jax.ShapeDtypeStruct((1, LANE), F32)],
        compiler_params=_params("arbitrary"),
    )(y, target)


def merge_fwd(z, y_pool, y_attn, *, name, ts=256, tc=256):
    s, d = y_pool.shape
    ts = _tile(s, ts, 16)

    def body(ga_ref, gb_ref, yp_ref, ya_ref, o_ref):
        o_ref[...] = (_sigmoid(ga_ref[...]) * yp_ref[...] + _sigmoid(gb_ref[...]) * ya_ref[...]).astype(BF16)

    blk = pl.BlockSpec((ts, tc), lambda i, j: (i, j))
    return pl.pallas_call(
        body, name=name, grid=(s // ts, d // tc),
        in_specs=[pl.BlockSpec((ts, tc), lambda i, j: (i, COL_GA // tc + j)),
                  pl.BlockSpec((ts, tc), lambda i, j: (i, COL_GB // tc + j)), blk, blk],
        out_specs=blk, out_shape=jax.ShapeDtypeStruct((s, d), BF16),
        compiler_params=_params("parallel", "parallel"),
    )(z, z, y_pool, y_attn)


def merge_bwd(dmerged, z, y_pool, y_attn, *, name, ts=256, tc=256):
    s, d = y_pool.shape
    ts = _tile(s, ts, 16)

    def body(dm_ref, ga_ref, gb_ref, yp_ref, ya_ref, dyp_ref, dya_ref, dga_ref, dgb_ref):
        dm = dm_ref[...]
        sa, sb = _sigmoid(ga_ref[...]), _sigmoid(gb_ref[...])
        dyp_ref[...] = (dm * sa).astype(BF16)
        dya_ref[...] = (dm * sb).astype(BF16)
        dga_ref[...] = (dm * yp_ref[...] * (sa * (1 - sa))).astype(BF16)
        dgb_ref[...] = (dm * ya_ref[...] * (sb * (1 - sb))).astype(BF16)

    blk = pl.BlockSpec((ts, tc), lambda i, j: (i, j))
    out = jax.ShapeDtypeStruct((s, d), BF16)
    return pl.pallas_call(
        body, name=name, grid=(s // ts, d // tc),
        in_specs=[blk, pl.BlockSpec((ts, tc), lambda i, j: (i, COL_GA // tc + j)),
                  pl.BlockSpec((ts, tc), lambda i, j: (i, COL_GB // tc + j)), blk, blk],
        out_specs=[blk] * 4, out_shape=[out] * 4,
        compiler_params=_params("parallel", "parallel"),
    )(dmerged, z, z, y_pool, y_attn)


def _window_counts(t0, rows):
    t1 = (t0 + 1 + lax.broadcasted_iota(jnp.int32, (rows, 1), 0)).astype(F32)
    return [jnp.minimum(t1, float(w)) for w in POOL_WINDOWS]


def pool_fwd(z, pool_mix, pool_scale, *, name, ts=256):
    s = z.shape[0]
    ts = _tile(s, ts, 16)
    per = ts // POOL_HALO

    def body(u_ref, halo_ref, pm_ref, ps_ref, pooled_ref, p_ref):
        i = pl.program_id(0)
        u = u_ref[...]
        halo = jnp.where(i > 0, halo_ref[...], 0.0)
        run = jnp.concatenate([halo, u], axis=0)
        sums, width = [], 1
        for w in POOL_WINDOWS:
            while width < w:
                run = run + pltpu.roll(run, width, 0)
                width *= 2
            sums.append(run[POOL_HALO:])
        counts = _window_counts(i * ts, ts)
        for gi in range(POOL_GROUPS):
            cols = slice(gi * POOL_GROUP_W, (gi + 1) * POOL_GROUP_W)
            pooled = (sums[gi][:, cols] / counts[gi] - u[:, cols]).astype(BF16)
            pooled_ref[:, cols] = pooled
            p_ref[:, cols] = (_dot(pooled, pm_ref[gi], NN) * ps_ref[:, cols]).astype(BF16)

    out = jax.ShapeDtypeStruct((s, POOL_W), BF16)
    return pl.pallas_call(
        body, name=name, grid=(s // ts,),
        in_specs=[_row_spec(ts, POOL_W),
                  pl.BlockSpec((POOL_HALO, POOL_W), lambda i: (jnp.maximum(i * per - 1, 0), 0)),
                  pl.BlockSpec((POOL_GROUPS, POOL_GROUP_W, POOL_GROUP_W), lambda i: (0, 0, 0)),
                  _vec_spec(POOL_W)],
        out_specs=[_row_spec(ts, POOL_W)] * 2, out_shape=[out, out],
        compiler_params=_params("parallel"),
    )(z, z, pool_mix, pool_scale)


def pool_bwd(dp, pooled, pool_mix, pool_scale, *, name, ts=256):
    s = dp.shape[0]
    ts = _tile(s, ts, 16)
    per = ts // POOL_HALO
    n_steps = s // ts
    rows = ts + POOL_HALO

    def body(dp_ref, halo_ref, pooled_ref, pm_ref, ps_ref, du_ref, dps_ref, dpm_ref):
        i = pl.program_id(0)
        dp_main = dp_ref[...]
        halo = jnp.where(i < n_steps - 1, halo_ref[...], 0.0)
        dmixed = jnp.concatenate([dp_main, halo], axis=0) * ps_ref[...]
        counts = _window_counts(i * ts, rows)
        dps_parts = []
        for gi, w in enumerate(POOL_WINDOWS):
            cols = slice(gi * POOL_GROUP_W, (gi + 1) * POOL_GROUP_W)
            dmx = dmixed[:, cols].astype(BF16)
            pooled = pooled_ref[:, cols]
            mixed = _dot(pooled, pm_ref[gi], NN)
            dps_parts.append(jnp.sum(dp_main[:, cols] * mixed, axis=0, keepdims=True))
            dpm_g = _dot(pooled, dmx[:ts], TN)

            @pl.when(i == 0)
            def _():
                dpm_ref[gi] = dpm_g

            @pl.when(i > 0)
            def _():
                dpm_ref[gi] += dpm_g

            dpooled = _dot(dmx, pm_ref[gi], NT)
            run, width = dpooled / counts[gi], 1
            while width < w:
                run = run + pltpu.roll(run, rows - width, 0)
                width *= 2
            du_ref[:, cols] = (run[:ts] - dpooled[:ts]).astype(BF16)
        _accumulate(dps_ref, jnp.concatenate(dps_parts, axis=1))

    return pl.pallas_call(
        body, name=name, grid=(n_steps,),
        in_specs=[_row_spec(ts, POOL_W),
                  pl.BlockSpec((POOL_HALO, POOL_W), lambda i: (jnp.minimum((i + 1) * per, s // POOL_HALO - 1), 0)),
                  _row_spec(ts, POOL_W),
                  pl.BlockSpec((POOL_GROUPS, POOL_GROUP_W, POOL_GROUP_W), lambda i: (0, 0, 0)),
                  _vec_spec(POOL_W)],
        out_specs=[_row_spec(ts, POOL_W), _vec_spec(POOL_W),
                   pl.BlockSpec((POOL_GROUPS, POOL_GROUP_W, POOL_GROUP_W), lambda i: (0, 0, 0))],
        out_shape=[jax.ShapeDtypeStruct((s, POOL_W), BF16), jax.ShapeDtypeStruct((1, POOL_W), F32),
                   jax.ShapeDtypeStruct((POOL_GROUPS, POOL_GROUP_W, POOL_GROUP_W), F32)],
        compiler_params=_params("arbitrary"),
    )(dp, dp, pooled, pool_mix, pool_scale)


def _bucket_one_hot():
    ql = np.arange(BLK)[:, None]
    j = np.arange(2 * BLK)[None, :]
    n = np.clip(BLK + ql - j, 0, None)
    nf = np.maximum(n, 1).astype(np.float32)
    large = MAX_EXACT + (np.log(nf / MAX_EXACT) / np.log(REL_MAX_DIST / MAX_EXACT)
                         * (NUM_BUCKETS - MAX_EXACT)).astype(np.int32)
    large = np.minimum(large, NUM_BUCKETS - 1)
    bucket = np.where(n < MAX_EXACT, n, large).astype(np.int32).reshape(-1)
    return (np.arange(NUM_BUCKETS)[:, None] == bucket[None, :]).astype(np.float32)


def bias_table(rel_bias_t, one_hot, *, name, tc=4096):
    n = one_hot.shape[1]

    def body(rb_ref, oh_ref, o_ref):
        o_ref[...] = _dot(rb_ref[...], oh_ref[...], NN, precision=lax.Precision.HIGHEST)

    return pl.pallas_call(
        body, name=name, grid=(n // tc,),
        in_specs=[pl.BlockSpec((N_HEADS, NUM_BUCKETS), lambda i: (0, 0)), pl.BlockSpec((NUM_BUCKETS, tc), lambda i: (0, i))],
        out_specs=pl.BlockSpec((N_HEADS, tc), lambda i: (0, i)),
        out_shape=jax.ShapeDtypeStruct((N_HEADS, n), F32),
        compiler_params=_params("parallel"),
    )(rel_bias_t, one_hot)


def bias_table_bwd(dbias, one_hot, *, name, tc=4096):
    n = one_hot.shape[1]

    def body(db_ref, oh_ref, o_ref):
        _accumulate(o_ref, _dot(db_ref[...], oh_ref[...], NT, precision=lax.Precision.HIGHEST))

    return pl.pallas_call(
        body, name=name, grid=(n // tc,),
        in_specs=[pl.BlockSpec((N_HEADS, tc), lambda i: (0, i)), pl.BlockSpec((NUM_BUCKETS, tc), lambda i: (0, i))],
        out_specs=pl.BlockSpec((N_HEADS, NUM_BUCKETS), lambda i: (0, 0)),
        out_shape=jax.ShapeDtypeStruct((N_HEADS, NUM_BUCKETS), F32),
        compiler_params=_params("arbitrary"),
    )(dbias, one_hot)


def _lane_half(shape):
    return lax.broadcasted_iota(jnp.int32, shape, len(shape) - 1) < HEAD_DIM


def _half_sums(v, first):
    s0 = jnp.sum(jnp.where(first, v, 0.0), axis=-1, keepdims=True)
    s1 = jnp.sum(jnp.where(first, 0.0, v), axis=-1, keepdims=True)
    return jnp.where(first, s0, s1)


def _band_mask(n):
    ql = lax.broadcasted_iota(jnp.int32, (BLK, 2 * BLK), 0)
    j = lax.broadcasted_iota(jnp.int32, (BLK, 2 * BLK), 1)
    return (j > ql) & (j <= ql + BLK) & ((j >= BLK) | (n > 0))


def _norm_keys(kband, kg2):
    first = _lane_half(kband.shape)
    r = lax.rsqrt(_half_sums(kband * kband, first) * (1.0 / HEAD_DIM) + EPS)
    return kband * r, r


def _head_setup(q_ref, pair, sub, kv, qg2):
    qp = q_ref[:, pair * LANE:(pair + 1) * LANE]
    if sub != kv:
        qp = pltpu.roll(qp, HEAD_DIM, 1)
    lanes = _lane_half(qp.shape) if kv == 0 else jnp.logical_not(_lane_half(qp.shape))
    qa = jnp.where(lanes, qp, 0.0)
    r = lax.rsqrt(jnp.sum(qa * qa, axis=-1, keepdims=True) * (1.0 / HEAD_DIM) + EPS)
    return qa * r, r, lanes


def _softmax_with_sink(logits, sink):
    m = jnp.maximum(jnp.max(logits, axis=-1, keepdims=True), sink)
    e = jnp.exp(logits - m)
    es = jnp.exp(sink - m)
    den = jnp.sum(e, axis=-1, keepdims=True) + es
    return e / den, es / den


def _attn_specs(nb, last):
    cur = lambda n: jnp.minimum(n, last)
    prev = lambda n: jnp.minimum(jnp.maximum(n - 1, 0), last)
    return [pl.BlockSpec((BLK, ATT_W), lambda n: (cur(n), COL_Q // ATT_W)),
            pl.BlockSpec((BLK, KV_W), lambda n: (prev(n), COL_K // KV_W)),
            pl.BlockSpec((BLK, KV_W), lambda n: (cur(n), COL_K // KV_W)),
            pl.BlockSpec((BLK, KV_W), lambda n: (prev(n), COL_V // KV_W)),
            pl.BlockSpec((BLK, KV_W), lambda n: (cur(n), COL_V // KV_W))]


def attn_fwd(z, qg2, kg2, sinks, bias, *, name):
    s = z.shape[0]
    nb = s // BLK

    def body(q_ref, kp_ref, kc_ref, vp_ref, vc_ref, qg_ref, kg_ref, sink_ref, bias_ref, o_ref):
        n = pl.program_id(0)
        mask = _band_mask(n)
        kn, _ = _norm_keys(jnp.concatenate([kp_ref[...], kc_ref[...]], axis=0), kg_ref[...])
        kn = (kn * kg_ref[...]).astype(BF16)
        vb = jnp.concatenate([vp_ref[...], vc_ref[...]], axis=0).astype(BF16)
        for pair in range(N_HEADS // 2):
            kv = (2 * pair) // GQA
            acc = None
            for sub in range(2):
                h = 2 * pair + sub
                xh, _, lanes = _head_setup(q_ref, pair, sub, kv, qg_ref[...])
                qn = (xh * qg_ref[...]).astype(BF16)
                logits = _dot(qn, kn, NT) * (HEAD_DIM ** -0.5) + bias_ref[h]
                p, _ = _softmax_with_sink(jnp.where(mask, logits, NEG_INF), sink_ref[h])
                out = jnp.where(lanes, _dot(p.astype(BF16), vb, NN), 0.0)
                if sub != kv:
                    out = pltpu.roll(out, HEAD_DIM, 1)
                acc = out if acc is None else acc + out
            o_ref[:, pair * LANE:(pair + 1) * LANE] = acc.astype(BF16)

    return pl.pallas_call(
        body, name=name, grid=(nb,),
        in_specs=_attn_specs(nb, nb - 1) + [
            _vec_spec(LANE), _vec_spec(LANE), pl.BlockSpec(memory_space=pltpu.SMEM),
            pl.BlockSpec((N_HEADS, BLK, 2 * BLK), lambda n: (0, 0, 0))],
        out_specs=pl.BlockSpec((BLK, ATT_W), lambda n: (n, 0)),
        out_shape=jax.ShapeDtypeStruct((s, ATT_W), BF16),
        compiler_params=_params("parallel"),
    )(z, z, z, z, z, qg2, kg2, sinks, bias)


def attn_bwd(z, d_out, qg2, kg2, sinks, bias, *, name):
    s = z.shape[0]
    nb = s // BLK
    scale = HEAD_DIM ** -0.5

    def body(q_ref, kp_ref, kc_ref, vp_ref, vc_ref, do_ref, qg_ref, kg_ref, sink_ref, bias_ref,
             dq_ref, dk_ref, dv_ref, dqg_ref, dkg_ref, dsink_ref, dbias_ref, band_k, band_v, carry_k, carry_v):
        n = pl.program_id(0)

        @pl.when(n == 0)
        def _():
            dqg_ref[...] = jnp.zeros_like(dqg_ref)
            dkg_ref[...] = jnp.zeros_like(dkg_ref)
            dsink_ref[...] = jnp.zeros_like(dsink_ref)
            dbias_ref[...] = jnp.zeros_like(dbias_ref)
            carry_k[...] = jnp.zeros_like(carry_k)
            carry_v[...] = jnp.zeros_like(carry_v)

        @pl.when(n == nb)
        def _():
            band_k[...] = jnp.zeros_like(band_k)
            band_v[...] = jnp.zeros_like(band_v)

        @pl.when(n < nb)
        def _():
            mask = _band_mask(n)
            kx, _ = _norm_keys(jnp.concatenate([kp_ref[...], kc_ref[...]], axis=0), kg_ref[...])
            kn = (kx * kg_ref[...]).astype(BF16)
            vb = jnp.concatenate([vp_ref[...], vc_ref[...]], axis=0).astype(BF16)
            lane16 = lax.broadcasted_iota(jnp.int32, (1, N_HEADS), 1)
            dkn = jnp.zeros((2 * BLK, KV_W), F32)
            dvb = jnp.zeros((2 * BLK, KV_W), F32)
            dqg = jnp.zeros((1, LANE), F32)
            dsink = jnp.zeros((1, N_HEADS), F32)
            for pair in range(N_HEADS // 2):
                kv = (2 * pair) // GQA
                acc = None
                for sub in range(2):
                    h = 2 * pair + sub
                    xh, r, lanes = _head_setup(q_ref, pair, sub, kv, qg_ref[...])
                    qn = (xh * qg_ref[...]).astype(BF16)
                    logits = _dot(qn, kn, NT) * scale + bias_ref[h]
                    p, p_sink = _softmax_with_sink(jnp.where(mask, logits, NEG_INF), sink_ref[h])
                    do = do_ref[:, pair * LANE:(pair + 1) * LANE].astype(F32)
                    if sub != kv:
                        do = pltpu.roll(do, HEAD_DIM, 1)
                    do = jnp.where(lanes, do, 0.0).astype(BF16)
                    dp = _dot(do, vb, NT)
                    delta = jnp.sum(p * dp, axis=-1, keepdims=True)
                    ds = p * (dp - delta)
                    dsink = dsink + jnp.where(lane16 == h, -jnp.sum(p_sink * delta, axis=0, keepdims=True), 0.0)
                    dbias_ref[h] += ds
                    ds16 = ds.astype(BF16)
                    dqn = jnp.where(lanes, _dot(ds16, kn, NN) * scale, 0.0)
                    dkn = dkn + _dot(ds16, qn, TN) * scale
                    dvb = dvb + _dot(p.astype(BF16), do, TN)
                    dqg = dqg + jnp.sum(dqn * xh, axis=0, keepdims=True)
                    dxh = dqn * qg_ref[...]
                    dq = r * (dxh - xh * (jnp.sum(dxh * xh, axis=-1, keepdims=True) * (1.0 / HEAD_DIM)))
                    if sub != kv:
                        dq = pltpu.roll(dq, HEAD_DIM, 1)
                    acc = dq if acc is None else acc + dq
                dq_ref[:, pair * LANE:(pair + 1) * LANE] = acc.astype(BF16)
            band_k[...] = dkn
            band_v[...] = dvb
            dqg_ref[...] += dqg
            dsink_ref[...] += dsink

        dkn_prev = carry_k[...] + band_k[:BLK]
        dv_ref[...] = (carry_v[...] + band_v[:BLK]).astype(BF16)
        carry_k[...] = band_k[BLK:]
        carry_v[...] = band_v[BLK:]
        kp = kp_ref[...]
        first = _lane_half(kp.shape)
        r = lax.rsqrt(_half_sums(kp * kp, first) * (1.0 / HEAD_DIM) + EPS)
        xh = kp * r
        dkg_ref[...] += jnp.sum(dkn_prev * xh, axis=0, keepdims=True)
        dxh = dkn_prev * kg_ref[...]
        dk_ref[...] = (r * (dxh - xh * (_half_sums(dxh * xh, first) * (1.0 / HEAD_DIM)))).astype(BF16)

        @pl.when(n == nb)
        def _():
            dqg_ref[...] += pltpu.roll(dqg_ref[...], HEAD_DIM, 1)
            dkg_ref[...] += pltpu.roll(dkg_ref[...], HEAD_DIM, 1)

    last = nb - 1
    cur = lambda n: jnp.minimum(n, last)
    back = lambda n: jnp.maximum(n - 1, 0)
    full3 = pl.BlockSpec((N_HEADS, BLK, 2 * BLK), lambda n: (0, 0, 0))
    return pl.pallas_call(
        body, name=name, grid=(nb + 1,),
        in_specs=_attn_specs(nb, last) + [
            pl.BlockSpec((BLK, ATT_W), lambda n: (cur(n), 0)),
            _vec_spec(LANE), _vec_spec(LANE), pl.BlockSpec(memory_space=pltpu.SMEM), full3],
        out_specs=[pl.BlockSpec((BLK, ATT_W), lambda n: (cur(n), 0)),
                   pl.BlockSpec((BLK, KV_W), lambda n: (back(n), 0)),
                   pl.BlockSpec((BLK, KV_W), lambda n: (back(n), 0)),
                   _vec_spec(LANE), _vec_spec(LANE), _vec_spec(N_HEADS), full3],
        out_shape=[jax.ShapeDtypeStruct((s, ATT_W), BF16), jax.ShapeDtypeStruct((s, KV_W), BF16),
                   jax.ShapeDtypeStruct((s, KV_W), BF16), jax.ShapeDtypeStruct((1, LANE), F32),
                   jax.ShapeDtypeStruct((1, LANE), F32), jax.ShapeDtypeStruct((1, N_HEADS), F32),
                   jax.ShapeDtypeStruct((N_HEADS, BLK, 2 * BLK), F32)],
        scratch_shapes=[pltpu.VMEM((2 * BLK, KV_W), F32), pltpu.VMEM((2 * BLK, KV_W), F32),
                        pltpu.VMEM((BLK, KV_W), F32), pltpu.VMEM((BLK, KV_W), F32)],
        compiler_params=_params("arbitrary"),
    )(z, z, z, z, z, d_out, qg2, kg2, sinks, bias)


def _adamw(w, g, m, v):
    m = ADAM_B1 * m + (1.0 - ADAM_B1) * g
    v = ADAM_B2 * v + (1.0 - ADAM_B2) * (g * g)
    m_hat = m / (1.0 - ADAM_B1 ** ADAM_STEP)
    v_hat = v / (1.0 - ADAM_B2 ** ADAM_STEP)
    delta = -ADAM_LR * (m_hat / (jnp.sqrt(v_hat) + ADAM_EPS) + ADAM_WD * w)
    return delta, m, v


def ada_fwd(c16, w, b, *, name, tn=768):
    k, n = w.shape
    tn = _tile(n, tn, LANE)

    def body(c_ref, w_ref, b_ref, o_ref):
        cc = c_ref[...]
        o_ref[...] = _dot((cc * _sigmoid(cc)).astype(BF16), w_ref[...].astype(BF16), NN) + b_ref[...]

    return pl.pallas_call(
        body, name=name, grid=(n // tn,),
        in_specs=[pl.BlockSpec((c16.shape[0], k), lambda j: (0, 0)), pl.BlockSpec((k, tn), lambda j: (0, j)),
                  pl.BlockSpec((1, tn), lambda j: (0, j))],
        out_specs=pl.BlockSpec((c16.shape[0], tn), lambda j: (0, j)),
        out_shape=jax.ShapeDtypeStruct((c16.shape[0], n), F32),
        compiler_params=_params("parallel"),
    )(c16, w, b)


def ada_bwd_adamw(c_t, dmod, w, m, v, *, name, tn=256):
    k, n = w.shape
    tn = _tile(n, tn, LANE)

    def body(c_ref, d_ref, w_ref, m_ref, v_ref, g_ref, dl_ref, mo_ref, vo_ref):
        cc = c_ref[...]
        g = _dot((cc * _sigmoid(cc)).astype(BF16), d_ref[...].astype(BF16), NN)
        g_ref[...] = g
        dl_ref[...], mo_ref[...], vo_ref[...] = _adamw(w_ref[...], g, m_ref[...], v_ref[...])

    blk = pl.BlockSpec((k, tn), lambda j: (0, j))
    out = jax.ShapeDtypeStruct((k, n), F32)
    return pl.pallas_call(
        body, name=name, grid=(n // tn,),
        in_specs=[pl.BlockSpec((k, LANE), lambda j: (0, 0)), pl.BlockSpec((LANE, tn), lambda j: (0, j)), blk, blk, blk],
        out_specs=[blk] * 4, out_shape=[out] * 4,
        compiler_params=_params("parallel"),
    )(c_t, dmod, w, m, v)


def adamw_from_parts(parts, w, m, v, *, name):
    r, c = w.shape
    tr = _tile(r, max(16, (256 * 1024) // c), 16)

    def body(p_ref, w_ref, m_ref, v_ref, g_ref, dl_ref, mo_ref, vo_ref):
        g = p_ref[0].astype(F32)
        for d in range(1, N_DEV):
            g = g + p_ref[d].astype(F32)
        g_ref[...] = g
        dl_ref[...], mo_ref[...], vo_ref[...] = _adamw(w_ref[...], g, m_ref[...], v_ref[...])

    blk = pl.BlockSpec((tr, c), lambda i: (i, 0))
    out = jax.ShapeDtypeStruct((r, c), F32)
    return pl.pallas_call(
        body, name=name, grid=(r // tr,),
        in_specs=[pl.BlockSpec((N_DEV, tr, c), lambda i: (0, i, 0)), blk, blk, blk],
        out_specs=[blk] * 4, out_shape=[out] * 4,
        compiler_params=_params("parallel"),
    )(parts, w, m, v)


def _ffn_fwd(x_in, g, shift, scale, gate, wgu3, wd, tag):
    h = norm_modulate(x_in, g, shift, scale, name=f"{tag}_norm")
    gg, uu, act = ffn_up(h, wgu3, name=f"{tag}_up")
    x_out, f = mm_nn_residual(act, wd, x_in, gate, 0.5, name=f"{tag}_down")
    return x_out, (h, gg, uu, act, f)


def _ffn_bwd(dx_out, x_in, g, scale, gate, wgu3, wd, saved, tag):
    h, gg, uu, act, f = saved
    df, dgate = gate_bwd(dx_out, f, gate, 0.5, name=f"{tag}_gate_bwd")
    dwd = mm_tn(act, df, name=f"{tag}_dwd", tk=_tile(act.shape[1], 1408, LANE), tn=D_MODEL)
    da = mm_nt(df, wd, out_dtype=BF16, name=f"{tag}_da")
    dgu = swiglu_bwd(da, gg, uu, name=f"{tag}_swiglu_bwd")
    dwgu = mm_tn(h, dgu, name=f"{tag}_dwgu", n_blocks=N_DEV)
    dh = mm_nt_blocks(dgu, wgu3, out_dtype=F32, name=f"{tag}_dh")
    dx_in, dshift, dscale, dg = norm_modulate_bwd(x_in, g, scale, dh, dx_out, name=f"{tag}_norm_bwd")
    return dx_in, (dshift, dscale, dgate), dg, dwgu, dwd.reshape(N_DEV, -1, D_MODEL)


def kernel(x, c, w_ada, b_ada, g_ffn1, w_ffn1_gu, w_ffn1_down, g_mix, w_in, pool_mix, pool_scale, w_pool_up, q_gain, k_gain, sinks, rel_bias, w_attn_up, w_o, g_ffn2, w_ffn2_gu, w_ffn2_down, loss_target, m_w_ada, m_b_ada, m_g_ffn1, m_w_ffn1_gu, m_w_ffn1_down, m_g_mix, m_w_in, m_pool_mix, m_pool_scale, m_w_pool_up, m_q_gain, m_k_gain, m_sinks, m_rel_bias, m_w_attn_up, m_w_o, m_g_ffn2, m_w_ffn2_gu, m_w_ffn2_down, v_w_ada, v_b_ada, v_g_ffn1, v_w_ffn1_gu, v_w_ffn1_down, v_g_mix, v_w_in, v_pool_mix, v_pool_scale, v_w_pool_up, v_q_gain, v_k_gain, v_sinks, v_rel_bias, v_w_attn_up, v_w_o, v_g_ffn2, v_w_ffn2_gu, v_w_ffn2_down):
    me = _slot(_mesh_pos())
    x0, target = x[0], loss_target[0]
    n_ada = w_ada.shape[2]
    pm_rows = pool_mix.shape[2]

    big = dict(w_ffn1_gu=(w_ffn1_gu, m_w_ffn1_gu, v_w_ffn1_gu), w_ffn1_down=(w_ffn1_down, m_w_ffn1_down, v_w_ffn1_down),
               w_in=(w_in, m_w_in, v_w_in), pool_mix=(pool_mix, m_pool_mix, v_pool_mix),
               w_pool_up=(w_pool_up, m_w_pool_up, v_w_pool_up), w_attn_up=(w_attn_up, m_w_attn_up, v_w_attn_up),
               w_o=(w_o, m_w_o, v_w_o), w_ffn2_gu=(w_ffn2_gu, m_w_ffn2_gu, v_w_ffn2_gu),
               w_ffn2_down=(w_ffn2_down, m_w_ffn2_down, v_w_ffn2_down))
    shard2d = {k: (POOL_GROUPS * pm_rows, POOL_GROUP_W) if k == "pool_mix" else t[0].shape[1:] for k, t in big.items()}
    names = list(big)
    gathered = dict(zip(names, all_gather_hbm([big[k][0].reshape(shard2d[k]).astype(BF16) for k in names], "gather_weights")))
    wgu1, wgu2 = gathered["w_ffn1_gu"], gathered["w_ffn2_gu"]
    wd1 = gathered["w_ffn1_down"].reshape(-1, D_MODEL)
    wd2 = gathered["w_ffn2_down"].reshape(-1, D_MODEL)
    w_in_full = jnp.transpose(gathered["w_in"], (1, 0, 2)).reshape(D_MODEL, IN_W)
    pm_full = jnp.transpose(gathered["pool_mix"].reshape(N_DEV, POOL_GROUPS, pm_rows, POOL_GROUP_W),
                            (1, 0, 2, 3)).reshape(POOL_GROUPS, POOL_GROUP_W, POOL_GROUP_W)
    wpu3, wau3 = gathered["w_pool_up"], gathered["w_attn_up"]
    wo_full = gathered["w_o"].reshape(D_MODEL, D_MODEL)

    c_all = all_gather_small(c.reshape(D_MODEL // LANE, LANE), "gather_c").reshape(N_DEV, D_MODEL)
    c16 = jnp.pad(c_all, ((0, 16 - N_DEV), (0, 0)))
    b_mine = lax.dynamic_slice(b_ada, (0, me * n_ada), (1, n_ada))
    mod_cols = ada_fwd(c16, w_ada[0], b_mine, name="ada_fwd")[:N_DEV]
    mod_all = all_gather_small(mod_cols.reshape(-1, LANE), "gather_mod").reshape(N_DEV, N_DEV, n_ada)
    mod = lax.dynamic_index_in_dim(mod_all, me, axis=1, keepdims=False).reshape(N_MOD, 1, D_MODEL)

    x1, saved1 = _ffn_fwd(x0, g_ffn1, mod[0], mod[1], mod[2], wgu1, wd1, "ffn1")
    h2 = norm_modulate(x1, g_mix, mod[3], mod[4], name="mix_norm")
    z = mm_nn(h2, w_in_full, out_dtype=F32, name="mix_in", tn=640)
    pooled, p_act = pool_fwd(z, pm_full, pool_scale, name="pool_fwd")
    y_pool = mm_nn(p_act, wpu3, out_dtype=F32, name="pool_up")
    one_hot = jnp.asarray(_bucket_one_hot())
    bias = bias_table(rel_bias.T, one_hot, name="bias_table").reshape(N_HEADS, BLK, 2 * BLK)
    qg2, kg2 = jnp.tile(q_gain, (1, 2)), jnp.tile(k_gain, (1, 2))
    attn = attn_fwd(z, qg2, kg2, sinks[0], bias, name="attn_fwd")
    y_attn = mm_nn(attn, wau3, out_dtype=F32, name="attn_up")
    merged = merge_fwd(z, y_pool, y_attn, name="merge_fwd")
    x2, o_act = mm_nn_residual(merged, wo_full, x1, mod[5], 1.0, name="mix_out")
    y, saved2 = _ffn_fwd(x2, g_ffn2, mod[6], mod[7], mod[8], wgu2, wd2, "ffn2")
    dy, loss_row = loss_head(y, target, name="loss_head")
    loss = lax.psum(loss_row[0, 0], ("x", "y", "c"))

    dx2, dmod3, dg_ffn2, dwgu2, dwd2 = _ffn_bwd(dy, x2, g_ffn2, mod[7], mod[8], wgu2, wd2, saved2, "ffn2")
    d_o, dgate2 = gate_bwd(dx2, o_act, mod[5], 1.0, name="mix_gate_bwd")
    dwo = mm_tn(merged, d_o, name="mix_dwo", tn=D_MODEL).reshape(N_DEV, -1, D_MODEL)
    dmerged = mm_nt(d_o, wo_full, out_dtype=F32, name="mix_dmerged")
    dyp, dya, dga, dgb = merge_bwd(dmerged, z, y_pool, y_attn, name="merge_bwd")
    dwpu = mm_tn(p_act, dyp, name="pool_dwup", n_blocks=N_DEV, tk=POOL_W)
    dp_act = mm_nt_blocks(dyp, wpu3, out_dtype=F32, name="pool_dp")
    du, dpool_scale, dpm = pool_bwd(dp_act, pooled, pm_full, pool_scale, name="pool_bwd")
    dwau = mm_tn(attn, dya, name="attn_dwup", n_blocks=N_DEV, tk=ATT_W)
    dattn = mm_nt_blocks(dya, wau3, out_dtype=BF16, name="attn_dout")
    dq, dk, dv, dqg, dkg, dsinks, dbias = attn_bwd(z, dattn, qg2, kg2, sinks[0], bias, name="attn_bwd")
    drel = bias_table_bwd(dbias.reshape(N_HEADS, -1), one_hot, name="bias_table_bwd").T
    dz = jnp.concatenate([du, dq, dk, dv, dga, dgb], axis=1)
    dwin = mm_tn(h2, dz, name="mix_dwin", tn=640)
    dh2 = mm_nt(dz, w_in_full, out_dtype=F32, name="mix_dh")
    dx1, dsh2, dsc2, dg_mix = norm_modulate_bwd(x1, g_mix, mod[4], dh2, dx2, name="mix_norm_bwd")
    dx0, dmod1, dg_ffn1, dwgu1, dwd1 = _ffn_bwd(dx1, x0, g_ffn1, mod[1], mod[2], wgu1, wd1, saved1, "ffn1")

    grads = dict(w_ffn1_gu=dwgu1, w_ffn1_down=dwd1,
                 w_in=jnp.transpose(dwin.reshape(D_MODEL, N_DEV, IN_W // N_DEV), (1, 0, 2)),
                 pool_mix=jnp.transpose(dpm.astype(BF16).reshape(POOL_GROUPS, N_DEV, pm_rows, POOL_GROUP_W),
                                        (1, 0, 2, 3)).reshape(N_DEV, POOL_GROUPS * pm_rows, POOL_GROUP_W),
                 w_pool_up=dwpu, w_attn_up=dwau, w_o=dwo, w_ffn2_gu=dwgu2, w_ffn2_down=dwd2)
    parts = dict(zip(names, all_to_all_hbm([grads[k] for k in names], "scatter_grads")))

    small = [("b_ada", b_ada, m_b_ada, v_b_ada, jnp.concatenate(list(dmod1 + (dsh2, dsc2, dgate2) + dmod3), axis=1)),
             ("g_ffn1", g_ffn1, m_g_ffn1, v_g_ffn1, dg_ffn1), ("g_mix", g_mix, m_g_mix, v_g_mix, dg_mix),
             ("g_ffn2", g_ffn2, m_g_ffn2, v_g_ffn2, dg_ffn2),
             ("pool_scale", pool_scale, m_pool_scale, v_pool_scale, dpool_scale),
             ("q_gain", q_gain, m_q_gain, v_q_gain, dqg[:, :HEAD_DIM]), ("k_gain", k_gain, m_k_gain, v_k_gain, dkg[:, :HEAD_DIM]),
             ("sinks", sinks, m_sinks, v_sinks, dsinks), ("rel_bias", rel_bias, m_rel_bias, v_rel_bias, drel)]
    n_small = sum(t[1].size for t in small)
    pad = -n_small % (8 * LANE)
    flat = lambda arrs: jnp.pad(jnp.concatenate([a.reshape(1, -1) for a in arrs], axis=1), ((0, 0), (0, pad)))
    small_parts = all_gather_small(flat([t[4] for t in small]).reshape(-1, LANE), "gather_small_grads")
    small_parts = small_parts.reshape(N_DEV, 1, n_small + pad)
    sg, sd, sm, sv = adamw_from_parts(small_parts, flat([t[1] for t in small]), flat([t[2] for t in small]),
                                      flat([t[3] for t in small]), name="adamw_small")

    dmod_all = small_parts[:, 0, :N_MOD * D_MODEL]
    dmod_mine = lax.dynamic_slice(dmod_all, (0, me * n_ada), (N_DEV, n_ada))
    c_t = jnp.pad(c_all.T, ((0, 0), (0, LANE - N_DEV)))
    ada_out = ada_bwd_adamw(c_t, jnp.pad(dmod_mine, ((0, LANE - N_DEV), (0, 0))), w_ada[0], m_w_ada[0], v_w_ada[0],
                            name="ada_bwd_adamw")

    res = {"w_ada": [o[None] for o in ada_out]}
    for k in names:
        w_, m_, v_ = big[k]
        outs = adamw_from_parts(parts[k], w_.reshape(shard2d[k]), m_.reshape(shard2d[k]), v_.reshape(shard2d[k]),
                                name=f"adamw_{k}")
        res[k] = [o.reshape(w_.shape) for o in outs]
    off = 0
    for k, w_, _, _, _ in small:
        res[k] = [o[0, off:off + w_.size].reshape(w_.shape) for o in (sg, sd, sm, sv)]
        off += w_.size
    order = ["w_ada", "b_ada", "g_ffn1", "w_ffn1_gu", "w_ffn1_down", "g_mix", "w_in", "pool_mix", "pool_scale",
             "w_pool_up", "q_gain", "k_gain", "sinks", "rel_bias", "w_attn_up", "w_o", "g_ffn2", "w_ffn2_gu", "w_ffn2_down"]
    return (loss, dx0[None], *[res[k][0] for k in order], *[res[k][1] for k in order],
            *[res[k][2] for k in order], *[res[k][3] for k in order])
```

```python
import functools

import numpy as np
import jax
import jax.numpy as jnp
from jax import lax
from jax.experimental import pallas as pl
from jax.experimental.pallas import tpu as pltpu

F32, BF16 = jnp.float32, jnp.bfloat16
MESH_ID = pl.DeviceIdType.MESH

N_DEV = 8
D_MODEL = 2048
N_MOD = 9
POOL_WINDOWS = (2, 4, 8, 16)
POOL_GROUPS = 4
POOL_GROUP_W = D_MODEL // 8
POOL_W = POOL_GROUPS * POOL_GROUP_W
POOL_HALO = 16
HEAD_DIM = 64
N_HEADS = 16
N_KV = 2
GQA = N_HEADS // N_KV
BLK = 128
NUM_BUCKETS = 32
MAX_EXACT = 16
REL_MAX_DIST = 128
EPS = 1e-6
NEG_INF = -1e30
ATT_W = N_HEADS * HEAD_DIM
KV_W = N_KV * HEAD_DIM
IN_W = POOL_W + ATT_W + 2 * KV_W + 2 * D_MODEL
COL_Q, COL_K, COL_V, COL_GA, COL_GB = POOL_W, POOL_W + ATT_W, POOL_W + ATT_W + KV_W, POOL_W + ATT_W + 2 * KV_W, POOL_W + ATT_W + 2 * KV_W + D_MODEL
LANE = 128

ADAM_LR = 0.001
ADAM_B1 = 0.9
ADAM_B2 = 0.999
ADAM_EPS = 1e-08
ADAM_WD = 0.01
ADAM_STEP = 10

NN = ((1,), (0,))
NT = ((1,), (1,))
TN = ((0,), (0,))


def _dot(a, b, dims, precision=None):
    return lax.dot_general(a, b, (dims, ((), ())), preferred_element_type=F32, precision=precision)


def _tile(n, pref, unit):
    t = (min(pref, n) // unit) * unit
    while t >= unit:
        if n % t == 0:
            return t
        t -= unit
    return n


def _params(*sem):
    return pltpu.CompilerParams(dimension_semantics=sem)


def _sigmoid(x):
    return 1.0 / (1.0 + jnp.exp(-x))


def _mesh_pos():
    return lax.axis_index("x"), lax.axis_index("y"), lax.axis_index("c")


def _slot(p):
    return 4 * p[0] + 2 * p[1] + p[2]


def all_gather_small(x_shard, name):
    m_per, n = x_shard.shape

    def body(x_ref, out_ref, send_sems, recv_sems, local_sem):
        x, y, c = _mesh_pos()
        me, sibling = (x, y, c), (x, y, 1 - c)
        chips = [(1 - x, y), (x, 1 - y), (1 - x, 1 - y)]

        def rows(p):
            return out_ref.at[pl.ds(_slot(p) * m_per, m_per), :]

        def copy(k, block, to, src=None):
            return pltpu.make_async_remote_copy(
                src_ref=rows(block) if src is None else src, dst_ref=rows(block),
                send_sem=send_sems.at[k], recv_sem=recv_sems.at[k], device_id=to, device_id_type=MESH_ID)

        mine = pltpu.make_async_copy(x_ref, rows(me), local_sem)
        mine.start()
        first = [copy(0, me, sibling, src=x_ref)]
        first += [copy(1 + j, me, (*chip, c), src=x_ref) for j, chip in enumerate(chips)]
        for cp in first:
            cp.start()
        passed = [copy(4 + j, (*chip, c), sibling) for j, chip in enumerate(chips)]
        for j, chip in enumerate(chips):
            copy(1 + j, (*chip, c), me).wait_recv()
            passed[j].start()
        copy(0, sibling, me).wait_recv()
        for j, chip in enumerate(chips):
            copy(4 + j, (*chip, 1 - c), me).wait_recv()
        for cp in first + passed:
            cp.wait_send()
        mine.wait()

    return pl.pallas_call(
        body, name=name,
        out_shape=jax.ShapeDtypeStruct((N_DEV * m_per, n), x_shard.dtype),
        in_specs=[pl.BlockSpec(memory_space=pltpu.VMEM)],
        out_specs=pl.BlockSpec(memory_space=pltpu.VMEM),
        scratch_shapes=[pltpu.SemaphoreType.DMA((7,)), pltpu.SemaphoreType.DMA((7,)), pltpu.SemaphoreType.DMA],
    )(x_shard)


def all_gather_hbm(shards, name):
    n_arr = len(shards)

    def body(*refs):
        ins, outs, token = refs[:n_arr], refs[n_arr:2 * n_arr], refs[2 * n_arr]
        send_sems, recv_sems, local_sems = refs[2 * n_arr + 1:]
        x, y, c = _mesh_pos()
        me, sibling = (x, y, c), (x, y, 1 - c)
        chips = [(1 - x, y), (x, 1 - y), (1 - x, 1 - y)]
        token[...] = jnp.zeros_like(token)

        def copy(a, k, block, to, src=None):
            dst = outs[a].at[_slot(block)]
            return pltpu.make_async_remote_copy(
                src_ref=dst if src is None else src, dst_ref=dst,
                send_sem=send_sems.at[7 * a + k], recv_sem=recv_sems.at[7 * a + k],
                device_id=to, device_id_type=MESH_ID)

        mine = [pltpu.make_async_copy(ins[a], outs[a].at[_slot(me)], local_sems.at[a]) for a in range(n_arr)]
        for cp in mine:
            cp.start()
        first = []
        for a in range(n_arr):
            first.append(copy(a, 0, me, sibling, src=ins[a]))
            first += [copy(a, 1 + j, me, (*chip, c), src=ins[a]) for j, chip in enumerate(chips)]
        for cp in first:
            cp.start()
        passed = []
        for a in range(n_arr):
            for j, chip in enumerate(chips):
                copy(a, 1 + j, (*chip, c), me).wait_recv()
                fwd = copy(a, 4 + j, (*chip, c), sibling)
                fwd.start()
                passed.append(fwd)
        for a in range(n_arr):
            copy(a, 0, sibling, me).wait_recv()
            for j, chip in enumerate(chips):
                copy(a, 4 + j, (*chip, 1 - c), me).wait_recv()
        for cp in first + passed:
            cp.wait_send()
        for cp in mine:
            cp.wait()

    any_spec = pl.BlockSpec(memory_space=pl.ANY)
    return pl.pallas_call(
        body, name=name,
        out_shape=[jax.ShapeDtypeStruct((N_DEV,) + s.shape, s.dtype) for s in shards] + [jax.ShapeDtypeStruct((8, LANE), F32)],
        in_specs=[any_spec] * n_arr, out_specs=[any_spec] * n_arr + [pl.BlockSpec(memory_space=pltpu.VMEM)],
        scratch_shapes=[pltpu.SemaphoreType.DMA((7 * n_arr,)), pltpu.SemaphoreType.DMA((7 * n_arr,)),
                        pltpu.SemaphoreType.DMA((n_arr,))],
    )(*shards)


def _peer_list(x, y, c):
    return [((1 - x) if k & 4 else x, (1 - y) if k & 2 else y, (1 - c) if k & 1 else c) for k in range(1, N_DEV)]


def _exchange_copies(srcs, lands, send_sems, recv_sems, slotted, arriving):
    x, y, c = _mesh_pos()
    me = _slot((x, y, c))
    copies = []
    for a in range(len(srcs)):
        for k, peer in enumerate(_peer_list(x, y, c)):
            copies.append(pltpu.make_async_remote_copy(
                src_ref=srcs[a].at[_slot(peer)] if slotted else srcs[a],
                dst_ref=lands[a].at[_slot(peer) if arriving else me],
                send_sem=send_sems.at[7 * a + k], recv_sem=recv_sems.at[7 * a + k],
                device_id=peer, device_id_type=MESH_ID))
    return copies


HBM_SPEC = pl.BlockSpec(memory_space=pltpu.HBM)
SEM_SPEC = pl.BlockSpec(memory_space=pltpu.SEMAPHORE)
DATAFLOW = pltpu.SideEffectType.DATAFLOW_SIDE_EFFECTING


def exchange_start(srcs, lands, *, slotted, name):
    n = len(srcs)

    def body(*refs):
        ins = refs[:2 * n]
        send_sems, recv_sems = refs[2 * n], refs[2 * n + 1]
        token = refs[-1]
        for cp in _exchange_copies(ins[:n], ins[n:], send_sems, recv_sems, slotted, False):
            cp.start()
        token[...] = jnp.zeros_like(token)

    operands = [pltpu.with_memory_space_constraint(v, pltpu.HBM) for v in list(srcs) + list(lands)]
    out = pl.pallas_call(
        body, name=name,
        out_shape=(pltpu.SemaphoreType.DMA((7 * n,)), pltpu.SemaphoreType.DMA((7 * n,)),
                   *[pltpu.HBM(v.shape, v.dtype) for v in operands], jax.ShapeDtypeStruct((8, LANE), F32)),
        in_specs=[HBM_SPEC] * (2 * n),
        out_specs=(SEM_SPEC, SEM_SPEC, *[HBM_SPEC] * (2 * n), pl.BlockSpec(memory_space=pltpu.VMEM)),
        input_output_aliases={i: 2 + i for i in range(2 * n)},
        compiler_params=pltpu.CompilerParams(has_side_effects=DATAFLOW),
    )(*operands)
    return out[0], out[1], list(out[2:2 + n]), list(out[2 + n:2 + 2 * n]), out[-1]


def exchange_wait(handle, after, *, slotted, name):
    send_sems, recv_sems, srcs, lands, _ = handle
    n = len(srcs)

    def body(*refs):
        ins = refs[:2 * n]
        for cp in _exchange_copies(ins[:n], ins[n:], refs[2 * n], refs[2 * n + 1], slotted, True):
            cp.wait_send()
            cp.wait_recv()

    out = pl.pallas_call(
        body, name=name,
        out_shape=tuple(pltpu.HBM(v.shape, v.dtype) for v in srcs + lands),
        in_specs=[HBM_SPEC] * (2 * n) + [SEM_SPEC, SEM_SPEC, pl.BlockSpec(memory_space=pl.ANY)],
        out_specs=[HBM_SPEC] * (2 * n),
        input_output_aliases={i: i for i in range(2 * n)},
        compiler_params=pltpu.CompilerParams(has_side_effects=DATAFLOW),
    )(*srcs, *lands, send_sems, recv_sems, after)
    return list(out[n:])


def mm_nn(a, w, *, out_dtype, name, tm=512, tn=512):
    m, k = a.shape
    tm = _tile(m, tm, 16)
    if w.ndim == 3:
        tn = w.shape[2]
        n = w.shape[0] * tn
        w_spec = pl.BlockSpec((None, k, tn), lambda j, i: (j, 0, 0))
    else:
        n = w.shape[1]
        tn = _tile(n, tn, LANE)
        w_spec = pl.BlockSpec((k, tn), lambda j, i: (0, j))

    def body(a_ref, w_ref, o_ref):
        o_ref[...] = _dot(a_ref[...], w_ref[...], NN).astype(o_ref.dtype)

    return pl.pallas_call(
        body, name=name, grid=(n // tn, m // tm),
        in_specs=[pl.BlockSpec((tm, k), lambda j, i: (i, 0)), w_spec],
        out_specs=pl.BlockSpec((tm, tn), lambda j, i: (i, j)),
        out_shape=jax.ShapeDtypeStruct((m, n), out_dtype),
        compiler_params=_params("parallel", "parallel"),
    )(a, w)


def mm_nn_residual(a, w, x_in, gate, coef, *, name, tm=512, tn=512):
    m, k = a.shape
    n = w.shape[1]
    tm, tn = _tile(m, tm, 16), _tile(n, tn, LANE)

    def body(a_ref, w_ref, x_ref, g_ref, o_ref, f_ref):
        f = _dot(a_ref[...], w_ref[...], NN)
        f_ref[...] = f.astype(BF16)
        o_ref[...] = x_ref[...] + (coef * g_ref[...]) * f

    return pl.pallas_call(
        body, name=name, grid=(n // tn, m // tm),
        in_specs=[pl.BlockSpec((tm, k), lambda j, i: (i, 0)), pl.BlockSpec((k, tn), lambda j, i: (0, j)),
                  pl.BlockSpec((tm, tn), lambda j, i: (i, j)), pl.BlockSpec((1, tn), lambda j, i: (0, j))],
        out_specs=[pl.BlockSpec((tm, tn), lambda j, i: (i, j)), pl.BlockSpec((tm, tn), lambda j, i: (i, j))],
        out_shape=[jax.ShapeDtypeStruct((m, n), F32), jax.ShapeDtypeStruct((m, n), BF16)],
        compiler_params=_params("parallel", "parallel"),
    )(a, w, x_in, gate)


TOKEN_SPEC = pl.BlockSpec((8, LANE), lambda *_: (0, 0))


def mm_nt(a, w, token, *, out_dtype, name, tm=512, tn=512):
    m, k = a.shape
    n = w.shape[0]
    tm, tn = _tile(m, tm, 16), _tile(n, tn, LANE)

    def body(a_ref, w_ref, token_ref, o_ref):
        o_ref[...] = _dot(a_ref[...], w_ref[...], NT).astype(o_ref.dtype)

    return pl.pallas_call(
        body, name=name, grid=(n // tn, m // tm),
        in_specs=[pl.BlockSpec((tm, k), lambda j, i: (i, 0)), pl.BlockSpec((tn, k), lambda j, i: (j, 0)), TOKEN_SPEC],
        out_specs=pl.BlockSpec((tm, tn), lambda j, i: (i, j)),
        out_shape=jax.ShapeDtypeStruct((m, n), out_dtype),
        compiler_params=_params("parallel", "parallel"),
    )(a, w, token)


def mm_nt_blocks(a, w3, token, *, out_dtype, name, tm=1024):
    m = a.shape[0]
    n_blk, n, tn = w3.shape
    tm = _tile(m, tm, 16)

    def body(a_ref, w_ref, token_ref, o_ref, acc_ref):
        j = pl.program_id(1)
        p = _dot(a_ref[...], w_ref[...], NT)

        @pl.when(j == 0)
        def _():
            acc_ref[...] = p

        @pl.when(j > 0)
        def _():
            acc_ref[...] += p

        @pl.when(j == n_blk - 1)
        def _():
            o_ref[...] = acc_ref[...].astype(o_ref.dtype)

    return pl.pallas_call(
        body, name=name, grid=(m // tm, n_blk),
        in_specs=[pl.BlockSpec((tm, tn), lambda i, j: (i, j)), pl.BlockSpec((None, n, tn), lambda i, j: (j, 0, 0)),
                  TOKEN_SPEC],
        out_specs=pl.BlockSpec((tm, n), lambda i, j: (i, 0)),
        out_shape=jax.ShapeDtypeStruct((m, n), out_dtype),
        scratch_shapes=[pltpu.VMEM((tm, n), F32)],
        compiler_params=_params("parallel", "arbitrary"),
    )(a, w3, token)


def mm_tn(a, dy, *, name, n_blocks=None, tk=2048, tn=1408, ts=512):
    s, k = a.shape
    n = dy.shape[1]
    tk, ts = _tile(k, tk, LANE), _tile(s, ts, 16)
    if n_blocks is None:
        tn = _tile(n, tn, LANE)
        out_spec = pl.BlockSpec((tk, tn), lambda kk, j, t: (kk, j))
        out_shape = jax.ShapeDtypeStruct((k, n), BF16)
    else:
        tn = n // n_blocks
        out_spec = pl.BlockSpec((None, tk, tn), lambda kk, j, t: (j, kk, 0))
        out_shape = jax.ShapeDtypeStruct((n_blocks, k, tn), BF16)
    n_steps = s // ts

    def body(a_ref, dy_ref, o_ref, acc_ref):
        t = pl.program_id(2)
        p = _dot(a_ref[...], dy_ref[...], TN)

        @pl.when(t == 0)
        def _():
            acc_ref[...] = p

        @pl.when(t > 0)
        def _():
            acc_ref[...] += p

        @pl.when(t == n_steps - 1)
        def _():
            o_ref[...] = acc_ref[...].astype(BF16)

    return pl.pallas_call(
        body, name=name, grid=(k // tk, n // tn, n_steps),
        in_specs=[pl.BlockSpec((ts, tk), lambda kk, j, t: (t, kk)), pl.BlockSpec((ts, tn), lambda kk, j, t: (t, j))],
        out_specs=out_spec, out_shape=out_shape,
        scratch_shapes=[pltpu.VMEM((tk, tn), F32)],
        compiler_params=_params("parallel", "parallel", "arbitrary"),
    )(a, dy)


def ffn_up(h, wgu3, *, name, tm=512):
    s, k = h.shape
    n = wgu3.shape[2]
    half = wgu3.shape[0] // 2
    tm = _tile(s, tm, 16)

    def body(h_ref, wg_ref, wu_ref, g_ref, u_ref, a_ref):
        hh = h_ref[...]
        g = _dot(hh, wg_ref[...], NN)
        u = _dot(hh, wu_ref[...], NN)
        g_ref[...] = g.astype(BF16)
        u_ref[...] = u.astype(BF16)
        a_ref[...] = (g * _sigmoid(g) * u).astype(BF16)

    out = jax.ShapeDtypeStruct((s, half * n), BF16)
    blk = pl.BlockSpec((tm, n), lambda j, i: (i, j))
    return pl.pallas_call(
        body, name=name, grid=(half, s // tm),
        in_specs=[pl.BlockSpec((tm, k), lambda j, i: (i, 0)),
                  pl.BlockSpec((None, k, n), lambda j, i: (j, 0, 0)),
                  pl.BlockSpec((None, k, n), lambda j, i: (j + half, 0, 0))],
        out_specs=[blk, blk, blk], out_shape=[out, out, out],
        compiler_params=_params("parallel", "parallel"),
    )(h, wgu3, wgu3)


def _row_spec(ts, width, col=0):
    return pl.BlockSpec((ts, width), lambda i: (i, col))


def _vec_spec(width):
    return pl.BlockSpec((1, width), lambda i: (0, 0))


def _accumulate(ref, value):
    i = pl.program_id(0)

    @pl.when(i == 0)
    def _():
        ref[...] = value

    @pl.when(i > 0)
    def _():
        ref[...] += value


def norm_modulate(x, g, shift, scale, token, *, name, ts=256):
    s, d = x.shape
    ts = _tile(s, ts, 16)

    def body(x_ref, g_ref, sh_ref, sc_ref, token_ref, h_ref):
        xx = x_ref[...]
        r = lax.rsqrt(jnp.mean(xx * xx, axis=-1, keepdims=True) + EPS)
        h_ref[...] = ((xx * r) * g_ref[...] * (1 + sc_ref[...]) + sh_ref[...]).astype(BF16)

    return pl.pallas_call(
        body, name=name, grid=(s // ts,),
        in_specs=[_row_spec(ts, d), _vec_spec(d), _vec_spec(d), _vec_spec(d), TOKEN_SPEC],
        out_specs=_row_spec(ts, d), out_shape=jax.ShapeDtypeStruct((s, d), BF16),
        compiler_params=_params("parallel"),
    )(x, g, shift, scale, token)


def norm_modulate_bwd(x, g, scale, dh, dx_out, *, name, ts=256):
    s, d = x.shape
    ts = _tile(s, ts, 16)

    def body(x_ref, g_ref, sc_ref, dh_ref, dxo_ref, dx_ref, dsh_ref, dsc_ref, dg_ref):
        xx, dh_ = x_ref[...], dh_ref[...]
        r = lax.rsqrt(jnp.mean(xx * xx, axis=-1, keepdims=True) + EPS)
        xh = xx * r
        dn = dh_ * (1 + sc_ref[...])
        dxh = dn * g_ref[...]
        dx_ref[...] = dxo_ref[...] + r * (dxh - xh * jnp.mean(dxh * xh, axis=-1, keepdims=True))
        _accumulate(dsh_ref, jnp.sum(dh_, axis=0, keepdims=True))
        _accumulate(dsc_ref, jnp.sum(dh_ * (xh * g_ref[...]), axis=0, keepdims=True))
        _accumulate(dg_ref, jnp.sum(dn * xh, axis=0, keepdims=True))

    vec = jax.ShapeDtypeStruct((1, d), F32)
    return pl.pallas_call(
        body, name=name, grid=(s // ts,),
        in_specs=[_row_spec(ts, d), _vec_spec(d), _vec_spec(d), _row_spec(ts, d), _row_spec(ts, d)],
        out_specs=[_row_spec(ts, d), _vec_spec(d), _vec_spec(d), _vec_spec(d)],
        out_shape=[jax.ShapeDtypeStruct((s, d), F32), vec, vec, vec],
        compiler_params=_params("arbitrary"),
    )(x, g, scale, dh, dx_out)


def gate_bwd(dx_out, f, gate, coef, *, name, ts=256):
    s, d = dx_out.shape
    ts = _tile(s, ts, 16)

    def body(dx_ref, f_ref, g_ref, df_ref, dg_ref):
        dx = dx_ref[...]
        df_ref[...] = ((coef * g_ref[...]) * dx).astype(BF16)
        _accumulate(dg_ref, coef * jnp.sum(dx * f_ref[...].astype(F32), axis=0, keepdims=True))

    return pl.pallas_call(
        body, name=name, grid=(s // ts,),
        in_specs=[_row_spec(ts, d), _row_spec(ts, d), _vec_spec(d)],
        out_specs=[_row_spec(ts, d), _vec_spec(d)],
        out_shape=[jax.ShapeDtypeStruct((s, d), BF16), jax.ShapeDtypeStruct((1, d), F32)],
        compiler_params=_params("arbitrary"),
    )(dx_out, f, gate)


def swiglu_bwd(da, g, u, *, name, ts=256):
    s, f = da.shape
    ts = _tile(s, ts, 16)

    def body(da_ref, g_ref, u_ref, o_ref):
        da_, gg, uu = da_ref[...].astype(F32), g_ref[...].astype(F32), u_ref[...].astype(F32)
        sg = _sigmoid(gg)
        o_ref[:, :f] = (da_ * uu * (sg * (1 + gg * (1 - sg)))).astype(BF16)
        o_ref[:, f:] = (da_ * (gg * sg)).astype(BF16)

    return pl.pallas_call(
        body, name=name, grid=(s // ts,),
        in_specs=[_row_spec(ts, f)] * 3,
        out_specs=_row_spec(ts, 2 * f), out_shape=jax.ShapeDtypeStruct((s, 2 * f), BF16),
        compiler_params=_params("parallel"),
    )(da, g, u)


def loss_head(y, target, *, name, ts=256):
    s, d = y.shape
    ts = _tile(s, ts, 16)

    def body(y_ref, t_ref, dy_ref, l_ref):
        err = y_ref[...] - t_ref[...]
        dy_ref[...] = err * (1.0 / d)
        part = jnp.sum(jnp.sum(err * err, axis=0, keepdims=True), axis=1, keepdims=True) * (0.5 / d)
        _accumulate(l_ref, jnp.broadcast_to(part, (1, LANE)))

    return pl.pallas_call(
        body, name=name, grid=(s // ts,),
        in_specs=[_row_spec(ts, d), _row_spec(ts, d)],
        out_specs=[_row_spec(ts, d), _vec_spec(LANE)],
        out_shape=[jax.ShapeDtypeStruct((s, d), F32), jax.ShapeDtypeStruct((1, LANE), F32)],
        compiler_params=_params("arbitrary"),
    )(y, target)


def merge_fwd(z, y_pool, y_attn, *, name, ts=256, tc=256):
    s, d = y_pool.shape
    ts = _tile(s, ts, 16)

    def body(ga_ref, gb_ref, yp_ref, ya_ref, o_ref):
        o_ref[...] = (_sigmoid(ga_ref[...]) * yp_ref[...] + _sigmoid(gb_ref[...]) * ya_ref[...]).astype(BF16)

    blk = pl.BlockSpec((ts, tc), lambda i, j: (i, j))
    return pl.pallas_call(
        body, name=name, grid=(s // ts, d // tc),
        in_specs=[pl.BlockSpec((ts, tc), lambda i, j: (i, COL_GA // tc + j)),
                  pl.BlockSpec((ts, tc), lambda i, j: (i, COL_GB // tc + j)), blk, blk],
        out_specs=blk, out_shape=jax.ShapeDtypeStruct((s, d), BF16),
        compiler_params=_params("parallel", "parallel"),
    )(z, z, y_pool, y_attn)


def merge_bwd(dmerged, z, y_pool, y_attn, *, name, ts=256, tc=256):
    s, d = y_pool.shape
    ts = _tile(s, ts, 16)

    def body(dm_ref, ga_ref, gb_ref, yp_ref, ya_ref, dyp_ref, dya_ref, dga_ref, dgb_ref):
        dm = dm_ref[...]
        sa, sb = _sigmoid(ga_ref[...]), _sigmoid(gb_ref[...])
        dyp_ref[...] = (dm * sa).astype(BF16)
        dya_ref[...] = (dm * sb).astype(BF16)
        dga_ref[...] = (dm * yp_ref[...] * (sa * (1 - sa))).astype(BF16)
        dgb_ref[...] = (dm * ya_ref[...] * (sb * (1 - sb))).astype(BF16)

    blk = pl.BlockSpec((ts, tc), lambda i, j: (i, j))
    out = jax.ShapeDtypeStruct((s, d), BF16)
    return pl.pallas_call(
        body, name=name, grid=(s // ts, d // tc),
        in_specs=[blk, pl.BlockSpec((ts, tc), lambda i, j: (i, COL_GA // tc + j)),
                  pl.BlockSpec((ts, tc), lambda i, j: (i, COL_GB // tc + j)), blk, blk],
        out_specs=[blk] * 4, out_shape=[out] * 4,
        compiler_params=_params("parallel", "parallel"),
    )(dmerged, z, z, y_pool, y_attn)


def _window_counts(t0, rows):
    t1 = (t0 + 1 + lax.broadcasted_iota(jnp.int32, (rows, 1), 0)).astype(F32)
    return [jnp.minimum(t1, float(w)) for w in POOL_WINDOWS]


def pool_fwd(z, pool_mix, pool_scale, *, name, ts=256):
    s = z.shape[0]
    ts = _tile(s, ts, 16)
    per = ts // POOL_HALO

    def body(u_ref, halo_ref, pm_ref, ps_ref, pooled_ref, p_ref):
        i = pl.program_id(0)
        u = u_ref[...]
        halo = jnp.where(i > 0, halo_ref[...], 0.0)
        run = jnp.concatenate([halo, u], axis=0)
        sums, width = [], 1
        for w in POOL_WINDOWS:
            while width < w:
                run = run + pltpu.roll(run, width, 0)
                width *= 2
            sums.append(run[POOL_HALO:])
        counts = _window_counts(i * ts, ts)
        for gi in range(POOL_GROUPS):
            cols = slice(gi * POOL_GROUP_W, (gi + 1) * POOL_GROUP_W)
            pooled = (sums[gi][:, cols] / counts[gi] - u[:, cols]).astype(BF16)
            pooled_ref[:, cols] = pooled
            p_ref[:, cols] = (_dot(pooled, pm_ref[gi], NN) * ps_ref[:, cols]).astype(BF16)

    out = jax.ShapeDtypeStruct((s, POOL_W), BF16)
    return pl.pallas_call(
        body, name=name, grid=(s // ts,),
        in_specs=[_row_spec(ts, POOL_W),
                  pl.BlockSpec((POOL_HALO, POOL_W), lambda i: (jnp.maximum(i * per - 1, 0), 0)),
                  pl.BlockSpec((POOL_GROUPS, POOL_GROUP_W, POOL_GROUP_W), lambda i: (0, 0, 0)),
                  _vec_spec(POOL_W)],
        out_specs=[_row_spec(ts, POOL_W)] * 2, out_shape=[out, out],
        compiler_params=_params("parallel"),
    )(z, z, pool_mix, pool_scale)


def pool_bwd(dp, pooled, pool_mix, pool_scale, *, name, ts=256):
    s = dp.shape[0]
    ts = _tile(s, ts, 16)
    per = ts // POOL_HALO
    n_steps = s // ts
    rows = ts + POOL_HALO

    def body(dp_ref, halo_ref, pooled_ref, pm_ref, ps_ref, du_ref, dps_ref, dpm_ref):
        i = pl.program_id(0)
        dp_main = dp_ref[...]
        halo = jnp.where(i < n_steps - 1, halo_ref[...], 0.0)
        dmixed = jnp.concatenate([dp_main, halo], axis=0) * ps_ref[...]
        counts = _window_counts(i * ts, rows)
        dps_parts = []
        for gi, w in enumerate(POOL_WINDOWS):
            cols = slice(gi * POOL_GROUP_W, (gi + 1) * POOL_GROUP_W)
            dmx = dmixed[:, cols].astype(BF16)
            pooled = pooled_ref[:, cols]
            mixed = _dot(pooled, pm_ref[gi], NN)
            dps_parts.append(jnp.sum(dp_main[:, cols] * mixed, axis=0, keepdims=True))
            dpm_g = _dot(pooled, dmx[:ts], TN)

            @pl.when(i == 0)
            def _():
                dpm_ref[gi] = dpm_g

            @pl.when(i > 0)
            def _():
                dpm_ref[gi] += dpm_g

            dpooled = _dot(dmx, pm_ref[gi], NT)
            run, width = dpooled / counts[gi], 1
            while width < w:
                run = run + pltpu.roll(run, rows - width, 0)
                width *= 2
            du_ref[:, cols] = (run[:ts] - dpooled[:ts]).astype(BF16)
        _accumulate(dps_ref, jnp.concatenate(dps_parts, axis=1))

    return pl.pallas_call(
        body, name=name, grid=(n_steps,),
        in_specs=[_row_spec(ts, POOL_W),
                  pl.BlockSpec((POOL_HALO, POOL_W), lambda i: (jnp.minimum((i + 1) * per, s // POOL_HALO - 1), 0)),
                  _row_spec(ts, POOL_W),
                  pl.BlockSpec((POOL_GROUPS, POOL_GROUP_W, POOL_GROUP_W), lambda i: (0, 0, 0)),
                  _vec_spec(POOL_W)],
        out_specs=[_row_spec(ts, POOL_W), _vec_spec(POOL_W),
                   pl.BlockSpec((POOL_GROUPS, POOL_GROUP_W, POOL_GROUP_W), lambda i: (0, 0, 0))],
        out_shape=[jax.ShapeDtypeStruct((s, POOL_W), BF16), jax.ShapeDtypeStruct((1, POOL_W), F32),
                   jax.ShapeDtypeStruct((POOL_GROUPS, POOL_GROUP_W, POOL_GROUP_W), F32)],
        compiler_params=_params("arbitrary"),
    )(dp, dp, pooled, pool_mix, pool_scale)


def _bucket_one_hot():
    ql = np.arange(BLK)[:, None]
    j = np.arange(2 * BLK)[None, :]
    n = np.clip(BLK + ql - j, 0, None)
    nf = np.maximum(n, 1).astype(np.float32)
    large = MAX_EXACT + (np.log(nf / MAX_EXACT) / np.log(REL_MAX_DIST / MAX_EXACT)
                         * (NUM_BUCKETS - MAX_EXACT)).astype(np.int32)
    large = np.minimum(large, NUM_BUCKETS - 1)
    bucket = np.where(n < MAX_EXACT, n, large).astype(np.int32).reshape(-1)
    return (np.arange(NUM_BUCKETS)[:, None] == bucket[None, :]).astype(np.float32)


def bias_table(rel_bias_t, one_hot, *, name, tc=4096):
    n = one_hot.shape[1]

    def body(rb_ref, oh_ref, o_ref):
        o_ref[...] = _dot(rb_ref[...], oh_ref[...], NN, precision=lax.Precision.HIGHEST)

    return pl.pallas_call(
        body, name=name, grid=(n // tc,),
        in_specs=[pl.BlockSpec((N_HEADS, NUM_BUCKETS), lambda i: (0, 0)), pl.BlockSpec((NUM_BUCKETS, tc), lambda i: (0, i))],
        out_specs=pl.BlockSpec((N_HEADS, tc), lambda i: (0, i)),
        out_shape=jax.ShapeDtypeStruct((N_HEADS, n), F32),
        compiler_params=_params("parallel"),
    )(rel_bias_t, one_hot)


def bias_table_bwd(dbias, one_hot, *, name, tc=4096):
    n = one_hot.shape[1]

    def body(db_ref, oh_ref, o_ref):
        _accumulate(o_ref, _dot(db_ref[...], oh_ref[...], NT, precision=lax.Precision.HIGHEST))

    return pl.pallas_call(
        body, name=name, grid=(n // tc,),
        in_specs=[pl.BlockSpec((N_HEADS, tc), lambda i: (0, i)), pl.BlockSpec((NUM_BUCKETS, tc), lambda i: (0, i))],
        out_specs=pl.BlockSpec((N_HEADS, NUM_BUCKETS), lambda i: (0, 0)),
        out_shape=jax.ShapeDtypeStruct((N_HEADS, NUM_BUCKETS), F32),
        compiler_params=_params("arbitrary"),
    )(dbias, one_hot)


def _lane_half(shape):
    return lax.broadcasted_iota(jnp.int32, shape, len(shape) - 1) < HEAD_DIM


def _half_sums(v, first):
    s0 = jnp.sum(jnp.where(first, v, 0.0), axis=-1, keepdims=True)
    s1 = jnp.sum(jnp.where(first, 0.0, v), axis=-1, keepdims=True)
    return jnp.where(first, s0, s1)


def _band_mask(n):
    ql = lax.broadcasted_iota(jnp.int32, (BLK, 2 * BLK), 0)
    j = lax.broadcasted_iota(jnp.int32, (BLK, 2 * BLK), 1)
    return (j > ql) & (j <= ql + BLK) & ((j >= BLK) | (n > 0))


def _norm_keys(kband, kg2):
    first = _lane_half(kband.shape)
    r = lax.rsqrt(_half_sums(kband * kband, first) * (1.0 / HEAD_DIM) + EPS)
    return kband * r, r


def _head_setup(q_ref, pair, sub, kv, qg2):
    qp = q_ref[:, pair * LANE:(pair + 1) * LANE]
    if sub != kv:
        qp = pltpu.roll(qp, HEAD_DIM, 1)
    lanes = _lane_half(qp.shape) if kv == 0 else jnp.logical_not(_lane_half(qp.shape))
    qa = jnp.where(lanes, qp, 0.0)
    r = lax.rsqrt(jnp.sum(qa * qa, axis=-1, keepdims=True) * (1.0 / HEAD_DIM) + EPS)
    return qa * r, r, lanes


def _softmax_with_sink(logits, sink):
    m = jnp.maximum(jnp.max(logits, axis=-1, keepdims=True), sink)
    e = jnp.exp(logits - m)
    es = jnp.exp(sink - m)
    den = jnp.sum(e, axis=-1, keepdims=True) + es
    return e / den, es / den


def _attn_specs(nb, last):
    cur = lambda n: jnp.minimum(n, last)
    prev = lambda n: jnp.minimum(jnp.maximum(n - 1, 0), last)
    return [pl.BlockSpec((BLK, ATT_W), lambda n: (cur(n), COL_Q // ATT_W)),
            pl.BlockSpec((BLK, KV_W), lambda n: (prev(n), COL_K // KV_W)),
            pl.BlockSpec((BLK, KV_W), lambda n: (cur(n), COL_K // KV_W)),
            pl.BlockSpec((BLK, KV_W), lambda n: (prev(n), COL_V // KV_W)),
            pl.BlockSpec((BLK, KV_W), lambda n: (cur(n), COL_V // KV_W))]


def attn_fwd(z, qg2, kg2, sinks, bias, *, name):
    s = z.shape[0]
    nb = s // BLK

    def body(q_ref, kp_ref, kc_ref, vp_ref, vc_ref, qg_ref, kg_ref, sink_ref, bias_ref, o_ref):
        n = pl.program_id(0)
        mask = _band_mask(n)
        kn, _ = _norm_keys(jnp.concatenate([kp_ref[...], kc_ref[...]], axis=0), kg_ref[...])
        kn = (kn * kg_ref[...]).astype(BF16)
        vb = jnp.concatenate([vp_ref[...], vc_ref[...]], axis=0).astype(BF16)
        for pair in range(N_HEADS // 2):
            kv = (2 * pair) // GQA
            acc = None
            for sub in range(2):
                h = 2 * pair + sub
                xh, _, lanes = _head_setup(q_ref, pair, sub, kv, qg_ref[...])
                qn = (xh * qg_ref[...]).astype(BF16)
                logits = _dot(qn, kn, NT) * (HEAD_DIM ** -0.5) + bias_ref[h]
                p, _ = _softmax_with_sink(jnp.where(mask, logits, NEG_INF), sink_ref[h])
                out = jnp.where(lanes, _dot(p.astype(BF16), vb, NN), 0.0)
                if sub != kv:
                    out = pltpu.roll(out, HEAD_DIM, 1)
                acc = out if acc is None else acc + out
            o_ref[:, pair * LANE:(pair + 1) * LANE] = acc.astype(BF16)

    return pl.pallas_call(
        body, name=name, grid=(nb,),
        in_specs=_attn_specs(nb, nb - 1) + [
            _vec_spec(LANE), _vec_spec(LANE), pl.BlockSpec(memory_space=pltpu.SMEM),
            pl.BlockSpec((N_HEADS, BLK, 2 * BLK), lambda n: (0, 0, 0))],
        out_specs=pl.BlockSpec((BLK, ATT_W), lambda n: (n, 0)),
        out_shape=jax.ShapeDtypeStruct((s, ATT_W), BF16),
        compiler_params=_params("parallel"),
    )(z, z, z, z, z, qg2, kg2, sinks, bias)


def attn_bwd(z, d_out, qg2, kg2, sinks, bias, *, name):
    s = z.shape[0]
    nb = s // BLK
    scale = HEAD_DIM ** -0.5

    def body(q_ref, kp_ref, kc_ref, vp_ref, vc_ref, do_ref, qg_ref, kg_ref, sink_ref, bias_ref,
             dq_ref, dk_ref, dv_ref, dqg_ref, dkg_ref, dsink_ref, dbias_ref, band_k, band_v, carry_k, carry_v):
        n = pl.program_id(0)

        @pl.when(n == 0)
        def _():
            dqg_ref[...] = jnp.zeros_like(dqg_ref)
            dkg_ref[...] = jnp.zeros_like(dkg_ref)
            dsink_ref[...] = jnp.zeros_like(dsink_ref)
            dbias_ref[...] = jnp.zeros_like(dbias_ref)
            carry_k[...] = jnp.zeros_like(carry_k)
            carry_v[...] = jnp.zeros_like(carry_v)

        @pl.when(n == nb)
        def _():
            band_k[...] = jnp.zeros_like(band_k)
            band_v[...] = jnp.zeros_like(band_v)

        @pl.when(n < nb)
        def _():
            mask = _band_mask(n)
            kx, _ = _norm_keys(jnp.concatenate([kp_ref[...], kc_ref[...]], axis=0), kg_ref[...])
            kn = (kx * kg_ref[...]).astype(BF16)
            vb = jnp.concatenate([vp_ref[...], vc_ref[...]], axis=0).astype(BF16)
            lane16 = lax.broadcasted_iota(jnp.int32, (1, N_HEADS), 1)
            dkn = jnp.zeros((2 * BLK, KV_W), F32)
            dvb = jnp.zeros((2 * BLK, KV_W), F32)
            dqg = jnp.zeros((1, LANE), F32)
            dsink = jnp.zeros((1, N_HEADS), F32)
            for pair in range(N_HEADS // 2):
                kv = (2 * pair) // GQA
                acc = None
                for sub in range(2):
                    h = 2 * pair + sub
                    xh, r, lanes = _head_setup(q_ref, pair, sub, kv, qg_ref[...])
                    qn = (xh * qg_ref[...]).astype(BF16)
                    logits = _dot(qn, kn, NT) * scale + bias_ref[h]
                    p, p_sink = _softmax_with_sink(jnp.where(mask, logits, NEG_INF), sink_ref[h])
                    do = do_ref[:, pair * LANE:(pair + 1) * LANE].astype(F32)
                    if sub != kv:
                        do = pltpu.roll(do, HEAD_DIM, 1)
                    do = jnp.where(lanes, do, 0.0).astype(BF16)
                    dp = _dot(do, vb, NT)
                    delta = jnp.sum(p * dp, axis=-1, keepdims=True)
                    ds = p * (dp - delta)
                    dsink = dsink + jnp.where(lane16 == h, -jnp.sum(p_sink * delta, axis=0, keepdims=True), 0.0)
                    dbias_ref[h] += ds
                    ds16 = ds.astype(BF16)
                    dqn = jnp.where(lanes, _dot(ds16, kn, NN) * scale, 0.0)
                    dkn = dkn + _dot(ds16, qn, TN) * scale
                    dvb = dvb + _dot(p.astype(BF16), do, TN)
                    dqg = dqg + jnp.sum(dqn * xh, axis=0, keepdims=True)
                    dxh = dqn * qg_ref[...]
                    dq = r * (dxh - xh * (jnp.sum(dxh * xh, axis=-1, keepdims=True) * (1.0 / HEAD_DIM)))
                    if sub != kv:
                        dq = pltpu.roll(dq, HEAD_DIM, 1)
                    acc = dq if acc is None else acc + dq
                dq_ref[:, pair * LANE:(pair + 1) * LANE] = acc.astype(BF16)
            band_k[...] = dkn
            band_v[...] = dvb
            dqg_ref[...] += dqg
            dsink_ref[...] += dsink

        dkn_prev = carry_k[...] + band_k[:BLK]
        dv_ref[...] = (carry_v[...] + band_v[:BLK]).astype(BF16)
        carry_k[...] = band_k[BLK:]
        carry_v[...] = band_v[BLK:]
        kp = kp_ref[...]
        first = _lane_half(kp.shape)
        r = lax.rsqrt(_half_sums(kp * kp, first) * (1.0 / HEAD_DIM) + EPS)
        xh = kp * r
        dkg_ref[...] += jnp.sum(dkn_prev * xh, axis=0, keepdims=True)
        dxh = dkn_prev * kg_ref[...]
        dk_ref[...] = (r * (dxh - xh * (_half_sums(dxh * xh, first) * (1.0 / HEAD_DIM)))).astype(BF16)

        @pl.when(n == nb)
        def _():
            dqg_ref[...] += pltpu.roll(dqg_ref[...], HEAD_DIM, 1)
            dkg_ref[...] += pltpu.roll(dkg_ref[...], HEAD_DIM, 1)

    last = nb - 1
    cur = lambda n: jnp.minimum(n, last)
    back = lambda n: jnp.maximum(n - 1, 0)
    full3 = pl.BlockSpec((N_HEADS, BLK, 2 * BLK), lambda n: (0, 0, 0))
    return pl.pallas_call(
        body, name=name, grid=(nb + 1,),
        in_specs=_attn_specs(nb, last) + [
            pl.BlockSpec((BLK, ATT_W), lambda n: (cur(n), 0)),
            _vec_spec(LANE), _vec_spec(LANE), pl.BlockSpec(memory_space=pltpu.SMEM), full3],
        out_specs=[pl.BlockSpec((BLK, ATT_W), lambda n: (cur(n), 0)),
                   pl.BlockSpec((BLK, KV_W), lambda n: (back(n), 0)),
                   pl.BlockSpec((BLK, KV_W), lambda n: (back(n), 0)),
                   _vec_spec(LANE), _vec_spec(LANE), _vec_spec(N_HEADS), full3],
        out_shape=[jax.ShapeDtypeStruct((s, ATT_W), BF16), jax.ShapeDtypeStruct((s, KV_W), BF16),
                   jax.ShapeDtypeStruct((s, KV_W), BF16), jax.ShapeDtypeStruct((1, LANE), F32),
                   jax.ShapeDtypeStruct((1, LANE), F32), jax.ShapeDtypeStruct((1, N_HEADS), F32),
                   jax.ShapeDtypeStruct((N_HEADS, BLK, 2 * BLK), F32)],
        scratch_shapes=[pltpu.VMEM((2 * BLK, KV_W), F32), pltpu.VMEM((2 * BLK, KV_W), F32),
                        pltpu.VMEM((BLK, KV_W), F32), pltpu.VMEM((BLK, KV_W), F32)],
        compiler_params=_params("arbitrary"),
    )(z, z, z, z, z, d_out, qg2, kg2, sinks, bias)


def _adamw(w, g, m, v):
    m = ADAM_B1 * m + (1.0 - ADAM_B1) * g
    v = ADAM_B2 * v + (1.0 - ADAM_B2) * (g * g)
    m_hat = m / (1.0 - ADAM_B1 ** ADAM_STEP)
    v_hat = v / (1.0 - ADAM_B2 ** ADAM_STEP)
    delta = -ADAM_LR * (m_hat / (jnp.sqrt(v_hat) + ADAM_EPS) + ADAM_WD * w)
    return delta, m, v


def ada_fwd(c16, w, b, *, name, tn=768):
    k, n = w.shape
    tn = _tile(n, tn, LANE)

    def body(c_ref, w_ref, b_ref, o_ref):
        cc = c_ref[...]
        o_ref[...] = _dot((cc * _sigmoid(cc)).astype(BF16), w_ref[...].astype(BF16), NN) + b_ref[...]

    return pl.pallas_call(
        body, name=name, grid=(n // tn,),
        in_specs=[pl.BlockSpec((c16.shape[0], k), lambda j: (0, 0)), pl.BlockSpec((k, tn), lambda j: (0, j)),
                  pl.BlockSpec((1, tn), lambda j: (0, j))],
        out_specs=pl.BlockSpec((c16.shape[0], tn), lambda j: (0, j)),
        out_shape=jax.ShapeDtypeStruct((c16.shape[0], n), F32),
        compiler_params=_params("parallel"),
    )(c16, w, b)


def ada_bwd_adamw(c_t, dmod, w, m, v, *, name, tn=256):
    k, n = w.shape
    tn = _tile(n, tn, LANE)

    def body(c_ref, d_ref, w_ref, m_ref, v_ref, g_ref, dl_ref, mo_ref, vo_ref):
        cc = c_ref[...]
        g = _dot((cc * _sigmoid(cc)).astype(BF16), d_ref[...].astype(BF16), NN)
        g_ref[...] = g
        dl_ref[...], mo_ref[...], vo_ref[...] = _adamw(w_ref[...], g, m_ref[...], v_ref[...])

    blk = pl.BlockSpec((k, tn), lambda j: (0, j))
    out = jax.ShapeDtypeStruct((k, n), F32)
    return pl.pallas_call(
        body, name=name, grid=(n // tn,),
        in_specs=[pl.BlockSpec((k, LANE), lambda j: (0, 0)), pl.BlockSpec((LANE, tn), lambda j: (0, j)), blk, blk, blk],
        out_specs=[blk] * 4, out_shape=[out] * 4,
        compiler_params=_params("parallel"),
    )(c_t, dmod, w, m, v)


def adamw_from_parts(parts, w, m, v, *, name):
    r, c = w.shape
    tr = _tile(r, max(16, (256 * 1024) // c), 16)

    def body(p_ref, w_ref, m_ref, v_ref, g_ref, dl_ref, mo_ref, vo_ref):
        g = p_ref[0].astype(F32)
        for d in range(1, N_DEV):
            g = g + p_ref[d].astype(F32)
        g_ref[...] = g
        dl_ref[...], mo_ref[...], vo_ref[...] = _adamw(w_ref[...], g, m_ref[...], v_ref[...])

    blk = pl.BlockSpec((tr, c), lambda i: (i, 0))
    out = jax.ShapeDtypeStruct((r, c), F32)
    return pl.pallas_call(
        body, name=name, grid=(r // tr,),
        in_specs=[pl.BlockSpec((N_DEV, tr, c), lambda i: (0, i, 0)), blk, blk, blk],
        out_specs=[blk] * 4, out_shape=[out] * 4,
        compiler_params=_params("parallel"),
    )(parts, w, m, v)


def _ffn_fwd(x_in, g, shift, scale, gate, wgu3, get_wd, token, tag):
    h = norm_modulate(x_in, g, shift, scale, token, name=f"{tag}_norm")
    gg, uu, act = ffn_up(h, wgu3, name=f"{tag}_up")
    wd = get_wd(gg)
    x_out, f = mm_nn_residual(act, wd, x_in, gate, 0.5, name=f"{tag}_down")
    return x_out, (h, gg, uu, act, f), wd


def _ffn_bwd(dx_out, x_in, g, scale, gate, wgu3, wd, saved, token, scatter, split, tag):
    h, gg, uu, act, f = saved
    df, dgate = gate_bwd(dx_out, f, gate, 0.5, name=f"{tag}_gate_bwd")
    dwd = mm_tn(act, df, name=f"{tag}_dwd", tk=_tile(act.shape[1], 1408, LANE), tn=D_MODEL).reshape(N_DEV, -1, D_MODEL)
    if split:
        token = scatter([f"w_{tag}_down"], [dwd], f"scatter_{tag}_down")
    da = mm_nt(df, wd, token, out_dtype=BF16, name=f"{tag}_da")
    dgu = swiglu_bwd(da, gg, uu, name=f"{tag}_swiglu_bwd")
    dwgu = mm_tn(h, dgu, name=f"{tag}_dwgu", n_blocks=N_DEV)
    if split:
        token = scatter([f"w_{tag}_gu"], [dwgu], f"scatter_{tag}_gu")
    else:
        token = scatter([f"w_{tag}_gu", f"w_{tag}_down"], [dwgu, dwd], f"scatter_{tag}")
    dh = mm_nt_blocks(dgu, wgu3, token, out_dtype=F32, name=f"{tag}_dh")
    dx_in, dshift, dscale, dg = norm_modulate_bwd(x_in, g, scale, dh, dx_out, name=f"{tag}_norm_bwd")
    return dx_in, (dshift, dscale, dgate), dg


def kernel(x, c, w_ada, b_ada, g_ffn1, w_ffn1_gu, w_ffn1_down, g_mix, w_in, pool_mix, pool_scale, w_pool_up, q_gain, k_gain, sinks, rel_bias, w_attn_up, w_o, g_ffn2, w_ffn2_gu, w_ffn2_down, loss_target, m_w_ada, m_b_ada, m_g_ffn1, m_w_ffn1_gu, m_w_ffn1_down, m_g_mix, m_w_in, m_pool_mix, m_pool_scale, m_w_pool_up, m_q_gain, m_k_gain, m_sinks, m_rel_bias, m_w_attn_up, m_w_o, m_g_ffn2, m_w_ffn2_gu, m_w_ffn2_down, v_w_ada, v_b_ada, v_g_ffn1, v_w_ffn1_gu, v_w_ffn1_down, v_g_mix, v_w_in, v_pool_mix, v_pool_scale, v_w_pool_up, v_q_gain, v_k_gain, v_sinks, v_rel_bias, v_w_attn_up, v_w_o, v_g_ffn2, v_w_ffn2_gu, v_w_ffn2_down):
    me = _slot(_mesh_pos())
    x0, target = x[0], loss_target[0]
    n_ada = w_ada.shape[2]
    pm_rows = pool_mix.shape[2]

    big = dict(w_ffn1_gu=(w_ffn1_gu, m_w_ffn1_gu, v_w_ffn1_gu), w_ffn1_down=(w_ffn1_down, m_w_ffn1_down, v_w_ffn1_down),
               w_in=(w_in, m_w_in, v_w_in), pool_mix=(pool_mix, m_pool_mix, v_pool_mix),
               w_pool_up=(w_pool_up, m_w_pool_up, v_w_pool_up), w_attn_up=(w_attn_up, m_w_attn_up, v_w_attn_up),
               w_o=(w_o, m_w_o, v_w_o), w_ffn2_gu=(w_ffn2_gu, m_w_ffn2_gu, v_w_ffn2_gu),
               w_ffn2_down=(w_ffn2_down, m_w_ffn2_down, v_w_ffn2_down))
    shard2d = {k: (POOL_GROUPS * pm_rows, POOL_GROUP_W) if k == "pool_mix" else t[0].shape[1:] for k, t in big.items()}
    mix_keys = ["w_in", "pool_mix", "w_pool_up", "w_attn_up", "w_o"]
    ffn2_keys = ["w_ffn2_gu", "w_ffn2_down"]

    def shard_bf16(k, token=None):
        w = big[k][0].reshape(shard2d[k])
        return (w if token is None else w + token[0, 0]).astype(BF16)

    def own_slot(block, shape):
        return lax.dynamic_update_slice(lax.empty(shape, block.dtype), block[None], (me, 0, 0))

    def start_gather(keys, token, tag):
        shards = [shard_bf16(k, token) for k in keys]
        return exchange_start(shards, [own_slot(s, (N_DEV,) + s.shape) for s in shards], slotted=False, name=f"{tag}_start")

    pending = []

    def scatter(keys, grads, tag):
        lands = [own_slot(lax.dynamic_index_in_dim(g, me, 0, keepdims=False), g.shape) for g in grads]
        handle = exchange_start(grads, lands, slotted=True, name=f"{tag}_start")
        pending.append((keys, handle, tag))
        return handle[4]

    wgu1, token = all_gather_hbm([shard_bf16("w_ffn1_gu")], "gather_ffn1_gu")
    gather_wd1 = start_gather(["w_ffn1_down"], token, "gather_ffn1_down")
    gather_mix = start_gather(mix_keys, gather_wd1[4], "gather_mix")
    gather_ffn2 = start_gather(ffn2_keys, gather_mix[4], "gather_ffn2")
    token = gather_ffn2[4]

    c_all = all_gather_small(c.reshape(D_MODEL // LANE, LANE), "gather_c").reshape(N_DEV, D_MODEL)
    c16 = jnp.pad(c_all, ((0, 16 - N_DEV), (0, 0)))
    b_mine = lax.dynamic_slice(b_ada, (0, me * n_ada), (1, n_ada))
    mod_cols = ada_fwd(c16, w_ada[0], b_mine, name="ada_fwd")[:N_DEV]
    mod_all = all_gather_small(mod_cols.reshape(-1, LANE), "gather_mod").reshape(N_DEV, N_DEV, n_ada)
    mod = lax.dynamic_index_in_dim(mod_all, me, axis=1, keepdims=False).reshape(N_MOD, 1, D_MODEL)

    def get_wd1(after):
        return exchange_wait(gather_wd1, after, slotted=False, name="gather_ffn1_down_wait")[0].reshape(-1, D_MODEL)

    x1, saved1, wd1 = _ffn_fwd(x0, g_ffn1, mod[0], mod[1], mod[2], wgu1, get_wd1, token, "ffn1")
    gathered = dict(zip(mix_keys, exchange_wait(gather_mix, x1, slotted=False, name="gather_mix_wait")))
    w_in_full = jnp.transpose(gathered["w_in"], (1, 0, 2)).reshape(D_MODEL, IN_W)
    pm_full = jnp.transpose(gathered["pool_mix"].reshape(N_DEV, POOL_GROUPS, pm_rows, POOL_GROUP_W),
                            (1, 0, 2, 3)).reshape(POOL_GROUPS, POOL_GROUP_W, POOL_GROUP_W)
    wpu3, wau3 = gathered["w_pool_up"], gathered["w_attn_up"]
    wo_full = gathered["w_o"].reshape(D_MODEL, D_MODEL)
    h2 = norm_modulate(x1, g_mix, mod[3], mod[4], token, name="mix_norm")
    z = mm_nn(h2, w_in_full, out_dtype=F32, name="mix_in", tn=640)
    pooled, p_act = pool_fwd(z, pm_full, pool_scale, name="pool_fwd")
    y_pool = mm_nn(p_act, wpu3, out_dtype=F32, name="pool_up")
    one_hot = jnp.asarray(_bucket_one_hot())
    bias = bias_table(rel_bias.T, one_hot, name="bias_table").reshape(N_HEADS, BLK, 2 * BLK)
    qg2, kg2 = jnp.tile(q_gain, (1, 2)), jnp.tile(k_gain, (1, 2))
    attn = attn_fwd(z, qg2, kg2, sinks[0], bias, name="attn_fwd")
    y_attn = mm_nn(attn, wau3, out_dtype=F32, name="attn_up")
    merged = merge_fwd(z, y_pool, y_attn, name="merge_fwd")
    x2, o_act = mm_nn_residual(merged, wo_full, x1, mod[5], 1.0, name="mix_out")
    wgu2, wd2 = exchange_wait(gather_ffn2, x2, slotted=False, name="gather_ffn2_wait")
    wd2 = wd2.reshape(-1, D_MODEL)
    y, saved2, _ = _ffn_fwd(x2, g_ffn2, mod[6], mod[7], mod[8], wgu2, lambda after: wd2, token, "ffn2")
    dy, loss_row = loss_head(y, target, name="loss_head")
    loss = lax.psum(loss_row[0, 0], ("x", "y", "c"))

    dx2, dmod3, dg_ffn2 = _ffn_bwd(dy, x2, g_ffn2, mod[7], mod[8], wgu2, wd2, saved2, token, scatter, False, "ffn2")
    d_o, dgate2 = gate_bwd(dx2, o_act, mod[5], 1.0, name="mix_gate_bwd")
    dwo = mm_tn(merged, d_o, name="mix_dwo", tk=1024, tn=D_MODEL).reshape(N_DEV, -1, D_MODEL)
    dmerged = mm_nt(d_o, wo_full, token, out_dtype=F32, name="mix_dmerged")
    dyp, dya, dga, dgb = merge_bwd(dmerged, z, y_pool, y_attn, name="merge_bwd")
    dwpu = mm_tn(p_act, dyp, name="pool_dwup", n_blocks=N_DEV, tk=POOL_W)
    dp_act = mm_nt_blocks(dyp, wpu3, token, out_dtype=F32, name="pool_dp")
    du, dpool_scale, dpm = pool_bwd(dp_act, pooled, pm_full, pool_scale, name="pool_bwd")
    dwau = mm_tn(attn, dya, name="attn_dwup", n_blocks=N_DEV, tk=ATT_W)
    dattn = mm_nt_blocks(dya, wau3, token, out_dtype=BF16, name="attn_dout")
    dq, dk, dv, dqg, dkg, dsinks, dbias = attn_bwd(z, dattn, qg2, kg2, sinks[0], bias, name="attn_bwd")
    drel = bias_table_bwd(dbias.reshape(N_HEADS, -1), one_hot, name="bias_table_bwd").T
    dz = jnp.concatenate([du, dq, dk, dv, dga, dgb], axis=1)
    dwin = mm_tn(h2, dz, name="mix_dwin", tn=640)
    mix_grads = dict(w_in=jnp.transpose(dwin.reshape(D_MODEL, N_DEV, IN_W // N_DEV), (1, 0, 2)),
                     pool_mix=jnp.transpose(dpm.astype(BF16).reshape(POOL_GROUPS, N_DEV, pm_rows, POOL_GROUP_W),
                                            (1, 0, 2, 3)).reshape(N_DEV, POOL_GROUPS * pm_rows, POOL_GROUP_W),
                     w_pool_up=dwpu, w_attn_up=dwau, w_o=dwo)
    token = scatter(mix_keys, [mix_grads[k] for k in mix_keys], "scatter_mix")
    dh2 = mm_nt(dz, w_in_full, token, out_dtype=F32, name="mix_dh")
    dx1, dsh2, dsc2, dg_mix = norm_modulate_bwd(x1, g_mix, mod[4], dh2, dx2, name="mix_norm_bwd")
    dx0, dmod1, dg_ffn1 = _ffn_bwd(dx1, x0, g_ffn1, mod[1], mod[2], wgu1, wd1, saved1, token, scatter, True, "ffn1")

    small = [("b_ada", b_ada, m_b_ada, v_b_ada, jnp.concatenate(list(dmod1 + (dsh2, dsc2, dgate2) + dmod3), axis=1)),
             ("g_ffn1", g_ffn1, m_g_ffn1, v_g_ffn1, dg_ffn1), ("g_mix", g_mix, m_g_mix, v_g_mix, dg_mix),
             ("g_ffn2", g_ffn2, m_g_ffn2, v_g_ffn2, dg_ffn2),
             ("pool_scale", pool_scale, m_pool_scale, v_pool_scale, dpool_scale),
             ("q_gain", q_gain, m_q_gain, v_q_gain, dqg[:, :HEAD_DIM]), ("k_gain", k_gain, m_k_gain, v_k_gain, dkg[:, :HEAD_DIM]),
             ("sinks", sinks, m_sinks, v_sinks, dsinks), ("rel_bias", rel_bias, m_rel_bias, v_rel_bias, drel)]
    n_small = sum(t[1].size for t in small)
    pad = -n_small % (8 * LANE)
    flat = lambda arrs: jnp.pad(jnp.concatenate([a.reshape(1, -1) for a in arrs], axis=1), ((0, 0), (0, pad)))
    small_parts = all_gather_small(flat([t[4] for t in small]).reshape(-1, LANE), "gather_small_grads")
    small_parts = small_parts.reshape(N_DEV, 1, n_small + pad)
    sg, sd, sm, sv = adamw_from_parts(small_parts, flat([t[1] for t in small]), flat([t[2] for t in small]),
                                      flat([t[3] for t in small]), name="adamw_small")

    dmod_all = small_parts[:, 0, :N_MOD * D_MODEL]
    dmod_mine = lax.dynamic_slice(dmod_all, (0, me * n_ada), (N_DEV, n_ada))
    c_t = jnp.pad(c_all.T, ((0, 0), (0, LANE - N_DEV)))
    ada_out = ada_bwd_adamw(c_t, jnp.pad(dmod_mine, ((0, LANE - N_DEV), (0, 0))), w_ada[0], m_w_ada[0], v_w_ada[0],
                            name="ada_bwd_adamw")

    res = {"w_ada": [o[None] for o in ada_out]}
    after = ada_out[0]
    for keys, handle, tag in pending:
        parts = exchange_wait(handle, after, slotted=True, name=f"{tag}_wait")
        for k, part in zip(keys, parts):
            w_, m_, v_ = big[k]
            outs = adamw_from_parts(part, w_.reshape(shard2d[k]), m_.reshape(shard2d[k]), v_.reshape(shard2d[k]),
                                    name=f"adamw_{k}")
            res[k] = [o.reshape(w_.shape) for o in outs]
            after = outs[0]
    off = 0
    for k, w_, _, _, _ in small:
        res[k] = [o[0, off:off + w_.size].reshape(w_.shape) for o in (sg, sd, sm, sv)]
        off += w_.size
    order = ["w_ada", "b_ada", "g_ffn1", "w_ffn1_gu", "w_ffn1_down", "g_mix", "w_in", "pool_mix", "pool_scale",
             "w_pool_up", "q_gain", "k_gain", "sinks", "rel_bias", "w_attn_up", "w_o", "g_ffn2", "w_ffn2_gu", "w_ffn2_down"]
    return (loss, dx0[None], *[res[k][0] for k in order], *[res[k][1] for k in order],
            *[res[k][2] for k in order], *[res[k][3] for k in order])
```

```python
import functools

import numpy as np
import jax
import jax.numpy as jnp
from jax import lax
from jax.experimental import pallas as pl
from jax.experimental.pallas import tpu as pltpu

F32, BF16 = jnp.float32, jnp.bfloat16
MESH_ID = pl.DeviceIdType.MESH

N_DEV = 8
D_MODEL = 2048
N_MOD = 9
POOL_WINDOWS = (2, 4, 8, 16)
POOL_GROUPS = 4
POOL_GROUP_W = D_MODEL // 8
POOL_W = POOL_GROUPS * POOL_GROUP_W
POOL_HALO = 16
HEAD_DIM = 64
N_HEADS = 16
N_KV = 2
GQA = N_HEADS // N_KV
BLK = 128
NUM_BUCKETS = 32
MAX_EXACT = 16
REL_MAX_DIST = 128
EPS = 1e-6
NEG_INF = -1e30
ATT_W = N_HEADS * HEAD_DIM
KV_W = N_KV * HEAD_DIM
IN_W = POOL_W + ATT_W + 2 * KV_W + 2 * D_MODEL
W_IN_PARTS = dict(u=(0, POOL_W), q=(POOL_W, ATT_W), k=(POOL_W + ATT_W, KV_W), v=(POOL_W + ATT_W + KV_W, KV_W),
                  ga=(POOL_W + ATT_W + 2 * KV_W, D_MODEL), gb=(POOL_W + ATT_W + 2 * KV_W + D_MODEL, D_MODEL))
Z_ORDER = ("ga", "gb", "u", "q", "k", "v")
COL_GA, COL_GB, COL_U, COL_Q, COL_K, COL_V = 0, D_MODEL, 2 * D_MODEL, 2 * D_MODEL + POOL_W, 2 * D_MODEL + POOL_W + ATT_W, 2 * D_MODEL + POOL_W + ATT_W + KV_W
LANE = 128

ADAM_LR = 0.001
ADAM_B1 = 0.9
ADAM_B2 = 0.999
ADAM_EPS = 1e-08
ADAM_WD = 0.01
ADAM_STEP = 10

NN = ((1,), (0,))
NT = ((1,), (1,))
TN = ((0,), (0,))


def _dot(a, b, dims, precision=None):
    return lax.dot_general(a, b, (dims, ((), ())), preferred_element_type=F32, precision=precision)


def _tile(n, pref, unit):
    t = (min(pref, n) // unit) * unit
    while t >= unit:
        if n % t == 0:
            return t
        t -= unit
    return n


def _params(*sem):
    return pltpu.CompilerParams(dimension_semantics=sem)


def _sigmoid(x):
    return 1.0 / (1.0 + jnp.exp(-x))


def _mesh_pos():
    return lax.axis_index("x"), lax.axis_index("y"), lax.axis_index("c")


def _slot(p):
    return 4 * p[0] + 2 * p[1] + p[2]


def all_gather_small(x_shard, name):
    m_per, n = x_shard.shape

    def body(x_ref, out_ref, token, send_sems, recv_sems, local_sem):
        x, y, c = _mesh_pos()
        me, sibling = (x, y, c), (x, y, 1 - c)
        chips = [(1 - x, y), (x, 1 - y), (1 - x, 1 - y)]
        token[...] = jnp.zeros_like(token)

        def rows(p):
            return out_ref.at[pl.ds(_slot(p) * m_per, m_per), :]

        def copy(k, block, to, src=None):
            return pltpu.make_async_remote_copy(
                src_ref=rows(block) if src is None else src, dst_ref=rows(block),
                send_sem=send_sems.at[k], recv_sem=recv_sems.at[k], device_id=to, device_id_type=MESH_ID)

        mine = pltpu.make_async_copy(x_ref, rows(me), local_sem)
        mine.start()
        first = [copy(0, me, sibling, src=x_ref)]
        first += [copy(1 + j, me, (*chip, c), src=x_ref) for j, chip in enumerate(chips)]
        for cp in first:
            cp.start()
        passed = [copy(4 + j, (*chip, c), sibling) for j, chip in enumerate(chips)]
        for j, chip in enumerate(chips):
            copy(1 + j, (*chip, c), me).wait_recv()
            passed[j].start()
        copy(0, sibling, me).wait_recv()
        for j, chip in enumerate(chips):
            copy(4 + j, (*chip, 1 - c), me).wait_recv()
        for cp in first + passed:
            cp.wait_send()
        mine.wait()

    return pl.pallas_call(
        body, name=name,
        out_shape=[jax.ShapeDtypeStruct((N_DEV * m_per, n), x_shard.dtype), jax.ShapeDtypeStruct((8, LANE), F32)],
        in_specs=[pl.BlockSpec(memory_space=pltpu.VMEM)],
        out_specs=[pl.BlockSpec(memory_space=pltpu.VMEM)] * 2,
        scratch_shapes=[pltpu.SemaphoreType.DMA((7,)), pltpu.SemaphoreType.DMA((7,)), pltpu.SemaphoreType.DMA],
    )(x_shard)


def all_gather_hbm(shards, name):
    n_arr = len(shards)

    def body(*refs):
        ins, outs, token = refs[:n_arr], refs[n_arr:2 * n_arr], refs[2 * n_arr]
        send_sems, recv_sems, local_sems = refs[2 * n_arr + 1:]
        x, y, c = _mesh_pos()
        me, sibling = (x, y, c), (x, y, 1 - c)
        chips = [(1 - x, y), (x, 1 - y), (1 - x, 1 - y)]
        token[...] = jnp.zeros_like(token)

        def copy(a, k, block, to, src=None):
            dst = outs[a].at[_slot(block)]
            return pltpu.make_async_remote_copy(
                src_ref=dst if src is None else src, dst_ref=dst,
                send_sem=send_sems.at[7 * a + k], recv_sem=recv_sems.at[7 * a + k],
                device_id=to, device_id_type=MESH_ID)

        mine = [pltpu.make_async_copy(ins[a], outs[a].at[_slot(me)], local_sems.at[a]) for a in range(n_arr)]
        for cp in mine:
            cp.start()
        first = []
        for a in range(n_arr):
            first.append(copy(a, 0, me, sibling, src=ins[a]))
            first += [copy(a, 1 + j, me, (*chip, c), src=ins[a]) for j, chip in enumerate(chips)]
        for cp in first:
            cp.start()
        passed = []
        for a in range(n_arr):
            for j, chip in enumerate(chips):
                copy(a, 1 + j, (*chip, c), me).wait_recv()
                fwd = copy(a, 4 + j, (*chip, c), sibling)
                fwd.start()
                passed.append(fwd)
        for a in range(n_arr):
            copy(a, 0, sibling, me).wait_recv()
            for j, chip in enumerate(chips):
                copy(a, 4 + j, (*chip, 1 - c), me).wait_recv()
        for cp in first + passed:
            cp.wait_send()
        for cp in mine:
            cp.wait()

    any_spec = pl.BlockSpec(memory_space=pl.ANY)
    return pl.pallas_call(
        body, name=name,
        out_shape=[jax.ShapeDtypeStruct((N_DEV,) + s.shape, s.dtype) for s in shards] + [jax.ShapeDtypeStruct((8, LANE), F32)],
        in_specs=[any_spec] * n_arr, out_specs=[any_spec] * n_arr + [pl.BlockSpec(memory_space=pltpu.VMEM)],
        scratch_shapes=[pltpu.SemaphoreType.DMA((7 * n_arr,)), pltpu.SemaphoreType.DMA((7 * n_arr,)),
                        pltpu.SemaphoreType.DMA((n_arr,))],
    )(*shards)


def _peer_list(x, y, c):
    return [((1 - x) if k & 4 else x, (1 - y) if k & 2 else y, (1 - c) if k & 1 else c) for k in range(1, N_DEV)]


def _exchange_copies(srcs, lands, send_sems, recv_sems, slotted, arriving):
    x, y, c = _mesh_pos()
    me = _slot((x, y, c))
    copies = []
    for a in range(len(srcs)):
        for k, peer in enumerate(_peer_list(x, y, c)):
            copies.append(pltpu.make_async_remote_copy(
                src_ref=srcs[a].at[_slot(peer)] if slotted else srcs[a],
                dst_ref=lands[a].at[_slot(peer) if arriving else me],
                send_sem=send_sems.at[7 * a + k], recv_sem=recv_sems.at[7 * a + k],
                device_id=peer, device_id_type=MESH_ID))
    return copies


HBM_SPEC = pl.BlockSpec(memory_space=pltpu.HBM)
SEM_SPEC = pl.BlockSpec(memory_space=pltpu.SEMAPHORE)
DATAFLOW = pltpu.SideEffectType.DATAFLOW_SIDE_EFFECTING


def exchange_start(srcs, lands, *, slotted, name):
    n = len(srcs)

    def body(*refs):
        ins = refs[:2 * n]
        send_sems, recv_sems = refs[2 * n], refs[2 * n + 1]
        token = refs[-1]
        for cp in _exchange_copies(ins[:n], ins[n:], send_sems, recv_sems, slotted, False):
            cp.start()
        token[...] = jnp.zeros_like(token)

    operands = [pltpu.with_memory_space_constraint(v, pltpu.HBM) for v in list(srcs) + list(lands)]
    out = pl.pallas_call(
        body, name=name,
        out_shape=(pltpu.SemaphoreType.DMA((7 * n,)), pltpu.SemaphoreType.DMA((7 * n,)),
                   *[pltpu.HBM(v.shape, v.dtype) for v in operands], jax.ShapeDtypeStruct((8, LANE), F32)),
        in_specs=[HBM_SPEC] * (2 * n),
        out_specs=(SEM_SPEC, SEM_SPEC, *[HBM_SPEC] * (2 * n), pl.BlockSpec(memory_space=pltpu.VMEM)),
        input_output_aliases={i: 2 + i for i in range(2 * n)},
        compiler_params=pltpu.CompilerParams(has_side_effects=DATAFLOW),
    )(*operands)
    return out[0], out[1], list(out[2:2 + n]), list(out[2 + n:2 + 2 * n]), out[-1]


def exchange_wait(handle, after, *, slotted, name):
    send_sems, recv_sems, srcs, lands, _ = handle
    n = len(srcs)

    def body(*refs):
        ins = refs[:2 * n]
        for cp in _exchange_copies(ins[:n], ins[n:], refs[2 * n], refs[2 * n + 1], slotted, True):
            cp.wait_send()
            cp.wait_recv()

    out = pl.pallas_call(
        body, name=name,
        out_shape=tuple(pltpu.HBM(v.shape, v.dtype) for v in srcs + lands),
        in_specs=[HBM_SPEC] * (2 * n) + [SEM_SPEC, SEM_SPEC, pl.BlockSpec(memory_space=pl.ANY)],
        out_specs=[HBM_SPEC] * (2 * n),
        input_output_aliases={i: i for i in range(2 * n)},
        compiler_params=pltpu.CompilerParams(has_side_effects=DATAFLOW),
    )(*srcs, *lands, send_sems, recv_sems, after)
    return list(out[n:])


VMEM_BLOCK_BUDGET = 46 * 2 ** 20
ROW_TILE, COL_TILE = 1024, 1408
ACC_BUDGET = 12 * 2 ** 20


def _mm_tiles(m, n, row_bytes, col_bytes, elem_bytes):
    tm, tn = _tile(m, ROW_TILE, 16), _tile(n, COL_TILE, LANE)
    while 2 * (tm * row_bytes + tn * col_bytes + tm * tn * elem_bytes) > VMEM_BLOCK_BUDGET:
        narrower = _tile(n, max(tn - LANE, LANE), LANE)
        if tn > 512 and narrower < tn:
            tn = narrower
        else:
            tm //= 2
    return tm, tn


def mm_nn(a, w, *, out_dtype, name):
    m, k = a.shape
    n = w.shape[1]
    tm, tn = _mm_tiles(m, n, 2 * k, 2 * k, jnp.dtype(out_dtype).itemsize)

    def body(a_ref, w_ref, o_ref):
        o_ref[...] = _dot(a_ref[...], w_ref[...], NN).astype(o_ref.dtype)

    return pl.pallas_call(
        body, name=name, grid=(n // tn, m // tm),
        in_specs=[pl.BlockSpec((tm, k), lambda j, i: (i, 0)), pl.BlockSpec((k, tn), lambda j, i: (0, j))],
        out_specs=pl.BlockSpec((tm, tn), lambda j, i: (i, j)),
        out_shape=jax.ShapeDtypeStruct((m, n), out_dtype),
        compiler_params=_params("parallel", "parallel"),
    )(a, w)


def mm_nn_residual(a, w, x_in, gate, coef, *, name):
    m, k = a.shape
    n = w.shape[1]
    tm, tn = _mm_tiles(m, n, 2 * k, 2 * k, 4 + 4 + 2)

    def body(a_ref, w_ref, x_ref, g_ref, o_ref, f_ref):
        f = _dot(a_ref[...], w_ref[...], NN)
        f_ref[...] = f.astype(BF16)
        o_ref[...] = x_ref[...] + (coef * g_ref[...]) * f

    return pl.pallas_call(
        body, name=name, grid=(n // tn, m // tm),
        in_specs=[pl.BlockSpec((tm, k), lambda j, i: (i, 0)), pl.BlockSpec((k, tn), lambda j, i: (0, j)),
                  pl.BlockSpec((tm, tn), lambda j, i: (i, j)), pl.BlockSpec((1, tn), lambda j, i: (0, j))],
        out_specs=[pl.BlockSpec((tm, tn), lambda j, i: (i, j)), pl.BlockSpec((tm, tn), lambda j, i: (i, j))],
        out_shape=[jax.ShapeDtypeStruct((m, n), F32), jax.ShapeDtypeStruct((m, n), BF16)],
        compiler_params=_params("parallel", "parallel"),
    )(a, w, x_in, gate)


TOKEN_SPEC = pl.BlockSpec((8, LANE), lambda *_: (0, 0))


def mm_nt(a, w, token, *, out_dtype, name):
    m, k = a.shape
    n = w.shape[0]
    tm, tn = _mm_tiles(m, n, 2 * k, 2 * k, jnp.dtype(out_dtype).itemsize)

    def body(a_ref, w_ref, token_ref, o_ref):
        o_ref[...] = _dot(a_ref[...], w_ref[...], NT).astype(o_ref.dtype)

    return pl.pallas_call(
        body, name=name, grid=(n // tn, m // tm),
        in_specs=[pl.BlockSpec((tm, k), lambda j, i: (i, 0)), pl.BlockSpec((tn, k), lambda j, i: (j, 0)), TOKEN_SPEC],
        out_specs=pl.BlockSpec((tm, tn), lambda j, i: (i, j)),
        out_shape=jax.ShapeDtypeStruct((m, n), out_dtype),
        compiler_params=_params("parallel", "parallel"),
    )(a, w, token)


def mm_nt_blocks(a, w3, token, *, name):
    m = a.shape[0]
    n_blk, n, tn = w3.shape
    tm = _tile(m, ROW_TILE, 16)

    def body(a_ref, w_ref, token_ref, o_ref):
        @pl.when(pl.program_id(1) == 0)
        def _():
            o_ref[...] = jnp.zeros_like(o_ref)

        o_ref[...] += _dot(a_ref[...], w_ref[...], NT)

    return pl.pallas_call(
        body, name=name, grid=(m // tm, n_blk),
        in_specs=[pl.BlockSpec((tm, tn), lambda i, j: (i, j)), pl.BlockSpec((None, n, tn), lambda i, j: (j, 0, 0)),
                  TOKEN_SPEC],
        out_specs=pl.BlockSpec((tm, n), lambda i, j: (i, 0)),
        out_shape=jax.ShapeDtypeStruct((m, n), F32),
        compiler_params=_params("parallel", "arbitrary"),
    )(a, w3, token)


def mm_tn(a, dy, *, name, n_blocks=None):
    s, k = a.shape
    n = dy.shape[1]
    ts = _tile(s, ROW_TILE, 16)
    tk = k if k <= 2048 else _tile(k, COL_TILE, LANE)
    if n_blocks is None:
        tn = _tile(n, ACC_BUDGET // (4 * tk), LANE)
        out_spec = pl.BlockSpec((tk, tn), lambda kk, j, t: (kk, j))
        out_shape = jax.ShapeDtypeStruct((k, n), BF16)
    else:
        tn = n // n_blocks
        out_spec = pl.BlockSpec((None, tk, tn), lambda kk, j, t: (j, kk, 0))
        out_shape = jax.ShapeDtypeStruct((n_blocks, k, tn), BF16)
    n_steps = s // ts

    def body(a_ref, dy_ref, o_ref, acc_ref):
        t = pl.program_id(2)

        @pl.when(t == 0)
        def _():
            acc_ref[...] = jnp.zeros_like(acc_ref)

        acc_ref[...] += _dot(a_ref[...], dy_ref[...], TN)

        @pl.when(t == n_steps - 1)
        def _():
            o_ref[...] = acc_ref[...].astype(BF16)

    return pl.pallas_call(
        body, name=name, grid=(k // tk, n // tn, n_steps),
        in_specs=[pl.BlockSpec((ts, tk), lambda kk, j, t: (t, kk)), pl.BlockSpec((ts, tn), lambda kk, j, t: (t, j))],
        out_specs=out_spec, out_shape=out_shape,
        scratch_shapes=[pltpu.VMEM((tk, tn), F32)],
        compiler_params=_params("parallel", "parallel", "arbitrary"),
    )(a, dy)


def ffn_up(h, wgu3, *, name, tm=512):
    s, k = h.shape
    n = wgu3.shape[2]
    half = wgu3.shape[0] // 2
    tm = _tile(s, tm, 16)

    def body(h_ref, wg_ref, wu_ref, g_ref, u_ref, a_ref):
        hh = h_ref[...]
        g = _dot(hh, wg_ref[...], NN)
        u = _dot(hh, wu_ref[...], NN)
        g_ref[...] = g.astype(BF16)
        u_ref[...] = u.astype(BF16)
        a_ref[...] = (g * _sigmoid(g) * u).astype(BF16)

    out = jax.ShapeDtypeStruct((s, half * n), BF16)
    blk = pl.BlockSpec((tm, n), lambda j, i: (i, j))
    return pl.pallas_call(
        body, name=name, grid=(half, s // tm),
        in_specs=[pl.BlockSpec((tm, k), lambda j, i: (i, 0)),
                  pl.BlockSpec((None, k, n), lambda j, i: (j, 0, 0)),
                  pl.BlockSpec((None, k, n), lambda j, i: (j + half, 0, 0))],
        out_specs=[blk, blk, blk], out_shape=[out, out, out],
        compiler_params=_params("parallel", "parallel"),
    )(h, wgu3, wgu3)


def _row_spec(ts, width, col=0):
    return pl.BlockSpec((ts, width), lambda i: (i, col))


def _vec_spec(width):
    return pl.BlockSpec((1, width), lambda i: (0, 0))


def _accumulate(ref, value):
    i = pl.program_id(0)

    @pl.when(i == 0)
    def _():
        ref[...] = value

    @pl.when(i > 0)
    def _():
        ref[...] += value


def norm_modulate(x, g, shift, scale, token, *, name, ts=512):
    s, d = x.shape
    ts = _tile(s, ts, 16)

    def body(x_ref, g_ref, sh_ref, sc_ref, token_ref, h_ref):
        xx = x_ref[...]
        r = lax.rsqrt(jnp.mean(xx * xx, axis=-1, keepdims=True) + EPS)
        h_ref[...] = ((xx * r) * g_ref[...] * (1 + sc_ref[...]) + sh_ref[...]).astype(BF16)

    return pl.pallas_call(
        body, name=name, grid=(s // ts,),
        in_specs=[_row_spec(ts, d), _vec_spec(d), _vec_spec(d), _vec_spec(d), TOKEN_SPEC],
        out_specs=_row_spec(ts, d), out_shape=jax.ShapeDtypeStruct((s, d), BF16),
        compiler_params=_params("parallel"),
    )(x, g, shift, scale, token)


def norm_modulate_bwd(x, g, scale, dh, dx_out, *, name, ts=512):
    s, d = x.shape
    ts = _tile(s, ts, 16)

    def body(x_ref, g_ref, sc_ref, dh_ref, dxo_ref, dx_ref, dsh_ref, dsc_ref, dg_ref):
        xx, dh_ = x_ref[...], dh_ref[...]
        r = lax.rsqrt(jnp.mean(xx * xx, axis=-1, keepdims=True) + EPS)
        xh = xx * r
        dn = dh_ * (1 + sc_ref[...])
        dxh = dn * g_ref[...]
        dx_ref[...] = dxo_ref[...] + r * (dxh - xh * jnp.mean(dxh * xh, axis=-1, keepdims=True))
        _accumulate(dsh_ref, jnp.sum(dh_, axis=0, keepdims=True))
        _accumulate(dsc_ref, jnp.sum(dh_ * (xh * g_ref[...]), axis=0, keepdims=True))
        _accumulate(dg_ref, jnp.sum(dn * xh, axis=0, keepdims=True))

    vec = jax.ShapeDtypeStruct((1, d), F32)
    return pl.pallas_call(
        body, name=name, grid=(s // ts,),
        in_specs=[_row_spec(ts, d), _vec_spec(d), _vec_spec(d), _row_spec(ts, d), _row_spec(ts, d)],
        out_specs=[_row_spec(ts, d), _vec_spec(d), _vec_spec(d), _vec_spec(d)],
        out_shape=[jax.ShapeDtypeStruct((s, d), F32), vec, vec, vec],
        compiler_params=_params("arbitrary"),
    )(x, g, scale, dh, dx_out)


def gate_bwd(dx_out, f, gate, coef, *, name, ts=512):
    s, d = dx_out.shape
    ts = _tile(s, ts, 16)

    def body(dx_ref, f_ref, g_ref, df_ref, dg_ref):
        dx = dx_ref[...]
        df_ref[...] = ((coef * g_ref[...]) * dx).astype(BF16)
        _accumulate(dg_ref, coef * jnp.sum(dx * f_ref[...].astype(F32), axis=0, keepdims=True))

    return pl.pallas_call(
        body, name=name, grid=(s // ts,),
        in_specs=[_row_spec(ts, d), _row_spec(ts, d), _vec_spec(d)],
        out_specs=[_row_spec(ts, d), _vec_spec(d)],
        out_shape=[jax.ShapeDtypeStruct((s, d), BF16), jax.ShapeDtypeStruct((1, d), F32)],
        compiler_params=_params("arbitrary"),
    )(dx_out, f, gate)


def swiglu_bwd(da, g, u, *, name, ts=256):
    s, f = da.shape
    ts = _tile(s, ts, 16)

    def body(da_ref, g_ref, u_ref, o_ref):
        da_, gg, uu = da_ref[...].astype(F32), g_ref[...].astype(F32), u_ref[...].astype(F32)
        sg = _sigmoid(gg)
        o_ref[:, :f] = (da_ * uu * (sg * (1 + gg * (1 - sg)))).astype(BF16)
        o_ref[:, f:] = (da_ * (gg * sg)).astype(BF16)

    return pl.pallas_call(
        body, name=name, grid=(s // ts,),
        in_specs=[_row_spec(ts, f)] * 3,
        out_specs=_row_spec(ts, 2 * f), out_shape=jax.ShapeDtypeStruct((s, 2 * f), BF16),
        compiler_params=_params("parallel"),
    )(da, g, u)


def loss_head(y, target, *, name, ts=512):
    s, d = y.shape
    ts = _tile(s, ts, 16)

    def body(y_ref, t_ref, dy_ref, l_ref):
        err = y_ref[...] - t_ref[...]
        dy_ref[...] = err * (1.0 / d)
        part = jnp.sum(jnp.sum(err * err, axis=0, keepdims=True), axis=1, keepdims=True) * (0.5 / d)
        _accumulate(l_ref, jnp.broadcast_to(part, (1, LANE)))

    return pl.pallas_call(
        body, name=name, grid=(s // ts,),
        in_specs=[_row_spec(ts, d), _row_spec(ts, d)],
        out_specs=[_row_spec(ts, d), _vec_spec(LANE)],
        out_shape=[jax.ShapeDtypeStruct((s, d), F32), jax.ShapeDtypeStruct((1, LANE), F32)],
        compiler_params=_params("arbitrary"),
    )(y, target)


def merge_fwd(z, y_pool, y_attn, *, name, ts=512, tc=1024):
    s, d = y_pool.shape
    ts = _tile(s, ts, 16)

    def body(ga_ref, gb_ref, yp_ref, ya_ref, o_ref):
        o_ref[...] = (_sigmoid(ga_ref[...]) * yp_ref[...] + _sigmoid(gb_ref[...]) * ya_ref[...]).astype(BF16)

    blk = pl.BlockSpec((ts, tc), lambda i, j: (i, j))
    return pl.pallas_call(
        body, name=name, grid=(s // ts, d // tc),
        in_specs=[pl.BlockSpec((ts, tc), lambda i, j: (i, COL_GA // tc + j)),
                  pl.BlockSpec((ts, tc), lambda i, j: (i, COL_GB // tc + j)), blk, blk],
        out_specs=blk, out_shape=jax.ShapeDtypeStruct((s, d), BF16),
        compiler_params=_params("parallel", "parallel"),
    )(z, z, y_pool, y_attn)


def merge_bwd(dmerged, z, y_pool, y_attn, *, name, ts=512, tc=1024):
    s, d = y_pool.shape
    ts = _tile(s, ts, 16)

    def body(dm_ref, ga_ref, gb_ref, yp_ref, ya_ref, dyp_ref, dya_ref, dga_ref, dgb_ref):
        dm = dm_ref[...]
        sa, sb = _sigmoid(ga_ref[...]), _sigmoid(gb_ref[...])
        dyp_ref[...] = (dm * sa).astype(BF16)
        dya_ref[...] = (dm * sb).astype(BF16)
        dga_ref[...] = (dm * yp_ref[...] * (sa * (1 - sa))).astype(BF16)
        dgb_ref[...] = (dm * ya_ref[...] * (sb * (1 - sb))).astype(BF16)

    blk = pl.BlockSpec((ts, tc), lambda i, j: (i, j))
    out = jax.ShapeDtypeStruct((s, d), BF16)
    return pl.pallas_call(
        body, name=name, grid=(s // ts, d // tc),
        in_specs=[blk, pl.BlockSpec((ts, tc), lambda i, j: (i, COL_GA // tc + j)),
                  pl.BlockSpec((ts, tc), lambda i, j: (i, COL_GB // tc + j)), blk, blk],
        out_specs=[blk] * 4, out_shape=[out] * 4,
        compiler_params=_params("parallel", "parallel"),
    )(dmerged, z, z, y_pool, y_attn)


def _window_counts(t0, rows):
    t1 = (t0 + 1 + lax.broadcasted_iota(jnp.int32, (rows, 1), 0)).astype(F32)
    return [jnp.minimum(t1, float(w)) for w in POOL_WINDOWS]


def pool_fwd(z, pool_mix, pool_scale, *, name, ts=256):
    s = z.shape[0]
    ts = _tile(s, ts, 16)
    per = ts // POOL_HALO

    def body(u_ref, halo_ref, pm_ref, ps_ref, pooled_ref, p_ref):
        i = pl.program_id(0)
        u = u_ref[...]
        halo = jnp.where(i > 0, halo_ref[...], 0.0)
        run = jnp.concatenate([halo, u], axis=0)
        sums, width = [], 1
        for w in POOL_WINDOWS:
            while width < w:
                run = run + pltpu.roll(run, width, 0)
                width *= 2
            sums.append(run[POOL_HALO:])
        counts = _window_counts(i * ts, ts)
        for gi in range(POOL_GROUPS):
            cols = slice(gi * POOL_GROUP_W, (gi + 1) * POOL_GROUP_W)
            pooled = (sums[gi][:, cols] / counts[gi] - u[:, cols]).astype(BF16)
            pooled_ref[:, cols] = pooled
            p_ref[:, cols] = (_dot(pooled, pm_ref[gi], NN) * ps_ref[:, cols]).astype(BF16)

    out = jax.ShapeDtypeStruct((s, POOL_W), BF16)
    return pl.pallas_call(
        body, name=name, grid=(s // ts,),
        in_specs=[_row_spec(ts, POOL_W, COL_U // POOL_W),
                  pl.BlockSpec((POOL_HALO, POOL_W), lambda i: (jnp.maximum(i * per - 1, 0), COL_U // POOL_W)),
                  pl.BlockSpec((POOL_GROUPS, POOL_GROUP_W, POOL_GROUP_W), lambda i: (0, 0, 0)),
                  _vec_spec(POOL_W)],
        out_specs=[_row_spec(ts, POOL_W)] * 2, out_shape=[out, out],
        compiler_params=_params("parallel"),
    )(z, z, pool_mix, pool_scale)


def pool_bwd(dp, pooled, pool_mix, pool_scale, *, name, ts=256):
    s = dp.shape[0]
    ts = _tile(s, ts, 16)
    per = ts // POOL_HALO
    n_steps = s // ts
    rows = ts + POOL_HALO

    def body(dp_ref, halo_ref, pooled_ref, pm_ref, ps_ref, du_ref, dps_ref, dpm_ref):
        i = pl.program_id(0)
        dp_main = dp_ref[...]
        halo = jnp.where(i < n_steps - 1, halo_ref[...], 0.0)
        dmixed = jnp.concatenate([dp_main, halo], axis=0) * ps_ref[...]
        counts = _window_counts(i * ts, rows)
        dps_parts = []
        for gi, w in enumerate(POOL_WINDOWS):
            cols = slice(gi * POOL_GROUP_W, (gi + 1) * POOL_GROUP_W)
            dmx = dmixed[:, cols].astype(BF16)
            pooled = pooled_ref[:, cols]
            mixed = _dot(pooled, pm_ref[gi], NN)
            dps_parts.append(jnp.sum(dp_main[:, cols] * mixed, axis=0, keepdims=True))
            dpm_g = _dot(pooled, dmx[:ts], TN)

            @pl.when(i == 0)
            def _():
                dpm_ref[gi] = dpm_g

            @pl.when(i > 0)
            def _():
                dpm_ref[gi] += dpm_g

            dpooled = _dot(dmx, pm_ref[gi], NT)
            run, width = dpooled / counts[gi], 1
            while width < w:
                run = run + pltpu.roll(run, rows - width, 0)
                width *= 2
            du_ref[:, cols] = (run[:ts] - dpooled[:ts]).astype(BF16)
        _accumulate(dps_ref, jnp.concatenate(dps_parts, axis=1))

    return pl.pallas_call(
        body, name=name, grid=(n_steps,),
        in_specs=[_row_spec(ts, POOL_W),
                  pl.BlockSpec((POOL_HALO, POOL_W), lambda i: (jnp.minimum((i + 1) * per, s // POOL_HALO - 1), 0)),
                  _row_spec(ts, POOL_W),
                  pl.BlockSpec((POOL_GROUPS, POOL_GROUP_W, POOL_GROUP_W), lambda i: (0, 0, 0)),
                  _vec_spec(POOL_W)],
        out_specs=[_row_spec(ts, POOL_W), _vec_spec(POOL_W),
                   pl.BlockSpec((POOL_GROUPS, POOL_GROUP_W, POOL_GROUP_W), lambda i: (0, 0, 0))],
        out_shape=[jax.ShapeDtypeStruct((s, POOL_W), BF16), jax.ShapeDtypeStruct((1, POOL_W), F32),
                   jax.ShapeDtypeStruct((POOL_GROUPS, POOL_GROUP_W, POOL_GROUP_W), F32)],
        compiler_params=_params("arbitrary"),
    )(dp, dp, pooled, pool_mix, pool_scale)


def _bucket_one_hot():
    ql = np.arange(BLK)[:, None]
    j = np.arange(2 * BLK)[None, :]
    n = np.clip(BLK + ql - j, 0, None)
    nf = np.maximum(n, 1).astype(np.float32)
    large = MAX_EXACT + (np.log(nf / MAX_EXACT) / np.log(REL_MAX_DIST / MAX_EXACT)
                         * (NUM_BUCKETS - MAX_EXACT)).astype(np.int32)
    large = np.minimum(large, NUM_BUCKETS - 1)
    bucket = np.where(n < MAX_EXACT, n, large).astype(np.int32).reshape(-1)
    return (np.arange(NUM_BUCKETS)[:, None] == bucket[None, :]).astype(np.float32)


def bias_table(rel_bias_t, one_hot, *, name, tc=4096):
    n = one_hot.shape[1]

    def body(rb_ref, oh_ref, o_ref):
        o_ref[...] = _dot(rb_ref[...], oh_ref[...], NN, precision=lax.Precision.HIGHEST)

    return pl.pallas_call(
        body, name=name, grid=(n // tc,),
        in_specs=[pl.BlockSpec((N_HEADS, NUM_BUCKETS), lambda i: (0, 0)), pl.BlockSpec((NUM_BUCKETS, tc), lambda i: (0, i))],
        out_specs=pl.BlockSpec((N_HEADS, tc), lambda i: (0, i)),
        out_shape=jax.ShapeDtypeStruct((N_HEADS, n), F32),
        compiler_params=_params("parallel"),
    )(rel_bias_t, one_hot)


def bias_table_bwd(dbias, one_hot, *, name, tc=4096):
    n = one_hot.shape[1]

    def body(db_ref, oh_ref, o_ref):
        _accumulate(o_ref, _dot(db_ref[...], oh_ref[...], NT, precision=lax.Precision.HIGHEST))

    return pl.pallas_call(
        body, name=name, grid=(n // tc,),
        in_specs=[pl.BlockSpec((N_HEADS, tc), lambda i: (0, i)), pl.BlockSpec((NUM_BUCKETS, tc), lambda i: (0, i))],
        out_specs=pl.BlockSpec((N_HEADS, NUM_BUCKETS), lambda i: (0, 0)),
        out_shape=jax.ShapeDtypeStruct((N_HEADS, NUM_BUCKETS), F32),
        compiler_params=_params("arbitrary"),
    )(dbias, one_hot)


def _lane_half(shape):
    return lax.broadcasted_iota(jnp.int32, shape, len(shape) - 1) < HEAD_DIM


def _half_sums(v, first):
    s0 = jnp.sum(jnp.where(first, v, 0.0), axis=-1, keepdims=True)
    s1 = jnp.sum(jnp.where(first, 0.0, v), axis=-1, keepdims=True)
    return jnp.where(first, s0, s1)


def _band_mask(n):
    ql = lax.broadcasted_iota(jnp.int32, (BLK, 2 * BLK), 0)
    j = lax.broadcasted_iota(jnp.int32, (BLK, 2 * BLK), 1)
    return (j > ql) & (j <= ql + BLK) & ((j >= BLK) | (n > 0))


def _norm_keys(kband, kg2):
    first = _lane_half(kband.shape)
    r = lax.rsqrt(_half_sums(kband * kband, first) * (1.0 / HEAD_DIM) + EPS)
    return kband * r, r


def _head_setup(q_ref, pair, sub, kv, qg2):
    qp = q_ref[:, pair * LANE:(pair + 1) * LANE]
    if sub != kv:
        qp = pltpu.roll(qp, HEAD_DIM, 1)
    lanes = _lane_half(qp.shape) if kv == 0 else jnp.logical_not(_lane_half(qp.shape))
    qa = jnp.where(lanes, qp, 0.0)
    r = lax.rsqrt(jnp.sum(qa * qa, axis=-1, keepdims=True) * (1.0 / HEAD_DIM) + EPS)
    return qa * r, r, lanes


def _softmax_with_sink(logits, sink):
    m = jnp.maximum(jnp.max(logits, axis=-1, keepdims=True), sink)
    e = jnp.exp(logits - m)
    es = jnp.exp(sink - m)
    den = jnp.sum(e, axis=-1, keepdims=True) + es
    return e / den, es / den


def _attn_specs(nb, last):
    cur = lambda n: jnp.minimum(n, last)
    prev = lambda n: jnp.minimum(jnp.maximum(n - 1, 0), last)
    return [pl.BlockSpec((BLK, ATT_W), lambda n: (cur(n), COL_Q // ATT_W)),
            pl.BlockSpec((BLK, KV_W), lambda n: (prev(n), COL_K // KV_W)),
            pl.BlockSpec((BLK, KV_W), lambda n: (cur(n), COL_K // KV_W)),
            pl.BlockSpec((BLK, KV_W), lambda n: (prev(n), COL_V // KV_W)),
            pl.BlockSpec((BLK, KV_W), lambda n: (cur(n), COL_V // KV_W))]


def attn_fwd(z, qg2, kg2, sinks, bias, *, name):
    s = z.shape[0]
    nb = s // BLK

    def body(q_ref, kp_ref, kc_ref, vp_ref, vc_ref, qg_ref, kg_ref, sink_ref, bias_ref, o_ref):
        n = pl.program_id(0)
        mask = _band_mask(n)
        kn, _ = _norm_keys(jnp.concatenate([kp_ref[...], kc_ref[...]], axis=0), kg_ref[...])
        kn = (kn * kg_ref[...]).astype(BF16)
        vb = jnp.concatenate([vp_ref[...], vc_ref[...]], axis=0).astype(BF16)
        for pair in range(N_HEADS // 2):
            kv = (2 * pair) // GQA
            acc = None
            for sub in range(2):
                h = 2 * pair + sub
                xh, _, lanes = _head_setup(q_ref, pair, sub, kv, qg_ref[...])
                qn = (xh * qg_ref[...]).astype(BF16)
                logits = _dot(qn, kn, NT) * (HEAD_DIM ** -0.5) + bias_ref[h]
                p, _ = _softmax_with_sink(jnp.where(mask, logits, NEG_INF), sink_ref[h])
                out = jnp.where(lanes, _dot(p.astype(BF16), vb, NN), 0.0)
                if sub != kv:
                    out = pltpu.roll(out, HEAD_DIM, 1)
                acc = out if acc is None else acc + out
            o_ref[:, pair * LANE:(pair + 1) * LANE] = acc.astype(BF16)

    return pl.pallas_call(
        body, name=name, grid=(nb,),
        in_specs=_attn_specs(nb, nb - 1) + [
            _vec_spec(LANE), _vec_spec(LANE), pl.BlockSpec(memory_space=pltpu.SMEM),
            pl.BlockSpec((N_HEADS, BLK, 2 * BLK), lambda n: (0, 0, 0))],
        out_specs=pl.BlockSpec((BLK, ATT_W), lambda n: (n, 0)),
        out_shape=jax.ShapeDtypeStruct((s, ATT_W), BF16),
        compiler_params=_params("parallel"),
    )(z, z, z, z, z, qg2, kg2, sinks, bias)


def attn_bwd(z, d_out, qg2, kg2, sinks, bias, *, name):
    s = z.shape[0]
    nb = s // BLK
    scale = HEAD_DIM ** -0.5

    def body(q_ref, kp_ref, kc_ref, vp_ref, vc_ref, do_ref, qg_ref, kg_ref, sink_ref, bias_ref,
             dq_ref, dk_ref, dv_ref, dqg_ref, dkg_ref, dsink_ref, dbias_ref, band_k, band_v, carry_k, carry_v):
        n = pl.program_id(0)

        @pl.when(n == 0)
        def _():
            dqg_ref[...] = jnp.zeros_like(dqg_ref)
            dkg_ref[...] = jnp.zeros_like(dkg_ref)
            dsink_ref[...] = jnp.zeros_like(dsink_ref)
            dbias_ref[...] = jnp.zeros_like(dbias_ref)
            carry_k[...] = jnp.zeros_like(carry_k)
            carry_v[...] = jnp.zeros_like(carry_v)

        @pl.when(n == nb)
        def _():
            band_k[...] = jnp.zeros_like(band_k)
            band_v[...] = jnp.zeros_like(band_v)

        @pl.when(n < nb)
        def _():
            mask = _band_mask(n)
            kx, _ = _norm_keys(jnp.concatenate([kp_ref[...], kc_ref[...]], axis=0), kg_ref[...])
            kn = (kx * kg_ref[...]).astype(BF16)
            vb = jnp.concatenate([vp_ref[...], vc_ref[...]], axis=0).astype(BF16)
            lane16 = lax.broadcasted_iota(jnp.int32, (1, N_HEADS), 1)
            dkn = jnp.zeros((2 * BLK, KV_W), F32)
            dvb = jnp.zeros((2 * BLK, KV_W), F32)
            dqg = jnp.zeros((1, LANE), F32)
            dsink = jnp.zeros((1, N_HEADS), F32)
            for pair in range(N_HEADS // 2):
                kv = (2 * pair) // GQA
                acc = None
                for sub in range(2):
                    h = 2 * pair + sub
                    xh, r, lanes = _head_setup(q_ref, pair, sub, kv, qg_ref[...])
                    qn = (xh * qg_ref[...]).astype(BF16)
                    logits = _dot(qn, kn, NT) * scale + bias_ref[h]
                    p, p_sink = _softmax_with_sink(jnp.where(mask, logits, NEG_INF), sink_ref[h])
                    do = do_ref[:, pair * LANE:(pair + 1) * LANE].astype(F32)
                    if sub != kv:
                        do = pltpu.roll(do, HEAD_DIM, 1)
                    do = jnp.where(lanes, do, 0.0).astype(BF16)
                    dp = _dot(do, vb, NT)
                    delta = jnp.sum(p * dp, axis=-1, keepdims=True)
                    ds = p * (dp - delta)
                    dsink = dsink + jnp.where(lane16 == h, -jnp.sum(p_sink * delta, axis=0, keepdims=True), 0.0)
                    dbias_ref[h] += ds
                    ds16 = ds.astype(BF16)
                    dqn = jnp.where(lanes, _dot(ds16, kn, NN) * scale, 0.0)
                    dkn = dkn + _dot(ds16, qn, TN) * scale
                    dvb = dvb + _dot(p.astype(BF16), do, TN)
                    dqg = dqg + jnp.sum(dqn * xh, axis=0, keepdims=True)
                    dxh = dqn * qg_ref[...]
                    dq = r * (dxh - xh * (jnp.sum(dxh * xh, axis=-1, keepdims=True) * (1.0 / HEAD_DIM)))
                    if sub != kv:
                        dq = pltpu.roll(dq, HEAD_DIM, 1)
                    acc = dq if acc is None else acc + dq
                dq_ref[:, pair * LANE:(pair + 1) * LANE] = acc.astype(BF16)
            band_k[...] = dkn
            band_v[...] = dvb
            dqg_ref[...] += dqg
            dsink_ref[...] += dsink

        dkn_prev = carry_k[...] + band_k[:BLK]
        dv_ref[...] = (carry_v[...] + band_v[:BLK]).astype(BF16)
        carry_k[...] = band_k[BLK:]
        carry_v[...] = band_v[BLK:]
        kp = kp_ref[...]
        first = _lane_half(kp.shape)
        r = lax.rsqrt(_half_sums(kp * kp, first) * (1.0 / HEAD_DIM) + EPS)
        xh = kp * r
        dkg_ref[...] += jnp.sum(dkn_prev * xh, axis=0, keepdims=True)
        dxh = dkn_prev * kg_ref[...]
        dk_ref[...] = (r * (dxh - xh * (_half_sums(dxh * xh, first) * (1.0 / HEAD_DIM)))).astype(BF16)

        @pl.when(n == nb)
        def _():
            dqg_ref[...] += pltpu.roll(dqg_ref[...], HEAD_DIM, 1)
            dkg_ref[...] += pltpu.roll(dkg_ref[...], HEAD_DIM, 1)

    last = nb - 1
    cur = lambda n: jnp.minimum(n, last)
    back = lambda n: jnp.maximum(n - 1, 0)
    full3 = pl.BlockSpec((N_HEADS, BLK, 2 * BLK), lambda n: (0, 0, 0))
    return pl.pallas_call(
        body, name=name, grid=(nb + 1,),
        in_specs=_attn_specs(nb, last) + [
            pl.BlockSpec((BLK, ATT_W), lambda n: (cur(n), 0)),
            _vec_spec(LANE), _vec_spec(LANE), pl.BlockSpec(memory_space=pltpu.SMEM), full3],
        out_specs=[pl.BlockSpec((BLK, ATT_W), lambda n: (cur(n), 0)),
                   pl.BlockSpec((BLK, KV_W), lambda n: (back(n), 0)),
                   pl.BlockSpec((BLK, KV_W), lambda n: (back(n), 0)),
                   _vec_spec(LANE), _vec_spec(LANE), _vec_spec(N_HEADS), full3],
        out_shape=[jax.ShapeDtypeStruct((s, ATT_W), BF16), jax.ShapeDtypeStruct((s, KV_W), BF16),
                   jax.ShapeDtypeStruct((s, KV_W), BF16), jax.ShapeDtypeStruct((1, LANE), F32),
                   jax.ShapeDtypeStruct((1, LANE), F32), jax.ShapeDtypeStruct((1, N_HEADS), F32),
                   jax.ShapeDtypeStruct((N_HEADS, BLK, 2 * BLK), F32)],
        scratch_shapes=[pltpu.VMEM((2 * BLK, KV_W), F32), pltpu.VMEM((2 * BLK, KV_W), F32),
                        pltpu.VMEM((BLK, KV_W), F32), pltpu.VMEM((BLK, KV_W), F32)],
        compiler_params=_params("arbitrary"),
    )(z, z, z, z, z, d_out, qg2, kg2, sinks, bias)


def _adamw(w, g, m, v):
    m = ADAM_B1 * m + (1.0 - ADAM_B1) * g
    v = ADAM_B2 * v + (1.0 - ADAM_B2) * (g * g)
    m_hat = m / (1.0 - ADAM_B1 ** ADAM_STEP)
    v_hat = v / (1.0 - ADAM_B2 ** ADAM_STEP)
    delta = -ADAM_LR * (m_hat / (jnp.sqrt(v_hat) + ADAM_EPS) + ADAM_WD * w)
    return delta, m, v


def ada_fwd(c16, w, b, *, name, tn=768):
    k, n = w.shape
    tn = _tile(n, tn, LANE)

    def body(c_ref, w_ref, b_ref, o_ref):
        cc = c_ref[...]
        o_ref[...] = _dot((cc * _sigmoid(cc)).astype(BF16), w_ref[...].astype(BF16), NN) + b_ref[...]

    return pl.pallas_call(
        body, name=name, grid=(n // tn,),
        in_specs=[pl.BlockSpec((c16.shape[0], k), lambda j: (0, 0)), pl.BlockSpec((k, tn), lambda j: (0, j)),
                  pl.BlockSpec((1, tn), lambda j: (0, j))],
        out_specs=pl.BlockSpec((c16.shape[0], tn), lambda j: (0, j)),
        out_shape=jax.ShapeDtypeStruct((c16.shape[0], n), F32),
        compiler_params=_params("parallel"),
    )(c16, w, b)


def ada_bwd_adamw(c_t, dmod, w, m, v, *, name, tn=256):
    k, n = w.shape
    tn = _tile(n, tn, LANE)

    def body(c_ref, d_ref, w_ref, m_ref, v_ref, g_ref, dl_ref, mo_ref, vo_ref):
        cc = c_ref[...]
        g = _dot((cc * _sigmoid(cc)).astype(BF16), d_ref[...].astype(BF16), NN)
        g_ref[...] = g
        dl_ref[...], mo_ref[...], vo_ref[...] = _adamw(w_ref[...], g, m_ref[...], v_ref[...])

    blk = pl.BlockSpec((k, tn), lambda j: (0, j))
    out = jax.ShapeDtypeStruct((k, n), F32)
    return pl.pallas_call(
        body, name=name, grid=(n // tn,),
        in_specs=[pl.BlockSpec((k, LANE), lambda j: (0, 0)), pl.BlockSpec((LANE, tn), lambda j: (0, j)), blk, blk, blk],
        out_specs=[blk] * 4, out_shape=[out] * 4,
        compiler_params=_params("parallel"),
    )(c_t, dmod, w, m, v)


def adamw_from_parts(parts, w, m, v, *, name):
    r, c = w.shape
    tr = _tile(r, max(16, (256 * 1024) // c), 16)

    def body(p_ref, w_ref, m_ref, v_ref, g_ref, dl_ref, mo_ref, vo_ref):
        g = p_ref[0].astype(F32)
        for d in range(1, N_DEV):
            g = g + p_ref[d].astype(F32)
        g_ref[...] = g
        dl_ref[...], mo_ref[...], vo_ref[...] = _adamw(w_ref[...], g, m_ref[...], v_ref[...])

    blk = pl.BlockSpec((tr, c), lambda i: (i, 0))
    out = jax.ShapeDtypeStruct((r, c), F32)
    return pl.pallas_call(
        body, name=name, grid=(r // tr,),
        in_specs=[pl.BlockSpec((N_DEV, tr, c), lambda i: (0, i, 0)), blk, blk, blk],
        out_specs=[blk] * 4, out_shape=[out] * 4,
        compiler_params=_params("parallel"),
    )(parts, w, m, v)


def _ffn_fwd(x_in, g, shift, scale, gate, wgu3, get_wd, token, tag):
    h = norm_modulate(x_in, g, shift, scale, token, name=f"{tag}_norm")
    gg, uu, act = ffn_up(h, wgu3, name=f"{tag}_up")
    wd = get_wd(gg)
    x_out, f = mm_nn_residual(act, wd, x_in, gate, 0.5, name=f"{tag}_down")
    return x_out, (h, gg, uu, act, f), wd


def _ffn_bwd(dx_out, x_in, g, scale, gate, wgu3, wd, saved, token, scatter, split, tag):
    h, gg, uu, act, f = saved
    df, dgate = gate_bwd(dx_out, f, gate, 0.5, name=f"{tag}_gate_bwd")
    dwd = mm_tn(act, df, name=f"{tag}_dwd").reshape(N_DEV, -1, D_MODEL)
    if split:
        token = scatter([f"w_{tag}_down"], [dwd], f"scatter_{tag}_down")
    da = mm_nt(df, wd, token, out_dtype=BF16, name=f"{tag}_da")
    dgu = swiglu_bwd(da, gg, uu, name=f"{tag}_swiglu_bwd")
    dwgu = mm_tn(h, dgu, name=f"{tag}_dwgu", n_blocks=N_DEV)
    if split:
        token = scatter([f"w_{tag}_gu"], [dwgu], f"scatter_{tag}_gu")
    else:
        token = scatter([f"w_{tag}_gu", f"w_{tag}_down"], [dwgu, dwd], f"scatter_{tag}")
    dh = mm_nt_blocks(dgu, wgu3, token, name=f"{tag}_dh")
    dx_in, dshift, dscale, dg = norm_modulate_bwd(x_in, g, scale, dh, dx_out, name=f"{tag}_norm_bwd")
    return dx_in, (dshift, dscale, dgate), dg


def kernel(x, c, w_ada, b_ada, g_ffn1, w_ffn1_gu, w_ffn1_down, g_mix, w_in, pool_mix, pool_scale, w_pool_up, q_gain, k_gain, sinks, rel_bias, w_attn_up, w_o, g_ffn2, w_ffn2_gu, w_ffn2_down, loss_target, m_w_ada, m_b_ada, m_g_ffn1, m_w_ffn1_gu, m_w_ffn1_down, m_g_mix, m_w_in, m_pool_mix, m_pool_scale, m_w_pool_up, m_q_gain, m_k_gain, m_sinks, m_rel_bias, m_w_attn_up, m_w_o, m_g_ffn2, m_w_ffn2_gu, m_w_ffn2_down, v_w_ada, v_b_ada, v_g_ffn1, v_w_ffn1_gu, v_w_ffn1_down, v_g_mix, v_w_in, v_pool_mix, v_pool_scale, v_w_pool_up, v_q_gain, v_k_gain, v_sinks, v_rel_bias, v_w_attn_up, v_w_o, v_g_ffn2, v_w_ffn2_gu, v_w_ffn2_down):
    me = _slot(_mesh_pos())
    x0, target = x[0], loss_target[0]
    n_ada = w_ada.shape[2]
    pm_rows = pool_mix.shape[2]

    big = dict(w_ffn1_gu=(w_ffn1_gu, m_w_ffn1_gu, v_w_ffn1_gu), w_ffn1_down=(w_ffn1_down, m_w_ffn1_down, v_w_ffn1_down),
               w_in=(w_in, m_w_in, v_w_in), pool_mix=(pool_mix, m_pool_mix, v_pool_mix),
               w_pool_up=(w_pool_up, m_w_pool_up, v_w_pool_up), w_attn_up=(w_attn_up, m_w_attn_up, v_w_attn_up),
               w_o=(w_o, m_w_o, v_w_o), w_ffn2_gu=(w_ffn2_gu, m_w_ffn2_gu, v_w_ffn2_gu),
               w_ffn2_down=(w_ffn2_down, m_w_ffn2_down, v_w_ffn2_down))
    shard2d = {k: (POOL_GROUPS * pm_rows, POOL_GROUP_W) if k == "pool_mix" else t[0].shape[1:] for k, t in big.items()}
    mix_keys = ["w_in", "pool_mix", "w_pool_up", "w_attn_up", "w_o"]
    ffn2_keys = ["w_ffn2_gu", "w_ffn2_down"]

    def shard_bf16(k, token=None):
        w = big[k][0].reshape(shard2d[k])
        return (w if token is None else w + token[0, 0]).astype(BF16)

    def own_slot(block, shape):
        return lax.dynamic_update_slice(lax.empty(shape, block.dtype), block[None], (me, 0, 0))

    def start_gather(keys, token, tag):
        shards = [shard_bf16(k, token) for k in keys]
        return exchange_start(shards, [own_slot(s, (N_DEV,) + s.shape) for s in shards], slotted=False, name=f"{tag}_start")

    pending = []

    def scatter(keys, grads, tag):
        lands = [own_slot(lax.dynamic_index_in_dim(g, me, 0, keepdims=False), g.shape) for g in grads]
        handle = exchange_start(grads, lands, slotted=True, name=f"{tag}_start")
        pending.append((keys, handle, tag))
        return handle[4]

    c_all, _ = all_gather_small(c.reshape(D_MODEL // LANE, LANE), "gather_c")
    c_all = c_all.reshape(N_DEV, D_MODEL)
    c16 = jnp.pad(c_all, ((0, 16 - N_DEV), (0, 0)))
    b_mine = lax.dynamic_slice(b_ada, (0, me * n_ada), (1, n_ada))
    mod_cols = ada_fwd(c16, w_ada[0], b_mine, name="ada_fwd")[:N_DEV]
    mod_all, token = all_gather_small(mod_cols.reshape(-1, LANE), "gather_mod")
    mod = lax.dynamic_index_in_dim(mod_all.reshape(N_DEV, N_DEV, n_ada), me, axis=1, keepdims=False)
    mod = mod.reshape(N_MOD, 1, D_MODEL)

    wgu1, token = all_gather_hbm([shard_bf16("w_ffn1_gu", token)], "gather_ffn1_gu")
    gather_wd1 = start_gather(["w_ffn1_down"], token, "gather_ffn1_down")
    gather_mix = start_gather(mix_keys, gather_wd1[4], "gather_mix")
    gather_ffn2 = start_gather(ffn2_keys, gather_mix[4], "gather_ffn2")
    token = gather_ffn2[4]

    def get_wd1(after):
        return exchange_wait(gather_wd1, after, slotted=False, name="gather_ffn1_down_wait")[0].reshape(-1, D_MODEL)

    x1, saved1, wd1 = _ffn_fwd(x0, g_ffn1, mod[0], mod[1], mod[2], wgu1, get_wd1, token, "ffn1")
    gathered = dict(zip(mix_keys, exchange_wait(gather_mix, x1, slotted=False, name="gather_mix_wait")))
    def columns_out(blocks):
        return jnp.transpose(blocks, (1, 0, 2)).reshape(blocks.shape[1], -1)

    def columns_in(full):
        return jnp.transpose(full.reshape(full.shape[0], N_DEV, -1), (1, 0, 2))

    w_in_full = columns_out(gathered["w_in"])
    w_in_z = jnp.concatenate([w_in_full[:, s:s + w] for s, w in (W_IN_PARTS[p] for p in Z_ORDER)], axis=1)
    pm_full = jnp.transpose(gathered["pool_mix"].reshape(N_DEV, POOL_GROUPS, pm_rows, POOL_GROUP_W),
                            (1, 0, 2, 3)).reshape(POOL_GROUPS, POOL_GROUP_W, POOL_GROUP_W)
    wpu, wau = columns_out(gathered["w_pool_up"]), columns_out(gathered["w_attn_up"])
    wo_full = gathered["w_o"].reshape(D_MODEL, D_MODEL)
    h2 = norm_modulate(x1, g_mix, mod[3], mod[4], token, name="mix_norm")
    z = mm_nn(h2, w_in_z, out_dtype=F32, name="mix_in")
    pooled, p_act = pool_fwd(z, pm_full, pool_scale, name="pool_fwd")
    y_pool = mm_nn(p_act, wpu, out_dtype=F32, name="pool_up")
    one_hot = jnp.asarray(_bucket_one_hot())
    bias = bias_table(rel_bias.T, one_hot, name="bias_table").reshape(N_HEADS, BLK, 2 * BLK)
    qg2, kg2 = jnp.tile(q_gain, (1, 2)), jnp.tile(k_gain, (1, 2))
    attn = attn_fwd(z, qg2, kg2, sinks[0], bias, name="attn_fwd")
    y_attn = mm_nn(attn, wau, out_dtype=F32, name="attn_up")
    merged = merge_fwd(z, y_pool, y_attn, name="merge_fwd")
    x2, o_act = mm_nn_residual(merged, wo_full, x1, mod[5], 1.0, name="mix_out")
    wgu2, wd2 = exchange_wait(gather_ffn2, x2, slotted=False, name="gather_ffn2_wait")
    wd2 = wd2.reshape(-1, D_MODEL)
    y, saved2, _ = _ffn_fwd(x2, g_ffn2, mod[6], mod[7], mod[8], wgu2, lambda after: wd2, token, "ffn2")
    dy, loss_row = loss_head(y, target, name="loss_head")
    loss = lax.psum(loss_row[0, 0], ("x", "y", "c"))

    dx2, dmod3, dg_ffn2 = _ffn_bwd(dy, x2, g_ffn2, mod[7], mod[8], wgu2, wd2, saved2, token, scatter, False, "ffn2")
    d_o, dgate2 = gate_bwd(dx2, o_act, mod[5], 1.0, name="mix_gate_bwd")
    dwo = mm_tn(merged, d_o, name="mix_dwo").reshape(N_DEV, -1, D_MODEL)
    dmerged = mm_nt(d_o, wo_full, token, out_dtype=F32, name="mix_dmerged")
    dyp, dya, dga, dgb = merge_bwd(dmerged, z, y_pool, y_attn, name="merge_bwd")
    dwpu = mm_tn(p_act, dyp, name="pool_dwup")
    dp_act = mm_nt(dyp, wpu, token, out_dtype=F32, name="pool_dp")
    du, dpool_scale, dpm = pool_bwd(dp_act, pooled, pm_full, pool_scale, name="pool_bwd")
    dwau = mm_tn(attn, dya, name="attn_dwup")
    dattn = mm_nt(dya, wau, token, out_dtype=BF16, name="attn_dout")
    dq, dk, dv, dqg, dkg, dsinks, dbias = attn_bwd(z, dattn, qg2, kg2, sinks[0], bias, name="attn_bwd")
    drel = bias_table_bwd(dbias.reshape(N_HEADS, -1), one_hot, name="bias_table_bwd").T
    dz_parts = dict(ga=dga, gb=dgb, u=du, q=dq, k=dk, v=dv)
    dz = jnp.concatenate([dz_parts[p] for p in Z_ORDER], axis=1)
    dwin_z = mm_tn(h2, dz, name="mix_dwin")
    z_start = dict(zip(Z_ORDER, np.cumsum([0] + [W_IN_PARTS[p][1] for p in Z_ORDER[:-1]])))
    dwin = jnp.concatenate([dwin_z[:, z_start[p]:z_start[p] + W_IN_PARTS[p][1]] for p in W_IN_PARTS], axis=1)
    mix_grads = dict(w_in=columns_in(dwin),
                     pool_mix=jnp.transpose(dpm.astype(BF16).reshape(POOL_GROUPS, N_DEV, pm_rows, POOL_GROUP_W),
                                            (1, 0, 2, 3)).reshape(N_DEV, POOL_GROUPS * pm_rows, POOL_GROUP_W),
                     w_pool_up=columns_in(dwpu), w_attn_up=columns_in(dwau), w_o=dwo)
    token = scatter(mix_keys, [mix_grads[k] for k in mix_keys], "scatter_mix")
    dh2 = mm_nt(dz, w_in_z, token, out_dtype=F32, name="mix_dh")
    dx1, dsh2, dsc2, dg_mix = norm_modulate_bwd(x1, g_mix, mod[4], dh2, dx2, name="mix_norm_bwd")
    dx0, dmod1, dg_ffn1 = _ffn_bwd(dx1, x0, g_ffn1, mod[1], mod[2], wgu1, wd1, saved1, token, scatter, True, "ffn1")

    small = [("b_ada", b_ada, m_b_ada, v_b_ada, jnp.concatenate(list(dmod1 + (dsh2, dsc2, dgate2) + dmod3), axis=1)),
             ("g_ffn1", g_ffn1, m_g_ffn1, v_g_ffn1, dg_ffn1), ("g_mix", g_mix, m_g_mix, v_g_mix, dg_mix),
             ("g_ffn2", g_ffn2, m_g_ffn2, v_g_ffn2, dg_ffn2),
             ("pool_scale", pool_scale, m_pool_scale, v_pool_scale, dpool_scale),
             ("q_gain", q_gain, m_q_gain, v_q_gain, dqg[:, :HEAD_DIM]), ("k_gain", k_gain, m_k_gain, v_k_gain, dkg[:, :HEAD_DIM]),
             ("sinks", sinks, m_sinks, v_sinks, dsinks), ("rel_bias", rel_bias, m_rel_bias, v_rel_bias, drel)]
    n_small = sum(t[1].size for t in small)
    pad = -n_small % (8 * LANE)
    flat = lambda arrs: jnp.pad(jnp.concatenate([a.reshape(1, -1) for a in arrs], axis=1), ((0, 0), (0, pad)))
    small_parts, _ = all_gather_small(flat([t[4] for t in small]).reshape(-1, LANE), "gather_small_grads")
    small_parts = small_parts.reshape(N_DEV, 1, n_small + pad)
    sg, sd, sm, sv = adamw_from_parts(small_parts, flat([t[1] for t in small]), flat([t[2] for t in small]),
                                      flat([t[3] for t in small]), name="adamw_small")

    dmod_all = small_parts[:, 0, :N_MOD * D_MODEL]
    dmod_mine = lax.dynamic_slice(dmod_all, (0, me * n_ada), (N_DEV, n_ada))
    c_t = jnp.pad(c_all.T, ((0, 0), (0, LANE - N_DEV)))
    ada_out = ada_bwd_adamw(c_t, jnp.pad(dmod_mine, ((0, LANE - N_DEV), (0, 0))), w_ada[0], m_w_ada[0], v_w_ada[0],
                            name="ada_bwd_adamw")

    res = {"w_ada": [o[None] for o in ada_out]}
    after = ada_out[0]
    for keys, handle, tag in pending:
        parts = exchange_wait(handle, after, slotted=True, name=f"{tag}_wait")
        for k, part in zip(keys, parts):
            w_, m_, v_ = big[k]
            outs = adamw_from_parts(part, w_.reshape(shard2d[k]), m_.reshape(shard2d[k]), v_.reshape(shard2d[k]),
                                    name=f"adamw_{k}")
            res[k] = [o.reshape(w_.shape) for o in outs]
            after = outs[0]
    off = 0
    for k, w_, _, _, _ in small:
        res[k] = [o[0, off:off + w_.size].reshape(w_.shape) for o in (sg, sd, sm, sv)]
        off += w_.size
    order = ["w_ada", "b_ada", "g_ffn1", "w_ffn1_gu", "w_ffn1_down", "g_mix", "w_in", "pool_mix", "pool_scale",
             "w_pool_up", "q_gain", "k_gain", "sinks", "rel_bias", "w_attn_up", "w_o", "g_ffn2", "w_ffn2_gu", "w_ffn2_down"]
    return (loss, dx0[None], *[res[k][0] for k in order], *[res[k][1] for k in order],
            *[res[k][2] for k in order], *[res[k][3] for k in order])
```

```python
import functools

import numpy as np
import jax
import jax.numpy as jnp
from jax import lax
from jax.experimental import pallas as pl
from jax.experimental.pallas import tpu as pltpu

F32, BF16 = jnp.float32, jnp.bfloat16
MESH_ID = pl.DeviceIdType.MESH

N_DEV = 8
D_MODEL = 2048
N_MOD = 9
POOL_WINDOWS = (2, 4, 8, 16)
POOL_GROUPS = 4
POOL_GROUP_W = D_MODEL // 8
POOL_W = POOL_GROUPS * POOL_GROUP_W
POOL_HALO = 16
HEAD_DIM = 64
N_HEADS = 16
N_KV = 2
GQA = N_HEADS // N_KV
BLK = 128
NUM_BUCKETS = 32
MAX_EXACT = 16
REL_MAX_DIST = 128
EPS = 1e-6
NEG_INF = -1e30
ATT_W = N_HEADS * HEAD_DIM
KV_W = N_KV * HEAD_DIM
IN_W = POOL_W + ATT_W + 2 * KV_W + 2 * D_MODEL
W_IN_PARTS = dict(u=(0, POOL_W), q=(POOL_W, ATT_W), k=(POOL_W + ATT_W, KV_W), v=(POOL_W + ATT_W + KV_W, KV_W),
                  ga=(POOL_W + ATT_W + 2 * KV_W, D_MODEL), gb=(POOL_W + ATT_W + 2 * KV_W + D_MODEL, D_MODEL))
Z_ORDER = ("ga", "gb", "u", "q", "k", "v")
COL_GA, COL_GB, COL_U, COL_Q, COL_K, COL_V = 0, D_MODEL, 2 * D_MODEL, 2 * D_MODEL + POOL_W, 2 * D_MODEL + POOL_W + ATT_W, 2 * D_MODEL + POOL_W + ATT_W + KV_W
LANE = 128

ADAM_LR = 0.001
ADAM_B1 = 0.9
ADAM_B2 = 0.999
ADAM_EPS = 1e-08
ADAM_WD = 0.01
ADAM_STEP = 10

NN = ((1,), (0,))
NT = ((1,), (1,))
TN = ((0,), (0,))


def _dot(a, b, dims, precision=None):
    return lax.dot_general(a, b, (dims, ((), ())), preferred_element_type=F32, precision=precision)


def _tile(n, pref, unit):
    t = (min(pref, n) // unit) * unit
    while t >= unit:
        if n % t == 0:
            return t
        t -= unit
    return n


def _params(*sem):
    return pltpu.CompilerParams(dimension_semantics=sem)


def _sigmoid(x):
    return 1.0 / (1.0 + jnp.exp(-x))


def _mesh_pos():
    return lax.axis_index("x"), lax.axis_index("y"), lax.axis_index("c")


def _slot(p):
    return 4 * p[0] + 2 * p[1] + p[2]


def all_gather_small(x_shard, name):
    m_per, n = x_shard.shape

    def body(x_ref, out_ref, token, send_sems, recv_sems, local_sem):
        x, y, c = _mesh_pos()
        me, sibling = (x, y, c), (x, y, 1 - c)
        chips = [(1 - x, y), (x, 1 - y), (1 - x, 1 - y)]
        token[...] = jnp.zeros_like(token)

        def rows(p):
            return out_ref.at[pl.ds(_slot(p) * m_per, m_per), :]

        def copy(k, block, to, src=None):
            return pltpu.make_async_remote_copy(
                src_ref=rows(block) if src is None else src, dst_ref=rows(block),
                send_sem=send_sems.at[k], recv_sem=recv_sems.at[k], device_id=to, device_id_type=MESH_ID)

        mine = pltpu.make_async_copy(x_ref, rows(me), local_sem)
        mine.start()
        first = [copy(0, me, sibling, src=x_ref)]
        first += [copy(1 + j, me, (*chip, c), src=x_ref) for j, chip in enumerate(chips)]
        for cp in first:
            cp.start()
        passed = [copy(4 + j, (*chip, c), sibling) for j, chip in enumerate(chips)]
        for j, chip in enumerate(chips):
            copy(1 + j, (*chip, c), me).wait_recv()
            passed[j].start()
        copy(0, sibling, me).wait_recv()
        for j, chip in enumerate(chips):
            copy(4 + j, (*chip, 1 - c), me).wait_recv()
        for cp in first + passed:
            cp.wait_send()
        mine.wait()

    return pl.pallas_call(
        body, name=name,
        out_shape=[jax.ShapeDtypeStruct((N_DEV * m_per, n), x_shard.dtype), jax.ShapeDtypeStruct((8, LANE), F32)],
        in_specs=[pl.BlockSpec(memory_space=pltpu.VMEM)],
        out_specs=[pl.BlockSpec(memory_space=pltpu.VMEM)] * 2,
        scratch_shapes=[pltpu.SemaphoreType.DMA((7,)), pltpu.SemaphoreType.DMA((7,)), pltpu.SemaphoreType.DMA],
    )(x_shard)


def all_gather_hbm(shards, name):
    n_arr = len(shards)

    def body(*refs):
        ins, outs, token = refs[:n_arr], refs[n_arr:2 * n_arr], refs[2 * n_arr]
        send_sems, recv_sems, local_sems = refs[2 * n_arr + 1:]
        x, y, c = _mesh_pos()
        me, sibling = (x, y, c), (x, y, 1 - c)
        chips = [(1 - x, y), (x, 1 - y), (1 - x, 1 - y)]
        token[...] = jnp.zeros_like(token)

        def copy(a, k, block, to, src=None):
            dst = outs[a].at[_slot(block)]
            return pltpu.make_async_remote_copy(
                src_ref=dst if src is None else src, dst_ref=dst,
                send_sem=send_sems.at[7 * a + k], recv_sem=recv_sems.at[7 * a + k],
                device_id=to, device_id_type=MESH_ID)

        mine = [pltpu.make_async_copy(ins[a], outs[a].at[_slot(me)], local_sems.at[a]) for a in range(n_arr)]
        for cp in mine:
            cp.start()
        first = []
        for a in range(n_arr):
            first.append(copy(a, 0, me, sibling, src=ins[a]))
            first += [copy(a, 1 + j, me, (*chip, c), src=ins[a]) for j, chip in enumerate(chips)]
        for cp in first:
            cp.start()
        passed = []
        for a in range(n_arr):
            for j, chip in enumerate(chips):
                copy(a, 1 + j, (*chip, c), me).wait_recv()
                fwd = copy(a, 4 + j, (*chip, c), sibling)
                fwd.start()
                passed.append(fwd)
        for a in range(n_arr):
            copy(a, 0, sibling, me).wait_recv()
            for j, chip in enumerate(chips):
                copy(a, 4 + j, (*chip, 1 - c), me).wait_recv()
        for cp in first + passed:
            cp.wait_send()
        for cp in mine:
            cp.wait()

    any_spec = pl.BlockSpec(memory_space=pl.ANY)
    return pl.pallas_call(
        body, name=name,
        out_shape=[jax.ShapeDtypeStruct((N_DEV,) + s.shape, s.dtype) for s in shards] + [jax.ShapeDtypeStruct((8, LANE), F32)],
        in_specs=[any_spec] * n_arr, out_specs=[any_spec] * n_arr + [pl.BlockSpec(memory_space=pltpu.VMEM)],
        scratch_shapes=[pltpu.SemaphoreType.DMA((7 * n_arr,)), pltpu.SemaphoreType.DMA((7 * n_arr,)),
                        pltpu.SemaphoreType.DMA((n_arr,))],
    )(*shards)


def _peer_list(x, y, c):
    return [((1 - x) if k & 4 else x, (1 - y) if k & 2 else y, (1 - c) if k & 1 else c) for k in range(1, N_DEV)]


def _exchange_copies(srcs, lands, send_sems, recv_sems, slotted, arriving):
    x, y, c = _mesh_pos()
    me = _slot((x, y, c))
    copies = []
    for a in range(len(srcs)):
        for k, peer in enumerate(_peer_list(x, y, c)):
            copies.append(pltpu.make_async_remote_copy(
                src_ref=srcs[a].at[_slot(peer)] if slotted else srcs[a],
                dst_ref=lands[a].at[_slot(peer) if arriving else me],
                send_sem=send_sems.at[7 * a + k], recv_sem=recv_sems.at[7 * a + k],
                device_id=peer, device_id_type=MESH_ID))
    return copies


HBM_SPEC = pl.BlockSpec(memory_space=pltpu.HBM)
SEM_SPEC = pl.BlockSpec(memory_space=pltpu.SEMAPHORE)
DATAFLOW = pltpu.SideEffectType.DATAFLOW_SIDE_EFFECTING


def exchange_start(srcs, lands, *, slotted, name):
    n = len(srcs)

    def body(*refs):
        ins = refs[:2 * n]
        send_sems, recv_sems = refs[2 * n], refs[2 * n + 1]
        token = refs[-1]
        for cp in _exchange_copies(ins[:n], ins[n:], send_sems, recv_sems, slotted, False):
            cp.start()
        token[...] = jnp.zeros_like(token)

    operands = [pltpu.with_memory_space_constraint(v, pltpu.HBM) for v in list(srcs) + list(lands)]
    out = pl.pallas_call(
        body, name=name,
        out_shape=(pltpu.SemaphoreType.DMA((7 * n,)), pltpu.SemaphoreType.DMA((7 * n,)),
                   *[pltpu.HBM(v.shape, v.dtype) for v in operands], jax.ShapeDtypeStruct((8, LANE), F32)),
        in_specs=[HBM_SPEC] * (2 * n),
        out_specs=(SEM_SPEC, SEM_SPEC, *[HBM_SPEC] * (2 * n), pl.BlockSpec(memory_space=pltpu.VMEM)),
        input_output_aliases={i: 2 + i for i in range(2 * n)},
        compiler_params=pltpu.CompilerParams(has_side_effects=DATAFLOW),
    )(*operands)
    return out[0], out[1], list(out[2:2 + n]), list(out[2 + n:2 + 2 * n]), out[-1]


def exchange_wait(handle, after, *, slotted, name):
    send_sems, recv_sems, srcs, lands, _ = handle
    n = len(srcs)

    def body(*refs):
        ins = refs[:2 * n]
        for cp in _exchange_copies(ins[:n], ins[n:], refs[2 * n], refs[2 * n + 1], slotted, True):
            cp.wait_send()
            cp.wait_recv()

    out = pl.pallas_call(
        body, name=name,
        out_shape=tuple(pltpu.HBM(v.shape, v.dtype) for v in srcs + lands),
        in_specs=[HBM_SPEC] * (2 * n) + [SEM_SPEC, SEM_SPEC, pl.BlockSpec(memory_space=pl.ANY)],
        out_specs=[HBM_SPEC] * (2 * n),
        input_output_aliases={i: i for i in range(2 * n)},
        compiler_params=pltpu.CompilerParams(has_side_effects=DATAFLOW),
    )(*srcs, *lands, send_sems, recv_sems, after)
    return list(out[n:])


VMEM_BLOCK_BUDGET = 46 * 2 ** 20
ROW_TILE, COL_TILE = 1024, 1408
ACC_BUDGET = 12 * 2 ** 20


def _mm_tiles(m, n, row_bytes, col_bytes, elem_bytes):
    tm, tn = _tile(m, ROW_TILE, 16), _tile(n, COL_TILE, LANE)
    while 2 * (tm * row_bytes + tn * col_bytes + tm * tn * elem_bytes) > VMEM_BLOCK_BUDGET:
        narrower = _tile(n, max(tn - LANE, LANE), LANE)
        if tn > 512 and narrower < tn:
            tn = narrower
        else:
            tm //= 2
    return tm, tn


def mm_nn(a, w, *, out_dtype, name):
    m, k = a.shape
    n = w.shape[1]
    tm, tn = _mm_tiles(m, n, 2 * k, 2 * k, jnp.dtype(out_dtype).itemsize)

    def body(a_ref, w_ref, o_ref):
        o_ref[...] = _dot(a_ref[...], w_ref[...], NN).astype(o_ref.dtype)

    return pl.pallas_call(
        body, name=name, grid=(n // tn, m // tm),
        in_specs=[pl.BlockSpec((tm, k), lambda j, i: (i, 0)), pl.BlockSpec((k, tn), lambda j, i: (0, j))],
        out_specs=pl.BlockSpec((tm, tn), lambda j, i: (i, j)),
        out_shape=jax.ShapeDtypeStruct((m, n), out_dtype),
        compiler_params=_params("parallel", "parallel"),
    )(a, w)


def mm_nn_residual(a, w, x_in, gate, coef, *, name):
    m, k = a.shape
    n = w.shape[1]
    tm, tn = _mm_tiles(m, n, 2 * k, 2 * k, 4 + 4 + 2)

    def body(a_ref, w_ref, x_ref, g_ref, o_ref, f_ref):
        f = _dot(a_ref[...], w_ref[...], NN)
        f_ref[...] = f.astype(BF16)
        o_ref[...] = x_ref[...] + (coef * g_ref[...]) * f

    return pl.pallas_call(
        body, name=name, grid=(n // tn, m // tm),
        in_specs=[pl.BlockSpec((tm, k), lambda j, i: (i, 0)), pl.BlockSpec((k, tn), lambda j, i: (0, j)),
                  pl.BlockSpec((tm, tn), lambda j, i: (i, j)), pl.BlockSpec((1, tn), lambda j, i: (0, j))],
        out_specs=[pl.BlockSpec((tm, tn), lambda j, i: (i, j)), pl.BlockSpec((tm, tn), lambda j, i: (i, j))],
        out_shape=[jax.ShapeDtypeStruct((m, n), F32), jax.ShapeDtypeStruct((m, n), BF16)],
        compiler_params=_params("parallel", "parallel"),
    )(a, w, x_in, gate)


def mm_nn_residual_loss(a, w, x_in, gate, coef, target, *, name):
    m, k = a.shape
    n = w.shape[1]
    tm, tn = _mm_tiles(m, n, 2 * k, 2 * k, 4 + 4 + 4 + 2)

    def body(a_ref, w_ref, x_ref, g_ref, t_ref, dy_ref, f_ref, l_ref):
        f = _dot(a_ref[...], w_ref[...], NN)
        f_ref[...] = f.astype(BF16)
        err = x_ref[...] + (coef * g_ref[...]) * f - t_ref[...]
        dy_ref[...] = err * (1.0 / n)
        part = jnp.sum(jnp.sum(err * err, axis=0, keepdims=True), axis=1, keepdims=True) * (0.5 / n)

        @pl.when((pl.program_id(0) == 0) & (pl.program_id(1) == 0))
        def _():
            l_ref[...] = jnp.zeros_like(l_ref)

        l_ref[...] += jnp.broadcast_to(part, l_ref.shape)

    blk = pl.BlockSpec((tm, tn), lambda j, i: (i, j))
    return pl.pallas_call(
        body, name=name, grid=(n // tn, m // tm),
        in_specs=[pl.BlockSpec((tm, k), lambda j, i: (i, 0)), pl.BlockSpec((k, tn), lambda j, i: (0, j)),
                  blk, pl.BlockSpec((1, tn), lambda j, i: (0, j)), blk],
        out_specs=[blk, blk, pl.BlockSpec((1, LANE), lambda j, i: (0, 0))],
        out_shape=[jax.ShapeDtypeStruct((m, n), F32), jax.ShapeDtypeStruct((m, n), BF16), jax.ShapeDtypeStruct((1, LANE), F32)],
        compiler_params=_params("arbitrary", "arbitrary"),
    )(a, w, x_in, gate, target)


TOKEN_SPEC = pl.BlockSpec((8, LANE), lambda *_: (0, 0))


def mm_nt(a, w, token, *, out_dtype, name):
    m, k = a.shape
    n = w.shape[0]
    tm, tn = _mm_tiles(m, n, 2 * k, 2 * k, jnp.dtype(out_dtype).itemsize)

    def body(a_ref, w_ref, token_ref, o_ref):
        o_ref[...] = _dot(a_ref[...], w_ref[...], NT).astype(o_ref.dtype)

    return pl.pallas_call(
        body, name=name, grid=(n // tn, m // tm),
        in_specs=[pl.BlockSpec((tm, k), lambda j, i: (i, 0)), pl.BlockSpec((tn, k), lambda j, i: (j, 0)), TOKEN_SPEC],
        out_specs=pl.BlockSpec((tm, tn), lambda j, i: (i, j)),
        out_shape=jax.ShapeDtypeStruct((m, n), out_dtype),
        compiler_params=_params("parallel", "parallel"),
    )(a, w, token)


def mm_nt_halves(a_lo, a_hi, w3, token, *, name):
    m = a_lo.shape[0]
    n_blk, n, tn = w3.shape
    half = n_blk // 2
    tm = _tile(m, ROW_TILE, 16)

    def body(lo_ref, hi_ref, w_ref, token_ref, o_ref):
        j = pl.program_id(1)

        @pl.when(j == 0)
        def _():
            o_ref[...] = jnp.zeros_like(o_ref)

        @pl.when(j < half)
        def _():
            o_ref[...] += _dot(lo_ref[...], w_ref[...], NT)

        @pl.when(j >= half)
        def _():
            o_ref[...] += _dot(hi_ref[...], w_ref[...], NT)

    return pl.pallas_call(
        body, name=name, grid=(m // tm, n_blk),
        in_specs=[pl.BlockSpec((tm, tn), lambda i, j: (i, jnp.minimum(j, half - 1))),
                  pl.BlockSpec((tm, tn), lambda i, j: (i, jnp.maximum(j - half, 0))),
                  pl.BlockSpec((None, n, tn), lambda i, j: (j, 0, 0)), TOKEN_SPEC],
        out_specs=pl.BlockSpec((tm, n), lambda i, j: (i, 0)),
        out_shape=jax.ShapeDtypeStruct((m, n), F32),
        compiler_params=_params("parallel", "arbitrary"),
    )(a_lo, a_hi, w3, token)


def mm_tn(a, dy, *, name):
    s, k = a.shape
    n = dy.shape[1]
    ts = _tile(s, ROW_TILE, 16)
    tk = k if k <= 2048 else _tile(k, COL_TILE, LANE)
    tn = _tile(n, ACC_BUDGET // (4 * tk), LANE)
    n_steps = s // ts

    def body(a_ref, dy_ref, o_ref, acc_ref):
        t = pl.program_id(2)

        @pl.when(t == 0)
        def _():
            acc_ref[...] = jnp.zeros_like(acc_ref)

        acc_ref[...] += _dot(a_ref[...], dy_ref[...], TN)

        @pl.when(t == n_steps - 1)
        def _():
            o_ref[...] = acc_ref[...].astype(BF16)

    return pl.pallas_call(
        body, name=name, grid=(k // tk, n // tn, n_steps),
        in_specs=[pl.BlockSpec((ts, tk), lambda kk, j, t: (t, kk)), pl.BlockSpec((ts, tn), lambda kk, j, t: (t, j))],
        out_specs=pl.BlockSpec((tk, tn), lambda kk, j, t: (kk, j)),
        out_shape=jax.ShapeDtypeStruct((k, n), BF16),
        scratch_shapes=[pltpu.VMEM((tk, tn), F32)],
        compiler_params=_params("parallel", "parallel", "arbitrary"),
    )(a, dy)


def mm_tn_halves(a, dy_lo, dy_hi, n_blocks, *, name):
    s, k = a.shape
    half = n_blocks // 2
    tn = dy_lo.shape[1] // half
    ts = _tile(s, ROW_TILE, 16)
    n_steps = s // ts

    def body(a_ref, lo_ref, hi_ref, o_ref, acc_ref):
        j, t = pl.program_id(0), pl.program_id(1)

        @pl.when(t == 0)
        def _():
            acc_ref[...] = jnp.zeros_like(acc_ref)

        @pl.when(j < half)
        def _():
            acc_ref[...] += _dot(a_ref[...], lo_ref[...], TN)

        @pl.when(j >= half)
        def _():
            acc_ref[...] += _dot(a_ref[...], hi_ref[...], TN)

        @pl.when(t == n_steps - 1)
        def _():
            o_ref[...] = acc_ref[...].astype(BF16)

    return pl.pallas_call(
        body, name=name, grid=(n_blocks, n_steps),
        in_specs=[pl.BlockSpec((ts, k), lambda j, t: (t, 0)),
                  pl.BlockSpec((ts, tn), lambda j, t: (jnp.where(j < half, t, n_steps - 1), jnp.minimum(j, half - 1))),
                  pl.BlockSpec((ts, tn), lambda j, t: (jnp.where(j < half, 0, t), jnp.maximum(j - half, 0)))],
        out_specs=pl.BlockSpec((None, k, tn), lambda j, t: (j, 0, 0)),
        out_shape=jax.ShapeDtypeStruct((n_blocks, k, tn), BF16),
        scratch_shapes=[pltpu.VMEM((k, tn), F32)],
        compiler_params=_params("parallel", "arbitrary"),
    )(a, dy_lo, dy_hi)


def ffn_dgu(df, wd, g, u, token, *, name):
    m, k = df.shape
    n = wd.shape[0]
    tm, tn = _mm_tiles(m, n, 2 * k, 2 * k, 4 * 2)

    def body(df_ref, w_ref, g_ref, u_ref, token_ref, dg_ref, du_ref):
        da = _dot(df_ref[...], w_ref[...], NT)
        gg, uu = g_ref[...].astype(F32), u_ref[...].astype(F32)
        sg = _sigmoid(gg)
        dg_ref[...] = (da * uu * (sg * (1 + gg * (1 - sg)))).astype(BF16)
        du_ref[...] = (da * (gg * sg)).astype(BF16)

    blk = pl.BlockSpec((tm, tn), lambda j, i: (i, j))
    out = jax.ShapeDtypeStruct((m, n), BF16)
    return pl.pallas_call(
        body, name=name, grid=(n // tn, m // tm),
        in_specs=[pl.BlockSpec((tm, k), lambda j, i: (i, 0)), pl.BlockSpec((tn, k), lambda j, i: (j, 0)), blk, blk, TOKEN_SPEC],
        out_specs=[blk, blk], out_shape=[out, out],
        compiler_params=_params("parallel", "parallel"),
    )(df, wd, g, u, token)


def ffn_up(h, wgu3, *, name, tm=512):
    s, k = h.shape
    n = wgu3.shape[2]
    half = wgu3.shape[0] // 2
    tm = _tile(s, tm, 16)

    def body(h_ref, wg_ref, wu_ref, g_ref, u_ref, a_ref):
        hh = h_ref[...]
        g = _dot(hh, wg_ref[...], NN)
        u = _dot(hh, wu_ref[...], NN)
        g_ref[...] = g.astype(BF16)
        u_ref[...] = u.astype(BF16)
        a_ref[...] = (g * _sigmoid(g) * u).astype(BF16)

    out = jax.ShapeDtypeStruct((s, half * n), BF16)
    blk = pl.BlockSpec((tm, n), lambda j, i: (i, j))
    return pl.pallas_call(
        body, name=name, grid=(half, s // tm),
        in_specs=[pl.BlockSpec((tm, k), lambda j, i: (i, 0)),
                  pl.BlockSpec((None, k, n), lambda j, i: (j, 0, 0)),
                  pl.BlockSpec((None, k, n), lambda j, i: (j + half, 0, 0))],
        out_specs=[blk, blk, blk], out_shape=[out, out, out],
        compiler_params=_params("parallel", "parallel"),
    )(h, wgu3, wgu3)


def _row_spec(ts, width, col=0):
    return pl.BlockSpec((ts, width), lambda i: (i, col))


def _vec_spec(width):
    return pl.BlockSpec((1, width), lambda i: (0, 0))


def _accumulate(ref, value):
    i = pl.program_id(0)

    @pl.when(i == 0)
    def _():
        ref[...] = value

    @pl.when(i > 0)
    def _():
        ref[...] += value


def norm_modulate(x, g, shift, scale, token, *, name, ts=512):
    s, d = x.shape
    ts = _tile(s, ts, 16)

    def body(x_ref, g_ref, sh_ref, sc_ref, token_ref, h_ref):
        xx = x_ref[...]
        r = lax.rsqrt(jnp.mean(xx * xx, axis=-1, keepdims=True) + EPS)
        h_ref[...] = ((xx * r) * g_ref[...] * (1 + sc_ref[...]) + sh_ref[...]).astype(BF16)

    return pl.pallas_call(
        body, name=name, grid=(s // ts,),
        in_specs=[_row_spec(ts, d), _vec_spec(d), _vec_spec(d), _vec_spec(d), TOKEN_SPEC],
        out_specs=_row_spec(ts, d), out_shape=jax.ShapeDtypeStruct((s, d), BF16),
        compiler_params=_params("parallel"),
    )(x, g, shift, scale, token)


def norm_modulate_bwd(x, g, scale, dh, dx_out, *, name, ts=512):
    s, d = x.shape
    ts = _tile(s, ts, 16)

    def body(x_ref, g_ref, sc_ref, dh_ref, dxo_ref, dx_ref, dsh_ref, dsc_ref, dg_ref):
        xx, dh_ = x_ref[...], dh_ref[...]
        r = lax.rsqrt(jnp.mean(xx * xx, axis=-1, keepdims=True) + EPS)
        xh = xx * r
        dn = dh_ * (1 + sc_ref[...])
        dxh = dn * g_ref[...]
        dx_ref[...] = dxo_ref[...] + r * (dxh - xh * jnp.mean(dxh * xh, axis=-1, keepdims=True))
        _accumulate(dsh_ref, jnp.sum(dh_, axis=0, keepdims=True))
        _accumulate(dsc_ref, jnp.sum(dh_ * (xh * g_ref[...]), axis=0, keepdims=True))
        _accumulate(dg_ref, jnp.sum(dn * xh, axis=0, keepdims=True))

    vec = jax.ShapeDtypeStruct((1, d), F32)
    return pl.pallas_call(
        body, name=name, grid=(s // ts,),
        in_specs=[_row_spec(ts, d), _vec_spec(d), _vec_spec(d), _row_spec(ts, d), _row_spec(ts, d)],
        out_specs=[_row_spec(ts, d), _vec_spec(d), _vec_spec(d), _vec_spec(d)],
        out_shape=[jax.ShapeDtypeStruct((s, d), F32), vec, vec, vec],
        compiler_params=_params("arbitrary"),
    )(x, g, scale, dh, dx_out)


def gate_bwd(dx_out, f, gate, coef, *, name, ts=512):
    s, d = dx_out.shape
    ts = _tile(s, ts, 16)

    def body(dx_ref, f_ref, g_ref, df_ref, dg_ref):
        dx = dx_ref[...]
        df_ref[...] = ((coef * g_ref[...]) * dx).astype(BF16)
        _accumulate(dg_ref, coef * jnp.sum(dx * f_ref[...].astype(F32), axis=0, keepdims=True))

    return pl.pallas_call(
        body, name=name, grid=(s // ts,),
        in_specs=[_row_spec(ts, d), _row_spec(ts, d), _vec_spec(d)],
        out_specs=[_row_spec(ts, d), _vec_spec(d)],
        out_shape=[jax.ShapeDtypeStruct((s, d), BF16), jax.ShapeDtypeStruct((1, d), F32)],
        compiler_params=_params("arbitrary"),
    )(dx_out, f, gate)


def merge_fwd(z, y_pool, y_attn, *, name, ts=512, tc=1024):
    s, d = y_pool.shape
    ts = _tile(s, ts, 16)

    def body(ga_ref, gb_ref, yp_ref, ya_ref, o_ref):
        o_ref[...] = (_sigmoid(ga_ref[...]) * yp_ref[...] + _sigmoid(gb_ref[...]) * ya_ref[...]).astype(BF16)

    blk = pl.BlockSpec((ts, tc), lambda i, j: (i, j))
    return pl.pallas_call(
        body, name=name, grid=(s // ts, d // tc),
        in_specs=[pl.BlockSpec((ts, tc), lambda i, j: (i, COL_GA // tc + j)),
                  pl.BlockSpec((ts, tc), lambda i, j: (i, COL_GB // tc + j)), blk, blk],
        out_specs=blk, out_shape=jax.ShapeDtypeStruct((s, d), BF16),
        compiler_params=_params("parallel", "parallel"),
    )(z, z, y_pool, y_attn)


def merge_bwd(dmerged, z, y_pool, y_attn, *, name, ts=512, tc=1024):
    s, d = y_pool.shape
    ts = _tile(s, ts, 16)

    def body(dm_ref, ga_ref, gb_ref, yp_ref, ya_ref, dyp_ref, dya_ref, dga_ref, dgb_ref):
        dm = dm_ref[...]
        sa, sb = _sigmoid(ga_ref[...]), _sigmoid(gb_ref[...])
        dyp_ref[...] = (dm * sa).astype(BF16)
        dya_ref[...] = (dm * sb).astype(BF16)
        dga_ref[...] = (dm * yp_ref[...] * (sa * (1 - sa))).astype(BF16)
        dgb_ref[...] = (dm * ya_ref[...] * (sb * (1 - sb))).astype(BF16)

    blk = pl.BlockSpec((ts, tc), lambda i, j: (i, j))
    out = jax.ShapeDtypeStruct((s, d), BF16)
    return pl.pallas_call(
        body, name=name, grid=(s // ts, d // tc),
        in_specs=[blk, pl.BlockSpec((ts, tc), lambda i, j: (i, COL_GA // tc + j)),
                  pl.BlockSpec((ts, tc), lambda i, j: (i, COL_GB // tc + j)), blk, blk],
        out_specs=[blk] * 4, out_shape=[out] * 4,
        compiler_params=_params("parallel", "parallel"),
    )(dmerged, z, z, y_pool, y_attn)


def _window_counts(t0, rows):
    t1 = (t0 + 1 + lax.broadcasted_iota(jnp.int32, (rows, 1), 0)).astype(F32)
    return [jnp.minimum(t1, float(w)) for w in POOL_WINDOWS]


def pool_fwd(z, pool_mix, pool_scale, *, name, ts=256):
    s = z.shape[0]
    ts = _tile(s, ts, 16)
    per = ts // POOL_HALO

    def body(u_ref, halo_ref, pm_ref, ps_ref, pooled_ref, p_ref):
        i = pl.program_id(0)
        u = u_ref[...]
        halo = jnp.where(i > 0, halo_ref[...], 0.0)
        run = jnp.concatenate([halo, u], axis=0)
        sums, width = [], 1
        for w in POOL_WINDOWS:
            while width < w:
                run = run + pltpu.roll(run, width, 0)
                width *= 2
            sums.append(run[POOL_HALO:])
        counts = _window_counts(i * ts, ts)
        for gi in range(POOL_GROUPS):
            cols = slice(gi * POOL_GROUP_W, (gi + 1) * POOL_GROUP_W)
            pooled = (sums[gi][:, cols] / counts[gi] - u[:, cols]).astype(BF16)
            pooled_ref[:, cols] = pooled
            p_ref[:, cols] = (_dot(pooled, pm_ref[gi], NN) * ps_ref[:, cols]).astype(BF16)

    out = jax.ShapeDtypeStruct((s, POOL_W), BF16)
    return pl.pallas_call(
        body, name=name, grid=(s // ts,),
        in_specs=[_row_spec(ts, POOL_W, COL_U // POOL_W),
                  pl.BlockSpec((POOL_HALO, POOL_W), lambda i: (jnp.maximum(i * per - 1, 0), COL_U // POOL_W)),
                  pl.BlockSpec((POOL_GROUPS, POOL_GROUP_W, POOL_GROUP_W), lambda i: (0, 0, 0)),
                  _vec_spec(POOL_W)],
        out_specs=[_row_spec(ts, POOL_W)] * 2, out_shape=[out, out],
        compiler_params=_params("parallel"),
    )(z, z, pool_mix, pool_scale)


def pool_bwd(dp, pooled, pool_mix, pool_scale, *, name, ts=256):
    s = dp.shape[0]
    ts = _tile(s, ts, 16)
    per = ts // POOL_HALO
    n_steps = s // ts
    rows = ts + POOL_HALO

    def body(dp_ref, halo_ref, pooled_ref, pm_ref, ps_ref, du_ref, dps_ref, dpm_ref):
        i = pl.program_id(0)
        dp_main = dp_ref[...]
        halo = jnp.where(i < n_steps - 1, halo_ref[...], 0.0)
        dmixed = jnp.concatenate([dp_main, halo], axis=0) * ps_ref[...]
        counts = _window_counts(i * ts, rows)
        dps_parts = []
        for gi, w in enumerate(POOL_WINDOWS):
            cols = slice(gi * POOL_GROUP_W, (gi + 1) * POOL_GROUP_W)
            dmx = dmixed[:, cols].astype(BF16)
            pooled = pooled_ref[:, cols]
            mixed = _dot(pooled, pm_ref[gi], NN)
            dps_parts.append(jnp.sum(dp_main[:, cols] * mixed, axis=0, keepdims=True))
            dpm_g = _dot(pooled, dmx[:ts], TN)

            @pl.when(i == 0)
            def _():
                dpm_ref[gi] = dpm_g

            @pl.when(i > 0)
            def _():
                dpm_ref[gi] += dpm_g

            dpooled = _dot(dmx, pm_ref[gi], NT)
            run, width = dpooled / counts[gi], 1
            while width < w:
                run = run + pltpu.roll(run, rows - width, 0)
                width *= 2
            du_ref[:, cols] = (run[:ts] - dpooled[:ts]).astype(BF16)
        _accumulate(dps_ref, jnp.concatenate(dps_parts, axis=1))

    return pl.pallas_call(
        body, name=name, grid=(n_steps,),
        in_specs=[_row_spec(ts, POOL_W),
                  pl.BlockSpec((POOL_HALO, POOL_W), lambda i: (jnp.minimum((i + 1) * per, s // POOL_HALO - 1), 0)),
                  _row_spec(ts, POOL_W),
                  pl.BlockSpec((POOL_GROUPS, POOL_GROUP_W, POOL_GROUP_W), lambda i: (0, 0, 0)),
                  _vec_spec(POOL_W)],
        out_specs=[_row_spec(ts, POOL_W), _vec_spec(POOL_W),
                   pl.BlockSpec((POOL_GROUPS, POOL_GROUP_W, POOL_GROUP_W), lambda i: (0, 0, 0))],
        out_shape=[jax.ShapeDtypeStruct((s, POOL_W), BF16), jax.ShapeDtypeStruct((1, POOL_W), F32),
                   jax.ShapeDtypeStruct((POOL_GROUPS, POOL_GROUP_W, POOL_GROUP_W), F32)],
        compiler_params=_params("arbitrary"),
    )(dp, dp, pooled, pool_mix, pool_scale)


def _bucket_one_hot():
    ql = np.arange(BLK)[:, None]
    j = np.arange(2 * BLK)[None, :]
    n = np.clip(BLK + ql - j, 0, None)
    nf = np.maximum(n, 1).astype(np.float32)
    large = MAX_EXACT + (np.log(nf / MAX_EXACT) / np.log(REL_MAX_DIST / MAX_EXACT)
                         * (NUM_BUCKETS - MAX_EXACT)).astype(np.int32)
    large = np.minimum(large, NUM_BUCKETS - 1)
    bucket = np.where(n < MAX_EXACT, n, large).astype(np.int32).reshape(-1)
    return (np.arange(NUM_BUCKETS)[:, None] == bucket[None, :]).astype(np.float32)


def bias_table(rel_bias_t, one_hot, *, name, tc=4096):
    n = one_hot.shape[1]

    def body(rb_ref, oh_ref, o_ref):
        o_ref[...] = _dot(rb_ref[...], oh_ref[...], NN, precision=lax.Precision.HIGHEST)

    return pl.pallas_call(
        body, name=name, grid=(n // tc,),
        in_specs=[pl.BlockSpec((N_HEADS, NUM_BUCKETS), lambda i: (0, 0)), pl.BlockSpec((NUM_BUCKETS, tc), lambda i: (0, i))],
        out_specs=pl.BlockSpec((N_HEADS, tc), lambda i: (0, i)),
        out_shape=jax.ShapeDtypeStruct((N_HEADS, n), F32),
        compiler_params=_params("parallel"),
    )(rel_bias_t, one_hot)


def bias_table_bwd(dbias, one_hot, *, name, tc=4096):
    n = one_hot.shape[1]

    def body(db_ref, oh_ref, o_ref):
        _accumulate(o_ref, _dot(db_ref[...], oh_ref[...], NT, precision=lax.Precision.HIGHEST))

    return pl.pallas_call(
        body, name=name, grid=(n // tc,),
        in_specs=[pl.BlockSpec((N_HEADS, tc), lambda i: (0, i)), pl.BlockSpec((NUM_BUCKETS, tc), lambda i: (0, i))],
        out_specs=pl.BlockSpec((N_HEADS, NUM_BUCKETS), lambda i: (0, 0)),
        out_shape=jax.ShapeDtypeStruct((N_HEADS, NUM_BUCKETS), F32),
        compiler_params=_params("arbitrary"),
    )(dbias, one_hot)


def _lane_half(shape):
    return lax.broadcasted_iota(jnp.int32, shape, len(shape) - 1) < HEAD_DIM


def _half_sums(v, first):
    s0 = jnp.sum(jnp.where(first, v, 0.0), axis=-1, keepdims=True)
    s1 = jnp.sum(jnp.where(first, 0.0, v), axis=-1, keepdims=True)
    return jnp.where(first, s0, s1)


GROUP_ROWS = GQA * BLK


def _band_mask(n):
    ql = lax.broadcasted_iota(jnp.int32, (GROUP_ROWS, 2 * BLK), 0) & (BLK - 1)
    j = lax.broadcasted_iota(jnp.int32, (GROUP_ROWS, 2 * BLK), 1)
    return (j > ql) & (j <= ql + BLK) & ((j >= BLK) | (n > 0))


def _norm_keys(kband):
    first = _lane_half(kband.shape)
    r = lax.rsqrt(_half_sums(kband * kband, first) * (1.0 / HEAD_DIM) + EPS)
    return kband * r


def _kv_lanes(kv, shape):
    return _lane_half(shape) if kv == 0 else jnp.logical_not(_lane_half(shape))


def _stack_group(ref, kv, dtype=F32):
    parts = []
    for g in range(GQA):
        h = kv * GQA + g
        part = ref[:, (h // 2) * LANE:(h // 2 + 1) * LANE].astype(dtype)
        parts.append(pltpu.roll(part, HEAD_DIM, 1) if h % 2 != kv else part)
    stacked = jnp.concatenate(parts, axis=0)
    return jnp.where(_kv_lanes(kv, stacked.shape), stacked, 0.0)


def _unstack_group(ref, kv, stacked):
    for i in range(GQA // 2):
        pair = None
        for sub in range(2):
            g = 2 * i + sub
            part = stacked[g * BLK:(g + 1) * BLK]
            part = pltpu.roll(part, HEAD_DIM, 1) if sub != kv else part
            pair = part if pair is None else pair + part
        col = (kv * GQA // 2 + i) * LANE
        ref[:, col:col + LANE] = pair.astype(BF16)


def _group_logits(q_ref, kv, qg, kn, bias_ref, sink_ref, mask):
    qa = _stack_group(q_ref, kv)
    r = lax.rsqrt(jnp.sum(qa * qa, axis=-1, keepdims=True) * (1.0 / HEAD_DIM) + EPS)
    xh = qa * r
    qn = (xh * qg).astype(BF16)
    bias = bias_ref[kv * GQA:(kv + 1) * GQA].reshape(GROUP_ROWS, 2 * BLK)
    logits = _dot(qn, kn, NT) * (HEAD_DIM ** -0.5) + bias
    p, p_sink = _softmax_with_sink(jnp.where(mask, logits, NEG_INF), sink_ref[kv * GROUP_ROWS:(kv + 1) * GROUP_ROWS])
    return xh, r, qn, p, p_sink


def _softmax_with_sink(logits, sink):
    m = jnp.maximum(jnp.max(logits, axis=-1, keepdims=True), sink)
    e = jnp.exp(logits - m)
    es = jnp.exp(sink - m)
    den = jnp.sum(e, axis=-1, keepdims=True) + es
    return e / den, es / den


def _attn_specs(nb, last):
    cur = lambda n: jnp.minimum(n, last)
    prev = lambda n: jnp.minimum(jnp.maximum(n - 1, 0), last)
    return [pl.BlockSpec((BLK, ATT_W), lambda n: (cur(n), COL_Q // ATT_W)),
            pl.BlockSpec((BLK, KV_W), lambda n: (prev(n), COL_K // KV_W)),
            pl.BlockSpec((BLK, KV_W), lambda n: (cur(n), COL_K // KV_W)),
            pl.BlockSpec((BLK, KV_W), lambda n: (prev(n), COL_V // KV_W)),
            pl.BlockSpec((BLK, KV_W), lambda n: (cur(n), COL_V // KV_W))]


def attn_fwd(z, qg2, kg2, sinks, bias, *, name):
    s = z.shape[0]
    nb = s // BLK

    def body(q_ref, kp_ref, kc_ref, vp_ref, vc_ref, qg_ref, kg_ref, sink_ref, bias_ref, o_ref):
        mask = _band_mask(pl.program_id(0))
        kn = (_norm_keys(jnp.concatenate([kp_ref[...], kc_ref[...]], axis=0)) * kg_ref[...]).astype(BF16)
        vb = jnp.concatenate([vp_ref[...], vc_ref[...]], axis=0).astype(BF16)
        for kv in range(N_KV):
            _, _, _, p, _ = _group_logits(q_ref, kv, qg_ref[...], kn, bias_ref, sink_ref, mask)
            out = _dot(p.astype(BF16), vb, NN)
            _unstack_group(o_ref, kv, jnp.where(_kv_lanes(kv, out.shape), out, 0.0))

    return pl.pallas_call(
        body, name=name, grid=(nb,),
        in_specs=_attn_specs(nb, nb - 1) + [
            _vec_spec(LANE), _vec_spec(LANE), pl.BlockSpec((N_HEADS * BLK, 1), lambda n: (0, 0)),
            pl.BlockSpec((N_HEADS, BLK, 2 * BLK), lambda n: (0, 0, 0))],
        out_specs=pl.BlockSpec((BLK, ATT_W), lambda n: (n, 0)),
        out_shape=jax.ShapeDtypeStruct((s, ATT_W), BF16),
        compiler_params=_params("parallel"),
    )(z, z, z, z, z, qg2, kg2, sinks, bias)


def attn_bwd(z, d_out, qg2, kg2, sinks, bias, *, name):
    s = z.shape[0]
    nb = s // BLK
    scale = HEAD_DIM ** -0.5

    def body(q_ref, kp_ref, kc_ref, vp_ref, vc_ref, do_ref, qg_ref, kg_ref, sink_ref, bias_ref,
             dq_ref, dk_ref, dv_ref, dqg_ref, dkg_ref, dsink_ref, dbias_ref, band_k, band_v, carry_k, carry_v, dsink_rows):
        n = pl.program_id(0)

        @pl.when(n == 0)
        def _():
            dqg_ref[...] = jnp.zeros_like(dqg_ref)
            dkg_ref[...] = jnp.zeros_like(dkg_ref)
            dbias_ref[...] = jnp.zeros_like(dbias_ref)
            carry_k[...] = jnp.zeros_like(carry_k)
            carry_v[...] = jnp.zeros_like(carry_v)
            dsink_rows[...] = jnp.zeros_like(dsink_rows)

        @pl.when(n == nb)
        def _():
            band_k[...] = jnp.zeros_like(band_k)
            band_v[...] = jnp.zeros_like(band_v)

        @pl.when(n < nb)
        def _():
            mask = _band_mask(n)
            kn = (_norm_keys(jnp.concatenate([kp_ref[...], kc_ref[...]], axis=0)) * kg_ref[...]).astype(BF16)
            vb = jnp.concatenate([vp_ref[...], vc_ref[...]], axis=0).astype(BF16)
            dkn = jnp.zeros((2 * BLK, KV_W), F32)
            dvb = jnp.zeros((2 * BLK, KV_W), F32)
            dqg = jnp.zeros((1, LANE), F32)
            for kv in range(N_KV):
                rows = slice(kv * GROUP_ROWS, (kv + 1) * GROUP_ROWS)
                xh, r, qn, p, p_sink = _group_logits(q_ref, kv, qg_ref[...], kn, bias_ref, sink_ref, mask)
                do = _stack_group(do_ref, kv).astype(BF16)
                dp = _dot(do, vb, NT)
                delta = jnp.sum(p * dp, axis=-1, keepdims=True)
                ds = p * (dp - delta)
                dsink_rows[rows] -= p_sink * delta
                dbias_ref[kv * GQA:(kv + 1) * GQA] += ds.reshape(GQA, BLK, 2 * BLK)
                ds16 = ds.astype(BF16)
                dqn = jnp.where(_kv_lanes(kv, xh.shape), _dot(ds16, kn, NN) * scale, 0.0)
                dkn = dkn + _dot(ds16, qn, TN) * scale
                dvb = dvb + _dot(p.astype(BF16), do, TN)
                dqg = dqg + jnp.sum(dqn * xh, axis=0, keepdims=True)
                dxh = dqn * qg_ref[...]
                _unstack_group(dq_ref, kv, r * (dxh - xh * (jnp.sum(dxh * xh, axis=-1, keepdims=True) * (1.0 / HEAD_DIM))))
            band_k[...] = dkn
            band_v[...] = dvb
            dqg_ref[...] += dqg

        dkn_prev = carry_k[...] + band_k[:BLK]
        dv_ref[...] = (carry_v[...] + band_v[:BLK]).astype(BF16)
        carry_k[...] = band_k[BLK:]
        carry_v[...] = band_v[BLK:]
        kp = kp_ref[...]
        first = _lane_half(kp.shape)
        r = lax.rsqrt(_half_sums(kp * kp, first) * (1.0 / HEAD_DIM) + EPS)
        xh = kp * r
        dkg_ref[...] += jnp.sum(dkn_prev * xh, axis=0, keepdims=True)
        dxh = dkn_prev * kg_ref[...]
        dk_ref[...] = (r * (dxh - xh * (_half_sums(dxh * xh, first) * (1.0 / HEAD_DIM)))).astype(BF16)

        @pl.when(n == nb)
        def _():
            dqg_ref[...] += pltpu.roll(dqg_ref[...], HEAD_DIM, 1)
            dkg_ref[...] += pltpu.roll(dkg_ref[...], HEAD_DIM, 1)
            lane16 = lax.broadcasted_iota(jnp.int32, (1, N_HEADS), 1)
            dsink = jnp.zeros((1, N_HEADS), F32)
            for h in range(N_HEADS):
                dsink = dsink + jnp.where(lane16 == h, jnp.sum(dsink_rows[h * BLK:(h + 1) * BLK], axis=0, keepdims=True), 0.0)
            dsink_ref[...] = dsink

    last = nb - 1
    cur = lambda n: jnp.minimum(n, last)
    back = lambda n: jnp.maximum(n - 1, 0)
    full3 = pl.BlockSpec((N_HEADS, BLK, 2 * BLK), lambda n: (0, 0, 0))
    return pl.pallas_call(
        body, name=name, grid=(nb + 1,),
        in_specs=_attn_specs(nb, last) + [
            pl.BlockSpec((BLK, ATT_W), lambda n: (cur(n), 0)),
            _vec_spec(LANE), _vec_spec(LANE), pl.BlockSpec((N_HEADS * BLK, 1), lambda n: (0, 0)), full3],
        out_specs=[pl.BlockSpec((BLK, ATT_W), lambda n: (cur(n), 0)),
                   pl.BlockSpec((BLK, KV_W), lambda n: (back(n), 0)),
                   pl.BlockSpec((BLK, KV_W), lambda n: (back(n), 0)),
                   _vec_spec(LANE), _vec_spec(LANE), _vec_spec(N_HEADS), full3],
        out_shape=[jax.ShapeDtypeStruct((s, ATT_W), BF16), jax.ShapeDtypeStruct((s, KV_W), BF16),
                   jax.ShapeDtypeStruct((s, KV_W), BF16), jax.ShapeDtypeStruct((1, LANE), F32),
                   jax.ShapeDtypeStruct((1, LANE), F32), jax.ShapeDtypeStruct((1, N_HEADS), F32),
                   jax.ShapeDtypeStruct((N_HEADS, BLK, 2 * BLK), F32)],
        scratch_shapes=[pltpu.VMEM((2 * BLK, KV_W), F32), pltpu.VMEM((2 * BLK, KV_W), F32),
                        pltpu.VMEM((BLK, KV_W), F32), pltpu.VMEM((BLK, KV_W), F32), pltpu.VMEM((N_HEADS * BLK, 1), F32)],
        compiler_params=_params("arbitrary"),
    )(z, z, z, z, z, d_out, qg2, kg2, sinks, bias)


def _adamw(w, g, m, v):
    m = ADAM_B1 * m + (1.0 - ADAM_B1) * g
    v = ADAM_B2 * v + (1.0 - ADAM_B2) * (g * g)
    m_hat = m / (1.0 - ADAM_B1 ** ADAM_STEP)
    v_hat = v / (1.0 - ADAM_B2 ** ADAM_STEP)
    delta = -ADAM_LR * (m_hat / (jnp.sqrt(v_hat) + ADAM_EPS) + ADAM_WD * w)
    return delta, m, v


def ada_fwd(c16, w, b, *, name, tn=768):
    k, n = w.shape
    tn = _tile(n, tn, LANE)

    def body(c_ref, w_ref, b_ref, o_ref):
        cc = c_ref[...]
        o_ref[...] = _dot((cc * _sigmoid(cc)).astype(BF16), w_ref[...].astype(BF16), NN) + b_ref[...]

    return pl.pallas_call(
        body, name=name, grid=(n // tn,),
        in_specs=[pl.BlockSpec((c16.shape[0], k), lambda j: (0, 0)), pl.BlockSpec((k, tn), lambda j: (0, j)),
                  pl.BlockSpec((1, tn), lambda j: (0, j))],
        out_specs=pl.BlockSpec((c16.shape[0], tn), lambda j: (0, j)),
        out_shape=jax.ShapeDtypeStruct((c16.shape[0], n), F32),
        compiler_params=_params("parallel"),
    )(c16, w, b)


def ada_bwd_adamw(c_t, dmod, w, m, v, *, name, tn=256):
    k, n = w.shape
    tn = _tile(n, tn, LANE)

    def body(c_ref, d_ref, w_ref, m_ref, v_ref, g_ref, dl_ref, mo_ref, vo_ref):
        cc = c_ref[...]
        g = _dot((cc * _sigmoid(cc)).astype(BF16), d_ref[...].astype(BF16), NN)
        g_ref[...] = g
        dl_ref[...], mo_ref[...], vo_ref[...] = _adamw(w_ref[...], g, m_ref[...], v_ref[...])

    blk = pl.BlockSpec((k, tn), lambda j: (0, j))
    out = jax.ShapeDtypeStruct((k, n), F32)
    return pl.pallas_call(
        body, name=name, grid=(n // tn,),
        in_specs=[pl.BlockSpec((k, LANE), lambda j: (0, 0)), pl.BlockSpec((LANE, tn), lambda j: (0, j)), blk, blk, blk],
        out_specs=[blk] * 4, out_shape=[out] * 4,
        compiler_params=_params("parallel"),
    )(c_t, dmod, w, m, v)


def adamw_from_parts(parts, w, m, v, *, name):
    r, c = w.shape
    tr = _tile(r, max(16, (256 * 1024) // c), 16)

    def body(p_ref, w_ref, m_ref, v_ref, g_ref, dl_ref, mo_ref, vo_ref):
        g = p_ref[0].astype(F32)
        for d in range(1, N_DEV):
            g = g + p_ref[d].astype(F32)
        g_ref[...] = g
        dl_ref[...], mo_ref[...], vo_ref[...] = _adamw(w_ref[...], g, m_ref[...], v_ref[...])

    blk = pl.BlockSpec((tr, c), lambda i: (i, 0))
    out = jax.ShapeDtypeStruct((r, c), F32)
    return pl.pallas_call(
        body, name=name, grid=(r // tr,),
        in_specs=[pl.BlockSpec((N_DEV, tr, c), lambda i: (0, i, 0)), blk, blk, blk],
        out_specs=[blk] * 4, out_shape=[out] * 4,
        compiler_params=_params("parallel"),
    )(parts, w, m, v)


def _ffn_fwd(x_in, g, shift, scale, gate, wgu3, get_wd, token, tag, target=None):
    h = norm_modulate(x_in, g, shift, scale, token, name=f"{tag}_norm")
    gg, uu, act = ffn_up(h, wgu3, name=f"{tag}_up")
    wd = get_wd(gg)
    if target is None:
        x_out, f = mm_nn_residual(act, wd, x_in, gate, 0.5, name=f"{tag}_down")
    else:
        dy, f, loss_row = mm_nn_residual_loss(act, wd, x_in, gate, 0.5, target, name=f"{tag}_down_loss")
        x_out = (dy, loss_row)
    return x_out, (h, gg, uu, act, f), wd


def _ffn_bwd(dx_out, x_in, g, scale, gate, wgu3, wd, saved, token, scatter, split, tag):
    h, gg, uu, act, f = saved
    df, dgate = gate_bwd(dx_out, f, gate, 0.5, name=f"{tag}_gate_bwd")
    dwd = mm_tn(act, df, name=f"{tag}_dwd").reshape(N_DEV, -1, D_MODEL)
    if split:
        token = scatter([f"w_{tag}_down"], [dwd], f"scatter_{tag}_down")
    dgg, duu = ffn_dgu(df, wd, gg, uu, token, name=f"{tag}_dgu")
    dwgu = mm_tn_halves(h, dgg, duu, N_DEV, name=f"{tag}_dwgu")
    if split:
        token = scatter([f"w_{tag}_gu"], [dwgu], f"scatter_{tag}_gu")
    else:
        token = scatter([f"w_{tag}_gu", f"w_{tag}_down"], [dwgu, dwd], f"scatter_{tag}")
    dh = mm_nt_halves(dgg, duu, wgu3, token, name=f"{tag}_dh")
    dx_in, dshift, dscale, dg = norm_modulate_bwd(x_in, g, scale, dh, dx_out, name=f"{tag}_norm_bwd")
    return dx_in, (dshift, dscale, dgate), dg


def kernel(x, c, w_ada, b_ada, g_ffn1, w_ffn1_gu, w_ffn1_down, g_mix, w_in, pool_mix, pool_scale, w_pool_up, q_gain, k_gain, sinks, rel_bias, w_attn_up, w_o, g_ffn2, w_ffn2_gu, w_ffn2_down, loss_target, m_w_ada, m_b_ada, m_g_ffn1, m_w_ffn1_gu, m_w_ffn1_down, m_g_mix, m_w_in, m_pool_mix, m_pool_scale, m_w_pool_up, m_q_gain, m_k_gain, m_sinks, m_rel_bias, m_w_attn_up, m_w_o, m_g_ffn2, m_w_ffn2_gu, m_w_ffn2_down, v_w_ada, v_b_ada, v_g_ffn1, v_w_ffn1_gu, v_w_ffn1_down, v_g_mix, v_w_in, v_pool_mix, v_pool_scale, v_w_pool_up, v_q_gain, v_k_gain, v_sinks, v_rel_bias, v_w_attn_up, v_w_o, v_g_ffn2, v_w_ffn2_gu, v_w_ffn2_down):
    me = _slot(_mesh_pos())
    x0, target = x[0], loss_target[0]
    n_ada = w_ada.shape[2]
    pm_rows = pool_mix.shape[2]

    big = dict(w_ffn1_gu=(w_ffn1_gu, m_w_ffn1_gu, v_w_ffn1_gu), w_ffn1_down=(w_ffn1_down, m_w_ffn1_down, v_w_ffn1_down),
               w_in=(w_in, m_w_in, v_w_in), pool_mix=(pool_mix, m_pool_mix, v_pool_mix),
               w_pool_up=(w_pool_up, m_w_pool_up, v_w_pool_up), w_attn_up=(w_attn_up, m_w_attn_up, v_w_attn_up),
               w_o=(w_o, m_w_o, v_w_o), w_ffn2_gu=(w_ffn2_gu, m_w_ffn2_gu, v_w_ffn2_gu),
               w_ffn2_down=(w_ffn2_down, m_w_ffn2_down, v_w_ffn2_down))
    shard2d = {k: (POOL_GROUPS * pm_rows, POOL_GROUP_W) if k == "pool_mix" else t[0].shape[1:] for k, t in big.items()}
    mix_keys = ["w_in", "pool_mix", "w_pool_up", "w_attn_up", "w_o"]
    ffn2_keys = ["w_ffn2_gu", "w_ffn2_down"]

    def shard_bf16(k, token=None):
        w = big[k][0].reshape(shard2d[k])
        return (w if token is None else w + token[0, 0]).astype(BF16)

    def own_slot(block, shape):
        return lax.dynamic_update_slice(lax.empty(shape, block.dtype), block[None], (me, 0, 0))

    def start_gather(keys, token, tag):
        shards = [shard_bf16(k, token) for k in keys]
        return exchange_start(shards, [own_slot(s, (N_DEV,) + s.shape) for s in shards], slotted=False, name=f"{tag}_start")

    pending = []

    def scatter(keys, grads, tag):
        lands = [own_slot(lax.dynamic_index_in_dim(g, me, 0, keepdims=False), g.shape) for g in grads]
        handle = exchange_start(grads, lands, slotted=True, name=f"{tag}_start")
        pending.append((keys, handle, tag))
        return handle[4]

    c_all, _ = all_gather_small(c.reshape(D_MODEL // LANE, LANE), "gather_c")
    c_all = c_all.reshape(N_DEV, D_MODEL)
    c16 = jnp.pad(c_all, ((0, 16 - N_DEV), (0, 0)))
    b_mine = lax.dynamic_slice(b_ada, (0, me * n_ada), (1, n_ada))
    mod_cols = ada_fwd(c16, w_ada[0], b_mine, name="ada_fwd")[:N_DEV]
    mod_all, token = all_gather_small(mod_cols.reshape(-1, LANE), "gather_mod")
    mod = lax.dynamic_index_in_dim(mod_all.reshape(N_DEV, N_DEV, n_ada), me, axis=1, keepdims=False)
    mod = mod.reshape(N_MOD, 1, D_MODEL)

    wgu1, token = all_gather_hbm([shard_bf16("w_ffn1_gu", token)], "gather_ffn1_gu")
    gather_wd1 = start_gather(["w_ffn1_down"], token, "gather_ffn1_down")
    gather_mix = start_gather(mix_keys, gather_wd1[4], "gather_mix")
    gather_ffn2 = start_gather(ffn2_keys, gather_mix[4], "gather_ffn2")
    token = gather_ffn2[4]

    def get_wd1(after):
        return exchange_wait(gather_wd1, after, slotted=False, name="gather_ffn1_down_wait")[0].reshape(-1, D_MODEL)

    x1, saved1, wd1 = _ffn_fwd(x0, g_ffn1, mod[0], mod[1], mod[2], wgu1, get_wd1, token, "ffn1")
    gathered = dict(zip(mix_keys, exchange_wait(gather_mix, x1, slotted=False, name="gather_mix_wait")))
    def columns_out(blocks):
        return jnp.transpose(blocks, (1, 0, 2)).reshape(blocks.shape[1], -1)

    def columns_in(full):
        return jnp.transpose(full.reshape(full.shape[0], N_DEV, -1), (1, 0, 2))

    w_in_full = columns_out(gathered["w_in"])
    w_in_z = jnp.concatenate([w_in_full[:, s:s + w] for s, w in (W_IN_PARTS[p] for p in Z_ORDER)], axis=1)
    pm_full = jnp.transpose(gathered["pool_mix"].reshape(N_DEV, POOL_GROUPS, pm_rows, POOL_GROUP_W),
                            (1, 0, 2, 3)).reshape(POOL_GROUPS, POOL_GROUP_W, POOL_GROUP_W)
    wpu, wau = columns_out(gathered["w_pool_up"]), columns_out(gathered["w_attn_up"])
    wo_full = gathered["w_o"].reshape(D_MODEL, D_MODEL)
    h2 = norm_modulate(x1, g_mix, mod[3], mod[4], token, name="mix_norm")
    z = mm_nn(h2, w_in_z, out_dtype=F32, name="mix_in")
    pooled, p_act = pool_fwd(z, pm_full, pool_scale, name="pool_fwd")
    y_pool = mm_nn(p_act, wpu, out_dtype=F32, name="pool_up")
    one_hot = jnp.asarray(_bucket_one_hot())
    bias = bias_table(rel_bias.T, one_hot, name="bias_table").reshape(N_HEADS, BLK, 2 * BLK)
    qg2, kg2 = jnp.tile(q_gain, (1, 2)), jnp.tile(k_gain, (1, 2))
    sink_rows = jnp.repeat(sinks[0], BLK).reshape(N_HEADS * BLK, 1)
    attn = attn_fwd(z, qg2, kg2, sink_rows, bias, name="attn_fwd")
    y_attn = mm_nn(attn, wau, out_dtype=F32, name="attn_up")
    merged = merge_fwd(z, y_pool, y_attn, name="merge_fwd")
    x2, o_act = mm_nn_residual(merged, wo_full, x1, mod[5], 1.0, name="mix_out")
    wgu2, wd2 = exchange_wait(gather_ffn2, x2, slotted=False, name="gather_ffn2_wait")
    wd2 = wd2.reshape(-1, D_MODEL)
    (dy, loss_row), saved2, _ = _ffn_fwd(x2, g_ffn2, mod[6], mod[7], mod[8], wgu2, lambda after: wd2, token, "ffn2", target)
    loss = lax.psum(loss_row[0, 0], ("x", "y", "c"))

    dx2, dmod3, dg_ffn2 = _ffn_bwd(dy, x2, g_ffn2, mod[7], mod[8], wgu2, wd2, saved2, token, scatter, False, "ffn2")
    d_o, dgate2 = gate_bwd(dx2, o_act, mod[5], 1.0, name="mix_gate_bwd")
    dwo = mm_tn(merged, d_o, name="mix_dwo").reshape(N_DEV, -1, D_MODEL)
    dmerged = mm_nt(d_o, wo_full, token, out_dtype=F32, name="mix_dmerged")
    dyp, dya, dga, dgb = merge_bwd(dmerged, z, y_pool, y_attn, name="merge_bwd")
    dwpu = mm_tn(p_act, dyp, name="pool_dwup")
    dp_act = mm_nt(dyp, wpu, token, out_dtype=F32, name="pool_dp")
    du, dpool_scale, dpm = pool_bwd(dp_act, pooled, pm_full, pool_scale, name="pool_bwd")
    dwau = mm_tn(attn, dya, name="attn_dwup")
    dattn = mm_nt(dya, wau, token, out_dtype=BF16, name="attn_dout")
    dq, dk, dv, dqg, dkg, dsinks, dbias = attn_bwd(z, dattn, qg2, kg2, sink_rows, bias, name="attn_bwd")
    drel = bias_table_bwd(dbias.reshape(N_HEADS, -1), one_hot, name="bias_table_bwd").T
    dz_parts = dict(ga=dga, gb=dgb, u=du, q=dq, k=dk, v=dv)
    dz = jnp.concatenate([dz_parts[p] for p in Z_ORDER], axis=1)
    dwin_z = mm_tn(h2, dz, name="mix_dwin")
    z_start = dict(zip(Z_ORDER, np.cumsum([0] + [W_IN_PARTS[p][1] for p in Z_ORDER[:-1]])))
    dwin = jnp.concatenate([dwin_z[:, z_start[p]:z_start[p] + W_IN_PARTS[p][1]] for p in W_IN_PARTS], axis=1)
    mix_grads = dict(w_in=columns_in(dwin),
                     pool_mix=jnp.transpose(dpm.astype(BF16).reshape(POOL_GROUPS, N_DEV, pm_rows, POOL_GROUP_W),
                                            (1, 0, 2, 3)).reshape(N_DEV, POOL_GROUPS * pm_rows, POOL_GROUP_W),
                     w_pool_up=columns_in(dwpu), w_attn_up=columns_in(dwau), w_o=dwo)
    token = scatter(mix_keys, [mix_grads[k] for k in mix_keys], "scatter_mix")
    dh2 = mm_nt(dz, w_in_z, token, out_dtype=F32, name="mix_dh")
    dx1, dsh2, dsc2, dg_mix = norm_modulate_bwd(x1, g_mix, mod[4], dh2, dx2, name="mix_norm_bwd")
    dx0, dmod1, dg_ffn1 = _ffn_bwd(dx1, x0, g_ffn1, mod[1], mod[2], wgu1, wd1, saved1, token, scatter, True, "ffn1")

    small = [("b_ada", b_ada, m_b_ada, v_b_ada, jnp.concatenate(list(dmod1 + (dsh2, dsc2, dgate2) + dmod3), axis=1)),
             ("g_ffn1", g_ffn1, m_g_ffn1, v_g_ffn1, dg_ffn1), ("g_mix", g_mix, m_g_mix, v_g_mix, dg_mix),
             ("g_ffn2", g_ffn2, m_g_ffn2, v_g_ffn2, dg_ffn2),
             ("pool_scale", pool_scale, m_pool_scale, v_pool_scale, dpool_scale),
             ("q_gain", q_gain, m_q_gain, v_q_gain, dqg[:, :HEAD_DIM]), ("k_gain", k_gain, m_k_gain, v_k_gain, dkg[:, :HEAD_DIM]),
             ("sinks", sinks, m_sinks, v_sinks, dsinks), ("rel_bias", rel_bias, m_rel_bias, v_rel_bias, drel)]
    n_small = sum(t[1].size for t in small)
    pad = -n_small % (8 * LANE)
    flat = lambda arrs: jnp.pad(jnp.concatenate([a.reshape(1, -1) for a in arrs], axis=1), ((0, 0), (0, pad)))
    small_parts, _ = all_gather_small(flat([t[4] for t in small]).reshape(-1, LANE), "gather_small_grads")
    small_parts = small_parts.reshape(N_DEV, 1, n_small + pad)
    sg, sd, sm, sv = adamw_from_parts(small_parts, flat([t[1] for t in small]), flat([t[2] for t in small]),
                                      flat([t[3] for t in small]), name="adamw_small")

    dmod_all = small_parts[:, 0, :N_MOD * D_MODEL]
    dmod_mine = lax.dynamic_slice(dmod_all, (0, me * n_ada), (N_DEV, n_ada))
    c_t = jnp.pad(c_all.T, ((0, 0), (0, LANE - N_DEV)))
    ada_out = ada_bwd_adamw(c_t, jnp.pad(dmod_mine, ((0, LANE - N_DEV), (0, 0))), w_ada[0], m_w_ada[0], v_w_ada[0],
                            name="ada_bwd_adamw")

    res = {"w_ada": [o[None] for o in ada_out]}
    after = ada_out[0]
    for keys, handle, tag in pending:
        parts = exchange_wait(handle, after, slotted=True, name=f"{tag}_wait")
        for k, part in zip(keys, parts):
            w_, m_, v_ = big[k]
            outs = adamw_from_parts(part, w_.reshape(shard2d[k]), m_.reshape(shard2d[k]), v_.reshape(shard2d[k]),
                                    name=f"adamw_{k}")
            res[k] = [o.reshape(w_.shape) for o in outs]
            after = outs[0]
    off = 0
    for k, w_, _, _, _ in small:
        res[k] = [o[0, off:off + w_.size].reshape(w_.shape) for o in (sg, sd, sm, sv)]
        off += w_.size
    order = ["w_ada", "b_ada", "g_ffn1", "w_ffn1_gu", "w_ffn1_down", "g_mix", "w_in", "pool_mix", "pool_scale",
             "w_pool_up", "q_gain", "k_gain", "sinks", "rel_bias", "w_attn_up", "w_o", "g_ffn2", "w_ffn2_gu", "w_ffn2_down"]
    return (loss, dx0[None], *[res[k][0] for k in order], *[res[k][1] for k in order],
            *[res[k][2] for k in order], *[res[k][3] for k in order])
```

```python
import functools

import numpy as np
import jax
import jax.numpy as jnp
from jax import lax
from jax.experimental import pallas as pl
from jax.experimental.pallas import tpu as pltpu

F32, BF16 = jnp.float32, jnp.bfloat16
MESH_ID = pl.DeviceIdType.MESH

N_DEV = 8
D_MODEL = 2048
N_MOD = 9
POOL_WINDOWS = (2, 4, 8, 16)
POOL_GROUPS = 4
POOL_GROUP_W = D_MODEL // 8
POOL_W = POOL_GROUPS * POOL_GROUP_W
POOL_HALO = 16
HEAD_DIM = 64
N_HEADS = 16
N_KV = 2
GQA = N_HEADS // N_KV
BLK = 128
NUM_BUCKETS = 32
MAX_EXACT = 16
REL_MAX_DIST = 128
EPS = 1e-6
NEG_INF = -1e30
ATT_W = N_HEADS * HEAD_DIM
KV_W = N_KV * HEAD_DIM
IN_W = POOL_W + ATT_W + 2 * KV_W + 2 * D_MODEL
W_IN_PARTS = dict(u=(0, POOL_W), q=(POOL_W, ATT_W), k=(POOL_W + ATT_W, KV_W), v=(POOL_W + ATT_W + KV_W, KV_W),
                  ga=(POOL_W + ATT_W + 2 * KV_W, D_MODEL), gb=(POOL_W + ATT_W + 2 * KV_W + D_MODEL, D_MODEL))
Z_ORDER = ("ga", "gb", "u", "q", "k", "v")
COL_GA, COL_GB, COL_U, COL_Q, COL_K, COL_V = 0, D_MODEL, 2 * D_MODEL, 2 * D_MODEL + POOL_W, 2 * D_MODEL + POOL_W + ATT_W, 2 * D_MODEL + POOL_W + ATT_W + KV_W
LANE = 128

ADAM_LR = 0.001
ADAM_B1 = 0.9
ADAM_B2 = 0.999
ADAM_EPS = 1e-08
ADAM_WD = 0.01
ADAM_STEP = 10

NN = ((1,), (0,))
NT = ((1,), (1,))
TN = ((0,), (0,))


def _dot(a, b, dims, precision=None):
    return lax.dot_general(a, b, (dims, ((), ())), preferred_element_type=F32, precision=precision)


def _tile(n, pref, unit):
    t = (min(pref, n) // unit) * unit
    while t >= unit:
        if n % t == 0:
            return t
        t -= unit
    return n


def _params(*sem):
    return pltpu.CompilerParams(dimension_semantics=sem)


def _sigmoid(x):
    return 1.0 / (1.0 + jnp.exp(-x))


def _mesh_pos():
    return lax.axis_index("x"), lax.axis_index("y"), lax.axis_index("c")


def _slot(p):
    return 4 * p[0] + 2 * p[1] + p[2]


def all_gather_small(x_shard, name):
    m_per, n = x_shard.shape

    def body(x_ref, out_ref, token, send_sems, recv_sems, local_sem):
        x, y, c = _mesh_pos()
        me, sibling = (x, y, c), (x, y, 1 - c)
        chips = [(1 - x, y), (x, 1 - y), (1 - x, 1 - y)]
        token[...] = jnp.zeros_like(token)

        def rows(p):
            return out_ref.at[pl.ds(_slot(p) * m_per, m_per), :]

        def copy(k, block, to, src=None):
            return pltpu.make_async_remote_copy(
                src_ref=rows(block) if src is None else src, dst_ref=rows(block),
                send_sem=send_sems.at[k], recv_sem=recv_sems.at[k], device_id=to, device_id_type=MESH_ID)

        mine = pltpu.make_async_copy(x_ref, rows(me), local_sem)
        mine.start()
        first = [copy(0, me, sibling, src=x_ref)]
        first += [copy(1 + j, me, (*chip, c), src=x_ref) for j, chip in enumerate(chips)]
        for cp in first:
            cp.start()
        passed = [copy(4 + j, (*chip, c), sibling) for j, chip in enumerate(chips)]
        for j, chip in enumerate(chips):
            copy(1 + j, (*chip, c), me).wait_recv()
            passed[j].start()
        copy(0, sibling, me).wait_recv()
        for j, chip in enumerate(chips):
            copy(4 + j, (*chip, 1 - c), me).wait_recv()
        for cp in first + passed:
            cp.wait_send()
        mine.wait()

    return pl.pallas_call(
        body, name=name,
        out_shape=[jax.ShapeDtypeStruct((N_DEV * m_per, n), x_shard.dtype), jax.ShapeDtypeStruct((8, LANE), F32)],
        in_specs=[pl.BlockSpec(memory_space=pltpu.VMEM)],
        out_specs=[pl.BlockSpec(memory_space=pltpu.VMEM)] * 2,
        scratch_shapes=[pltpu.SemaphoreType.DMA((7,)), pltpu.SemaphoreType.DMA((7,)), pltpu.SemaphoreType.DMA],
    )(x_shard)


def all_gather_hbm(shards, name):
    n_arr = len(shards)

    def body(*refs):
        ins, outs, token = refs[:n_arr], refs[n_arr:2 * n_arr], refs[2 * n_arr]
        send_sems, recv_sems, local_sems = refs[2 * n_arr + 1:]
        x, y, c = _mesh_pos()
        me, sibling = (x, y, c), (x, y, 1 - c)
        chips = [(1 - x, y), (x, 1 - y), (1 - x, 1 - y)]
        token[...] = jnp.zeros_like(token)

        def copy(a, k, block, to, src=None):
            dst = outs[a].at[_slot(block)]
            return pltpu.make_async_remote_copy(
                src_ref=dst if src is None else src, dst_ref=dst,
                send_sem=send_sems.at[7 * a + k], recv_sem=recv_sems.at[7 * a + k],
                device_id=to, device_id_type=MESH_ID)

        mine = [pltpu.make_async_copy(ins[a], outs[a].at[_slot(me)], local_sems.at[a]) for a in range(n_arr)]
        for cp in mine:
            cp.start()
        first = []
        for a in range(n_arr):
            first.append(copy(a, 0, me, sibling, src=ins[a]))
            first += [copy(a, 1 + j, me, (*chip, c), src=ins[a]) for j, chip in enumerate(chips)]
        for cp in first:
            cp.start()
        passed = []
        for a in range(n_arr):
            for j, chip in enumerate(chips):
                copy(a, 1 + j, (*chip, c), me).wait_recv()
                fwd = copy(a, 4 + j, (*chip, c), sibling)
                fwd.start()
                passed.append(fwd)
        for a in range(n_arr):
            copy(a, 0, sibling, me).wait_recv()
            for j, chip in enumerate(chips):
                copy(a, 4 + j, (*chip, 1 - c), me).wait_recv()
        for cp in first + passed:
            cp.wait_send()
        for cp in mine:
            cp.wait()

    any_spec = pl.BlockSpec(memory_space=pl.ANY)
    return pl.pallas_call(
        body, name=name,
        out_shape=[jax.ShapeDtypeStruct((N_DEV,) + s.shape, s.dtype) for s in shards] + [jax.ShapeDtypeStruct((8, LANE), F32)],
        in_specs=[any_spec] * n_arr, out_specs=[any_spec] * n_arr + [pl.BlockSpec(memory_space=pltpu.VMEM)],
        scratch_shapes=[pltpu.SemaphoreType.DMA((7 * n_arr,)), pltpu.SemaphoreType.DMA((7 * n_arr,)),
                        pltpu.SemaphoreType.DMA((n_arr,))],
    )(*shards)


def _peer_list(x, y, c):
    return [((1 - x) if k & 4 else x, (1 - y) if k & 2 else y, (1 - c) if k & 1 else c) for k in range(1, N_DEV)]


def _exchange_copies(srcs, lands, send_sems, recv_sems, slotted, arriving):
    x, y, c = _mesh_pos()
    me = _slot((x, y, c))
    copies = []
    for a in range(len(srcs)):
        for k, peer in enumerate(_peer_list(x, y, c)):
            copies.append(pltpu.make_async_remote_copy(
                src_ref=srcs[a].at[_slot(peer)] if slotted else srcs[a],
                dst_ref=lands[a].at[_slot(peer) if arriving else me],
                send_sem=send_sems.at[7 * a + k], recv_sem=recv_sems.at[7 * a + k],
                device_id=peer, device_id_type=MESH_ID))
    return copies


HBM_SPEC = pl.BlockSpec(memory_space=pltpu.HBM)
SEM_SPEC = pl.BlockSpec(memory_space=pltpu.SEMAPHORE)
DATAFLOW = pltpu.SideEffectType.DATAFLOW_SIDE_EFFECTING


def exchange_start(srcs, lands, *, slotted, name):
    n = len(srcs)

    def body(*refs):
        ins = refs[:2 * n]
        send_sems, recv_sems = refs[2 * n], refs[2 * n + 1]
        token = refs[-1]
        for cp in _exchange_copies(ins[:n], ins[n:], send_sems, recv_sems, slotted, False):
            cp.start()
        token[...] = jnp.zeros_like(token)

    operands = [pltpu.with_memory_space_constraint(v, pltpu.HBM) for v in list(srcs) + list(lands)]
    out = pl.pallas_call(
        body, name=name,
        out_shape=(pltpu.SemaphoreType.DMA((7 * n,)), pltpu.SemaphoreType.DMA((7 * n,)),
                   *[pltpu.HBM(v.shape, v.dtype) for v in operands], jax.ShapeDtypeStruct((8, LANE), F32)),
        in_specs=[HBM_SPEC] * (2 * n),
        out_specs=(SEM_SPEC, SEM_SPEC, *[HBM_SPEC] * (2 * n), pl.BlockSpec(memory_space=pltpu.VMEM)),
        input_output_aliases={i: 2 + i for i in range(2 * n)},
        compiler_params=pltpu.CompilerParams(has_side_effects=DATAFLOW),
    )(*operands)
    return out[0], out[1], list(out[2:2 + n]), list(out[2 + n:2 + 2 * n]), out[-1]


def exchange_wait(handle, after, *, slotted, name):
    send_sems, recv_sems, srcs, lands, _ = handle
    n = len(srcs)

    def body(*refs):
        ins = refs[:2 * n]
        for cp in _exchange_copies(ins[:n], ins[n:], refs[2 * n], refs[2 * n + 1], slotted, True):
            cp.wait_send()
            cp.wait_recv()

    out = pl.pallas_call(
        body, name=name,
        out_shape=tuple(pltpu.HBM(v.shape, v.dtype) for v in srcs + lands),
        in_specs=[HBM_SPEC] * (2 * n) + [SEM_SPEC, SEM_SPEC, pl.BlockSpec(memory_space=pl.ANY)],
        out_specs=[HBM_SPEC] * (2 * n),
        input_output_aliases={i: i for i in range(2 * n)},
        compiler_params=pltpu.CompilerParams(has_side_effects=DATAFLOW),
    )(*srcs, *lands, send_sems, recv_sems, after)
    return list(out[n:])


VMEM_BLOCK_BUDGET = 46 * 2 ** 20
ROW_TILE, COL_TILE = 1024, 1408
ACC_BUDGET = 12 * 2 ** 20


def _mm_tiles(m, n, row_bytes, col_bytes, elem_bytes):
    tm, tn = _tile(m, ROW_TILE, 16), _tile(n, COL_TILE, LANE)
    while 2 * (tm * row_bytes + tn * col_bytes + tm * tn * elem_bytes) > VMEM_BLOCK_BUDGET:
        narrower = _tile(n, max(tn - LANE, LANE), LANE)
        if tn > 512 and narrower < tn:
            tn = narrower
        else:
            tm //= 2
    return tm, tn


def mm_nn(a, w, *, out_dtype, name):
    m, k = a.shape
    n = w.shape[1]
    tm, tn = _mm_tiles(m, n, 2 * k, 2 * k, jnp.dtype(out_dtype).itemsize)

    def body(a_ref, w_ref, o_ref):
        o_ref[...] = _dot(a_ref[...], w_ref[...], NN).astype(o_ref.dtype)

    return pl.pallas_call(
        body, name=name, grid=(n // tn, m // tm),
        in_specs=[pl.BlockSpec((tm, k), lambda j, i: (i, 0)), pl.BlockSpec((k, tn), lambda j, i: (0, j))],
        out_specs=pl.BlockSpec((tm, tn), lambda j, i: (i, j)),
        out_shape=jax.ShapeDtypeStruct((m, n), out_dtype),
        compiler_params=_params("parallel", "parallel"),
    )(a, w)


def mm_nn_residual(a, w, x_in, gate, coef, *, name):
    m, k = a.shape
    n = w.shape[1]
    tm, tn = _mm_tiles(m, n, 2 * k, 2 * k, 4 + 4 + 2)

    def body(a_ref, w_ref, x_ref, g_ref, o_ref, f_ref):
        f = _dot(a_ref[...], w_ref[...], NN)
        f_ref[...] = f.astype(BF16)
        o_ref[...] = x_ref[...] + (coef * g_ref[...]) * f

    return pl.pallas_call(
        body, name=name, grid=(n // tn, m // tm),
        in_specs=[pl.BlockSpec((tm, k), lambda j, i: (i, 0)), pl.BlockSpec((k, tn), lambda j, i: (0, j)),
                  pl.BlockSpec((tm, tn), lambda j, i: (i, j)), pl.BlockSpec((1, tn), lambda j, i: (0, j))],
        out_specs=[pl.BlockSpec((tm, tn), lambda j, i: (i, j)), pl.BlockSpec((tm, tn), lambda j, i: (i, j))],
        out_shape=[jax.ShapeDtypeStruct((m, n), F32), jax.ShapeDtypeStruct((m, n), BF16)],
        compiler_params=_params("parallel", "parallel"),
    )(a, w, x_in, gate)


def mm_nn_residual_loss(a, w, x_in, gate, coef, target, *, name):
    m, k = a.shape
    n = w.shape[1]
    tm, tn = _mm_tiles(m, n, 2 * k, 2 * k, 4 + 4 + 4 + 2)

    def body(a_ref, w_ref, x_ref, g_ref, t_ref, dy_ref, df_ref, dg_ref, l_ref):
        f = _dot(a_ref[...], w_ref[...], NN)
        err = x_ref[...] + (coef * g_ref[...]) * f - t_ref[...]
        dy = err * (1.0 / n)
        dy_ref[...] = dy
        df_ref[...] = ((coef * g_ref[...]) * dy).astype(BF16)
        dgate = coef * jnp.sum(dy * f, axis=0, keepdims=True)
        part = jnp.sum(jnp.sum(err * err, axis=0, keepdims=True), axis=1, keepdims=True) * (0.5 / n)

        @pl.when(pl.program_id(1) == 0)
        def _():
            dg_ref[...] = jnp.zeros_like(dg_ref)

        @pl.when((pl.program_id(0) == 0) & (pl.program_id(1) == 0))
        def _():
            l_ref[...] = jnp.zeros_like(l_ref)

        dg_ref[...] += dgate
        l_ref[...] += jnp.broadcast_to(part, l_ref.shape)

    blk = pl.BlockSpec((tm, tn), lambda j, i: (i, j))
    vec = pl.BlockSpec((1, tn), lambda j, i: (0, j))
    return pl.pallas_call(
        body, name=name, grid=(n // tn, m // tm),
        in_specs=[pl.BlockSpec((tm, k), lambda j, i: (i, 0)), pl.BlockSpec((k, tn), lambda j, i: (0, j)), blk, vec, blk],
        out_specs=[blk, blk, vec, pl.BlockSpec((1, LANE), lambda j, i: (0, 0))],
        out_shape=[jax.ShapeDtypeStruct((m, n), F32), jax.ShapeDtypeStruct((m, n), BF16),
                   jax.ShapeDtypeStruct((1, n), F32), jax.ShapeDtypeStruct((1, LANE), F32)],
        compiler_params=_params("arbitrary", "arbitrary"),
    )(a, w, x_in, gate, target)


TOKEN_SPEC = pl.BlockSpec((8, LANE), lambda *_: (0, 0))


def mm_nt(a, w, token, *, out_dtype, name):
    m, k = a.shape
    n = w.shape[0]
    tm, tn = _mm_tiles(m, n, 2 * k, 2 * k, jnp.dtype(out_dtype).itemsize)

    def body(a_ref, w_ref, token_ref, o_ref):
        o_ref[...] = _dot(a_ref[...], w_ref[...], NT).astype(o_ref.dtype)

    return pl.pallas_call(
        body, name=name, grid=(n // tn, m // tm),
        in_specs=[pl.BlockSpec((tm, k), lambda j, i: (i, 0)), pl.BlockSpec((tn, k), lambda j, i: (j, 0)), TOKEN_SPEC],
        out_specs=pl.BlockSpec((tm, tn), lambda j, i: (i, j)),
        out_shape=jax.ShapeDtypeStruct((m, n), out_dtype),
        compiler_params=_params("parallel", "parallel"),
    )(a, w, token)


def mm_nt_halves(a_lo, a_hi, w3, token, *, name):
    m = a_lo.shape[0]
    n_blk, n, tn = w3.shape
    half = n_blk // 2
    tm = _tile(m, ROW_TILE, 16)

    def body(lo_ref, hi_ref, w_ref, token_ref, o_ref):
        j = pl.program_id(1)

        @pl.when(j == 0)
        def _():
            o_ref[...] = jnp.zeros_like(o_ref)

        @pl.when(j < half)
        def _():
            o_ref[...] += _dot(lo_ref[...], w_ref[...], NT)

        @pl.when(j >= half)
        def _():
            o_ref[...] += _dot(hi_ref[...], w_ref[...], NT)

    return pl.pallas_call(
        body, name=name, grid=(m // tm, n_blk),
        in_specs=[pl.BlockSpec((tm, tn), lambda i, j: (i, jnp.minimum(j, half - 1))),
                  pl.BlockSpec((tm, tn), lambda i, j: (i, jnp.maximum(j - half, 0))),
                  pl.BlockSpec((None, n, tn), lambda i, j: (j, 0, 0)), TOKEN_SPEC],
        out_specs=pl.BlockSpec((tm, n), lambda i, j: (i, 0)),
        out_shape=jax.ShapeDtypeStruct((m, n), F32),
        compiler_params=_params("parallel", "arbitrary"),
    )(a_lo, a_hi, w3, token)


def mm_tn(a, dy, *, name):
    s, k = a.shape
    n = dy.shape[1]
    ts = _tile(s, ROW_TILE, 16)
    tk = k if k <= 2048 else _tile(k, COL_TILE, LANE)
    tn = _tile(n, ACC_BUDGET // (4 * tk), LANE)
    n_steps = s // ts

    def body(a_ref, dy_ref, o_ref, acc_ref):
        t = pl.program_id(2)

        @pl.when(t == 0)
        def _():
            acc_ref[...] = jnp.zeros_like(acc_ref)

        acc_ref[...] += _dot(a_ref[...], dy_ref[...], TN)

        @pl.when(t == n_steps - 1)
        def _():
            o_ref[...] = acc_ref[...].astype(BF16)

    return pl.pallas_call(
        body, name=name, grid=(k // tk, n // tn, n_steps),
        in_specs=[pl.BlockSpec((ts, tk), lambda kk, j, t: (t, kk)), pl.BlockSpec((ts, tn), lambda kk, j, t: (t, j))],
        out_specs=pl.BlockSpec((tk, tn), lambda kk, j, t: (kk, j)),
        out_shape=jax.ShapeDtypeStruct((k, n), BF16),
        scratch_shapes=[pltpu.VMEM((tk, tn), F32)],
        compiler_params=_params("parallel", "parallel", "arbitrary"),
    )(a, dy)


def mm_tn_halves(a, dy_lo, dy_hi, n_blocks, *, name):
    s, k = a.shape
    half = n_blocks // 2
    tn = dy_lo.shape[1] // half
    ts = _tile(s, ROW_TILE, 16)
    n_steps = s // ts

    def body(a_ref, lo_ref, hi_ref, o_ref, acc_ref):
        j, t = pl.program_id(0), pl.program_id(1)

        @pl.when(t == 0)
        def _():
            acc_ref[...] = jnp.zeros_like(acc_ref)

        @pl.when(j < half)
        def _():
            acc_ref[...] += _dot(a_ref[...], lo_ref[...], TN)

        @pl.when(j >= half)
        def _():
            acc_ref[...] += _dot(a_ref[...], hi_ref[...], TN)

        @pl.when(t == n_steps - 1)
        def _():
            o_ref[...] = acc_ref[...].astype(BF16)

    return pl.pallas_call(
        body, name=name, grid=(n_blocks, n_steps),
        in_specs=[pl.BlockSpec((ts, k), lambda j, t: (t, 0)),
                  pl.BlockSpec((ts, tn), lambda j, t: (jnp.where(j < half, t, n_steps - 1), jnp.minimum(j, half - 1))),
                  pl.BlockSpec((ts, tn), lambda j, t: (jnp.where(j < half, 0, t), jnp.maximum(j - half, 0)))],
        out_specs=pl.BlockSpec((None, k, tn), lambda j, t: (j, 0, 0)),
        out_shape=jax.ShapeDtypeStruct((n_blocks, k, tn), BF16),
        scratch_shapes=[pltpu.VMEM((k, tn), F32)],
        compiler_params=_params("parallel", "arbitrary"),
    )(a, dy_lo, dy_hi)


def ffn_dgu(df, wd, g, u, token, *, name):
    m, k = df.shape
    n = wd.shape[0]
    tm, tn = _mm_tiles(m, n, 2 * k, 2 * k, 4 * 2)

    def body(df_ref, w_ref, g_ref, u_ref, token_ref, dg_ref, du_ref):
        da = _dot(df_ref[...], w_ref[...], NT)
        gg, uu = g_ref[...].astype(F32), u_ref[...].astype(F32)
        sg = _sigmoid(gg)
        dg_ref[...] = (da * uu * (sg * (1 + gg * (1 - sg)))).astype(BF16)
        du_ref[...] = (da * (gg * sg)).astype(BF16)

    blk = pl.BlockSpec((tm, tn), lambda j, i: (i, j))
    out = jax.ShapeDtypeStruct((m, n), BF16)
    return pl.pallas_call(
        body, name=name, grid=(n // tn, m // tm),
        in_specs=[pl.BlockSpec((tm, k), lambda j, i: (i, 0)), pl.BlockSpec((tn, k), lambda j, i: (j, 0)), blk, blk, TOKEN_SPEC],
        out_specs=[blk, blk], out_shape=[out, out],
        compiler_params=_params("parallel", "parallel"),
    )(df, wd, g, u, token)


def ffn_up(h, wgu3, *, name, tm=512):
    s, k = h.shape
    n = wgu3.shape[2]
    half = wgu3.shape[0] // 2
    tm = _tile(s, tm, 16)

    def body(h_ref, wg_ref, wu_ref, g_ref, u_ref, a_ref):
        hh = h_ref[...]
        g = _dot(hh, wg_ref[...], NN)
        u = _dot(hh, wu_ref[...], NN)
        g_ref[...] = g.astype(BF16)
        u_ref[...] = u.astype(BF16)
        a_ref[...] = (g * _sigmoid(g) * u).astype(BF16)

    out = jax.ShapeDtypeStruct((s, half * n), BF16)
    blk = pl.BlockSpec((tm, n), lambda j, i: (i, j))
    return pl.pallas_call(
        body, name=name, grid=(half, s // tm),
        in_specs=[pl.BlockSpec((tm, k), lambda j, i: (i, 0)),
                  pl.BlockSpec((None, k, n), lambda j, i: (j, 0, 0)),
                  pl.BlockSpec((None, k, n), lambda j, i: (j + half, 0, 0))],
        out_specs=[blk, blk, blk], out_shape=[out, out, out],
        compiler_params=_params("parallel", "parallel"),
    )(h, wgu3, wgu3)


def _row_spec(ts, width, col=0):
    return pl.BlockSpec((ts, width), lambda i: (i, col))


def _vec_spec(width):
    return pl.BlockSpec((1, width), lambda i: (0, 0))


def _accumulate(ref, value):
    i = pl.program_id(0)

    @pl.when(i == 0)
    def _():
        ref[...] = value

    @pl.when(i > 0)
    def _():
        ref[...] += value


def norm_modulate(x, g, shift, scale, token, *, name, ts=512):
    s, d = x.shape
    ts = _tile(s, ts, 16)

    def body(x_ref, g_ref, sh_ref, sc_ref, token_ref, h_ref):
        xx = x_ref[...]
        r = lax.rsqrt(jnp.mean(xx * xx, axis=-1, keepdims=True) + EPS)
        h_ref[...] = ((xx * r) * g_ref[...] * (1 + sc_ref[...]) + sh_ref[...]).astype(BF16)

    return pl.pallas_call(
        body, name=name, grid=(s // ts,),
        in_specs=[_row_spec(ts, d), _vec_spec(d), _vec_spec(d), _vec_spec(d), TOKEN_SPEC],
        out_specs=_row_spec(ts, d), out_shape=jax.ShapeDtypeStruct((s, d), BF16),
        compiler_params=_params("parallel"),
    )(x, g, shift, scale, token)


def norm_modulate_bwd(x, g, scale, dh, dx_out, below=None, *, name, ts=256):
    s, d = x.shape
    ts = _tile(s, ts, 16)
    coef = None if below is None else below[2]

    def body(*refs):
        x_ref, g_ref, sc_ref, dh_ref, dxo_ref = refs[:5]
        dx_ref, dsh_ref, dsc_ref, dg_ref = refs[-6:-2] if below else refs[-4:]
        xx, dh_ = x_ref[...], dh_ref[...].astype(F32)
        r = lax.rsqrt(jnp.mean(xx * xx, axis=-1, keepdims=True) + EPS)
        xh = xx * r
        dn = dh_ * (1 + sc_ref[...])
        dxh = dn * g_ref[...]
        dx = dxo_ref[...] + r * (dxh - xh * jnp.mean(dxh * xh, axis=-1, keepdims=True))
        dx_ref[...] = dx
        _accumulate(dsh_ref, jnp.sum(dh_, axis=0, keepdims=True))
        _accumulate(dsc_ref, jnp.sum(dh_ * (xh * g_ref[...]), axis=0, keepdims=True))
        _accumulate(dg_ref, jnp.sum(dn * xh, axis=0, keepdims=True))
        if below:
            f_ref, gate_ref, df_ref, dgate_ref = refs[5], refs[6], refs[-2], refs[-1]
            df_ref[...] = ((coef * gate_ref[...]) * dx).astype(BF16)
            _accumulate(dgate_ref, coef * jnp.sum(dx * f_ref[...].astype(F32), axis=0, keepdims=True))

    vec = jax.ShapeDtypeStruct((1, d), F32)
    extra_in, extra_out, extra_shape, extra_args = [], [], [], []
    if below:
        extra_in, extra_args = [_row_spec(ts, d), _vec_spec(d)], [below[0], below[1]]
        extra_out, extra_shape = [_row_spec(ts, d), _vec_spec(d)], [jax.ShapeDtypeStruct((s, d), BF16), vec]
    return pl.pallas_call(
        body, name=name, grid=(s // ts,),
        in_specs=[_row_spec(ts, d), _vec_spec(d), _vec_spec(d), _row_spec(ts, d), _row_spec(ts, d)] + extra_in,
        out_specs=[_row_spec(ts, d), _vec_spec(d), _vec_spec(d), _vec_spec(d)] + extra_out,
        out_shape=[jax.ShapeDtypeStruct((s, d), F32), vec, vec, vec] + extra_shape,
        compiler_params=_params("arbitrary"),
    )(x, g, scale, dh, dx_out, *extra_args)


def merge_fwd(z, y_pool, y_attn, *, name, ts=512, tc=1024):
    s, d = y_pool.shape
    ts = _tile(s, ts, 16)

    def body(ga_ref, gb_ref, yp_ref, ya_ref, o_ref):
        sa, sb = _sigmoid(ga_ref[...].astype(F32)), _sigmoid(gb_ref[...].astype(F32))
        o_ref[...] = (sa * yp_ref[...] + sb * ya_ref[...]).astype(BF16)

    blk = pl.BlockSpec((ts, tc), lambda i, j: (i, j))
    return pl.pallas_call(
        body, name=name, grid=(s // ts, d // tc),
        in_specs=[pl.BlockSpec((ts, tc), lambda i, j: (i, COL_GA // tc + j)),
                  pl.BlockSpec((ts, tc), lambda i, j: (i, COL_GB // tc + j)), blk, blk],
        out_specs=blk, out_shape=jax.ShapeDtypeStruct((s, d), BF16),
        compiler_params=_params("parallel", "parallel"),
    )(z, z, y_pool, y_attn)


def merge_bwd(dmerged, z, y_pool, y_attn, *, name, ts=512, tc=1024):
    s, d = y_pool.shape
    ts = _tile(s, ts, 16)

    def body(dm_ref, ga_ref, gb_ref, yp_ref, ya_ref, dyp_ref, dya_ref, dga_ref, dgb_ref):
        dm = dm_ref[...]
        sa, sb = _sigmoid(ga_ref[...].astype(F32)), _sigmoid(gb_ref[...].astype(F32))
        dyp_ref[...] = (dm * sa).astype(BF16)
        dya_ref[...] = (dm * sb).astype(BF16)
        dga_ref[...] = (dm * yp_ref[...] * (sa * (1 - sa))).astype(BF16)
        dgb_ref[...] = (dm * ya_ref[...] * (sb * (1 - sb))).astype(BF16)

    blk = pl.BlockSpec((ts, tc), lambda i, j: (i, j))
    out = jax.ShapeDtypeStruct((s, d), BF16)
    return pl.pallas_call(
        body, name=name, grid=(s // ts, d // tc),
        in_specs=[blk, pl.BlockSpec((ts, tc), lambda i, j: (i, COL_GA // tc + j)),
                  pl.BlockSpec((ts, tc), lambda i, j: (i, COL_GB // tc + j)), blk, blk],
        out_specs=[blk] * 4, out_shape=[out] * 4,
        compiler_params=_params("parallel", "parallel"),
    )(dmerged, z, z, y_pool, y_attn)


def _window_counts(t0, rows):
    t1 = (t0 + 1 + lax.broadcasted_iota(jnp.int32, (rows, 1), 0)).astype(F32)
    return [jnp.minimum(t1, float(w)) for w in POOL_WINDOWS]


def pool_fwd(z, pool_mix, pool_scale, *, name, ts=256):
    s = z.shape[0]
    ts = _tile(s, ts, 16)
    per = ts // POOL_HALO

    def body(u_ref, halo_ref, pm_ref, ps_ref, pooled_ref, p_ref):
        i = pl.program_id(0)
        u = u_ref[...].astype(F32)
        halo = jnp.where(i > 0, halo_ref[...].astype(F32), 0.0)
        run = jnp.concatenate([halo, u], axis=0)
        sums, width = [], 1
        for w in POOL_WINDOWS:
            while width < w:
                run = run + pltpu.roll(run, width, 0)
                width *= 2
            sums.append(run[POOL_HALO:])
        counts = _window_counts(i * ts, ts)
        for gi in range(POOL_GROUPS):
            cols = slice(gi * POOL_GROUP_W, (gi + 1) * POOL_GROUP_W)
            pooled = (sums[gi][:, cols] / counts[gi] - u[:, cols]).astype(BF16)
            pooled_ref[:, cols] = pooled
            p_ref[:, cols] = (_dot(pooled, pm_ref[gi], NN) * ps_ref[:, cols]).astype(BF16)

    out = jax.ShapeDtypeStruct((s, POOL_W), BF16)
    return pl.pallas_call(
        body, name=name, grid=(s // ts,),
        in_specs=[_row_spec(ts, POOL_W, COL_U // POOL_W),
                  pl.BlockSpec((POOL_HALO, POOL_W), lambda i: (jnp.maximum(i * per - 1, 0), COL_U // POOL_W)),
                  pl.BlockSpec((POOL_GROUPS, POOL_GROUP_W, POOL_GROUP_W), lambda i: (0, 0, 0)),
                  _vec_spec(POOL_W)],
        out_specs=[_row_spec(ts, POOL_W)] * 2, out_shape=[out, out],
        compiler_params=_params("parallel"),
    )(z, z, pool_mix, pool_scale)


def pool_bwd(dp, pooled, pool_mix, pool_scale, *, name, ts=256):
    s = dp.shape[0]
    ts = _tile(s, ts, 16)
    per = ts // POOL_HALO
    n_steps = s // ts
    rows = ts + POOL_HALO

    def body(dp_ref, halo_ref, pooled_ref, pm_ref, ps_ref, du_ref, dps_ref, dpm_ref):
        i = pl.program_id(0)
        dp_main = dp_ref[...]
        halo = jnp.where(i < n_steps - 1, halo_ref[...], 0.0)
        dmixed = jnp.concatenate([dp_main, halo], axis=0) * ps_ref[...]
        counts = _window_counts(i * ts, rows)
        dps_parts = []
        for gi, w in enumerate(POOL_WINDOWS):
            cols = slice(gi * POOL_GROUP_W, (gi + 1) * POOL_GROUP_W)
            dmx = dmixed[:, cols].astype(BF16)
            pooled = pooled_ref[:, cols]
            mixed = _dot(pooled, pm_ref[gi], NN)
            dps_parts.append(jnp.sum(dp_main[:, cols] * mixed, axis=0, keepdims=True))
            dpm_g = _dot(pooled, dmx[:ts], TN)

            @pl.when(i == 0)
            def _():
                dpm_ref[gi] = dpm_g

            @pl.when(i > 0)
            def _():
                dpm_ref[gi] += dpm_g

            dpooled = _dot(dmx, pm_ref[gi], NT)
            run, width = dpooled / counts[gi], 1
            while width < w:
                run = run + pltpu.roll(run, rows - width, 0)
                width *= 2
            du_ref[:, cols] = (run[:ts] - dpooled[:ts]).astype(BF16)
        _accumulate(dps_ref, jnp.concatenate(dps_parts, axis=1))

    return pl.pallas_call(
        body, name=name, grid=(n_steps,),
        in_specs=[_row_spec(ts, POOL_W),
                  pl.BlockSpec((POOL_HALO, POOL_W), lambda i: (jnp.minimum((i + 1) * per, s // POOL_HALO - 1), 0)),
                  _row_spec(ts, POOL_W),
                  pl.BlockSpec((POOL_GROUPS, POOL_GROUP_W, POOL_GROUP_W), lambda i: (0, 0, 0)),
                  _vec_spec(POOL_W)],
        out_specs=[_row_spec(ts, POOL_W), _vec_spec(POOL_W),
                   pl.BlockSpec((POOL_GROUPS, POOL_GROUP_W, POOL_GROUP_W), lambda i: (0, 0, 0))],
        out_shape=[jax.ShapeDtypeStruct((s, POOL_W), BF16), jax.ShapeDtypeStruct((1, POOL_W), F32),
                   jax.ShapeDtypeStruct((POOL_GROUPS, POOL_GROUP_W, POOL_GROUP_W), F32)],
        compiler_params=_params("arbitrary"),
    )(dp, dp, pooled, pool_mix, pool_scale)


def _bucket_one_hot():
    ql = np.arange(BLK)[:, None]
    j = np.arange(2 * BLK)[None, :]
    n = np.clip(BLK + ql - j, 0, None)
    nf = np.maximum(n, 1).astype(np.float32)
    large = MAX_EXACT + (np.log(nf / MAX_EXACT) / np.log(REL_MAX_DIST / MAX_EXACT)
                         * (NUM_BUCKETS - MAX_EXACT)).astype(np.int32)
    large = np.minimum(large, NUM_BUCKETS - 1)
    bucket = np.where(n < MAX_EXACT, n, large).astype(np.int32).reshape(-1)
    return (np.arange(NUM_BUCKETS)[:, None] == bucket[None, :]).astype(np.float32)


def bias_table(rel_bias_t, one_hot, *, name, tc=4096):
    n = one_hot.shape[1]

    def body(rb_ref, oh_ref, o_ref):
        o_ref[...] = _dot(rb_ref[...], oh_ref[...], NN, precision=lax.Precision.HIGHEST)

    return pl.pallas_call(
        body, name=name, grid=(n // tc,),
        in_specs=[pl.BlockSpec((N_HEADS, NUM_BUCKETS), lambda i: (0, 0)), pl.BlockSpec((NUM_BUCKETS, tc), lambda i: (0, i))],
        out_specs=pl.BlockSpec((N_HEADS, tc), lambda i: (0, i)),
        out_shape=jax.ShapeDtypeStruct((N_HEADS, n), F32),
        compiler_params=_params("parallel"),
    )(rel_bias_t, one_hot)


def bias_table_bwd(dbias, one_hot, *, name, tc=4096):
    n = one_hot.shape[1]

    def body(db_ref, oh_ref, o_ref):
        _accumulate(o_ref, _dot(db_ref[...], oh_ref[...], NT, precision=lax.Precision.HIGHEST))

    return pl.pallas_call(
        body, name=name, grid=(n // tc,),
        in_specs=[pl.BlockSpec((N_HEADS, tc), lambda i: (0, i)), pl.BlockSpec((NUM_BUCKETS, tc), lambda i: (0, i))],
        out_specs=pl.BlockSpec((N_HEADS, NUM_BUCKETS), lambda i: (0, 0)),
        out_shape=jax.ShapeDtypeStruct((N_HEADS, NUM_BUCKETS), F32),
        compiler_params=_params("arbitrary"),
    )(dbias, one_hot)


def _lane_half(shape):
    return lax.broadcasted_iota(jnp.int32, shape, len(shape) - 1) < HEAD_DIM


def _half_sums(v, first):
    s0 = jnp.sum(jnp.where(first, v, 0.0), axis=-1, keepdims=True)
    s1 = jnp.sum(jnp.where(first, 0.0, v), axis=-1, keepdims=True)
    return jnp.where(first, s0, s1)


GROUP_ROWS = GQA * BLK


def _band_mask(n):
    ql = lax.broadcasted_iota(jnp.int32, (GROUP_ROWS, 2 * BLK), 0) & (BLK - 1)
    j = lax.broadcasted_iota(jnp.int32, (GROUP_ROWS, 2 * BLK), 1)
    return (j > ql) & (j <= ql + BLK) & ((j >= BLK) | (n > 0))


def _norm_keys(kband):
    first = _lane_half(kband.shape)
    r = lax.rsqrt(_half_sums(kband * kband, first) * (1.0 / HEAD_DIM) + EPS)
    return kband * r


def _kv_lanes(kv, shape):
    return _lane_half(shape) if kv == 0 else jnp.logical_not(_lane_half(shape))


def _stack_group(ref, kv, dtype=F32):
    parts = []
    for g in range(GQA):
        h = kv * GQA + g
        part = ref[:, (h // 2) * LANE:(h // 2 + 1) * LANE].astype(dtype)
        parts.append(pltpu.roll(part, HEAD_DIM, 1) if h % 2 != kv else part)
    stacked = jnp.concatenate(parts, axis=0)
    return jnp.where(_kv_lanes(kv, stacked.shape), stacked, 0.0)


def _unstack_group(ref, kv, stacked):
    for i in range(GQA // 2):
        pair = None
        for sub in range(2):
            g = 2 * i + sub
            part = stacked[g * BLK:(g + 1) * BLK]
            part = pltpu.roll(part, HEAD_DIM, 1) if sub != kv else part
            pair = part if pair is None else pair + part
        col = (kv * GQA // 2 + i) * LANE
        ref[:, col:col + LANE] = pair.astype(BF16)


def _group_logits(q_ref, kv, qg, kn, bias_ref, sink_ref, mask):
    qa = _stack_group(q_ref, kv)
    r = lax.rsqrt(jnp.sum(qa * qa, axis=-1, keepdims=True) * (1.0 / HEAD_DIM) + EPS)
    xh = qa * r
    qn = (xh * qg).astype(BF16)
    bias = bias_ref[kv * GQA:(kv + 1) * GQA].reshape(GROUP_ROWS, 2 * BLK)
    logits = _dot(qn, kn, NT) * (HEAD_DIM ** -0.5) + bias
    p, p_sink = _softmax_with_sink(jnp.where(mask, logits, NEG_INF), sink_ref[kv * GROUP_ROWS:(kv + 1) * GROUP_ROWS])
    return xh, r, qn, p, p_sink


def _softmax_with_sink(logits, sink):
    m = jnp.maximum(jnp.max(logits, axis=-1, keepdims=True), sink)
    e = jnp.exp(logits - m)
    es = jnp.exp(sink - m)
    den = jnp.sum(e, axis=-1, keepdims=True) + es
    return e / den, es / den


def _attn_specs(nb, last):
    cur = lambda n: jnp.minimum(n, last)
    prev = lambda n: jnp.minimum(jnp.maximum(n - 1, 0), last)
    return [pl.BlockSpec((BLK, ATT_W), lambda n: (cur(n), COL_Q // ATT_W)),
            pl.BlockSpec((BLK, KV_W), lambda n: (prev(n), COL_K // KV_W)),
            pl.BlockSpec((BLK, KV_W), lambda n: (cur(n), COL_K // KV_W)),
            pl.BlockSpec((BLK, KV_W), lambda n: (prev(n), COL_V // KV_W)),
            pl.BlockSpec((BLK, KV_W), lambda n: (cur(n), COL_V // KV_W))]


def attn_fwd(z, qg2, kg2, sinks, bias, *, name):
    s = z.shape[0]
    nb = s // BLK

    def body(q_ref, kp_ref, kc_ref, vp_ref, vc_ref, qg_ref, kg_ref, sink_ref, bias_ref, o_ref):
        mask = _band_mask(pl.program_id(0))
        kn = (_norm_keys(jnp.concatenate([kp_ref[...], kc_ref[...]], axis=0).astype(F32)) * kg_ref[...]).astype(BF16)
        vb = jnp.concatenate([vp_ref[...], vc_ref[...]], axis=0).astype(BF16)
        for kv in range(N_KV):
            _, _, _, p, _ = _group_logits(q_ref, kv, qg_ref[...], kn, bias_ref, sink_ref, mask)
            out = _dot(p.astype(BF16), vb, NN)
            _unstack_group(o_ref, kv, jnp.where(_kv_lanes(kv, out.shape), out, 0.0))

    return pl.pallas_call(
        body, name=name, grid=(nb,),
        in_specs=_attn_specs(nb, nb - 1) + [
            _vec_spec(LANE), _vec_spec(LANE), pl.BlockSpec((N_HEADS * BLK, 1), lambda n: (0, 0)),
            pl.BlockSpec((N_HEADS, BLK, 2 * BLK), lambda n: (0, 0, 0))],
        out_specs=pl.BlockSpec((BLK, ATT_W), lambda n: (n, 0)),
        out_shape=jax.ShapeDtypeStruct((s, ATT_W), BF16),
        compiler_params=_params("parallel"),
    )(z, z, z, z, z, qg2, kg2, sinks, bias)


def attn_bwd(z, d_out, qg2, kg2, sinks, bias, *, name):
    s = z.shape[0]
    nb = s // BLK
    scale = HEAD_DIM ** -0.5

    def body(q_ref, kp_ref, kc_ref, vp_ref, vc_ref, do_ref, qg_ref, kg_ref, sink_ref, bias_ref,
             dq_ref, dk_ref, dv_ref, dqg_ref, dkg_ref, dsink_ref, dbias_ref, band_k, band_v, carry_k, carry_v, dsink_rows):
        n = pl.program_id(0)

        @pl.when(n == 0)
        def _():
            dqg_ref[...] = jnp.zeros_like(dqg_ref)
            dkg_ref[...] = jnp.zeros_like(dkg_ref)
            dbias_ref[...] = jnp.zeros_like(dbias_ref)
            carry_k[...] = jnp.zeros_like(carry_k)
            carry_v[...] = jnp.zeros_like(carry_v)
            dsink_rows[...] = jnp.zeros_like(dsink_rows)

        @pl.when(n == nb)
        def _():
            band_k[...] = jnp.zeros_like(band_k)
            band_v[...] = jnp.zeros_like(band_v)

        @pl.when(n < nb)
        def _():
            mask = _band_mask(n)
            kn = (_norm_keys(jnp.concatenate([kp_ref[...], kc_ref[...]], axis=0).astype(F32)) * kg_ref[...]).astype(BF16)
            vb = jnp.concatenate([vp_ref[...], vc_ref[...]], axis=0).astype(BF16)
            dkn = jnp.zeros((2 * BLK, KV_W), F32)
            dvb = jnp.zeros((2 * BLK, KV_W), F32)
            dqg = jnp.zeros((1, LANE), F32)
            for kv in range(N_KV):
                rows = slice(kv * GROUP_ROWS, (kv + 1) * GROUP_ROWS)
                xh, r, qn, p, p_sink = _group_logits(q_ref, kv, qg_ref[...], kn, bias_ref, sink_ref, mask)
                do = _stack_group(do_ref, kv).astype(BF16)
                dp = _dot(do, vb, NT)
                delta = jnp.sum(p * dp, axis=-1, keepdims=True)
                ds = p * (dp - delta)
                dsink_rows[rows] -= p_sink * delta
                dbias_ref[kv * GQA:(kv + 1) * GQA] += ds.reshape(GQA, BLK, 2 * BLK)
                ds16 = ds.astype(BF16)
                dqn = jnp.where(_kv_lanes(kv, xh.shape), _dot(ds16, kn, NN) * scale, 0.0)
                dkn = dkn + _dot(ds16, qn, TN) * scale
                dvb = dvb + _dot(p.astype(BF16), do, TN)
                dqg = dqg + jnp.sum(dqn * xh, axis=0, keepdims=True)
                dxh = dqn * qg_ref[...]
                _unstack_group(dq_ref, kv, r * (dxh - xh * (jnp.sum(dxh * xh, axis=-1, keepdims=True) * (1.0 / HEAD_DIM))))
            band_k[...] = dkn
            band_v[...] = dvb
            dqg_ref[...] += dqg

        dkn_prev = carry_k[...] + band_k[:BLK]
        dv_ref[...] = (carry_v[...] + band_v[:BLK]).astype(BF16)
        carry_k[...] = band_k[BLK:]
        carry_v[...] = band_v[BLK:]
        kp = kp_ref[...].astype(F32)
        first = _lane_half(kp.shape)
        r = lax.rsqrt(_half_sums(kp * kp, first) * (1.0 / HEAD_DIM) + EPS)
        xh = kp * r
        dkg_ref[...] += jnp.sum(dkn_prev * xh, axis=0, keepdims=True)
        dxh = dkn_prev * kg_ref[...]
        dk_ref[...] = (r * (dxh - xh * (_half_sums(dxh * xh, first) * (1.0 / HEAD_DIM)))).astype(BF16)

        @pl.when(n == nb)
        def _():
            dqg_ref[...] += pltpu.roll(dqg_ref[...], HEAD_DIM, 1)
            dkg_ref[...] += pltpu.roll(dkg_ref[...], HEAD_DIM, 1)
            lane16 = lax.broadcasted_iota(jnp.int32, (1, N_HEADS), 1)
            dsink = jnp.zeros((1, N_HEADS), F32)
            for h in range(N_HEADS):
                dsink = dsink + jnp.where(lane16 == h, jnp.sum(dsink_rows[h * BLK:(h + 1) * BLK], axis=0, keepdims=True), 0.0)
            dsink_ref[...] = dsink

    last = nb - 1
    cur = lambda n: jnp.minimum(n, last)
    back = lambda n: jnp.maximum(n - 1, 0)
    full3 = pl.BlockSpec((N_HEADS, BLK, 2 * BLK), lambda n: (0, 0, 0))
    return pl.pallas_call(
        body, name=name, grid=(nb + 1,),
        in_specs=_attn_specs(nb, last) + [
            pl.BlockSpec((BLK, ATT_W), lambda n: (cur(n), 0)),
            _vec_spec(LANE), _vec_spec(LANE), pl.BlockSpec((N_HEADS * BLK, 1), lambda n: (0, 0)), full3],
        out_specs=[pl.BlockSpec((BLK, ATT_W), lambda n: (cur(n), 0)),
                   pl.BlockSpec((BLK, KV_W), lambda n: (back(n), 0)),
                   pl.BlockSpec((BLK, KV_W), lambda n: (back(n), 0)),
                   _vec_spec(LANE), _vec_spec(LANE), _vec_spec(N_HEADS), full3],
        out_shape=[jax.ShapeDtypeStruct((s, ATT_W), BF16), jax.ShapeDtypeStruct((s, KV_W), BF16),
                   jax.ShapeDtypeStruct((s, KV_W), BF16), jax.ShapeDtypeStruct((1, LANE), F32),
                   jax.ShapeDtypeStruct((1, LANE), F32), jax.ShapeDtypeStruct((1, N_HEADS), F32),
                   jax.ShapeDtypeStruct((N_HEADS, BLK, 2 * BLK), F32)],
        scratch_shapes=[pltpu.VMEM((2 * BLK, KV_W), F32), pltpu.VMEM((2 * BLK, KV_W), F32),
                        pltpu.VMEM((BLK, KV_W), F32), pltpu.VMEM((BLK, KV_W), F32), pltpu.VMEM((N_HEADS * BLK, 1), F32)],
        compiler_params=_params("arbitrary"),
    )(z, z, z, z, z, d_out, qg2, kg2, sinks, bias)


def _adamw(w, g, m, v):
    m = ADAM_B1 * m + (1.0 - ADAM_B1) * g
    v = ADAM_B2 * v + (1.0 - ADAM_B2) * (g * g)
    m_hat = m / (1.0 - ADAM_B1 ** ADAM_STEP)
    v_hat = v / (1.0 - ADAM_B2 ** ADAM_STEP)
    delta = -ADAM_LR * (m_hat / (jnp.sqrt(v_hat) + ADAM_EPS) + ADAM_WD * w)
    return delta, m, v


def ada_fwd(c16, w, b, *, name, tn=768):
    k, n = w.shape
    tn = _tile(n, tn, LANE)

    def body(c_ref, w_ref, b_ref, o_ref):
        cc = c_ref[...]
        o_ref[...] = _dot((cc * _sigmoid(cc)).astype(BF16), w_ref[...].astype(BF16), NN) + b_ref[...]

    return pl.pallas_call(
        body, name=name, grid=(n // tn,),
        in_specs=[pl.BlockSpec((c16.shape[0], k), lambda j: (0, 0)), pl.BlockSpec((k, tn), lambda j: (0, j)),
                  pl.BlockSpec((1, tn), lambda j: (0, j))],
        out_specs=pl.BlockSpec((c16.shape[0], tn), lambda j: (0, j)),
        out_shape=jax.ShapeDtypeStruct((c16.shape[0], n), F32),
        compiler_params=_params("parallel"),
    )(c16, w, b)


def ada_bwd_adamw(c_t, dmod, w, m, v, *, name, tn=256):
    k, n = w.shape
    tn = _tile(n, tn, LANE)

    def body(c_ref, d_ref, w_ref, m_ref, v_ref, g_ref, dl_ref, mo_ref, vo_ref):
        cc = c_ref[...]
        g = _dot((cc * _sigmoid(cc)).astype(BF16), d_ref[...].astype(BF16), NN)
        g_ref[...] = g
        dl_ref[...], mo_ref[...], vo_ref[...] = _adamw(w_ref[...], g, m_ref[...], v_ref[...])

    blk = pl.BlockSpec((k, tn), lambda j: (0, j))
    out = jax.ShapeDtypeStruct((k, n), F32)
    return pl.pallas_call(
        body, name=name, grid=(n // tn,),
        in_specs=[pl.BlockSpec((k, LANE), lambda j: (0, 0)), pl.BlockSpec((LANE, tn), lambda j: (0, j)), blk, blk, blk],
        out_specs=[blk] * 4, out_shape=[out] * 4,
        compiler_params=_params("parallel"),
    )(c_t, dmod, w, m, v)


def adamw_from_parts(parts, w, m, v, *, name):
    r, c = w.shape
    tr = _tile(r, max(16, (256 * 1024) // c), 16)

    def body(p_ref, w_ref, m_ref, v_ref, g_ref, dl_ref, mo_ref, vo_ref):
        g = p_ref[0].astype(F32)
        for d in range(1, N_DEV):
            g = g + p_ref[d].astype(F32)
        g_ref[...] = g
        dl_ref[...], mo_ref[...], vo_ref[...] = _adamw(w_ref[...], g, m_ref[...], v_ref[...])

    blk = pl.BlockSpec((tr, c), lambda i: (i, 0))
    out = jax.ShapeDtypeStruct((r, c), F32)
    return pl.pallas_call(
        body, name=name, grid=(r // tr,),
        in_specs=[pl.BlockSpec((N_DEV, tr, c), lambda i: (0, i, 0)), blk, blk, blk],
        out_specs=[blk] * 4, out_shape=[out] * 4,
        compiler_params=_params("parallel"),
    )(parts, w, m, v)


def _ffn_fwd(x_in, g, shift, scale, gate, wgu3, get_wd, token, tag, target=None):
    h = norm_modulate(x_in, g, shift, scale, token, name=f"{tag}_norm")
    gg, uu, act = ffn_up(h, wgu3, name=f"{tag}_up")
    wd = get_wd(gg)
    if target is None:
        x_out, f = mm_nn_residual(act, wd, x_in, gate, 0.5, name=f"{tag}_down")
    else:
        x_out, f = mm_nn_residual_loss(act, wd, x_in, gate, 0.5, target, name=f"{tag}_down_loss"), None
    return x_out, (h, gg, uu, act, f), wd


def _ffn_bwd(dx_out, df, dgate, x_in, g, scale, wgu3, wd, saved, token, scatter, split, tag, below=None):
    h, gg, uu, act, _ = saved
    dwd = mm_tn(act, df, name=f"{tag}_dwd").reshape(N_DEV, -1, D_MODEL)
    if split:
        token = scatter([f"w_{tag}_down"], [dwd], f"scatter_{tag}_down")
    dgg, duu = ffn_dgu(df, wd, gg, uu, token, name=f"{tag}_dgu")
    dwgu = mm_tn_halves(h, dgg, duu, N_DEV, name=f"{tag}_dwgu")
    if split:
        token = scatter([f"w_{tag}_gu"], [dwgu], f"scatter_{tag}_gu")
    else:
        token = scatter([f"w_{tag}_gu", f"w_{tag}_down"], [dwgu, dwd], f"scatter_{tag}")
    dh = mm_nt_halves(dgg, duu, wgu3, token, name=f"{tag}_dh")
    dx_in, dshift, dscale, dg, *rest = norm_modulate_bwd(x_in, g, scale, dh, dx_out, below, name=f"{tag}_norm_bwd")
    return (dx_in, (dshift, dscale, dgate), dg, *rest)


def kernel(x, c, w_ada, b_ada, g_ffn1, w_ffn1_gu, w_ffn1_down, g_mix, w_in, pool_mix, pool_scale, w_pool_up, q_gain, k_gain, sinks, rel_bias, w_attn_up, w_o, g_ffn2, w_ffn2_gu, w_ffn2_down, loss_target, m_w_ada, m_b_ada, m_g_ffn1, m_w_ffn1_gu, m_w_ffn1_down, m_g_mix, m_w_in, m_pool_mix, m_pool_scale, m_w_pool_up, m_q_gain, m_k_gain, m_sinks, m_rel_bias, m_w_attn_up, m_w_o, m_g_ffn2, m_w_ffn2_gu, m_w_ffn2_down, v_w_ada, v_b_ada, v_g_ffn1, v_w_ffn1_gu, v_w_ffn1_down, v_g_mix, v_w_in, v_pool_mix, v_pool_scale, v_w_pool_up, v_q_gain, v_k_gain, v_sinks, v_rel_bias, v_w_attn_up, v_w_o, v_g_ffn2, v_w_ffn2_gu, v_w_ffn2_down):
    me = _slot(_mesh_pos())
    x0, target = x[0], loss_target[0]
    n_ada = w_ada.shape[2]
    pm_rows = pool_mix.shape[2]

    big = dict(w_ffn1_gu=(w_ffn1_gu, m_w_ffn1_gu, v_w_ffn1_gu), w_ffn1_down=(w_ffn1_down, m_w_ffn1_down, v_w_ffn1_down),
               w_in=(w_in, m_w_in, v_w_in), pool_mix=(pool_mix, m_pool_mix, v_pool_mix),
               w_pool_up=(w_pool_up, m_w_pool_up, v_w_pool_up), w_attn_up=(w_attn_up, m_w_attn_up, v_w_attn_up),
               w_o=(w_o, m_w_o, v_w_o), w_ffn2_gu=(w_ffn2_gu, m_w_ffn2_gu, v_w_ffn2_gu),
               w_ffn2_down=(w_ffn2_down, m_w_ffn2_down, v_w_ffn2_down))
    shard2d = {k: (POOL_GROUPS * pm_rows, POOL_GROUP_W) if k == "pool_mix" else t[0].shape[1:] for k, t in big.items()}
    mix_keys = ["w_in", "pool_mix", "w_pool_up", "w_attn_up", "w_o"]
    ffn2_keys = ["w_ffn2_gu", "w_ffn2_down"]

    def shard_bf16(k, token=None):
        w = big[k][0].reshape(shard2d[k])
        return (w if token is None else w + token[0, 0]).astype(BF16)

    def own_slot(block, shape):
        return lax.dynamic_update_slice(lax.empty(shape, block.dtype), block[None], (me, 0, 0))

    def start_gather(keys, token, tag):
        shards = [shard_bf16(k, token) for k in keys]
        return exchange_start(shards, [own_slot(s, (N_DEV,) + s.shape) for s in shards], slotted=False, name=f"{tag}_start")

    pending = []

    def scatter(keys, grads, tag):
        lands = [own_slot(lax.dynamic_index_in_dim(g, me, 0, keepdims=False), g.shape) for g in grads]
        handle = exchange_start(grads, lands, slotted=True, name=f"{tag}_start")
        pending.append((keys, handle, tag))
        return handle[4]

    c_all, _ = all_gather_small(c.reshape(D_MODEL // LANE, LANE), "gather_c")
    c_all = c_all.reshape(N_DEV, D_MODEL)
    c16 = jnp.pad(c_all, ((0, 16 - N_DEV), (0, 0)))
    b_mine = lax.dynamic_slice(b_ada, (0, me * n_ada), (1, n_ada))
    mod_cols = ada_fwd(c16, w_ada[0], b_mine, name="ada_fwd")[:N_DEV]
    mod_all, token = all_gather_small(mod_cols.reshape(-1, LANE), "gather_mod")
    mod = lax.dynamic_index_in_dim(mod_all.reshape(N_DEV, N_DEV, n_ada), me, axis=1, keepdims=False)
    mod = mod.reshape(N_MOD, 1, D_MODEL)

    wgu1, token = all_gather_hbm([shard_bf16("w_ffn1_gu", token)], "gather_ffn1_gu")
    gather_wd1 = start_gather(["w_ffn1_down"], token, "gather_ffn1_down")
    gather_mix = start_gather(mix_keys, gather_wd1[4], "gather_mix")
    gather_ffn2 = start_gather(ffn2_keys, gather_mix[4], "gather_ffn2")
    token = gather_ffn2[4]

    def get_wd1(after):
        return exchange_wait(gather_wd1, after, slotted=False, name="gather_ffn1_down_wait")[0].reshape(-1, D_MODEL)

    x1, saved1, wd1 = _ffn_fwd(x0, g_ffn1, mod[0], mod[1], mod[2], wgu1, get_wd1, token, "ffn1")
    gathered = dict(zip(mix_keys, exchange_wait(gather_mix, x1, slotted=False, name="gather_mix_wait")))
    def columns_out(blocks):
        return jnp.transpose(blocks, (1, 0, 2)).reshape(blocks.shape[1], -1)

    def columns_in(full):
        return jnp.transpose(full.reshape(full.shape[0], N_DEV, -1), (1, 0, 2))

    w_in_full = columns_out(gathered["w_in"])
    w_in_z = jnp.concatenate([w_in_full[:, s:s + w] for s, w in (W_IN_PARTS[p] for p in Z_ORDER)], axis=1)
    pm_full = jnp.transpose(gathered["pool_mix"].reshape(N_DEV, POOL_GROUPS, pm_rows, POOL_GROUP_W),
                            (1, 0, 2, 3)).reshape(POOL_GROUPS, POOL_GROUP_W, POOL_GROUP_W)
    wpu, wau = columns_out(gathered["w_pool_up"]), columns_out(gathered["w_attn_up"])
    wo_full = gathered["w_o"].reshape(D_MODEL, D_MODEL)
    h2 = norm_modulate(x1, g_mix, mod[3], mod[4], token, name="mix_norm")
    z = mm_nn(h2, w_in_z, out_dtype=BF16, name="mix_in")
    pooled, p_act = pool_fwd(z, pm_full, pool_scale, name="pool_fwd")
    y_pool = mm_nn(p_act, wpu, out_dtype=BF16, name="pool_up")
    one_hot = jnp.asarray(_bucket_one_hot())
    bias = bias_table(rel_bias.T, one_hot, name="bias_table").reshape(N_HEADS, BLK, 2 * BLK)
    qg2, kg2 = jnp.tile(q_gain, (1, 2)), jnp.tile(k_gain, (1, 2))
    sink_rows = jnp.repeat(sinks[0], BLK).reshape(N_HEADS * BLK, 1)
    attn = attn_fwd(z, qg2, kg2, sink_rows, bias, name="attn_fwd")
    y_attn = mm_nn(attn, wau, out_dtype=BF16, name="attn_up")
    merged = merge_fwd(z, y_pool, y_attn, name="merge_fwd")
    x2, o_act = mm_nn_residual(merged, wo_full, x1, mod[5], 1.0, name="mix_out")
    wgu2, wd2 = exchange_wait(gather_ffn2, x2, slotted=False, name="gather_ffn2_wait")
    wd2 = wd2.reshape(-1, D_MODEL)
    (dy, df2, dgate3, loss_row), saved2, _ = _ffn_fwd(x2, g_ffn2, mod[6], mod[7], mod[8], wgu2, lambda after: wd2, token,
                                                     "ffn2", target)
    loss = lax.psum(loss_row[0, 0], ("x", "y", "c"))

    dx2, dmod3, dg_ffn2, d_o, dgate2 = _ffn_bwd(dy, df2, dgate3, x2, g_ffn2, mod[7], wgu2, wd2, saved2, token, scatter, False,
                                               "ffn2", below=(o_act, mod[5], 1.0))
    dwo = mm_tn(merged, d_o, name="mix_dwo").reshape(N_DEV, -1, D_MODEL)
    dmerged = mm_nt(d_o, wo_full, token, out_dtype=BF16, name="mix_dmerged")
    dyp, dya, dga, dgb = merge_bwd(dmerged, z, y_pool, y_attn, name="merge_bwd")
    dwpu = mm_tn(p_act, dyp, name="pool_dwup")
    dp_act = mm_nt(dyp, wpu, token, out_dtype=BF16, name="pool_dp")
    du, dpool_scale, dpm = pool_bwd(dp_act, pooled, pm_full, pool_scale, name="pool_bwd")
    dwau = mm_tn(attn, dya, name="attn_dwup")
    dattn = mm_nt(dya, wau, token, out_dtype=BF16, name="attn_dout")
    dq, dk, dv, dqg, dkg, dsinks, dbias = attn_bwd(z, dattn, qg2, kg2, sink_rows, bias, name="attn_bwd")
    drel = bias_table_bwd(dbias.reshape(N_HEADS, -1), one_hot, name="bias_table_bwd").T
    dz_parts = dict(ga=dga, gb=dgb, u=du, q=dq, k=dk, v=dv)
    dz = jnp.concatenate([dz_parts[p] for p in Z_ORDER], axis=1)
    dwin_z = mm_tn(h2, dz, name="mix_dwin")
    z_start = dict(zip(Z_ORDER, np.cumsum([0] + [W_IN_PARTS[p][1] for p in Z_ORDER[:-1]])))
    dwin = jnp.concatenate([dwin_z[:, z_start[p]:z_start[p] + W_IN_PARTS[p][1]] for p in W_IN_PARTS], axis=1)
    mix_grads = dict(w_in=columns_in(dwin),
                     pool_mix=jnp.transpose(dpm.astype(BF16).reshape(POOL_GROUPS, N_DEV, pm_rows, POOL_GROUP_W),
                                            (1, 0, 2, 3)).reshape(N_DEV, POOL_GROUPS * pm_rows, POOL_GROUP_W),
                     w_pool_up=columns_in(dwpu), w_attn_up=columns_in(dwau), w_o=dwo)
    token = scatter(mix_keys, [mix_grads[k] for k in mix_keys], "scatter_mix")
    dh2 = mm_nt(dz, w_in_z, token, out_dtype=BF16, name="mix_dh")
    dx1, dsh2, dsc2, dg_mix, df1, dgate1 = norm_modulate_bwd(x1, g_mix, mod[4], dh2, dx2, (saved1[4], mod[2], 0.5),
                                                           name="mix_norm_bwd")
    dx0, dmod1, dg_ffn1 = _ffn_bwd(dx1, df1, dgate1, x0, g_ffn1, mod[1], wgu1, wd1, saved1, token, scatter, True, "ffn1")

    small = [("b_ada", b_ada, m_b_ada, v_b_ada, jnp.concatenate(list(dmod1 + (dsh2, dsc2, dgate2) + dmod3), axis=1)),
             ("g_ffn1", g_ffn1, m_g_ffn1, v_g_ffn1, dg_ffn1), ("g_mix", g_mix, m_g_mix, v_g_mix, dg_mix),
             ("g_ffn2", g_ffn2, m_g_ffn2, v_g_ffn2, dg_ffn2),
             ("pool_scale", pool_scale, m_pool_scale, v_pool_scale, dpool_scale),
             ("q_gain", q_gain, m_q_gain, v_q_gain, dqg[:, :HEAD_DIM]), ("k_gain", k_gain, m_k_gain, v_k_gain, dkg[:, :HEAD_DIM]),
             ("sinks", sinks, m_sinks, v_sinks, dsinks), ("rel_bias", rel_bias, m_rel_bias, v_rel_bias, drel)]
    n_small = sum(t[1].size for t in small)
    pad = -n_small % (8 * LANE)
    flat = lambda arrs: jnp.pad(jnp.concatenate([a.reshape(1, -1) for a in arrs], axis=1), ((0, 0), (0, pad)))
    small_parts, _ = all_gather_small(flat([t[4] for t in small]).reshape(-1, LANE), "gather_small_grads")
    small_parts = small_parts.reshape(N_DEV, 1, n_small + pad)
    sg, sd, sm, sv = adamw_from_parts(small_parts, flat([t[1] for t in small]), flat([t[2] for t in small]),
                                      flat([t[3] for t in small]), name="adamw_small")

    dmod_all = small_parts[:, 0, :N_MOD * D_MODEL]
    dmod_mine = lax.dynamic_slice(dmod_all, (0, me * n_ada), (N_DEV, n_ada))
    c_t = jnp.pad(c_all.T, ((0, 0), (0, LANE - N_DEV)))
    ada_out = ada_bwd_adamw(c_t, jnp.pad(dmod_mine, ((0, LANE - N_DEV), (0, 0))), w_ada[0], m_w_ada[0], v_w_ada[0],
                            name="ada_bwd_adamw")

    res = {"w_ada": [o[None] for o in ada_out]}
    after = ada_out[0]
    for keys, handle, tag in pending:
        parts = exchange_wait(handle, after, slotted=True, name=f"{tag}_wait")
        for k, part in zip(keys, parts):
            w_, m_, v_ = big[k]
            outs = adamw_from_parts(part, w_.reshape(shard2d[k]), m_.reshape(shard2d[k]), v_.reshape(shard2d[k]),
                                    name=f"adamw_{k}")
            res[k] = [o.reshape(w_.shape) for o in outs]
            after = outs[0]
    off = 0
    for k, w_, _, _, _ in small:
        res[k] = [o[0, off:off + w_.size].reshape(w_.shape) for o in (sg, sd, sm, sv)]
        off += w_.size
    order = ["w_ada", "b_ada", "g_ffn1", "w_ffn1_gu", "w_ffn1_down", "g_mix", "w_in", "pool_mix", "pool_scale",
             "w_pool_up", "q_gain", "k_gain", "sinks", "rel_bias", "w_attn_up", "w_o", "g_ffn2", "w_ffn2_gu", "w_ffn2_down"]
    return (loss, dx0[None], *[res[k][0] for k in order], *[res[k][1] for k in order],
            *[res[k][2] for k in order], *[res[k][3] for k in order])
```

```python
import functools

import numpy as np
import jax
import jax.numpy as jnp
from jax import lax
from jax.experimental import pallas as pl
from jax.experimental.pallas import tpu as pltpu

F32, BF16 = jnp.float32, jnp.bfloat16
MESH_ID = pl.DeviceIdType.MESH

N_DEV = 8
D_MODEL = 2048
N_MOD = 9
POOL_WINDOWS = (2, 4, 8, 16)
POOL_GROUPS = 4
POOL_GROUP_W = D_MODEL // 8
POOL_W = POOL_GROUPS * POOL_GROUP_W
POOL_HALO = 16
HEAD_DIM = 64
N_HEADS = 16
N_KV = 2
GQA = N_HEADS // N_KV
BLK = 128
NUM_BUCKETS = 32
MAX_EXACT = 16
REL_MAX_DIST = 128
EPS = 1e-6
NEG_INF = -1e30
ATT_W = N_HEADS * HEAD_DIM
KV_W = N_KV * HEAD_DIM
IN_W = POOL_W + ATT_W + 2 * KV_W + 2 * D_MODEL
GATE_TILE = 512
W_IN_PARTS = dict(u=(0, POOL_W), q=(POOL_W, ATT_W), k=(POOL_W + ATT_W, KV_W), v=(POOL_W + ATT_W + KV_W, KV_W),
                  ga=(POOL_W + ATT_W + 2 * KV_W, D_MODEL), gb=(POOL_W + ATT_W + 2 * KV_W + D_MODEL, D_MODEL))
Z_PIECES = [(W_IN_PARTS[p][0] + j * GATE_TILE, GATE_TILE) for j in range(D_MODEL // GATE_TILE) for p in ("ga", "gb")]
Z_PIECES += [W_IN_PARTS[p] for p in ("u", "q", "k", "v")]
COL_U, COL_Q, COL_K, COL_V = 2 * D_MODEL, 2 * D_MODEL + POOL_W, 2 * D_MODEL + POOL_W + ATT_W, 2 * D_MODEL + POOL_W + ATT_W + KV_W
LANE = 128

ADAM_LR = 0.001
ADAM_B1 = 0.9
ADAM_B2 = 0.999
ADAM_EPS = 1e-08
ADAM_WD = 0.01
ADAM_STEP = 10

NN = ((1,), (0,))
NT = ((1,), (1,))
TN = ((0,), (0,))


def _dot(a, b, dims, precision=None):
    return lax.dot_general(a, b, (dims, ((), ())), preferred_element_type=F32, precision=precision)


def _tile(n, pref, unit):
    t = (min(pref, n) // unit) * unit
    while t >= unit:
        if n % t == 0:
            return t
        t -= unit
    return n


def _params(*sem):
    return pltpu.CompilerParams(dimension_semantics=sem)


def _sigmoid(x):
    return 1.0 / (1.0 + jnp.exp(-x))


def _mesh_pos():
    return lax.axis_index("x"), lax.axis_index("y"), lax.axis_index("c")


def _slot(p):
    return 4 * p[0] + 2 * p[1] + p[2]


def all_gather_small(x_shard, name):
    m_per, n = x_shard.shape

    def body(x_ref, out_ref, token, send_sems, recv_sems, local_sem):
        x, y, c = _mesh_pos()
        me, sibling = (x, y, c), (x, y, 1 - c)
        chips = [(1 - x, y), (x, 1 - y), (1 - x, 1 - y)]
        token[...] = jnp.zeros_like(token)

        def rows(p):
            return out_ref.at[pl.ds(_slot(p) * m_per, m_per), :]

        def copy(k, block, to, src=None):
            return pltpu.make_async_remote_copy(
                src_ref=rows(block) if src is None else src, dst_ref=rows(block),
                send_sem=send_sems.at[k], recv_sem=recv_sems.at[k], device_id=to, device_id_type=MESH_ID)

        mine = pltpu.make_async_copy(x_ref, rows(me), local_sem)
        mine.start()
        first = [copy(0, me, sibling, src=x_ref)]
        first += [copy(1 + j, me, (*chip, c), src=x_ref) for j, chip in enumerate(chips)]
        for cp in first:
            cp.start()
        passed = [copy(4 + j, (*chip, c), sibling) for j, chip in enumerate(chips)]
        for j, chip in enumerate(chips):
            copy(1 + j, (*chip, c), me).wait_recv()
            passed[j].start()
        copy(0, sibling, me).wait_recv()
        for j, chip in enumerate(chips):
            copy(4 + j, (*chip, 1 - c), me).wait_recv()
        for cp in first + passed:
            cp.wait_send()
        mine.wait()

    return pl.pallas_call(
        body, name=name,
        out_shape=[jax.ShapeDtypeStruct((N_DEV * m_per, n), x_shard.dtype), jax.ShapeDtypeStruct((8, LANE), F32)],
        in_specs=[pl.BlockSpec(memory_space=pltpu.VMEM)],
        out_specs=[pl.BlockSpec(memory_space=pltpu.VMEM)] * 2,
        scratch_shapes=[pltpu.SemaphoreType.DMA((7,)), pltpu.SemaphoreType.DMA((7,)), pltpu.SemaphoreType.DMA],
    )(x_shard)


def all_gather_hbm(shards, name):
    n_arr = len(shards)

    def body(*refs):
        ins, outs, token = refs[:n_arr], refs[n_arr:2 * n_arr], refs[2 * n_arr]
        send_sems, recv_sems, local_sems = refs[2 * n_arr + 1:]
        x, y, c = _mesh_pos()
        me, sibling = (x, y, c), (x, y, 1 - c)
        chips = [(1 - x, y), (x, 1 - y), (1 - x, 1 - y)]
        token[...] = jnp.zeros_like(token)

        def copy(a, k, block, to, src=None):
            dst = outs[a].at[_slot(block)]
            return pltpu.make_async_remote_copy(
                src_ref=dst if src is None else src, dst_ref=dst,
                send_sem=send_sems.at[7 * a + k], recv_sem=recv_sems.at[7 * a + k],
                device_id=to, device_id_type=MESH_ID)

        mine = [pltpu.make_async_copy(ins[a], outs[a].at[_slot(me)], local_sems.at[a]) for a in range(n_arr)]
        for cp in mine:
            cp.start()
        first = []
        for a in range(n_arr):
            first.append(copy(a, 0, me, sibling, src=ins[a]))
            first += [copy(a, 1 + j, me, (*chip, c), src=ins[a]) for j, chip in enumerate(chips)]
        for cp in first:
            cp.start()
        passed = []
        for a in range(n_arr):
            for j, chip in enumerate(chips):
                copy(a, 1 + j, (*chip, c), me).wait_recv()
                fwd = copy(a, 4 + j, (*chip, c), sibling)
                fwd.start()
                passed.append(fwd)
        for a in range(n_arr):
            copy(a, 0, sibling, me).wait_recv()
            for j, chip in enumerate(chips):
                copy(a, 4 + j, (*chip, 1 - c), me).wait_recv()
        for cp in first + passed:
            cp.wait_send()
        for cp in mine:
            cp.wait()

    any_spec = pl.BlockSpec(memory_space=pl.ANY)
    return pl.pallas_call(
        body, name=name,
        out_shape=[jax.ShapeDtypeStruct((N_DEV,) + s.shape, s.dtype) for s in shards] + [jax.ShapeDtypeStruct((8, LANE), F32)],
        in_specs=[any_spec] * n_arr, out_specs=[any_spec] * n_arr + [pl.BlockSpec(memory_space=pltpu.VMEM)],
        scratch_shapes=[pltpu.SemaphoreType.DMA((7 * n_arr,)), pltpu.SemaphoreType.DMA((7 * n_arr,)),
                        pltpu.SemaphoreType.DMA((n_arr,))],
    )(*shards)


def _peer_list(x, y, c):
    return [((1 - x) if k & 4 else x, (1 - y) if k & 2 else y, (1 - c) if k & 1 else c) for k in range(1, N_DEV)]


def _exchange_copies(srcs, lands, send_sems, recv_sems, slotted, arriving):
    x, y, c = _mesh_pos()
    me = _slot((x, y, c))
    copies = []
    for a in range(len(srcs)):
        for k, peer in enumerate(_peer_list(x, y, c)):
            copies.append(pltpu.make_async_remote_copy(
                src_ref=srcs[a].at[_slot(peer)] if slotted else srcs[a],
                dst_ref=lands[a].at[_slot(peer) if arriving else me],
                send_sem=send_sems.at[7 * a + k], recv_sem=recv_sems.at[7 * a + k],
                device_id=peer, device_id_type=MESH_ID))
    return copies


def unwritten_hbm(shapes, dtype, name):
    def body(*refs):
        pass

    return pl.pallas_call(
        body, name=name, out_shape=[jax.ShapeDtypeStruct(s, dtype) for s in shapes],
        out_specs=[pl.BlockSpec(memory_space=pl.ANY)] * len(shapes),
    )()


HBM_SPEC = pl.BlockSpec(memory_space=pltpu.HBM)
SEM_SPEC = pl.BlockSpec(memory_space=pltpu.SEMAPHORE)
DATAFLOW = pltpu.SideEffectType.DATAFLOW_SIDE_EFFECTING


def exchange_start(srcs, lands, *, slotted, name):
    n = len(srcs)

    def body(*refs):
        ins = refs[:2 * n]
        send_sems, recv_sems = refs[2 * n], refs[2 * n + 1]
        token = refs[-1]
        for cp in _exchange_copies(ins[:n], ins[n:], send_sems, recv_sems, slotted, False):
            cp.start()
        token[...] = jnp.zeros_like(token)

    operands = [pltpu.with_memory_space_constraint(v, pltpu.HBM) for v in list(srcs) + list(lands)]
    out = pl.pallas_call(
        body, name=name,
        out_shape=(pltpu.SemaphoreType.DMA((7 * n,)), pltpu.SemaphoreType.DMA((7 * n,)),
                   *[pltpu.HBM(v.shape, v.dtype) for v in operands], jax.ShapeDtypeStruct((8, LANE), F32)),
        in_specs=[HBM_SPEC] * (2 * n),
        out_specs=(SEM_SPEC, SEM_SPEC, *[HBM_SPEC] * (2 * n), pl.BlockSpec(memory_space=pltpu.VMEM)),
        input_output_aliases={i: 2 + i for i in range(2 * n)},
        compiler_params=pltpu.CompilerParams(has_side_effects=DATAFLOW),
    )(*operands)
    return out[0], out[1], list(out[2:2 + n]), list(out[2 + n:2 + 2 * n]), out[-1]


def exchange_wait(handle, after, *, slotted, name):
    send_sems, recv_sems, srcs, lands, _ = handle
    n = len(srcs)

    def body(*refs):
        ins = refs[:2 * n]
        for cp in _exchange_copies(ins[:n], ins[n:], refs[2 * n], refs[2 * n + 1], slotted, True):
            cp.wait_send()
            cp.wait_recv()

    out = pl.pallas_call(
        body, name=name,
        out_shape=tuple(pltpu.HBM(v.shape, v.dtype) for v in srcs + lands),
        in_specs=[HBM_SPEC] * (2 * n) + [SEM_SPEC, SEM_SPEC, pl.BlockSpec(memory_space=pl.ANY)],
        out_specs=[HBM_SPEC] * (2 * n),
        input_output_aliases={i: i for i in range(2 * n)},
        compiler_params=pltpu.CompilerParams(has_side_effects=DATAFLOW),
    )(*srcs, *lands, send_sems, recv_sems, after)
    return list(out[n:])


VMEM_BLOCK_BUDGET = 46 * 2 ** 20
ROW_TILE, COL_TILE = 1024, 1408
ACC_BUDGET = 12 * 2 ** 20


def _mm_tiles(m, n, row_bytes, col_bytes, elem_bytes):
    tm, tn = _tile(m, ROW_TILE, 16), _tile(n, COL_TILE, LANE)
    while 2 * (tm * row_bytes + tn * col_bytes + tm * tn * elem_bytes) > VMEM_BLOCK_BUDGET:
        narrower = _tile(n, max(tn - LANE, LANE), LANE)
        if tn > 512 and narrower < tn:
            tn = narrower
        else:
            tm //= 2
    return tm, tn


def mm_nn(a, w, *, out_dtype, name):
    m, k = a.shape
    n = w.shape[1]
    tm, tn = _mm_tiles(m, n, 2 * k, 2 * k, jnp.dtype(out_dtype).itemsize)

    def body(a_ref, w_ref, o_ref):
        o_ref[...] = _dot(a_ref[...], w_ref[...], NN).astype(o_ref.dtype)

    return pl.pallas_call(
        body, name=name, grid=(n // tn, m // tm),
        in_specs=[pl.BlockSpec((tm, k), lambda j, i: (i, 0)), pl.BlockSpec((k, tn), lambda j, i: (0, j))],
        out_specs=pl.BlockSpec((tm, tn), lambda j, i: (i, j)),
        out_shape=jax.ShapeDtypeStruct((m, n), out_dtype),
        compiler_params=_params("parallel", "parallel"),
    )(a, w)


def mm_nn_residual(a, w, x_in, gate, coef, *, name):
    m, k = a.shape
    n = w.shape[1]
    tm, tn = _mm_tiles(m, n, 2 * k, 2 * k, 4 + 4 + 2)

    def body(a_ref, w_ref, x_ref, g_ref, o_ref, f_ref):
        f = _dot(a_ref[...], w_ref[...], NN)
        f_ref[...] = f.astype(BF16)
        o_ref[...] = x_ref[...] + (coef * g_ref[...]) * f

    return pl.pallas_call(
        body, name=name, grid=(n // tn, m // tm),
        in_specs=[pl.BlockSpec((tm, k), lambda j, i: (i, 0)), pl.BlockSpec((k, tn), lambda j, i: (0, j)),
                  pl.BlockSpec((tm, tn), lambda j, i: (i, j)), pl.BlockSpec((1, tn), lambda j, i: (0, j))],
        out_specs=[pl.BlockSpec((tm, tn), lambda j, i: (i, j)), pl.BlockSpec((tm, tn), lambda j, i: (i, j))],
        out_shape=[jax.ShapeDtypeStruct((m, n), F32), jax.ShapeDtypeStruct((m, n), BF16)],
        compiler_params=_params("parallel", "parallel"),
    )(a, w, x_in, gate)


def mm_nn_residual_loss(a, w, x_in, gate, coef, target, *, name):
    m, k = a.shape
    n = w.shape[1]
    tm, tn = _mm_tiles(m, n, 2 * k, 2 * k, 4 + 4 + 4 + 2)

    def body(a_ref, w_ref, x_ref, g_ref, t_ref, dy_ref, df_ref, dg_ref, l_ref):
        f = _dot(a_ref[...], w_ref[...], NN)
        err = x_ref[...] + (coef * g_ref[...]) * f - t_ref[...]
        dy = err * (1.0 / n)
        dy_ref[...] = dy
        df_ref[...] = ((coef * g_ref[...]) * dy).astype(BF16)
        dgate = coef * jnp.sum(dy * f, axis=0, keepdims=True)
        part = jnp.sum(jnp.sum(err * err, axis=0, keepdims=True), axis=1, keepdims=True) * (0.5 / n)

        @pl.when(pl.program_id(1) == 0)
        def _():
            dg_ref[...] = jnp.zeros_like(dg_ref)

        @pl.when((pl.program_id(0) == 0) & (pl.program_id(1) == 0))
        def _():
            l_ref[...] = jnp.zeros_like(l_ref)

        dg_ref[...] += dgate
        l_ref[...] += jnp.broadcast_to(part, l_ref.shape)

    blk = pl.BlockSpec((tm, tn), lambda j, i: (i, j))
    vec = pl.BlockSpec((1, tn), lambda j, i: (0, j))
    return pl.pallas_call(
        body, name=name, grid=(n // tn, m // tm),
        in_specs=[pl.BlockSpec((tm, k), lambda j, i: (i, 0)), pl.BlockSpec((k, tn), lambda j, i: (0, j)), blk, vec, blk],
        out_specs=[blk, blk, vec, pl.BlockSpec((1, LANE), lambda j, i: (0, 0))],
        out_shape=[jax.ShapeDtypeStruct((m, n), F32), jax.ShapeDtypeStruct((m, n), BF16),
                   jax.ShapeDtypeStruct((1, n), F32), jax.ShapeDtypeStruct((1, LANE), F32)],
        compiler_params=_params("arbitrary", "arbitrary"),
    )(a, w, x_in, gate, target)


TOKEN_SPEC = pl.BlockSpec((8, LANE), lambda *_: (0, 0))


def mm_nt(a, w, token, *, out_dtype, name):
    m, k = a.shape
    n = w.shape[0]
    tm, tn = _mm_tiles(m, n, 2 * k, 2 * k, jnp.dtype(out_dtype).itemsize)

    def body(a_ref, w_ref, token_ref, o_ref):
        o_ref[...] = _dot(a_ref[...], w_ref[...], NT).astype(o_ref.dtype)

    return pl.pallas_call(
        body, name=name, grid=(n // tn, m // tm),
        in_specs=[pl.BlockSpec((tm, k), lambda j, i: (i, 0)), pl.BlockSpec((tn, k), lambda j, i: (j, 0)), TOKEN_SPEC],
        out_specs=pl.BlockSpec((tm, tn), lambda j, i: (i, j)),
        out_shape=jax.ShapeDtypeStruct((m, n), out_dtype),
        compiler_params=_params("parallel", "parallel"),
    )(a, w, token)


def mm_nt_halves(a_lo, a_hi, w3, token, *, name):
    m = a_lo.shape[0]
    n_blk, n, tn = w3.shape
    half = n_blk // 2
    tm = _tile(m, ROW_TILE, 16)

    def body(lo_ref, hi_ref, w_ref, token_ref, o_ref):
        j = pl.program_id(1)

        @pl.when(j == 0)
        def _():
            o_ref[...] = jnp.zeros_like(o_ref)

        @pl.when(j < half)
        def _():
            o_ref[...] += _dot(lo_ref[...], w_ref[...], NT)

        @pl.when(j >= half)
        def _():
            o_ref[...] += _dot(hi_ref[...], w_ref[...], NT)

    return pl.pallas_call(
        body, name=name, grid=(m // tm, n_blk),
        in_specs=[pl.BlockSpec((tm, tn), lambda i, j: (i, jnp.minimum(j, half - 1))),
                  pl.BlockSpec((tm, tn), lambda i, j: (i, jnp.maximum(j - half, 0))),
                  pl.BlockSpec((None, n, tn), lambda i, j: (j, 0, 0)), TOKEN_SPEC],
        out_specs=pl.BlockSpec((tm, n), lambda i, j: (i, 0)),
        out_shape=jax.ShapeDtypeStruct((m, n), F32),
        compiler_params=_params("parallel", "arbitrary"),
    )(a_lo, a_hi, w3, token)


def mm_tn(a, dy, *, name):
    s, k = a.shape
    n = dy.shape[1]
    ts = _tile(s, ROW_TILE, 16)
    tk = k if k <= 2048 else _tile(k, COL_TILE, LANE)
    tn = _tile(n, ACC_BUDGET // (4 * tk), LANE)
    n_steps = s // ts

    def body(a_ref, dy_ref, o_ref, acc_ref):
        t = pl.program_id(2)

        @pl.when(t == 0)
        def _():
            acc_ref[...] = jnp.zeros_like(acc_ref)

        acc_ref[...] += _dot(a_ref[...], dy_ref[...], TN)

        @pl.when(t == n_steps - 1)
        def _():
            o_ref[...] = acc_ref[...].astype(BF16)

    return pl.pallas_call(
        body, name=name, grid=(k // tk, n // tn, n_steps),
        in_specs=[pl.BlockSpec((ts, tk), lambda kk, j, t: (t, kk)), pl.BlockSpec((ts, tn), lambda kk, j, t: (t, j))],
        out_specs=pl.BlockSpec((tk, tn), lambda kk, j, t: (kk, j)),
        out_shape=jax.ShapeDtypeStruct((k, n), BF16),
        scratch_shapes=[pltpu.VMEM((tk, tn), F32)],
        compiler_params=_params("parallel", "parallel", "arbitrary"),
    )(a, dy)


def mm_tn_halves(a, dy_lo, dy_hi, n_blocks, *, name):
    s, k = a.shape
    half = n_blocks // 2
    tn = dy_lo.shape[1] // half
    ts = _tile(s, ROW_TILE, 16)
    n_steps = s // ts

    def body(a_ref, lo_ref, hi_ref, o_ref, acc_ref):
        j, t = pl.program_id(0), pl.program_id(1)

        @pl.when(t == 0)
        def _():
            acc_ref[...] = jnp.zeros_like(acc_ref)

        @pl.when(j < half)
        def _():
            acc_ref[...] += _dot(a_ref[...], lo_ref[...], TN)

        @pl.when(j >= half)
        def _():
            acc_ref[...] += _dot(a_ref[...], hi_ref[...], TN)

        @pl.when(t == n_steps - 1)
        def _():
            o_ref[...] = acc_ref[...].astype(BF16)

    return pl.pallas_call(
        body, name=name, grid=(n_blocks, n_steps),
        in_specs=[pl.BlockSpec((ts, k), lambda j, t: (t, 0)),
                  pl.BlockSpec((ts, tn), lambda j, t: (jnp.where(j < half, t, n_steps - 1), jnp.minimum(j, half - 1))),
                  pl.BlockSpec((ts, tn), lambda j, t: (jnp.where(j < half, 0, t), jnp.maximum(j - half, 0)))],
        out_specs=pl.BlockSpec((None, k, tn), lambda j, t: (j, 0, 0)),
        out_shape=jax.ShapeDtypeStruct((n_blocks, k, tn), BF16),
        scratch_shapes=[pltpu.VMEM((k, tn), F32)],
        compiler_params=_params("parallel", "arbitrary"),
    )(a, dy_lo, dy_hi)


def ffn_dgu(df, wd, g, u, token, *, name):
    m, k = df.shape
    n = wd.shape[0]
    tm, tn = _mm_tiles(m, n, 2 * k, 2 * k, 4 * 2)

    def body(df_ref, w_ref, g_ref, u_ref, token_ref, dg_ref, du_ref):
        da = _dot(df_ref[...], w_ref[...], NT)
        gg, uu = g_ref[...].astype(F32), u_ref[...].astype(F32)
        sg = _sigmoid(gg)
        dg_ref[...] = (da * uu * (sg * (1 + gg * (1 - sg)))).astype(BF16)
        du_ref[...] = (da * (gg * sg)).astype(BF16)

    blk = pl.BlockSpec((tm, tn), lambda j, i: (i, j))
    out = jax.ShapeDtypeStruct((m, n), BF16)
    return pl.pallas_call(
        body, name=name, grid=(n // tn, m // tm),
        in_specs=[pl.BlockSpec((tm, k), lambda j, i: (i, 0)), pl.BlockSpec((tn, k), lambda j, i: (j, 0)), blk, blk, TOKEN_SPEC],
        out_specs=[blk, blk], out_shape=[out, out],
        compiler_params=_params("parallel", "parallel"),
    )(df, wd, g, u, token)


def ffn_up(h, wgu3, *, name, tm=512):
    s, k = h.shape
    n = wgu3.shape[2]
    half = wgu3.shape[0] // 2
    tm = _tile(s, tm, 16)

    def body(h_ref, wg_ref, wu_ref, g_ref, u_ref, a_ref):
        hh = h_ref[...]
        g = _dot(hh, wg_ref[...], NN)
        u = _dot(hh, wu_ref[...], NN)
        g_ref[...] = g.astype(BF16)
        u_ref[...] = u.astype(BF16)
        a_ref[...] = (g * _sigmoid(g) * u).astype(BF16)

    out = jax.ShapeDtypeStruct((s, half * n), BF16)
    blk = pl.BlockSpec((tm, n), lambda j, i: (i, j))
    return pl.pallas_call(
        body, name=name, grid=(half, s // tm),
        in_specs=[pl.BlockSpec((tm, k), lambda j, i: (i, 0)),
                  pl.BlockSpec((None, k, n), lambda j, i: (j, 0, 0)),
                  pl.BlockSpec((None, k, n), lambda j, i: (j + half, 0, 0))],
        out_specs=[blk, blk, blk], out_shape=[out, out, out],
        compiler_params=_params("parallel", "parallel"),
    )(h, wgu3, wgu3)


def _row_spec(ts, width, col=0):
    return pl.BlockSpec((ts, width), lambda i: (i, col))


def _vec_spec(width):
    return pl.BlockSpec((1, width), lambda i: (0, 0))


def _accumulate(ref, value):
    i = pl.program_id(0)

    @pl.when(i == 0)
    def _():
        ref[...] = value

    @pl.when(i > 0)
    def _():
        ref[...] += value


def norm_modulate(x, g, shift, scale, token, *, name, ts=512):
    s, d = x.shape
    ts = _tile(s, ts, 16)

    def body(x_ref, g_ref, sh_ref, sc_ref, token_ref, h_ref):
        xx = x_ref[...]
        r = lax.rsqrt(jnp.mean(xx * xx, axis=-1, keepdims=True) + EPS)
        h_ref[...] = ((xx * r) * g_ref[...] * (1 + sc_ref[...]) + sh_ref[...]).astype(BF16)

    return pl.pallas_call(
        body, name=name, grid=(s // ts,),
        in_specs=[_row_spec(ts, d), _vec_spec(d), _vec_spec(d), _vec_spec(d), TOKEN_SPEC],
        out_specs=_row_spec(ts, d), out_shape=jax.ShapeDtypeStruct((s, d), BF16),
        compiler_params=_params("parallel"),
    )(x, g, shift, scale, token)


def norm_modulate_bwd(x, g, scale, dh, dx_out, below=None, *, name, ts=256):
    s, d = x.shape
    ts = _tile(s, ts, 16)
    coef = None if below is None else below[2]

    def body(*refs):
        x_ref, g_ref, sc_ref, dh_ref, dxo_ref = refs[:5]
        dx_ref, dsh_ref, dsc_ref, dg_ref = refs[-6:-2] if below else refs[-4:]
        xx, dh_ = x_ref[...], dh_ref[...].astype(F32)
        r = lax.rsqrt(jnp.mean(xx * xx, axis=-1, keepdims=True) + EPS)
        xh = xx * r
        dn = dh_ * (1 + sc_ref[...])
        dxh = dn * g_ref[...]
        dx = dxo_ref[...] + r * (dxh - xh * jnp.mean(dxh * xh, axis=-1, keepdims=True))
        dx_ref[...] = dx
        _accumulate(dsh_ref, jnp.sum(dh_, axis=0, keepdims=True))
        _accumulate(dsc_ref, jnp.sum(dh_ * (xh * g_ref[...]), axis=0, keepdims=True))
        _accumulate(dg_ref, jnp.sum(dn * xh, axis=0, keepdims=True))
        if below:
            f_ref, gate_ref, df_ref, dgate_ref = refs[5], refs[6], refs[-2], refs[-1]
            df_ref[...] = ((coef * gate_ref[...]) * dx).astype(BF16)
            _accumulate(dgate_ref, coef * jnp.sum(dx * f_ref[...].astype(F32), axis=0, keepdims=True))

    vec = jax.ShapeDtypeStruct((1, d), F32)
    extra_in, extra_out, extra_shape, extra_args = [], [], [], []
    if below:
        extra_in, extra_args = [_row_spec(ts, d), _vec_spec(d)], [below[0], below[1]]
        extra_out, extra_shape = [_row_spec(ts, d), _vec_spec(d)], [jax.ShapeDtypeStruct((s, d), BF16), vec]
    return pl.pallas_call(
        body, name=name, grid=(s // ts,),
        in_specs=[_row_spec(ts, d), _vec_spec(d), _vec_spec(d), _row_spec(ts, d), _row_spec(ts, d)] + extra_in,
        out_specs=[_row_spec(ts, d), _vec_spec(d), _vec_spec(d), _vec_spec(d)] + extra_out,
        out_shape=[jax.ShapeDtypeStruct((s, d), F32), vec, vec, vec] + extra_shape,
        compiler_params=_params("arbitrary"),
    )(x, g, scale, dh, dx_out, *extra_args)


def merge_fwd(z, y_pool, y_attn, *, name, ts=1024):
    s, d = y_pool.shape
    ts = _tile(s, ts, 16)

    def body(gate_ref, yp_ref, ya_ref, o_ref):
        sa = _sigmoid(gate_ref[:, :GATE_TILE].astype(F32))
        sb = _sigmoid(gate_ref[:, GATE_TILE:].astype(F32))
        o_ref[...] = (sa * yp_ref[...] + sb * ya_ref[...]).astype(BF16)

    blk = pl.BlockSpec((ts, GATE_TILE), lambda i, j: (i, j))
    return pl.pallas_call(
        body, name=name, grid=(s // ts, d // GATE_TILE),
        in_specs=[pl.BlockSpec((ts, 2 * GATE_TILE), lambda i, j: (i, j)), blk, blk],
        out_specs=blk, out_shape=jax.ShapeDtypeStruct((s, d), BF16),
        compiler_params=_params("parallel", "parallel"),
    )(z, y_pool, y_attn)


def merge_bwd(d_o, wo, z, y_pool, y_attn, token, *, name):
    m, k = d_o.shape
    n = wo.shape[0]
    tm, tn = _tile(m, ROW_TILE, 16), GATE_TILE

    def body(do_ref, w_ref, gate_ref, yp_ref, ya_ref, token_ref, dyp_ref, dya_ref, dz_ref):
        dm = _dot(do_ref[...], w_ref[...], NT)
        sa = _sigmoid(gate_ref[:, :tn].astype(F32))
        sb = _sigmoid(gate_ref[:, tn:].astype(F32))
        dyp_ref[...] = (dm * sa).astype(BF16)
        dya_ref[...] = (dm * sb).astype(BF16)
        dz_ref[:, :tn] = (dm * yp_ref[...] * (sa * (1 - sa))).astype(BF16)
        dz_ref[:, tn:] = (dm * ya_ref[...] * (sb * (1 - sb))).astype(BF16)

    blk = pl.BlockSpec((tm, tn), lambda j, i: (i, j))
    pair = pl.BlockSpec((tm, 2 * tn), lambda j, i: (i, j))
    out = jax.ShapeDtypeStruct((m, n), BF16)
    return pl.pallas_call(
        body, name=name, grid=(n // tn, m // tm),
        in_specs=[pl.BlockSpec((tm, k), lambda j, i: (i, 0)), pl.BlockSpec((tn, k), lambda j, i: (j, 0)),
                  pair, blk, blk, TOKEN_SPEC],
        out_specs=[blk, blk, pair], out_shape=[out, out, jax.ShapeDtypeStruct(z.shape, BF16)],
        compiler_params=_params("parallel", "parallel"),
    )(d_o, wo, z, y_pool, y_attn, token)


def _window_counts(t0, rows):
    t1 = (t0 + 1 + lax.broadcasted_iota(jnp.int32, (rows, 1), 0)).astype(F32)
    return [jnp.minimum(t1, float(w)) for w in POOL_WINDOWS]


def pool_fwd(z, pool_mix, pool_scale, *, name, ts=256):
    s = z.shape[0]
    ts = _tile(s, ts, 16)
    per = ts // POOL_HALO

    def body(u_ref, halo_ref, pm_ref, ps_ref, pooled_ref, p_ref):
        i = pl.program_id(0)
        u = u_ref[...].astype(F32)
        halo = jnp.where(i > 0, halo_ref[...].astype(F32), 0.0)
        run = jnp.concatenate([halo, u], axis=0)
        sums, width = [], 1
        for w in POOL_WINDOWS:
            while width < w:
                run = run + pltpu.roll(run, width, 0)
                width *= 2
            sums.append(run[POOL_HALO:])
        counts = _window_counts(i * ts, ts)
        for gi in range(POOL_GROUPS):
            cols = slice(gi * POOL_GROUP_W, (gi + 1) * POOL_GROUP_W)
            pooled = (sums[gi][:, cols] / counts[gi] - u[:, cols]).astype(BF16)
            pooled_ref[:, cols] = pooled
            p_ref[:, cols] = (_dot(pooled, pm_ref[gi], NN) * ps_ref[:, cols]).astype(BF16)

    out = jax.ShapeDtypeStruct((s, POOL_W), BF16)
    return pl.pallas_call(
        body, name=name, grid=(s // ts,),
        in_specs=[_row_spec(ts, POOL_W, COL_U // POOL_W),
                  pl.BlockSpec((POOL_HALO, POOL_W), lambda i: (jnp.maximum(i * per - 1, 0), COL_U // POOL_W)),
                  pl.BlockSpec((POOL_GROUPS, POOL_GROUP_W, POOL_GROUP_W), lambda i: (0, 0, 0)),
                  _vec_spec(POOL_W)],
        out_specs=[_row_spec(ts, POOL_W)] * 2, out_shape=[out, out],
        compiler_params=_params("parallel"),
    )(z, z, pool_mix, pool_scale)


def pool_bwd(dp, pooled, pool_mix, pool_scale, dz, *, name, ts=256):
    s = dp.shape[0]
    ts = _tile(s, ts, 16)
    per = ts // POOL_HALO
    n_steps = s // ts
    rows = ts + POOL_HALO

    def body(dp_ref, halo_ref, pooled_ref, pm_ref, ps_ref, dz_in_ref, du_ref, dps_ref, dpm_ref):
        i = pl.program_id(0)
        dp_main = dp_ref[...]
        halo = jnp.where(i < n_steps - 1, halo_ref[...], 0.0)
        dmixed = jnp.concatenate([dp_main, halo], axis=0) * ps_ref[...]
        counts = _window_counts(i * ts, rows)
        dps_parts = []
        for gi, w in enumerate(POOL_WINDOWS):
            cols = slice(gi * POOL_GROUP_W, (gi + 1) * POOL_GROUP_W)
            dmx = dmixed[:, cols].astype(BF16)
            pooled = pooled_ref[:, cols]
            mixed = _dot(pooled, pm_ref[gi], NN)
            dps_parts.append(jnp.sum(dp_main[:, cols] * mixed, axis=0, keepdims=True))
            dpm_g = _dot(pooled, dmx[:ts], TN)

            @pl.when(i == 0)
            def _():
                dpm_ref[gi] = dpm_g

            @pl.when(i > 0)
            def _():
                dpm_ref[gi] += dpm_g

            dpooled = _dot(dmx, pm_ref[gi], NT)
            run, width = dpooled / counts[gi], 1
            while width < w:
                run = run + pltpu.roll(run, rows - width, 0)
                width *= 2
            du_ref[:, cols] = (run[:ts] - dpooled[:ts]).astype(BF16)
        _accumulate(dps_ref, jnp.concatenate(dps_parts, axis=1))

    return pl.pallas_call(
        body, name=name, grid=(n_steps,),
        in_specs=[_row_spec(ts, POOL_W),
                  pl.BlockSpec((POOL_HALO, POOL_W), lambda i: (jnp.minimum((i + 1) * per, s // POOL_HALO - 1), 0)),
                  _row_spec(ts, POOL_W),
                  pl.BlockSpec((POOL_GROUPS, POOL_GROUP_W, POOL_GROUP_W), lambda i: (0, 0, 0)),
                  _vec_spec(POOL_W), pl.BlockSpec(memory_space=pl.ANY)],
        out_specs=[_row_spec(ts, POOL_W, COL_U // POOL_W), _vec_spec(POOL_W),
                   pl.BlockSpec((POOL_GROUPS, POOL_GROUP_W, POOL_GROUP_W), lambda i: (0, 0, 0))],
        out_shape=[jax.ShapeDtypeStruct(dz.shape, BF16), jax.ShapeDtypeStruct((1, POOL_W), F32),
                   jax.ShapeDtypeStruct((POOL_GROUPS, POOL_GROUP_W, POOL_GROUP_W), F32)],
        input_output_aliases={5: 0},
        compiler_params=_params("arbitrary"),
    )(dp, dp, pooled, pool_mix, pool_scale, dz)


def _bucket_one_hot():
    ql = np.arange(BLK)[:, None]
    j = np.arange(2 * BLK)[None, :]
    n = np.clip(BLK + ql - j, 0, None)
    nf = np.maximum(n, 1).astype(np.float32)
    large = MAX_EXACT + (np.log(nf / MAX_EXACT) / np.log(REL_MAX_DIST / MAX_EXACT)
                         * (NUM_BUCKETS - MAX_EXACT)).astype(np.int32)
    large = np.minimum(large, NUM_BUCKETS - 1)
    bucket = np.where(n < MAX_EXACT, n, large).astype(np.int32).reshape(-1)
    return (np.arange(NUM_BUCKETS)[:, None] == bucket[None, :]).astype(np.float32)


def bias_table(rel_bias_t, one_hot, *, name, tc=4096):
    n = one_hot.shape[1]

    def body(rb_ref, oh_ref, o_ref):
        o_ref[...] = _dot(rb_ref[...], oh_ref[...], NN, precision=lax.Precision.HIGHEST)

    return pl.pallas_call(
        body, name=name, grid=(n // tc,),
        in_specs=[pl.BlockSpec((N_HEADS, NUM_BUCKETS), lambda i: (0, 0)), pl.BlockSpec((NUM_BUCKETS, tc), lambda i: (0, i))],
        out_specs=pl.BlockSpec((N_HEADS, tc), lambda i: (0, i)),
        out_shape=jax.ShapeDtypeStruct((N_HEADS, n), F32),
        compiler_params=_params("parallel"),
    )(rel_bias_t, one_hot)


def bias_table_bwd(dbias, one_hot, *, name, tc=4096):
    n = one_hot.shape[1]

    def body(db_ref, oh_ref, o_ref):
        _accumulate(o_ref, _dot(db_ref[...], oh_ref[...], NT, precision=lax.Precision.HIGHEST))

    return pl.pallas_call(
        body, name=name, grid=(n // tc,),
        in_specs=[pl.BlockSpec((N_HEADS, tc), lambda i: (0, i)), pl.BlockSpec((NUM_BUCKETS, tc), lambda i: (0, i))],
        out_specs=pl.BlockSpec((N_HEADS, NUM_BUCKETS), lambda i: (0, 0)),
        out_shape=jax.ShapeDtypeStruct((N_HEADS, NUM_BUCKETS), F32),
        compiler_params=_params("arbitrary"),
    )(dbias, one_hot)


def _lane_half(shape):
    return lax.broadcasted_iota(jnp.int32, shape, len(shape) - 1) < HEAD_DIM


def _half_sums(v, first):
    s0 = jnp.sum(jnp.where(first, v, 0.0), axis=-1, keepdims=True)
    s1 = jnp.sum(jnp.where(first, 0.0, v), axis=-1, keepdims=True)
    return jnp.where(first, s0, s1)


GROUP_ROWS = GQA * BLK


def _band_mask(n):
    ql = lax.broadcasted_iota(jnp.int32, (GROUP_ROWS, 2 * BLK), 0) & (BLK - 1)
    j = lax.broadcasted_iota(jnp.int32, (GROUP_ROWS, 2 * BLK), 1)
    return (j > ql) & (j <= ql + BLK) & ((j >= BLK) | (n > 0))


def _norm_keys(kband):
    first = _lane_half(kband.shape)
    r = lax.rsqrt(_half_sums(kband * kband, first) * (1.0 / HEAD_DIM) + EPS)
    return kband * r


def _kv_lanes(kv, shape):
    return _lane_half(shape) if kv == 0 else jnp.logical_not(_lane_half(shape))


def _stack_group(ref, kv, dtype=F32):
    parts = []
    for g in range(GQA):
        h = kv * GQA + g
        part = ref[:, (h // 2) * LANE:(h // 2 + 1) * LANE].astype(dtype)
        parts.append(pltpu.roll(part, HEAD_DIM, 1) if h % 2 != kv else part)
    stacked = jnp.concatenate(parts, axis=0)
    return jnp.where(_kv_lanes(kv, stacked.shape), stacked, 0.0)


def _unstack_group(ref, kv, stacked):
    for i in range(GQA // 2):
        pair = None
        for sub in range(2):
            g = 2 * i + sub
            part = stacked[g * BLK:(g + 1) * BLK]
            part = pltpu.roll(part, HEAD_DIM, 1) if sub != kv else part
            pair = part if pair is None else pair + part
        col = (kv * GQA // 2 + i) * LANE
        ref[:, col:col + LANE] = pair.astype(BF16)


def _group_logits(q_ref, kv, qg, kn, bias_ref, sink_ref, mask):
    qa = _stack_group(q_ref, kv)
    r = lax.rsqrt(jnp.sum(qa * qa, axis=-1, keepdims=True) * (1.0 / HEAD_DIM) + EPS)
    xh = qa * r
    qn = (xh * qg).astype(BF16)
    bias = bias_ref[kv * GQA:(kv + 1) * GQA].reshape(GROUP_ROWS, 2 * BLK)
    logits = _dot(qn, kn, NT) * (HEAD_DIM ** -0.5) + bias
    p, p_sink = _softmax_with_sink(jnp.where(mask, logits, NEG_INF), sink_ref[kv * GROUP_ROWS:(kv + 1) * GROUP_ROWS])
    return xh, r, qn, p, p_sink


def _softmax_with_sink(logits, sink):
    m = jnp.maximum(jnp.max(logits, axis=-1, keepdims=True), sink)
    e = jnp.exp(logits - m)
    es = jnp.exp(sink - m)
    den = jnp.sum(e, axis=-1, keepdims=True) + es
    return e / den, es / den


def _attn_specs(nb, last):
    cur = lambda n: jnp.minimum(n, last)
    prev = lambda n: jnp.minimum(jnp.maximum(n - 1, 0), last)
    return [pl.BlockSpec((BLK, ATT_W), lambda n: (cur(n), COL_Q // ATT_W)),
            pl.BlockSpec((BLK, KV_W), lambda n: (prev(n), COL_K // KV_W)),
            pl.BlockSpec((BLK, KV_W), lambda n: (cur(n), COL_K // KV_W)),
            pl.BlockSpec((BLK, KV_W), lambda n: (prev(n), COL_V // KV_W)),
            pl.BlockSpec((BLK, KV_W), lambda n: (cur(n), COL_V // KV_W))]


def attn_fwd(z, qg2, kg2, sinks, bias, *, name):
    s = z.shape[0]
    nb = s // BLK

    def body(q_ref, kp_ref, kc_ref, vp_ref, vc_ref, qg_ref, kg_ref, sink_ref, bias_ref, o_ref):
        mask = _band_mask(pl.program_id(0))
        kn = (_norm_keys(jnp.concatenate([kp_ref[...], kc_ref[...]], axis=0).astype(F32)) * kg_ref[...]).astype(BF16)
        vb = jnp.concatenate([vp_ref[...], vc_ref[...]], axis=0).astype(BF16)
        for kv in range(N_KV):
            _, _, _, p, _ = _group_logits(q_ref, kv, qg_ref[...], kn, bias_ref, sink_ref, mask)
            out = _dot(p.astype(BF16), vb, NN)
            _unstack_group(o_ref, kv, jnp.where(_kv_lanes(kv, out.shape), out, 0.0))

    return pl.pallas_call(
        body, name=name, grid=(nb,),
        in_specs=_attn_specs(nb, nb - 1) + [
            _vec_spec(LANE), _vec_spec(LANE), pl.BlockSpec((N_HEADS * BLK, 1), lambda n: (0, 0)),
            pl.BlockSpec((N_HEADS, BLK, 2 * BLK), lambda n: (0, 0, 0))],
        out_specs=pl.BlockSpec((BLK, ATT_W), lambda n: (n, 0)),
        out_shape=jax.ShapeDtypeStruct((s, ATT_W), BF16),
        compiler_params=_params("parallel"),
    )(z, z, z, z, z, qg2, kg2, sinks, bias)


def attn_bwd(z, d_out, qg2, kg2, sinks, bias, dz, *, name):
    s = z.shape[0]
    nb = s // BLK
    scale = HEAD_DIM ** -0.5

    def body(q_ref, kp_ref, kc_ref, vp_ref, vc_ref, do_ref, qg_ref, kg_ref, sink_ref, bias_ref, dz_in_ref,
             dq_ref, dk_ref, dv_ref, dqg_ref, dkg_ref, dsink_ref, dbias_ref, band_k, band_v, carry_k, carry_v, dsink_rows):
        n = pl.program_id(0)

        @pl.when(n == 0)
        def _():
            dqg_ref[...] = jnp.zeros_like(dqg_ref)
            dkg_ref[...] = jnp.zeros_like(dkg_ref)
            dbias_ref[...] = jnp.zeros_like(dbias_ref)
            carry_k[...] = jnp.zeros_like(carry_k)
            carry_v[...] = jnp.zeros_like(carry_v)
            dsink_rows[...] = jnp.zeros_like(dsink_rows)

        @pl.when(n == nb)
        def _():
            band_k[...] = jnp.zeros_like(band_k)
            band_v[...] = jnp.zeros_like(band_v)

        @pl.when(n < nb)
        def _():
            mask = _band_mask(n)
            kn = (_norm_keys(jnp.concatenate([kp_ref[...], kc_ref[...]], axis=0).astype(F32)) * kg_ref[...]).astype(BF16)
            vb = jnp.concatenate([vp_ref[...], vc_ref[...]], axis=0).astype(BF16)
            dkn = jnp.zeros((2 * BLK, KV_W), F32)
            dvb = jnp.zeros((2 * BLK, KV_W), F32)
            dqg = jnp.zeros((1, LANE), F32)
            for kv in range(N_KV):
                rows = slice(kv * GROUP_ROWS, (kv + 1) * GROUP_ROWS)
                xh, r, qn, p, p_sink = _group_logits(q_ref, kv, qg_ref[...], kn, bias_ref, sink_ref, mask)
                do = _stack_group(do_ref, kv).astype(BF16)
                dp = _dot(do, vb, NT)
                delta = jnp.sum(p * dp, axis=-1, keepdims=True)
                ds = p * (dp - delta)
                dsink_rows[rows] -= p_sink * delta
                dbias_ref[kv * GQA:(kv + 1) * GQA] += ds.reshape(GQA, BLK, 2 * BLK)
                ds16 = ds.astype(BF16)
                dqn = jnp.where(_kv_lanes(kv, xh.shape), _dot(ds16, kn, NN) * scale, 0.0)
                dkn = dkn + _dot(ds16, qn, TN) * scale
                dvb = dvb + _dot(p.astype(BF16), do, TN)
                dqg = dqg + jnp.sum(dqn * xh, axis=0, keepdims=True)
                dxh = dqn * qg_ref[...]
                _unstack_group(dq_ref, kv, r * (dxh - xh * (jnp.sum(dxh * xh, axis=-1, keepdims=True) * (1.0 / HEAD_DIM))))
            band_k[...] = dkn
            band_v[...] = dvb
            dqg_ref[...] += dqg

        dkn_prev = carry_k[...] + band_k[:BLK]
        dv_ref[...] = (carry_v[...] + band_v[:BLK]).astype(BF16)
        carry_k[...] = band_k[BLK:]
        carry_v[...] = band_v[BLK:]
        kp = kp_ref[...].astype(F32)
        first = _lane_half(kp.shape)
        r = lax.rsqrt(_half_sums(kp * kp, first) * (1.0 / HEAD_DIM) + EPS)
        xh = kp * r
        dkg_ref[...] += jnp.sum(dkn_prev * xh, axis=0, keepdims=True)
        dxh = dkn_prev * kg_ref[...]
        dk_ref[...] = (r * (dxh - xh * (_half_sums(dxh * xh, first) * (1.0 / HEAD_DIM)))).astype(BF16)

        @pl.when(n == nb)
        def _():
            dqg_ref[...] += pltpu.roll(dqg_ref[...], HEAD_DIM, 1)
            dkg_ref[...] += pltpu.roll(dkg_ref[...], HEAD_DIM, 1)
            lane16 = lax.broadcasted_iota(jnp.int32, (1, N_HEADS), 1)
            dsink = jnp.zeros((1, N_HEADS), F32)
            for h in range(N_HEADS):
                dsink = dsink + jnp.where(lane16 == h, jnp.sum(dsink_rows[h * BLK:(h + 1) * BLK], axis=0, keepdims=True), 0.0)
            dsink_ref[...] = dsink

    last = nb - 1
    cur = lambda n: jnp.minimum(n, last)
    back = lambda n: jnp.maximum(n - 1, 0)
    full3 = pl.BlockSpec((N_HEADS, BLK, 2 * BLK), lambda n: (0, 0, 0))
    return pl.pallas_call(
        body, name=name, grid=(nb + 1,),
        in_specs=_attn_specs(nb, last) + [
            pl.BlockSpec((BLK, ATT_W), lambda n: (cur(n), 0)),
            _vec_spec(LANE), _vec_spec(LANE), pl.BlockSpec((N_HEADS * BLK, 1), lambda n: (0, 0)), full3,
            pl.BlockSpec(memory_space=pl.ANY)],
        out_specs=[pl.BlockSpec((BLK, ATT_W), lambda n: (cur(n), COL_Q // ATT_W)),
                   pl.BlockSpec((BLK, KV_W), lambda n: (back(n), 0)),
                   pl.BlockSpec((BLK, KV_W), lambda n: (back(n), 0)),
                   _vec_spec(LANE), _vec_spec(LANE), _vec_spec(N_HEADS), full3],
        input_output_aliases={10: 0},
        out_shape=[jax.ShapeDtypeStruct(dz.shape, BF16), jax.ShapeDtypeStruct((s, KV_W), BF16),
                   jax.ShapeDtypeStruct((s, KV_W), BF16), jax.ShapeDtypeStruct((1, LANE), F32),
                   jax.ShapeDtypeStruct((1, LANE), F32), jax.ShapeDtypeStruct((1, N_HEADS), F32),
                   jax.ShapeDtypeStruct((N_HEADS, BLK, 2 * BLK), F32)],
        scratch_shapes=[pltpu.VMEM((2 * BLK, KV_W), F32), pltpu.VMEM((2 * BLK, KV_W), F32),
                        pltpu.VMEM((BLK, KV_W), F32), pltpu.VMEM((BLK, KV_W), F32), pltpu.VMEM((N_HEADS * BLK, 1), F32)],
        compiler_params=_params("arbitrary"),
    )(z, z, z, z, z, d_out, qg2, kg2, sinks, bias, dz)


def _adamw(w, g, m, v):
    m = ADAM_B1 * m + (1.0 - ADAM_B1) * g
    v = ADAM_B2 * v + (1.0 - ADAM_B2) * (g * g)
    m_hat = m / (1.0 - ADAM_B1 ** ADAM_STEP)
    v_hat = v / (1.0 - ADAM_B2 ** ADAM_STEP)
    delta = -ADAM_LR * (m_hat / (jnp.sqrt(v_hat) + ADAM_EPS) + ADAM_WD * w)
    return delta, m, v


def ada_fwd(c16, w, b, *, name, tn=768):
    k, n = w.shape
    tn = _tile(n, tn, LANE)

    def body(c_ref, w_ref, b_ref, o_ref):
        cc = c_ref[...]
        o_ref[...] = _dot((cc * _sigmoid(cc)).astype(BF16), w_ref[...].astype(BF16), NN) + b_ref[...]

    return pl.pallas_call(
        body, name=name, grid=(n // tn,),
        in_specs=[pl.BlockSpec((c16.shape[0], k), lambda j: (0, 0)), pl.BlockSpec((k, tn), lambda j: (0, j)),
                  pl.BlockSpec((1, tn), lambda j: (0, j))],
        out_specs=pl.BlockSpec((c16.shape[0], tn), lambda j: (0, j)),
        out_shape=jax.ShapeDtypeStruct((c16.shape[0], n), F32),
        compiler_params=_params("parallel"),
    )(c16, w, b)


def ada_bwd_adamw(c_t, dmod, w, m, v, *, name, tn=256):
    k, n = w.shape
    tn = _tile(n, tn, LANE)

    def body(c_ref, d_ref, w_ref, m_ref, v_ref, g_ref, dl_ref, mo_ref, vo_ref):
        cc = c_ref[...]
        g = _dot((cc * _sigmoid(cc)).astype(BF16), d_ref[...].astype(BF16), NN)
        g_ref[...] = g
        dl_ref[...], mo_ref[...], vo_ref[...] = _adamw(w_ref[...], g, m_ref[...], v_ref[...])

    blk = pl.BlockSpec((k, tn), lambda j: (0, j))
    out = jax.ShapeDtypeStruct((k, n), F32)
    return pl.pallas_call(
        body, name=name, grid=(n // tn,),
        in_specs=[pl.BlockSpec((k, LANE), lambda j: (0, 0)), pl.BlockSpec((LANE, tn), lambda j: (0, j)), blk, blk, blk],
        out_specs=[blk] * 4, out_shape=[out] * 4,
        compiler_params=_params("parallel"),
    )(c_t, dmod, w, m, v)


def adamw_from_parts(parts, w, m, v, *, name):
    r, c = w.shape
    tr = _tile(r, max(16, (256 * 1024) // c), 16)

    def body(p_ref, w_ref, m_ref, v_ref, g_ref, dl_ref, mo_ref, vo_ref):
        g = p_ref[0].astype(F32)
        for d in range(1, N_DEV):
            g = g + p_ref[d].astype(F32)
        g_ref[...] = g
        dl_ref[...], mo_ref[...], vo_ref[...] = _adamw(w_ref[...], g, m_ref[...], v_ref[...])

    blk = pl.BlockSpec((tr, c), lambda i: (i, 0))
    out = jax.ShapeDtypeStruct((r, c), F32)
    return pl.pallas_call(
        body, name=name, grid=(r // tr,),
        in_specs=[pl.BlockSpec((N_DEV, tr, c), lambda i: (0, i, 0)), blk, blk, blk],
        out_specs=[blk] * 4, out_shape=[out] * 4,
        compiler_params=_params("parallel"),
    )(parts, w, m, v)


def _ffn_fwd(x_in, g, shift, scale, gate, wgu3, get_wd, token, tag, target=None):
    h = norm_modulate(x_in, g, shift, scale, token, name=f"{tag}_norm")
    gg, uu, act = ffn_up(h, wgu3, name=f"{tag}_up")
    wd = get_wd(gg)
    if target is None:
        x_out, f = mm_nn_residual(act, wd, x_in, gate, 0.5, name=f"{tag}_down")
    else:
        x_out, f = mm_nn_residual_loss(act, wd, x_in, gate, 0.5, target, name=f"{tag}_down_loss"), None
    return x_out, (h, gg, uu, act, f), wd


def _ffn_bwd(dx_out, df, dgate, x_in, g, scale, wgu3, wd, saved, token, scatter, split, tag, below=None):
    h, gg, uu, act, _ = saved
    dwd = mm_tn(act, df, name=f"{tag}_dwd").reshape(N_DEV, -1, D_MODEL)
    if split:
        token = scatter([f"w_{tag}_down"], [dwd], f"scatter_{tag}_down")
    dgg, duu = ffn_dgu(df, wd, gg, uu, token, name=f"{tag}_dgu")
    dwgu = mm_tn_halves(h, dgg, duu, N_DEV, name=f"{tag}_dwgu")
    if split:
        token = scatter([f"w_{tag}_gu"], [dwgu], f"scatter_{tag}_gu")
    else:
        token = scatter([f"w_{tag}_gu", f"w_{tag}_down"], [dwgu, dwd], f"scatter_{tag}")
    dh = mm_nt_halves(dgg, duu, wgu3, token, name=f"{tag}_dh")
    dx_in, dshift, dscale, dg, *rest = norm_modulate_bwd(x_in, g, scale, dh, dx_out, below, name=f"{tag}_norm_bwd")
    return (dx_in, (dshift, dscale, dgate), dg, *rest)


def kernel(x, c, w_ada, b_ada, g_ffn1, w_ffn1_gu, w_ffn1_down, g_mix, w_in, pool_mix, pool_scale, w_pool_up, q_gain, k_gain, sinks, rel_bias, w_attn_up, w_o, g_ffn2, w_ffn2_gu, w_ffn2_down, loss_target, m_w_ada, m_b_ada, m_g_ffn1, m_w_ffn1_gu, m_w_ffn1_down, m_g_mix, m_w_in, m_pool_mix, m_pool_scale, m_w_pool_up, m_q_gain, m_k_gain, m_sinks, m_rel_bias, m_w_attn_up, m_w_o, m_g_ffn2, m_w_ffn2_gu, m_w_ffn2_down, v_w_ada, v_b_ada, v_g_ffn1, v_w_ffn1_gu, v_w_ffn1_down, v_g_mix, v_w_in, v_pool_mix, v_pool_scale, v_w_pool_up, v_q_gain, v_k_gain, v_sinks, v_rel_bias, v_w_attn_up, v_w_o, v_g_ffn2, v_w_ffn2_gu, v_w_ffn2_down):
    me = _slot(_mesh_pos())
    x0, target = x[0], loss_target[0]
    n_ada = w_ada.shape[2]
    pm_rows = pool_mix.shape[2]

    big = dict(w_ffn1_gu=(w_ffn1_gu, m_w_ffn1_gu, v_w_ffn1_gu), w_ffn1_down=(w_ffn1_down, m_w_ffn1_down, v_w_ffn1_down),
               w_in=(w_in, m_w_in, v_w_in), pool_mix=(pool_mix, m_pool_mix, v_pool_mix),
               w_pool_up=(w_pool_up, m_w_pool_up, v_w_pool_up), w_attn_up=(w_attn_up, m_w_attn_up, v_w_attn_up),
               w_o=(w_o, m_w_o, v_w_o), w_ffn2_gu=(w_ffn2_gu, m_w_ffn2_gu, v_w_ffn2_gu),
               w_ffn2_down=(w_ffn2_down, m_w_ffn2_down, v_w_ffn2_down))
    shard2d = {k: (POOL_GROUPS * pm_rows, POOL_GROUP_W) if k == "pool_mix" else t[0].shape[1:] for k, t in big.items()}
    mix_keys = ["w_in", "pool_mix", "w_pool_up", "w_attn_up", "w_o"]
    ffn2_keys = ["w_ffn2_gu", "w_ffn2_down"]

    def shard_bf16(k, token=None):
        w = big[k][0].reshape(shard2d[k])
        return (w if token is None else w + token[0, 0]).astype(BF16)

    def landing_zones(blocks, tag):
        zones = unwritten_hbm([(N_DEV,) + b.shape for b in blocks], BF16, f"{tag}_zones")
        return [lax.dynamic_update_slice(z, b[None], (me, 0, 0)) for z, b in zip(zones, blocks)]

    def start_gather(keys, token, tag):
        shards = [shard_bf16(k, token) for k in keys]
        return exchange_start(shards, landing_zones(shards, tag), slotted=False, name=f"{tag}_start")

    pending = []

    def scatter(keys, grads, tag):
        lands = landing_zones([lax.dynamic_index_in_dim(g, me, 0, keepdims=False) for g in grads], tag)
        handle = exchange_start(grads, lands, slotted=True, name=f"{tag}_start")
        pending.append((keys, handle, tag))
        return handle[4]

    c_all, _ = all_gather_small(c.reshape(D_MODEL // LANE, LANE), "gather_c")
    c_all = c_all.reshape(N_DEV, D_MODEL)
    c16 = jnp.pad(c_all, ((0, 16 - N_DEV), (0, 0)))
    b_mine = lax.dynamic_slice(b_ada, (0, me * n_ada), (1, n_ada))
    mod_cols = ada_fwd(c16, w_ada[0], b_mine, name="ada_fwd")[:N_DEV]
    mod_all, token = all_gather_small(mod_cols.reshape(-1, LANE), "gather_mod")
    mod = lax.dynamic_index_in_dim(mod_all.reshape(N_DEV, N_DEV, n_ada), me, axis=1, keepdims=False)
    mod = mod.reshape(N_MOD, 1, D_MODEL)

    wgu1, token = all_gather_hbm([shard_bf16("w_ffn1_gu", token)], "gather_ffn1_gu")
    gather_wd1 = start_gather(["w_ffn1_down"], token, "gather_ffn1_down")
    gather_mix = start_gather(mix_keys, gather_wd1[4], "gather_mix")
    gather_ffn2 = start_gather(ffn2_keys, gather_mix[4], "gather_ffn2")
    token = gather_ffn2[4]

    def get_wd1(after):
        return exchange_wait(gather_wd1, after, slotted=False, name="gather_ffn1_down_wait")[0].reshape(-1, D_MODEL)

    x1, saved1, wd1 = _ffn_fwd(x0, g_ffn1, mod[0], mod[1], mod[2], wgu1, get_wd1, token, "ffn1")
    gathered = dict(zip(mix_keys, exchange_wait(gather_mix, x1, slotted=False, name="gather_mix_wait")))
    def columns_out(blocks):
        return jnp.transpose(blocks, (1, 0, 2)).reshape(blocks.shape[1], -1)

    def columns_in(full):
        return jnp.transpose(full.reshape(full.shape[0], N_DEV, -1), (1, 0, 2))

    w_in_full = columns_out(gathered["w_in"])
    w_in_z = jnp.concatenate([w_in_full[:, s:s + w] for s, w in Z_PIECES], axis=1)
    pm_full = jnp.transpose(gathered["pool_mix"].reshape(N_DEV, POOL_GROUPS, pm_rows, POOL_GROUP_W),
                            (1, 0, 2, 3)).reshape(POOL_GROUPS, POOL_GROUP_W, POOL_GROUP_W)
    wpu, wau = columns_out(gathered["w_pool_up"]), columns_out(gathered["w_attn_up"])
    wo_full = gathered["w_o"].reshape(D_MODEL, D_MODEL)
    h2 = norm_modulate(x1, g_mix, mod[3], mod[4], token, name="mix_norm")
    z = mm_nn(h2, w_in_z, out_dtype=BF16, name="mix_in")
    pooled, p_act = pool_fwd(z, pm_full, pool_scale, name="pool_fwd")
    y_pool = mm_nn(p_act, wpu, out_dtype=BF16, name="pool_up")
    one_hot = jnp.asarray(_bucket_one_hot())
    bias = bias_table(rel_bias.T, one_hot, name="bias_table").reshape(N_HEADS, BLK, 2 * BLK)
    qg2, kg2 = jnp.tile(q_gain, (1, 2)), jnp.tile(k_gain, (1, 2))
    sink_rows = jnp.repeat(sinks[0], BLK).reshape(N_HEADS * BLK, 1)
    attn = attn_fwd(z, qg2, kg2, sink_rows, bias, name="attn_fwd")
    y_attn = mm_nn(attn, wau, out_dtype=BF16, name="attn_up")
    merged = merge_fwd(z, y_pool, y_attn, name="merge_fwd")
    x2, o_act = mm_nn_residual(merged, wo_full, x1, mod[5], 1.0, name="mix_out")
    wgu2, wd2 = exchange_wait(gather_ffn2, x2, slotted=False, name="gather_ffn2_wait")
    wd2 = wd2.reshape(-1, D_MODEL)
    (dy, df2, dgate3, loss_row), saved2, _ = _ffn_fwd(x2, g_ffn2, mod[6], mod[7], mod[8], wgu2, lambda after: wd2, token,
                                                     "ffn2", target)
    loss = lax.psum(loss_row[0, 0], ("x", "y", "c"))

    dx2, dmod3, dg_ffn2, d_o, dgate2 = _ffn_bwd(dy, df2, dgate3, x2, g_ffn2, mod[7], wgu2, wd2, saved2, token, scatter, False,
                                               "ffn2", below=(o_act, mod[5], 1.0))
    dwo = mm_tn(merged, d_o, name="mix_dwo").reshape(N_DEV, -1, D_MODEL)
    dyp, dya, dz = merge_bwd(d_o, wo_full, z, y_pool, y_attn, token, name="merge_bwd")
    dwpu = mm_tn(p_act, dyp, name="pool_dwup")
    dp_act = mm_nt(dyp, wpu, token, out_dtype=BF16, name="pool_dp")
    dz, dpool_scale, dpm = pool_bwd(dp_act, pooled, pm_full, pool_scale, dz, name="pool_bwd")
    dwau = mm_tn(attn, dya, name="attn_dwup")
    dattn = mm_nt(dya, wau, token, out_dtype=BF16, name="attn_dout")
    dz, dk, dv, dqg, dkg, dsinks, dbias = attn_bwd(z, dattn, qg2, kg2, sink_rows, bias, dz, name="attn_bwd")
    dz = lax.dynamic_update_slice(dz, jnp.concatenate([dk, dv], axis=1), (0, COL_K))
    drel = bias_table_bwd(dbias.reshape(N_HEADS, -1), one_hot, name="bias_table_bwd").T
    dwin_z = mm_tn(h2, dz, name="mix_dwin")
    z_start = np.cumsum([0] + [w for _, w in Z_PIECES[:-1]])
    in_w_in_order = sorted(zip(Z_PIECES, z_start))
    dwin = jnp.concatenate([dwin_z[:, int(at):int(at) + w] for (_, w), at in in_w_in_order], axis=1)
    mix_grads = dict(w_in=columns_in(dwin),
                     pool_mix=jnp.transpose(dpm.astype(BF16).reshape(POOL_GROUPS, N_DEV, pm_rows, POOL_GROUP_W),
                                            (1, 0, 2, 3)).reshape(N_DEV, POOL_GROUPS * pm_rows, POOL_GROUP_W),
                     w_pool_up=columns_in(dwpu), w_attn_up=columns_in(dwau), w_o=dwo)
    token = scatter(mix_keys, [mix_grads[k] for k in mix_keys], "scatter_mix")
    dh2 = mm_nt(dz, w_in_z, token, out_dtype=BF16, name="mix_dh")
    dx1, dsh2, dsc2, dg_mix, df1, dgate1 = norm_modulate_bwd(x1, g_mix, mod[4], dh2, dx2, (saved1[4], mod[2], 0.5),
                                                           name="mix_norm_bwd")
    dx0, dmod1, dg_ffn1 = _ffn_bwd(dx1, df1, dgate1, x0, g_ffn1, mod[1], wgu1, wd1, saved1, token, scatter, True, "ffn1")

    small = [("b_ada", b_ada, m_b_ada, v_b_ada, jnp.concatenate(list(dmod1 + (dsh2, dsc2, dgate2) + dmod3), axis=1)),
             ("g_ffn1", g_ffn1, m_g_ffn1, v_g_ffn1, dg_ffn1), ("g_mix", g_mix, m_g_mix, v_g_mix, dg_mix),
             ("g_ffn2", g_ffn2, m_g_ffn2, v_g_ffn2, dg_ffn2),
             ("pool_scale", pool_scale, m_pool_scale, v_pool_scale, dpool_scale),
             ("q_gain", q_gain, m_q_gain, v_q_gain, dqg[:, :HEAD_DIM]), ("k_gain", k_gain, m_k_gain, v_k_gain, dkg[:, :HEAD_DIM]),
             ("sinks", sinks, m_sinks, v_sinks, dsinks), ("rel_bias", rel_bias, m_rel_bias, v_rel_bias, drel)]
    n_small = sum(t[1].size for t in small)
    pad = -n_small % (8 * LANE)
    flat = lambda arrs: jnp.pad(jnp.concatenate([a.reshape(1, -1) for a in arrs], axis=1), ((0, 0), (0, pad)))
    small_parts, _ = all_gather_small(flat([t[4] for t in small]).reshape(-1, LANE), "gather_small_grads")
    small_parts = small_parts.reshape(N_DEV, 1, n_small + pad)
    sg, sd, sm, sv = adamw_from_parts(small_parts, flat([t[1] for t in small]), flat([t[2] for t in small]),
                                      flat([t[3] for t in small]), name="adamw_small")

    dmod_all = small_parts[:, 0, :N_MOD * D_MODEL]
    dmod_mine = lax.dynamic_slice(dmod_all, (0, me * n_ada), (N_DEV, n_ada))
    c_t = jnp.pad(c_all.T, ((0, 0), (0, LANE - N_DEV)))
    ada_out = ada_bwd_adamw(c_t, jnp.pad(dmod_mine, ((0, LANE - N_DEV), (0, 0))), w_ada[0], m_w_ada[0], v_w_ada[0],
                            name="ada_bwd_adamw")

    res = {"w_ada": [o[None] for o in ada_out]}
    after = ada_out[0]
    for keys, handle, tag in pending:
        parts = exchange_wait(handle, after, slotted=True, name=f"{tag}_wait")
        for k, part in zip(keys, parts):
            w_, m_, v_ = big[k]
            outs = adamw_from_parts(part, w_.reshape(shard2d[k]), m_.reshape(shard2d[k]), v_.reshape(shard2d[k]),
                                    name=f"adamw_{k}")
            res[k] = [o.reshape(w_.shape) for o in outs]
            after = outs[0]
    off = 0
    for k, w_, _, _, _ in small:
        res[k] = [o[0, off:off + w_.size].reshape(w_.shape) for o in (sg, sd, sm, sv)]
        off += w_.size
    order = ["w_ada", "b_ada", "g_ffn1", "w_ffn1_gu", "w_ffn1_down", "g_mix", "w_in", "pool_mix", "pool_scale",
             "w_pool_up", "q_gain", "k_gain", "sinks", "rel_bias", "w_attn_up", "w_o", "g_ffn2", "w_ffn2_gu", "w_ffn2_down"]
    return (loss, dx0[None], *[res[k][0] for k in order], *[res[k][1] for k in order],
            *[res[k][2] for k in order], *[res[k][3] for k in order])
```

```python
import numpy as np
import jax
import jax.numpy as jnp
from jax import lax
from jax.experimental import pallas as pl
from jax.experimental.pallas import tpu as pltpu

F32, BF16 = jnp.float32, jnp.bfloat16
MESH_ID = pl.DeviceIdType.MESH

N_DEV = 8
D_MODEL = 2048
N_MOD = 9
POOL_WINDOWS = (2, 4, 8, 16)
POOL_GROUPS = 4
POOL_GROUP_W = D_MODEL // 8
POOL_W = POOL_GROUPS * POOL_GROUP_W
POOL_HALO = 16
HEAD_DIM = 64
N_HEADS = 16
N_KV = 2
GQA = N_HEADS // N_KV
BLK = 128
NUM_BUCKETS = 32
MAX_EXACT = 16
REL_MAX_DIST = 128
EPS = 1e-6
NEG_INF = -1e30
ATT_W = N_HEADS * HEAD_DIM
KV_W = N_KV * HEAD_DIM
IN_W = POOL_W + ATT_W + 2 * KV_W + 2 * D_MODEL
GATE_TILE = 512
W_IN_PARTS = dict(u=(0, POOL_W), q=(POOL_W, ATT_W), k=(POOL_W + ATT_W, KV_W), v=(POOL_W + ATT_W + KV_W, KV_W),
                  ga=(POOL_W + ATT_W + 2 * KV_W, D_MODEL), gb=(POOL_W + ATT_W + 2 * KV_W + D_MODEL, D_MODEL))
Z_PIECES = [(W_IN_PARTS[p][0] + j * GATE_TILE, GATE_TILE) for j in range(D_MODEL // GATE_TILE) for p in ("ga", "gb")]
Z_PIECES += [W_IN_PARTS[p] for p in ("u", "q", "k", "v")]
COL_U, COL_Q, COL_K, COL_V = 2 * D_MODEL, 2 * D_MODEL + POOL_W, 2 * D_MODEL + POOL_W + ATT_W, 2 * D_MODEL + POOL_W + ATT_W + KV_W
LANE = 128

ADAM_LR = 0.001
ADAM_B1 = 0.9
ADAM_B2 = 0.999
ADAM_EPS = 1e-08
ADAM_WD = 0.01
ADAM_STEP = 10

NN = ((1,), (0,))
NT = ((1,), (1,))
TN = ((0,), (0,))


def _dot(a, b, dims, precision=None):
    return lax.dot_general(a, b, (dims, ((), ())), preferred_element_type=F32, precision=precision)


def _tile(n, pref, unit):
    t = (min(pref, n) // unit) * unit
    while t >= unit:
        if n % t == 0:
            return t
        t -= unit
    return n


def _params(*sem):
    return pltpu.CompilerParams(dimension_semantics=sem)


def _sigmoid(x):
    return 1.0 / (1.0 + jnp.exp(-x))


def _mesh_pos():
    return lax.axis_index("x"), lax.axis_index("y"), lax.axis_index("c")


def _slot(p):
    return 4 * p[0] + 2 * p[1] + p[2]


def all_gather_small(x_shard, name):
    m_per, n = x_shard.shape

    def body(x_ref, out_ref, token, send_sems, recv_sems, local_sem):
        x, y, c = _mesh_pos()
        me, sibling = (x, y, c), (x, y, 1 - c)
        chips = [(1 - x, y), (x, 1 - y), (1 - x, 1 - y)]
        token[...] = jnp.zeros_like(token)

        def rows(p):
            return out_ref.at[pl.ds(_slot(p) * m_per, m_per), :]

        def copy(k, block, to, src=None):
            return pltpu.make_async_remote_copy(
                src_ref=rows(block) if src is None else src, dst_ref=rows(block),
                send_sem=send_sems.at[k], recv_sem=recv_sems.at[k], device_id=to, device_id_type=MESH_ID)

        mine = pltpu.make_async_copy(x_ref, rows(me), local_sem)
        mine.start()
        first = [copy(0, me, sibling, src=x_ref)]
        first += [copy(1 + j, me, (*chip, c), src=x_ref) for j, chip in enumerate(chips)]
        for cp in first:
            cp.start()
        passed = [copy(4 + j, (*chip, c), sibling) for j, chip in enumerate(chips)]
        for j, chip in enumerate(chips):
            copy(1 + j, (*chip, c), me).wait_recv()
            passed[j].start()
        copy(0, sibling, me).wait_recv()
        for j, chip in enumerate(chips):
            copy(4 + j, (*chip, 1 - c), me).wait_recv()
        for cp in first + passed:
            cp.wait_send()
        mine.wait()

    return pl.pallas_call(
        body, name=name,
        out_shape=[jax.ShapeDtypeStruct((N_DEV * m_per, n), x_shard.dtype), jax.ShapeDtypeStruct((8, LANE), F32)],
        in_specs=[pl.BlockSpec(memory_space=pltpu.VMEM)],
        out_specs=[pl.BlockSpec(memory_space=pltpu.VMEM)] * 2,
        scratch_shapes=[pltpu.SemaphoreType.DMA((7,)), pltpu.SemaphoreType.DMA((7,)), pltpu.SemaphoreType.DMA],
    )(x_shard)


def all_gather_hbm(shard, name):
    rows = shard.shape[0]
    half = rows // 2
    assert rows == 2 * half and half % 16 == 0, shard.shape

    def body(in_ref, out_ref, token, send_sems, recv_sems, local_sem):
        x, y, c = _mesh_pos()
        me, sibling = (x, y, c), (x, y, 1 - c)
        xn, yn, dg = (1 - x, y), (x, 1 - y), (1 - x, 1 - y)
        top, bottom = pl.ds(0, half), pl.ds(half, half)
        token[...] = jnp.zeros_like(token)

        def copy(k, block, to, part=None, src=None):
            dst = out_ref.at[_slot(block)] if part is None else out_ref.at[_slot(block), part]
            return pltpu.make_async_remote_copy(
                src_ref=dst if src is None else src, dst_ref=dst, send_sem=send_sems.at[k], recv_sem=recv_sems.at[k],
                device_id=to, device_id_type=MESH_ID)

        mine = pltpu.make_async_copy(in_ref, out_ref.at[_slot(me)], local_sem)
        mine.start()
        sends = [copy(0, me, sibling, src=in_ref), copy(1, me, (*xn, c), src=in_ref), copy(2, me, (*yn, c), src=in_ref)]
        for cp in sends:
            cp.start()
        copy(1, (*xn, c), me).wait_recv()
        sends += [copy(3, (*xn, c), sibling), copy(4, (*xn, c), (*yn, c), top)]
        sends[-2].start()
        sends[-1].start()
        copy(2, (*yn, c), me).wait_recv()
        sends += [copy(5, (*yn, c), sibling), copy(6, (*yn, c), (*xn, c), bottom)]
        sends[-2].start()
        sends[-1].start()
        copy(4, (*dg, c), me, top).wait_recv()
        copy(6, (*dg, c), me, bottom).wait_recv()
        sends.append(copy(7, (*dg, c), sibling))
        sends[-1].start()
        copy(0, sibling, me).wait_recv()
        copy(3, (*xn, 1 - c), me).wait_recv()
        copy(5, (*yn, 1 - c), me).wait_recv()
        copy(7, (*dg, 1 - c), me).wait_recv()
        for cp in sends:
            cp.wait_send()
        mine.wait()

    any_spec = pl.BlockSpec(memory_space=pl.ANY)
    return pl.pallas_call(
        body, name=name,
        out_shape=[jax.ShapeDtypeStruct((N_DEV,) + shard.shape, shard.dtype), jax.ShapeDtypeStruct((8, LANE), F32)],
        in_specs=[any_spec], out_specs=[any_spec, pl.BlockSpec(memory_space=pltpu.VMEM)],
        scratch_shapes=[pltpu.SemaphoreType.DMA((8,)), pltpu.SemaphoreType.DMA((8,)), pltpu.SemaphoreType.DMA],
    )(shard)


def _peer_list(x, y, c):
    return [((1 - x) if k & 4 else x, (1 - y) if k & 2 else y, (1 - c) if k & 1 else c) for k in range(1, N_DEV)]


def _exchange_copies(srcs, lands, send_sems, recv_sems, slotted, arriving):
    x, y, c = _mesh_pos()
    me = _slot((x, y, c))
    copies = []
    for a in range(len(srcs)):
        for k, peer in enumerate(_peer_list(x, y, c)):
            copies.append(pltpu.make_async_remote_copy(
                src_ref=srcs[a].at[_slot(peer)] if slotted else srcs[a],
                dst_ref=lands[a].at[_slot(peer) if arriving else me],
                send_sem=send_sems.at[7 * a + k], recv_sem=recv_sems.at[7 * a + k],
                device_id=peer, device_id_type=MESH_ID))
    return copies


def unwritten_hbm(shapes, dtype, name):
    def body(*refs):
        pass

    return pl.pallas_call(
        body, name=name, out_shape=[jax.ShapeDtypeStruct(s, dtype) for s in shapes],
        out_specs=[pl.BlockSpec(memory_space=pl.ANY)] * len(shapes),
    )()


HBM_SPEC = pl.BlockSpec(memory_space=pltpu.HBM)
SEM_SPEC = pl.BlockSpec(memory_space=pltpu.SEMAPHORE)
DATAFLOW = pltpu.SideEffectType.DATAFLOW_SIDE_EFFECTING


def exchange_start(srcs, lands, *, slotted, name):
    n = len(srcs)

    def body(*refs):
        ins = refs[:2 * n]
        send_sems, recv_sems = refs[2 * n], refs[2 * n + 1]
        token = refs[-1]
        for cp in _exchange_copies(ins[:n], ins[n:], send_sems, recv_sems, slotted, False):
            cp.start()
        token[...] = jnp.zeros_like(token)

    operands = [pltpu.with_memory_space_constraint(v, pltpu.HBM) for v in list(srcs) + list(lands)]
    out = pl.pallas_call(
        body, name=name,
        out_shape=(pltpu.SemaphoreType.DMA((7 * n,)), pltpu.SemaphoreType.DMA((7 * n,)),
                   *[pltpu.HBM(v.shape, v.dtype) for v in operands], jax.ShapeDtypeStruct((8, LANE), F32)),
        in_specs=[HBM_SPEC] * (2 * n),
        out_specs=(SEM_SPEC, SEM_SPEC, *[HBM_SPEC] * (2 * n), pl.BlockSpec(memory_space=pltpu.VMEM)),
        input_output_aliases={i: 2 + i for i in range(2 * n)},
        compiler_params=pltpu.CompilerParams(has_side_effects=DATAFLOW),
    )(*operands)
    return out[0], out[1], list(out[2:2 + n]), list(out[2 + n:2 + 2 * n]), out[-1]


def exchange_wait(handle, after, *, slotted, name):
    send_sems, recv_sems, srcs, lands, _ = handle
    n = len(srcs)

    def body(*refs):
        ins = refs[:2 * n]
        for cp in _exchange_copies(ins[:n], ins[n:], refs[2 * n], refs[2 * n + 1], slotted, True):
            cp.wait_send()
            cp.wait_recv()

    out = pl.pallas_call(
        body, name=name,
        out_shape=tuple(pltpu.HBM(v.shape, v.dtype) for v in srcs + lands),
        in_specs=[HBM_SPEC] * (2 * n) + [SEM_SPEC, SEM_SPEC, pl.BlockSpec(memory_space=pl.ANY)],
        out_specs=[HBM_SPEC] * (2 * n),
        input_output_aliases={i: i for i in range(2 * n)},
        compiler_params=pltpu.CompilerParams(has_side_effects=DATAFLOW),
    )(*srcs, *lands, send_sems, recv_sems, after)
    return list(out[n:])


VMEM_BLOCK_BUDGET = 46 * 2 ** 20
ROW_TILE, COL_TILE = 1024, 1408
ACC_BUDGET = 12 * 2 ** 20


def _mm_tiles(m, n, row_bytes, col_bytes, elem_bytes):
    tm, tn = _tile(m, ROW_TILE, 16), _tile(n, COL_TILE, LANE)
    while 2 * (tm * row_bytes + tn * col_bytes + tm * tn * elem_bytes) > VMEM_BLOCK_BUDGET:
        narrower = _tile(n, max(tn - LANE, LANE), LANE)
        if tn > 512 and narrower < tn:
            tn = narrower
        else:
            tm //= 2
    return tm, tn


def mm_nn(a, w, *, out_dtype, name):
    m, k = a.shape
    n = w.shape[1]
    tm, tn = _mm_tiles(m, n, 2 * k, 2 * k, jnp.dtype(out_dtype).itemsize)

    def body(a_ref, w_ref, o_ref):
        o_ref[...] = _dot(a_ref[...], w_ref[...], NN).astype(o_ref.dtype)

    return pl.pallas_call(
        body, name=name, grid=(n // tn, m // tm),
        in_specs=[pl.BlockSpec((tm, k), lambda j, i: (i, 0)), pl.BlockSpec((k, tn), lambda j, i: (0, j))],
        out_specs=pl.BlockSpec((tm, tn), lambda j, i: (i, j)),
        out_shape=jax.ShapeDtypeStruct((m, n), out_dtype),
        compiler_params=_params("parallel", "parallel"),
    )(a, w)


def mm_nn_residual(a, w, x_in, gate, coef, *, name):
    m, k = a.shape
    n = w.shape[1]
    tm, tn = _mm_tiles(m, n, 2 * k, 2 * k, 4 + 4 + 2)

    def body(a_ref, w_ref, x_ref, g_ref, o_ref, f_ref):
        f = _dot(a_ref[...], w_ref[...], NN)
        f_ref[...] = f.astype(BF16)
        o_ref[...] = x_ref[...] + (coef * g_ref[...]) * f

    return pl.pallas_call(
        body, name=name, grid=(n // tn, m // tm),
        in_specs=[pl.BlockSpec((tm, k), lambda j, i: (i, 0)), pl.BlockSpec((k, tn), lambda j, i: (0, j)),
                  pl.BlockSpec((tm, tn), lambda j, i: (i, j)), pl.BlockSpec((1, tn), lambda j, i: (0, j))],
        out_specs=[pl.BlockSpec((tm, tn), lambda j, i: (i, j)), pl.BlockSpec((tm, tn), lambda j, i: (i, j))],
        out_shape=[jax.ShapeDtypeStruct((m, n), F32), jax.ShapeDtypeStruct((m, n), BF16)],
        compiler_params=_params("parallel", "parallel"),
    )(a, w, x_in, gate)


def mm_nn_residual_loss(a, w, x_in, gate, coef, target, *, name):
    m, k = a.shape
    n = w.shape[1]
    tm, tn = _mm_tiles(m, n, 2 * k, 2 * k, 4 + 4 + 4 + 2)

    def body(a_ref, w_ref, x_ref, g_ref, t_ref, dy_ref, df_ref, dg_ref, l_ref):
        f = _dot(a_ref[...], w_ref[...], NN)
        err = x_ref[...] + (coef * g_ref[...]) * f - t_ref[...]
        dy = err * (1.0 / n)
        dy_ref[...] = dy
        df_ref[...] = ((coef * g_ref[...]) * dy).astype(BF16)
        dgate = coef * jnp.sum(dy * f, axis=0, keepdims=True)
        part = jnp.sum(jnp.sum(err * err, axis=0, keepdims=True), axis=1, keepdims=True) * (0.5 / n)

        @pl.when(pl.program_id(1) == 0)
        def _():
            dg_ref[...] = jnp.zeros_like(dg_ref)

        @pl.when((pl.program_id(0) == 0) & (pl.program_id(1) == 0))
        def _():
            l_ref[...] = jnp.zeros_like(l_ref)

        dg_ref[...] += dgate
        l_ref[...] += jnp.broadcast_to(part, l_ref.shape)

    blk = pl.BlockSpec((tm, tn), lambda j, i: (i, j))
    vec = pl.BlockSpec((1, tn), lambda j, i: (0, j))
    return pl.pallas_call(
        body, name=name, grid=(n // tn, m // tm),
        in_specs=[pl.BlockSpec((tm, k), lambda j, i: (i, 0)), pl.BlockSpec((k, tn), lambda j, i: (0, j)), blk, vec, blk],
        out_specs=[blk, blk, vec, pl.BlockSpec((1, LANE), lambda j, i: (0, 0))],
        out_shape=[jax.ShapeDtypeStruct((m, n), F32), jax.ShapeDtypeStruct((m, n), BF16),
                   jax.ShapeDtypeStruct((1, n), F32), jax.ShapeDtypeStruct((1, LANE), F32)],
        compiler_params=_params("arbitrary", "arbitrary"),
    )(a, w, x_in, gate, target)


TOKEN_SPEC = pl.BlockSpec((8, LANE), lambda *_: (0, 0))


def mm_nt(a, w, token, *, out_dtype, name):
    m, k = a.shape
    n = w.shape[0]
    tm, tn = _mm_tiles(m, n, 2 * k, 2 * k, jnp.dtype(out_dtype).itemsize)

    def body(a_ref, w_ref, token_ref, o_ref):
        o_ref[...] = _dot(a_ref[...], w_ref[...], NT).astype(o_ref.dtype)

    return pl.pallas_call(
        body, name=name, grid=(n // tn, m // tm),
        in_specs=[pl.BlockSpec((tm, k), lambda j, i: (i, 0)), pl.BlockSpec((tn, k), lambda j, i: (j, 0)), TOKEN_SPEC],
        out_specs=pl.BlockSpec((tm, tn), lambda j, i: (i, j)),
        out_shape=jax.ShapeDtypeStruct((m, n), out_dtype),
        compiler_params=_params("parallel", "parallel"),
    )(a, w, token)


def mm_nt_halves(a_lo, a_hi, w3, token, *, name):
    m = a_lo.shape[0]
    n_blk, n, tn = w3.shape
    half = n_blk // 2
    tm = _tile(m, ROW_TILE, 16)

    def body(lo_ref, hi_ref, w_ref, token_ref, o_ref):
        j = pl.program_id(1)

        @pl.when(j == 0)
        def _():
            o_ref[...] = jnp.zeros_like(o_ref)

        @pl.when(j < half)
        def _():
            o_ref[...] += _dot(lo_ref[...], w_ref[...], NT)

        @pl.when(j >= half)
        def _():
            o_ref[...] += _dot(hi_ref[...], w_ref[...], NT)

    return pl.pallas_call(
        body, name=name, grid=(m // tm, n_blk),
        in_specs=[pl.BlockSpec((tm, tn), lambda i, j: (i, jnp.minimum(j, half - 1))),
                  pl.BlockSpec((tm, tn), lambda i, j: (i, jnp.maximum(j - half, 0))),
                  pl.BlockSpec((None, n, tn), lambda i, j: (j, 0, 0)), TOKEN_SPEC],
        out_specs=pl.BlockSpec((tm, n), lambda i, j: (i, 0)),
        out_shape=jax.ShapeDtypeStruct((m, n), F32),
        compiler_params=_params("parallel", "arbitrary"),
    )(a_lo, a_hi, w3, token)


def mm_tn(a, dy, *, name):
    s, k = a.shape
    n = dy.shape[1]
    ts = _tile(s, ROW_TILE, 16)
    tk = k if k <= 2048 else _tile(k, COL_TILE, LANE)
    tn = _tile(n, ACC_BUDGET // (4 * tk), LANE)
    n_steps = s // ts

    def body(a_ref, dy_ref, o_ref, acc_ref):
        t = pl.program_id(2)

        @pl.when(t == 0)
        def _():
            acc_ref[...] = jnp.zeros_like(acc_ref)

        acc_ref[...] += _dot(a_ref[...], dy_ref[...], TN)

        @pl.when(t == n_steps - 1)
        def _():
            o_ref[...] = acc_ref[...].astype(BF16)

    return pl.pallas_call(
        body, name=name, grid=(k // tk, n // tn, n_steps),
        in_specs=[pl.BlockSpec((ts, tk), lambda kk, j, t: (t, kk)), pl.BlockSpec((ts, tn), lambda kk, j, t: (t, j))],
        out_specs=pl.BlockSpec((tk, tn), lambda kk, j, t: (kk, j)),
        out_shape=jax.ShapeDtypeStruct((k, n), BF16),
        scratch_shapes=[pltpu.VMEM((tk, tn), F32)],
        compiler_params=_params("parallel", "parallel", "arbitrary"),
    )(a, dy)


def mm_tn_halves(a, dy_lo, dy_hi, n_blocks, *, name):
    s, k = a.shape
    half = n_blocks // 2
    tn = dy_lo.shape[1] // half
    ts = _tile(s, ROW_TILE, 16)
    n_steps = s // ts

    def body(a_ref, lo_ref, hi_ref, o_ref, acc_ref):
        j, t = pl.program_id(0), pl.program_id(1)

        @pl.when(t == 0)
        def _():
            acc_ref[...] = jnp.zeros_like(acc_ref)

        @pl.when(j < half)
        def _():
            acc_ref[...] += _dot(a_ref[...], lo_ref[...], TN)

        @pl.when(j >= half)
        def _():
            acc_ref[...] += _dot(a_ref[...], hi_ref[...], TN)

        @pl.when(t == n_steps - 1)
        def _():
            o_ref[...] = acc_ref[...].astype(BF16)

    return pl.pallas_call(
        body, name=name, grid=(n_blocks, n_steps),
        in_specs=[pl.BlockSpec((ts, k), lambda j, t: (t, 0)),
                  pl.BlockSpec((ts, tn), lambda j, t: (jnp.where(j < half, t, n_steps - 1), jnp.minimum(j, half - 1))),
                  pl.BlockSpec((ts, tn), lambda j, t: (jnp.where(j < half, 0, t), jnp.maximum(j - half, 0)))],
        out_specs=pl.BlockSpec((None, k, tn), lambda j, t: (j, 0, 0)),
        out_shape=jax.ShapeDtypeStruct((n_blocks, k, tn), BF16),
        scratch_shapes=[pltpu.VMEM((k, tn), F32)],
        compiler_params=_params("parallel", "arbitrary"),
    )(a, dy_lo, dy_hi)


def ffn_dgu(df, wd, g, u, token, *, name):
    m, k = df.shape
    n = wd.shape[0]
    tm, tn = _mm_tiles(m, n, 2 * k, 2 * k, 4 * 2)

    def body(df_ref, w_ref, g_ref, u_ref, token_ref, dg_ref, du_ref):
        da = _dot(df_ref[...], w_ref[...], NT)
        gg, uu = g_ref[...].astype(F32), u_ref[...].astype(F32)
        sg = _sigmoid(gg)
        dg_ref[...] = (da * uu * (sg * (1 + gg * (1 - sg)))).astype(BF16)
        du_ref[...] = (da * (gg * sg)).astype(BF16)

    blk = pl.BlockSpec((tm, tn), lambda j, i: (i, j))
    out = jax.ShapeDtypeStruct((m, n), BF16)
    return pl.pallas_call(
        body, name=name, grid=(n // tn, m // tm),
        in_specs=[pl.BlockSpec((tm, k), lambda j, i: (i, 0)), pl.BlockSpec((tn, k), lambda j, i: (j, 0)), blk, blk, TOKEN_SPEC],
        out_specs=[blk, blk], out_shape=[out, out],
        compiler_params=_params("parallel", "parallel"),
    )(df, wd, g, u, token)


def ffn_up(h, wgu3, *, name, tm=512):
    s, k = h.shape
    n = wgu3.shape[2]
    half = wgu3.shape[0] // 2
    tm = _tile(s, tm, 16)

    def body(h_ref, wg_ref, wu_ref, g_ref, u_ref, a_ref):
        hh = h_ref[...]
        g = _dot(hh, wg_ref[...], NN)
        u = _dot(hh, wu_ref[...], NN)
        g_ref[...] = g.astype(BF16)
        u_ref[...] = u.astype(BF16)
        a_ref[...] = (g * _sigmoid(g) * u).astype(BF16)

    out = jax.ShapeDtypeStruct((s, half * n), BF16)
    blk = pl.BlockSpec((tm, n), lambda j, i: (i, j))
    return pl.pallas_call(
        body, name=name, grid=(half, s // tm),
        in_specs=[pl.BlockSpec((tm, k), lambda j, i: (i, 0)),
                  pl.BlockSpec((None, k, n), lambda j, i: (j, 0, 0)),
                  pl.BlockSpec((None, k, n), lambda j, i: (j + half, 0, 0))],
        out_specs=[blk, blk, blk], out_shape=[out, out, out],
        compiler_params=_params("parallel", "parallel"),
    )(h, wgu3, wgu3)


def _row_spec(ts, width, col=0):
    return pl.BlockSpec((ts, width), lambda i: (i, col))


def _vec_spec(width):
    return pl.BlockSpec((1, width), lambda i: (0, 0))


def _accumulate(ref, value):
    i = pl.program_id(0)

    @pl.when(i == 0)
    def _():
        ref[...] = value

    @pl.when(i > 0)
    def _():
        ref[...] += value


def norm_modulate(x, g, shift, scale, token, *, name, ts=512):
    s, d = x.shape
    ts = _tile(s, ts, 16)

    def body(x_ref, g_ref, sh_ref, sc_ref, token_ref, h_ref):
        xx = x_ref[...]
        r = lax.rsqrt(jnp.mean(xx * xx, axis=-1, keepdims=True) + EPS)
        h_ref[...] = ((xx * r) * g_ref[...] * (1 + sc_ref[...]) + sh_ref[...]).astype(BF16)

    return pl.pallas_call(
        body, name=name, grid=(s // ts,),
        in_specs=[_row_spec(ts, d), _vec_spec(d), _vec_spec(d), _vec_spec(d), TOKEN_SPEC],
        out_specs=_row_spec(ts, d), out_shape=jax.ShapeDtypeStruct((s, d), BF16),
        compiler_params=_params("parallel"),
    )(x, g, shift, scale, token)


def norm_modulate_bwd(x, g, scale, dh, dx_out, below=None, *, name, ts=256):
    s, d = x.shape
    ts = _tile(s, ts, 16)
    coef = None if below is None else below[2]

    def body(*refs):
        x_ref, g_ref, sc_ref, dh_ref, dxo_ref = refs[:5]
        dx_ref, dsh_ref, dsc_ref, dg_ref = refs[-6:-2] if below else refs[-4:]
        xx, dh_ = x_ref[...], dh_ref[...].astype(F32)
        r = lax.rsqrt(jnp.mean(xx * xx, axis=-1, keepdims=True) + EPS)
        xh = xx * r
        dn = dh_ * (1 + sc_ref[...])
        dxh = dn * g_ref[...]
        dx = dxo_ref[...] + r * (dxh - xh * jnp.mean(dxh * xh, axis=-1, keepdims=True))
        dx_ref[...] = dx
        _accumulate(dsh_ref, jnp.sum(dh_, axis=0, keepdims=True))
        _accumulate(dsc_ref, jnp.sum(dh_ * (xh * g_ref[...]), axis=0, keepdims=True))
        _accumulate(dg_ref, jnp.sum(dn * xh, axis=0, keepdims=True))
        if below:
            f_ref, gate_ref, df_ref, dgate_ref = refs[5], refs[6], refs[-2], refs[-1]
            df_ref[...] = ((coef * gate_ref[...]) * dx).astype(BF16)
            _accumulate(dgate_ref, coef * jnp.sum(dx * f_ref[...].astype(F32), axis=0, keepdims=True))

    vec = jax.ShapeDtypeStruct((1, d), F32)
    extra_in, extra_out, extra_shape, extra_args = [], [], [], []
    if below:
        extra_in, extra_args = [_row_spec(ts, d), _vec_spec(d)], [below[0], below[1]]
        extra_out, extra_shape = [_row_spec(ts, d), _vec_spec(d)], [jax.ShapeDtypeStruct((s, d), BF16), vec]
    return pl.pallas_call(
        body, name=name, grid=(s // ts,),
        in_specs=[_row_spec(ts, d), _vec_spec(d), _vec_spec(d), _row_spec(ts, d), _row_spec(ts, d)] + extra_in,
        out_specs=[_row_spec(ts, d), _vec_spec(d), _vec_spec(d), _vec_spec(d)] + extra_out,
        out_shape=[jax.ShapeDtypeStruct((s, d), F32), vec, vec, vec] + extra_shape,
        compiler_params=_params("arbitrary"),
    )(x, g, scale, dh, dx_out, *extra_args)


def merge_fwd(z, y_pool, y_attn, *, name, ts=1024):
    s, d = y_pool.shape
    ts = _tile(s, ts, 16)

    def body(gate_ref, yp_ref, ya_ref, o_ref):
        sa = _sigmoid(gate_ref[:, :GATE_TILE].astype(F32))
        sb = _sigmoid(gate_ref[:, GATE_TILE:].astype(F32))
        o_ref[...] = (sa * yp_ref[...] + sb * ya_ref[...]).astype(BF16)

    blk = pl.BlockSpec((ts, GATE_TILE), lambda i, j: (i, j))
    return pl.pallas_call(
        body, name=name, grid=(s // ts, d // GATE_TILE),
        in_specs=[pl.BlockSpec((ts, 2 * GATE_TILE), lambda i, j: (i, j)), blk, blk],
        out_specs=blk, out_shape=jax.ShapeDtypeStruct((s, d), BF16),
        compiler_params=_params("parallel", "parallel"),
    )(z, y_pool, y_attn)


def merge_bwd(d_o, wo, z, y_pool, y_attn, token, *, name):
    m, k = d_o.shape
    n = wo.shape[0]
    tm, tn = _tile(m, ROW_TILE, 16), GATE_TILE

    def body(do_ref, w_ref, gate_ref, yp_ref, ya_ref, token_ref, dyp_ref, dya_ref, dz_ref):
        dm = _dot(do_ref[...], w_ref[...], NT)
        sa = _sigmoid(gate_ref[:, :tn].astype(F32))
        sb = _sigmoid(gate_ref[:, tn:].astype(F32))
        dyp_ref[...] = (dm * sa).astype(BF16)
        dya_ref[...] = (dm * sb).astype(BF16)
        dz_ref[:, :tn] = (dm * yp_ref[...] * (sa * (1 - sa))).astype(BF16)
        dz_ref[:, tn:] = (dm * ya_ref[...] * (sb * (1 - sb))).astype(BF16)

    blk = pl.BlockSpec((tm, tn), lambda j, i: (i, j))
    pair = pl.BlockSpec((tm, 2 * tn), lambda j, i: (i, j))
    out = jax.ShapeDtypeStruct((m, n), BF16)
    return pl.pallas_call(
        body, name=name, grid=(n // tn, m // tm),
        in_specs=[pl.BlockSpec((tm, k), lambda j, i: (i, 0)), pl.BlockSpec((tn, k), lambda j, i: (j, 0)),
                  pair, blk, blk, TOKEN_SPEC],
        out_specs=[blk, blk, pair], out_shape=[out, out, jax.ShapeDtypeStruct(z.shape, BF16)],
        compiler_params=_params("parallel", "parallel"),
    )(d_o, wo, z, y_pool, y_attn, token)


def _window_counts(t0, rows):
    t1 = (t0 + 1 + lax.broadcasted_iota(jnp.int32, (rows, 1), 0)).astype(F32)
    return [jnp.minimum(t1, float(w)) for w in POOL_WINDOWS]


def pool_fwd(z, pool_mix, pool_scale, *, name, ts=256):
    s = z.shape[0]
    ts = _tile(s, ts, 16)
    per = ts // POOL_HALO

    def body(u_ref, halo_ref, pm_ref, ps_ref, pooled_ref, p_ref):
        i = pl.program_id(0)
        u = u_ref[...].astype(F32)
        halo = jnp.where(i > 0, halo_ref[...].astype(F32), 0.0)
        run = jnp.concatenate([halo, u], axis=0)
        sums, width = [], 1
        for w in POOL_WINDOWS:
            while width < w:
                run = run + pltpu.roll(run, width, 0)
                width *= 2
            sums.append(run[POOL_HALO:])
        counts = _window_counts(i * ts, ts)
        for gi in range(POOL_GROUPS):
            cols = slice(gi * POOL_GROUP_W, (gi + 1) * POOL_GROUP_W)
            pooled = (sums[gi][:, cols] / counts[gi] - u[:, cols]).astype(BF16)
            pooled_ref[:, cols] = pooled
            p_ref[:, cols] = (_dot(pooled, pm_ref[gi], NN) * ps_ref[:, cols]).astype(BF16)

    out = jax.ShapeDtypeStruct((s, POOL_W), BF16)
    return pl.pallas_call(
        body, name=name, grid=(s // ts,),
        in_specs=[_row_spec(ts, POOL_W, COL_U // POOL_W),
                  pl.BlockSpec((POOL_HALO, POOL_W), lambda i: (jnp.maximum(i * per - 1, 0), COL_U // POOL_W)),
                  pl.BlockSpec((POOL_GROUPS, POOL_GROUP_W, POOL_GROUP_W), lambda i: (0, 0, 0)),
                  _vec_spec(POOL_W)],
        out_specs=[_row_spec(ts, POOL_W)] * 2, out_shape=[out, out],
        compiler_params=_params("parallel"),
    )(z, z, pool_mix, pool_scale)


def pool_bwd(dp, pooled, pool_mix, pool_scale, dz, *, name, ts=256):
    s = dp.shape[0]
    ts = _tile(s, ts, 16)
    per = ts // POOL_HALO
    n_steps = s // ts
    rows = ts + POOL_HALO

    def body(dp_ref, halo_ref, pooled_ref, pm_ref, ps_ref, dz_in_ref, du_ref, dps_ref, dpm_ref):
        i = pl.program_id(0)
        dp_main = dp_ref[...]
        halo = jnp.where(i < n_steps - 1, halo_ref[...], 0.0)
        dmixed = jnp.concatenate([dp_main, halo], axis=0) * ps_ref[...]
        counts = _window_counts(i * ts, rows)
        dps_parts = []
        for gi, w in enumerate(POOL_WINDOWS):
            cols = slice(gi * POOL_GROUP_W, (gi + 1) * POOL_GROUP_W)
            dmx = dmixed[:, cols].astype(BF16)
            pooled = pooled_ref[:, cols]
            mixed = _dot(pooled, pm_ref[gi], NN)
            dps_parts.append(jnp.sum(dp_main[:, cols] * mixed, axis=0, keepdims=True))
            dpm_g = _dot(pooled, dmx[:ts], TN)

            @pl.when(i == 0)
            def _():
                dpm_ref[gi] = dpm_g

            @pl.when(i > 0)
            def _():
                dpm_ref[gi] += dpm_g

            dpooled = _dot(dmx, pm_ref[gi], NT)
            run, width = dpooled / counts[gi], 1
            while width < w:
                run = run + pltpu.roll(run, rows - width, 0)
                width *= 2
            du_ref[:, cols] = (run[:ts] - dpooled[:ts]).astype(BF16)
        _accumulate(dps_ref, jnp.concatenate(dps_parts, axis=1))

    return pl.pallas_call(
        body, name=name, grid=(n_steps,),
        in_specs=[_row_spec(ts, POOL_W),
                  pl.BlockSpec((POOL_HALO, POOL_W), lambda i: (jnp.minimum((i + 1) * per, s // POOL_HALO - 1), 0)),
                  _row_spec(ts, POOL_W),
                  pl.BlockSpec((POOL_GROUPS, POOL_GROUP_W, POOL_GROUP_W), lambda i: (0, 0, 0)),
                  _vec_spec(POOL_W), pl.BlockSpec(memory_space=pl.ANY)],
        out_specs=[_row_spec(ts, POOL_W, COL_U // POOL_W), _vec_spec(POOL_W),
                   pl.BlockSpec((POOL_GROUPS, POOL_GROUP_W, POOL_GROUP_W), lambda i: (0, 0, 0))],
        out_shape=[jax.ShapeDtypeStruct(dz.shape, BF16), jax.ShapeDtypeStruct((1, POOL_W), F32),
                   jax.ShapeDtypeStruct((POOL_GROUPS, POOL_GROUP_W, POOL_GROUP_W), F32)],
        input_output_aliases={5: 0},
        compiler_params=_params("arbitrary"),
    )(dp, dp, pooled, pool_mix, pool_scale, dz)


def _bucket_one_hot():
    ql = np.arange(BLK)[:, None]
    j = np.arange(2 * BLK)[None, :]
    n = np.clip(BLK + ql - j, 0, None)
    nf = np.maximum(n, 1).astype(np.float32)
    large = MAX_EXACT + (np.log(nf / MAX_EXACT) / np.log(REL_MAX_DIST / MAX_EXACT)
                         * (NUM_BUCKETS - MAX_EXACT)).astype(np.int32)
    large = np.minimum(large, NUM_BUCKETS - 1)
    bucket = np.where(n < MAX_EXACT, n, large).astype(np.int32).reshape(-1)
    return (np.arange(NUM_BUCKETS)[:, None] == bucket[None, :]).astype(np.float32)


def bias_table(rel_bias_t, one_hot, *, name, tc=4096):
    n = one_hot.shape[1]

    def body(rb_ref, oh_ref, o_ref):
        o_ref[...] = _dot(rb_ref[...], oh_ref[...], NN, precision=lax.Precision.HIGHEST)

    return pl.pallas_call(
        body, name=name, grid=(n // tc,),
        in_specs=[pl.BlockSpec((N_HEADS, NUM_BUCKETS), lambda i: (0, 0)), pl.BlockSpec((NUM_BUCKETS, tc), lambda i: (0, i))],
        out_specs=pl.BlockSpec((N_HEADS, tc), lambda i: (0, i)),
        out_shape=jax.ShapeDtypeStruct((N_HEADS, n), F32),
        compiler_params=_params("parallel"),
    )(rel_bias_t, one_hot)


def bias_table_bwd(dbias, one_hot, *, name, tc=4096):
    n = one_hot.shape[1]

    def body(db_ref, oh_ref, o_ref):
        _accumulate(o_ref, _dot(db_ref[...], oh_ref[...], NT, precision=lax.Precision.HIGHEST))

    return pl.pallas_call(
        body, name=name, grid=(n // tc,),
        in_specs=[pl.BlockSpec((N_HEADS, tc), lambda i: (0, i)), pl.BlockSpec((NUM_BUCKETS, tc), lambda i: (0, i))],
        out_specs=pl.BlockSpec((N_HEADS, NUM_BUCKETS), lambda i: (0, 0)),
        out_shape=jax.ShapeDtypeStruct((N_HEADS, NUM_BUCKETS), F32),
        compiler_params=_params("arbitrary"),
    )(dbias, one_hot)


def _lane_half(shape):
    return lax.broadcasted_iota(jnp.int32, shape, len(shape) - 1) < HEAD_DIM


def _half_sums(v, first):
    s0 = jnp.sum(jnp.where(first, v, 0.0), axis=-1, keepdims=True)
    s1 = jnp.sum(jnp.where(first, 0.0, v), axis=-1, keepdims=True)
    return jnp.where(first, s0, s1)


BWD_STACK, FWD_STACK = N_HEADS, N_HEADS


def _band_mask(n, heads):
    ql = lax.broadcasted_iota(jnp.int32, (heads * BLK, 2 * BLK), 0) & (BLK - 1)
    j = lax.broadcasted_iota(jnp.int32, (heads * BLK, 2 * BLK), 1)
    return (j > ql) & (j <= ql + BLK) & ((j >= BLK) | (n > 0))


def _norm_keys(kband):
    first = _lane_half(kband.shape)
    r = lax.rsqrt(_half_sums(kband * kband, first) * (1.0 / HEAD_DIM) + EPS)
    return kband * r


def _kv_lanes(h0, heads):
    shape = (heads * BLK, LANE)
    head = h0 + lax.broadcasted_iota(jnp.int32, shape, 0) // BLK
    return _lane_half(shape) == (head < GQA)


def _stack_heads(ref, h0, heads, dtype=F32):
    parts = []
    for h in range(h0, h0 + heads):
        part = ref[:, (h // 2) * LANE:(h // 2 + 1) * LANE].astype(dtype)
        parts.append(pltpu.roll(part, HEAD_DIM, 1) if h % 2 != h // GQA else part)
    return jnp.where(_kv_lanes(h0, heads), jnp.concatenate(parts, axis=0), 0.0)


def _unstack_heads(ref, h0, heads, stacked):
    for i in range(heads // 2):
        pair = None
        for sub in range(2):
            part = stacked[(2 * i + sub) * BLK:(2 * i + sub + 1) * BLK]
            part = pltpu.roll(part, HEAD_DIM, 1) if sub != (h0 + 2 * i) // GQA else part
            pair = part if pair is None else pair + part
        col = (h0 // 2 + i) * LANE
        ref[:, col:col + LANE] = pair.astype(BF16)


def _stack_logits(q_ref, h0, heads, qg, kn, bias_ref, sink_ref, mask):
    qa = _stack_heads(q_ref, h0, heads)
    r = lax.rsqrt(jnp.sum(qa * qa, axis=-1, keepdims=True) * (1.0 / HEAD_DIM) + EPS)
    xh = qa * r
    qn = (xh * qg).astype(BF16)
    bias = bias_ref[h0:h0 + heads].reshape(heads * BLK, 2 * BLK)
    logits = _dot(qn, kn, NT) * (HEAD_DIM ** -0.5) + bias
    p, p_sink = _softmax_with_sink(jnp.where(mask, logits, NEG_INF), sink_ref[h0 * BLK:(h0 + heads) * BLK])
    return xh, r, qn, p, p_sink


def _softmax_with_sink(logits, sink):
    m = jnp.maximum(jnp.max(logits, axis=-1, keepdims=True), sink)
    e = jnp.exp(logits - m)
    es = jnp.exp(sink - m)
    den = jnp.sum(e, axis=-1, keepdims=True) + es
    return e / den, es / den


def _attn_specs(nb, last):
    cur = lambda n: jnp.minimum(n, last)
    prev = lambda n: jnp.minimum(jnp.maximum(n - 1, 0), last)
    return [pl.BlockSpec((BLK, ATT_W), lambda n: (cur(n), COL_Q // ATT_W)),
            pl.BlockSpec((BLK, KV_W), lambda n: (prev(n), COL_K // KV_W)),
            pl.BlockSpec((BLK, KV_W), lambda n: (cur(n), COL_K // KV_W)),
            pl.BlockSpec((BLK, KV_W), lambda n: (prev(n), COL_V // KV_W)),
            pl.BlockSpec((BLK, KV_W), lambda n: (cur(n), COL_V // KV_W))]


def attn_fwd(z, qg2, kg2, sinks, bias, *, name):
    s = z.shape[0]
    nb = s // BLK

    def body(q_ref, kp_ref, kc_ref, vp_ref, vc_ref, qg_ref, kg_ref, sink_ref, bias_ref, o_ref):
        mask = _band_mask(pl.program_id(0), FWD_STACK)
        kn = (_norm_keys(jnp.concatenate([kp_ref[...], kc_ref[...]], axis=0).astype(F32)) * kg_ref[...]).astype(BF16)
        vb = jnp.concatenate([vp_ref[...], vc_ref[...]], axis=0).astype(BF16)
        for h0 in range(0, N_HEADS, FWD_STACK):
            _, _, _, p, _ = _stack_logits(q_ref, h0, FWD_STACK, qg_ref[...], kn, bias_ref, sink_ref, mask)
            out = _dot(p.astype(BF16), vb, NN)
            _unstack_heads(o_ref, h0, FWD_STACK, jnp.where(_kv_lanes(h0, FWD_STACK), out, 0.0))

    return pl.pallas_call(
        body, name=name, grid=(nb,),
        in_specs=_attn_specs(nb, nb - 1) + [
            _vec_spec(LANE), _vec_spec(LANE), pl.BlockSpec((N_HEADS * BLK, 1), lambda n: (0, 0)),
            pl.BlockSpec((N_HEADS, BLK, 2 * BLK), lambda n: (0, 0, 0))],
        out_specs=pl.BlockSpec((BLK, ATT_W), lambda n: (n, 0)),
        out_shape=jax.ShapeDtypeStruct((s, ATT_W), BF16),
        compiler_params=_params("parallel"),
    )(z, z, z, z, z, qg2, kg2, sinks, bias)


def attn_bwd(z, d_out, qg2, kg2, sinks, bias, dz, *, name):
    s = z.shape[0]
    nb = s // BLK
    scale = HEAD_DIM ** -0.5

    def body(q_ref, kp_ref, kc_ref, vp_ref, vc_ref, do_ref, qg_ref, kg_ref, sink_ref, bias_ref, dz_in_ref,
             dq_ref, dk_ref, dv_ref, dqg_ref, dkg_ref, dsink_ref, dbias_ref, band_k, band_v, carry_k, carry_v, dsink_rows):
        n = pl.program_id(0)

        @pl.when(n == 0)
        def _():
            dqg_ref[...] = jnp.zeros_like(dqg_ref)
            dkg_ref[...] = jnp.zeros_like(dkg_ref)
            dbias_ref[...] = jnp.zeros_like(dbias_ref)
            carry_k[...] = jnp.zeros_like(carry_k)
            carry_v[...] = jnp.zeros_like(carry_v)
            dsink_rows[...] = jnp.zeros_like(dsink_rows)

        @pl.when(n == nb)
        def _():
            band_k[...] = jnp.zeros_like(band_k)
            band_v[...] = jnp.zeros_like(band_v)

        @pl.when(n < nb)
        def _():
            mask = _band_mask(n, BWD_STACK)
            kn = (_norm_keys(jnp.concatenate([kp_ref[...], kc_ref[...]], axis=0).astype(F32)) * kg_ref[...]).astype(BF16)
            vb = jnp.concatenate([vp_ref[...], vc_ref[...]], axis=0).astype(BF16)
            dkn = jnp.zeros((2 * BLK, KV_W), F32)
            dvb = jnp.zeros((2 * BLK, KV_W), F32)
            dqg = jnp.zeros((1, LANE), F32)
            for h0 in range(0, N_HEADS, BWD_STACK):
                rows = slice(h0 * BLK, (h0 + BWD_STACK) * BLK)
                xh, r, qn, p, p_sink = _stack_logits(q_ref, h0, BWD_STACK, qg_ref[...], kn, bias_ref, sink_ref, mask)
                do = _stack_heads(do_ref, h0, BWD_STACK).astype(BF16)
                dp = _dot(do, vb, NT)
                delta = jnp.sum(p * dp, axis=-1, keepdims=True)
                ds = p * (dp - delta)
                dsink_rows[rows] -= p_sink * delta
                dbias_ref[h0:h0 + BWD_STACK] += ds.reshape(BWD_STACK, BLK, 2 * BLK)
                ds16 = ds.astype(BF16)
                dqn = jnp.where(_kv_lanes(h0, BWD_STACK), _dot(ds16, kn, NN) * scale, 0.0)
                dkn = dkn + _dot(ds16, qn, TN) * scale
                dvb = dvb + _dot(p.astype(BF16), do, TN)
                dqg = dqg + jnp.sum(dqn * xh, axis=0, keepdims=True)
                dxh = dqn * qg_ref[...]
                _unstack_heads(dq_ref, h0, BWD_STACK,
                               r * (dxh - xh * (jnp.sum(dxh * xh, axis=-1, keepdims=True) * (1.0 / HEAD_DIM))))
            band_k[...] = dkn
            band_v[...] = dvb
            dqg_ref[...] += dqg

        dkn_prev = carry_k[...] + band_k[:BLK]
        dv_ref[...] = (carry_v[...] + band_v[:BLK]).astype(BF16)
        carry_k[...] = band_k[BLK:]
        carry_v[...] = band_v[BLK:]
        kp = kp_ref[...].astype(F32)
        first = _lane_half(kp.shape)
        r = lax.rsqrt(_half_sums(kp * kp, first) * (1.0 / HEAD_DIM) + EPS)
        xh = kp * r
        dkg_ref[...] += jnp.sum(dkn_prev * xh, axis=0, keepdims=True)
        dxh = dkn_prev * kg_ref[...]
        dk_ref[...] = (r * (dxh - xh * (_half_sums(dxh * xh, first) * (1.0 / HEAD_DIM)))).astype(BF16)

        @pl.when(n == nb)
        def _():
            dqg_ref[...] += pltpu.roll(dqg_ref[...], HEAD_DIM, 1)
            dkg_ref[...] += pltpu.roll(dkg_ref[...], HEAD_DIM, 1)
            lane16 = lax.broadcasted_iota(jnp.int32, (1, N_HEADS), 1)
            dsink = jnp.zeros((1, N_HEADS), F32)
            for h in range(N_HEADS):
                dsink = dsink + jnp.where(lane16 == h, jnp.sum(dsink_rows[h * BLK:(h + 1) * BLK], axis=0, keepdims=True), 0.0)
            dsink_ref[...] = dsink

    last = nb - 1
    cur = lambda n: jnp.minimum(n, last)
    back = lambda n: jnp.maximum(n - 1, 0)
    full3 = pl.BlockSpec((N_HEADS, BLK, 2 * BLK), lambda n: (0, 0, 0))
    return pl.pallas_call(
        body, name=name, grid=(nb + 1,),
        in_specs=_attn_specs(nb, last) + [
            pl.BlockSpec((BLK, ATT_W), lambda n: (cur(n), 0)),
            _vec_spec(LANE), _vec_spec(LANE), pl.BlockSpec((N_HEADS * BLK, 1), lambda n: (0, 0)), full3,
            pl.BlockSpec(memory_space=pl.ANY)],
        out_specs=[pl.BlockSpec((BLK, ATT_W), lambda n: (cur(n), COL_Q // ATT_W)),
                   pl.BlockSpec((BLK, KV_W), lambda n: (back(n), 0)),
                   pl.BlockSpec((BLK, KV_W), lambda n: (back(n), 0)),
                   _vec_spec(LANE), _vec_spec(LANE), _vec_spec(N_HEADS), full3],
        input_output_aliases={10: 0},
        out_shape=[jax.ShapeDtypeStruct(dz.shape, BF16), jax.ShapeDtypeStruct((s, KV_W), BF16),
                   jax.ShapeDtypeStruct((s, KV_W), BF16), jax.ShapeDtypeStruct((1, LANE), F32),
                   jax.ShapeDtypeStruct((1, LANE), F32), jax.ShapeDtypeStruct((1, N_HEADS), F32),
                   jax.ShapeDtypeStruct((N_HEADS, BLK, 2 * BLK), F32)],
        scratch_shapes=[pltpu.VMEM((2 * BLK, KV_W), F32), pltpu.VMEM((2 * BLK, KV_W), F32),
                        pltpu.VMEM((BLK, KV_W), F32), pltpu.VMEM((BLK, KV_W), F32), pltpu.VMEM((N_HEADS * BLK, 1), F32)],
        compiler_params=_params("arbitrary"),
    )(z, z, z, z, z, d_out, qg2, kg2, sinks, bias, dz)


def _adamw(w, g, m, v):
    m = ADAM_B1 * m + (1.0 - ADAM_B1) * g
    v = ADAM_B2 * v + (1.0 - ADAM_B2) * (g * g)
    m_hat = m / (1.0 - ADAM_B1 ** ADAM_STEP)
    v_hat = v / (1.0 - ADAM_B2 ** ADAM_STEP)
    delta = -ADAM_LR * (m_hat / (jnp.sqrt(v_hat) + ADAM_EPS) + ADAM_WD * w)
    return delta, m, v


def ada_fwd(c16, w, b, *, name, tn=768):
    k, n = w.shape
    tn = _tile(n, tn, LANE)

    def body(c_ref, w_ref, b_ref, o_ref):
        cc = c_ref[...]
        o_ref[...] = _dot((cc * _sigmoid(cc)).astype(BF16), w_ref[...].astype(BF16), NN) + b_ref[...]

    return pl.pallas_call(
        body, name=name, grid=(n // tn,),
        in_specs=[pl.BlockSpec((c16.shape[0], k), lambda j: (0, 0)), pl.BlockSpec((k, tn), lambda j: (0, j)),
                  pl.BlockSpec((1, tn), lambda j: (0, j))],
        out_specs=pl.BlockSpec((c16.shape[0], tn), lambda j: (0, j)),
        out_shape=jax.ShapeDtypeStruct((c16.shape[0], n), F32),
        compiler_params=_params("parallel"),
    )(c16, w, b)


def ada_bwd_adamw(c_t, dmod, w, m, v, *, name, tn=256):
    k, n = w.shape
    tn = _tile(n, tn, LANE)

    def body(c_ref, d_ref, w_ref, m_ref, v_ref, g_ref, dl_ref, mo_ref, vo_ref):
        cc = c_ref[...]
        g = _dot((cc * _sigmoid(cc)).astype(BF16), d_ref[...].astype(BF16), NN)
        g_ref[...] = g
        dl_ref[...], mo_ref[...], vo_ref[...] = _adamw(w_ref[...], g, m_ref[...], v_ref[...])

    blk = pl.BlockSpec((k, tn), lambda j: (0, j))
    out = jax.ShapeDtypeStruct((k, n), F32)
    return pl.pallas_call(
        body, name=name, grid=(n // tn,),
        in_specs=[pl.BlockSpec((k, LANE), lambda j: (0, 0)), pl.BlockSpec((LANE, tn), lambda j: (0, j)), blk, blk, blk],
        out_specs=[blk] * 4, out_shape=[out] * 4,
        compiler_params=_params("parallel"),
    )(c_t, dmod, w, m, v)


def adamw_from_parts(parts, w, m, v, *, name):
    r, c = w.shape
    tr = _tile(r, max(16, (256 * 1024) // c), 16)

    def body(p_ref, w_ref, m_ref, v_ref, g_ref, dl_ref, mo_ref, vo_ref):
        g = p_ref[0].astype(F32)
        for d in range(1, N_DEV):
            g = g + p_ref[d].astype(F32)
        g_ref[...] = g
        dl_ref[...], mo_ref[...], vo_ref[...] = _adamw(w_ref[...], g, m_ref[...], v_ref[...])

    blk = pl.BlockSpec((tr, c), lambda i: (i, 0))
    out = jax.ShapeDtypeStruct((r, c), F32)
    return pl.pallas_call(
        body, name=name, grid=(r // tr,),
        in_specs=[pl.BlockSpec((N_DEV, tr, c), lambda i: (0, i, 0)), blk, blk, blk],
        out_specs=[blk] * 4, out_shape=[out] * 4,
        compiler_params=_params("parallel"),
    )(parts, w, m, v)


def _ffn_fwd(x_in, g, shift, scale, gate, wgu3, get_wd, token, tag, target=None):
    h = norm_modulate(x_in, g, shift, scale, token, name=f"{tag}_norm")
    gg, uu, act = ffn_up(h, wgu3, name=f"{tag}_up")
    wd = get_wd(gg)
    if target is None:
        x_out, f = mm_nn_residual(act, wd, x_in, gate, 0.5, name=f"{tag}_down")
    else:
        x_out, f = mm_nn_residual_loss(act, wd, x_in, gate, 0.5, target, name=f"{tag}_down_loss"), None
    return x_out, (h, gg, uu, act, f), wd


def _ffn_bwd(dx_out, df, dgate, x_in, g, scale, wgu3, wd, saved, token, scatter, split, tag, below=None):
    h, gg, uu, act, _ = saved
    dwd = mm_tn(act, df, name=f"{tag}_dwd").reshape(N_DEV, -1, D_MODEL)
    if split:
        token = scatter([f"w_{tag}_down"], [dwd], f"scatter_{tag}_down")
    dgg, duu = ffn_dgu(df, wd, gg, uu, token, name=f"{tag}_dgu")
    dwgu = mm_tn_halves(h, dgg, duu, N_DEV, name=f"{tag}_dwgu")
    if split:
        token = scatter([f"w_{tag}_gu"], [dwgu], f"scatter_{tag}_gu")
    else:
        token = scatter([f"w_{tag}_gu", f"w_{tag}_down"], [dwgu, dwd], f"scatter_{tag}")
    dh = mm_nt_halves(dgg, duu, wgu3, token, name=f"{tag}_dh")
    dx_in, dshift, dscale, dg, *rest = norm_modulate_bwd(x_in, g, scale, dh, dx_out, below, name=f"{tag}_norm_bwd")
    return (dx_in, (dshift, dscale, dgate), dg, *rest)


def kernel(x, c, w_ada, b_ada, g_ffn1, w_ffn1_gu, w_ffn1_down, g_mix, w_in, pool_mix, pool_scale, w_pool_up, q_gain, k_gain, sinks, rel_bias, w_attn_up, w_o, g_ffn2, w_ffn2_gu, w_ffn2_down, loss_target, m_w_ada, m_b_ada, m_g_ffn1, m_w_ffn1_gu, m_w_ffn1_down, m_g_mix, m_w_in, m_pool_mix, m_pool_scale, m_w_pool_up, m_q_gain, m_k_gain, m_sinks, m_rel_bias, m_w_attn_up, m_w_o, m_g_ffn2, m_w_ffn2_gu, m_w_ffn2_down, v_w_ada, v_b_ada, v_g_ffn1, v_w_ffn1_gu, v_w_ffn1_down, v_g_mix, v_w_in, v_pool_mix, v_pool_scale, v_w_pool_up, v_q_gain, v_k_gain, v_sinks, v_rel_bias, v_w_attn_up, v_w_o, v_g_ffn2, v_w_ffn2_gu, v_w_ffn2_down):
    me = _slot(_mesh_pos())
    x0, target = x[0], loss_target[0]
    n_ada = w_ada.shape[2]
    pm_rows = pool_mix.shape[2]

    big = dict(w_ffn1_gu=(w_ffn1_gu, m_w_ffn1_gu, v_w_ffn1_gu), w_ffn1_down=(w_ffn1_down, m_w_ffn1_down, v_w_ffn1_down),
               w_in=(w_in, m_w_in, v_w_in), pool_mix=(pool_mix, m_pool_mix, v_pool_mix),
               w_pool_up=(w_pool_up, m_w_pool_up, v_w_pool_up), w_attn_up=(w_attn_up, m_w_attn_up, v_w_attn_up),
               w_o=(w_o, m_w_o, v_w_o), w_ffn2_gu=(w_ffn2_gu, m_w_ffn2_gu, v_w_ffn2_gu),
               w_ffn2_down=(w_ffn2_down, m_w_ffn2_down, v_w_ffn2_down))
    shard2d = {k: (POOL_GROUPS * pm_rows, POOL_GROUP_W) if k == "pool_mix" else t[0].shape[1:] for k, t in big.items()}
    mix_keys = ["w_in", "pool_mix", "w_pool_up", "w_attn_up", "w_o"]
    ffn2_keys = ["w_ffn2_gu", "w_ffn2_down"]

    def shard_bf16(k, token=None):
        w = big[k][0].reshape(shard2d[k])
        return (w if token is None else w + token[0, 0]).astype(BF16)

    def landing_zones(blocks, tag):
        zones = unwritten_hbm([(N_DEV,) + b.shape for b in blocks], BF16, f"{tag}_zones")
        return [lax.dynamic_update_slice(z, b[None], (me, 0, 0)) for z, b in zip(zones, blocks)]

    def start_gather(keys, token, tag):
        shards = [shard_bf16(k, token) for k in keys]
        return exchange_start(shards, landing_zones(shards, tag), slotted=False, name=f"{tag}_start")

    pending = []

    def scatter(keys, grads, tag):
        lands = landing_zones([lax.dynamic_index_in_dim(g, me, 0, keepdims=False) for g in grads], tag)
        handle = exchange_start(grads, lands, slotted=True, name=f"{tag}_start")
        pending.append((keys, handle, tag))
        return handle[4]

    c_all, _ = all_gather_small(c.reshape(D_MODEL // LANE, LANE), "gather_c")
    c_all = c_all.reshape(N_DEV, D_MODEL)
    c16 = jnp.pad(c_all, ((0, 16 - N_DEV), (0, 0)))
    b_mine = lax.dynamic_slice(b_ada, (0, me * n_ada), (1, n_ada))
    mod_cols = ada_fwd(c16, w_ada[0], b_mine, name="ada_fwd")[:N_DEV]
    mod_all, token = all_gather_small(mod_cols.reshape(-1, LANE), "gather_mod")
    mod = lax.dynamic_index_in_dim(mod_all.reshape(N_DEV, N_DEV, n_ada), me, axis=1, keepdims=False)
    mod = mod.reshape(N_MOD, 1, D_MODEL)

    wgu1, token = all_gather_hbm(shard_bf16("w_ffn1_gu", token), "gather_ffn1_gu")
    gather_wd1 = start_gather(["w_ffn1_down"], token, "gather_ffn1_down")
    gather_mix = start_gather(mix_keys, gather_wd1[4], "gather_mix")
    gather_ffn2 = start_gather(ffn2_keys, gather_mix[4], "gather_ffn2")
    token = gather_ffn2[4]

    def get_wd1(after):
        return exchange_wait(gather_wd1, after, slotted=False, name="gather_ffn1_down_wait")[0].reshape(-1, D_MODEL)

    x1, saved1, wd1 = _ffn_fwd(x0, g_ffn1, mod[0], mod[1], mod[2], wgu1, get_wd1, token, "ffn1")
    gathered = dict(zip(mix_keys, exchange_wait(gather_mix, x1, slotted=False, name="gather_mix_wait")))
    def columns_out(blocks):
        return jnp.transpose(blocks, (1, 0, 2)).reshape(blocks.shape[1], -1)

    def columns_in(full):
        return jnp.transpose(full.reshape(full.shape[0], N_DEV, -1), (1, 0, 2))

    w_in_full = columns_out(gathered["w_in"])
    w_in_z = jnp.concatenate([w_in_full[:, s:s + w] for s, w in Z_PIECES], axis=1)
    pm_full = jnp.transpose(gathered["pool_mix"].reshape(N_DEV, POOL_GROUPS, pm_rows, POOL_GROUP_W),
                            (1, 0, 2, 3)).reshape(POOL_GROUPS, POOL_GROUP_W, POOL_GROUP_W)
    wpu, wau = columns_out(gathered["w_pool_up"]), columns_out(gathered["w_attn_up"])
    wo_full = gathered["w_o"].reshape(D_MODEL, D_MODEL)
    h2 = norm_modulate(x1, g_mix, mod[3], mod[4], token, name="mix_norm")
    z = mm_nn(h2, w_in_z, out_dtype=BF16, name="mix_in")
    pooled, p_act = pool_fwd(z, pm_full, pool_scale, name="pool_fwd")
    y_pool = mm_nn(p_act, wpu, out_dtype=BF16, name="pool_up")
    one_hot = jnp.asarray(_bucket_one_hot())
    bias = bias_table(rel_bias.T, one_hot, name="bias_table").reshape(N_HEADS, BLK, 2 * BLK)
    qg2, kg2 = jnp.tile(q_gain, (1, 2)), jnp.tile(k_gain, (1, 2))
    sink_rows = jnp.repeat(sinks[0], BLK).reshape(N_HEADS * BLK, 1)
    attn = attn_fwd(z, qg2, kg2, sink_rows, bias, name="attn_fwd")
    y_attn = mm_nn(attn, wau, out_dtype=BF16, name="attn_up")
    merged = merge_fwd(z, y_pool, y_attn, name="merge_fwd")
    x2, o_act = mm_nn_residual(merged, wo_full, x1, mod[5], 1.0, name="mix_out")
    wgu2, wd2 = exchange_wait(gather_ffn2, x2, slotted=False, name="gather_ffn2_wait")
    wd2 = wd2.reshape(-1, D_MODEL)
    (dy, df2, dgate3, loss_row), saved2, _ = _ffn_fwd(x2, g_ffn2, mod[6], mod[7], mod[8], wgu2, lambda after: wd2, token,
                                                     "ffn2", target)

    dx2, dmod3, dg_ffn2, d_o, dgate2 = _ffn_bwd(dy, df2, dgate3, x2, g_ffn2, mod[7], wgu2, wd2, saved2, token, scatter, False,
                                               "ffn2", below=(o_act, mod[5], 1.0))
    dwo = mm_tn(merged, d_o, name="mix_dwo").reshape(N_DEV, -1, D_MODEL)
    dyp, dya, dz = merge_bwd(d_o, wo_full, z, y_pool, y_attn, token, name="merge_bwd")
    dwpu = mm_tn(p_act, dyp, name="pool_dwup")
    dp_act = mm_nt(dyp, wpu, token, out_dtype=BF16, name="pool_dp")
    dz, dpool_scale, dpm = pool_bwd(dp_act, pooled, pm_full, pool_scale, dz, name="pool_bwd")
    dwau = mm_tn(attn, dya, name="attn_dwup")
    dattn = mm_nt(dya, wau, token, out_dtype=BF16, name="attn_dout")
    dz, dk, dv, dqg, dkg, dsinks, dbias = attn_bwd(z, dattn, qg2, kg2, sink_rows, bias, dz, name="attn_bwd")
    dz = lax.dynamic_update_slice(dz, jnp.concatenate([dk, dv], axis=1), (0, COL_K))
    drel = bias_table_bwd(dbias.reshape(N_HEADS, -1), one_hot, name="bias_table_bwd").T
    dwin_z = mm_tn(h2, dz, name="mix_dwin")
    z_start = np.cumsum([0] + [w for _, w in Z_PIECES[:-1]])
    in_w_in_order = sorted(zip(Z_PIECES, z_start))
    dwin = jnp.concatenate([dwin_z[:, int(at):int(at) + w] for (_, w), at in in_w_in_order], axis=1)
    mix_grads = dict(w_in=columns_in(dwin),
                     pool_mix=jnp.transpose(dpm.astype(BF16).reshape(POOL_GROUPS, N_DEV, pm_rows, POOL_GROUP_W),
                                            (1, 0, 2, 3)).reshape(N_DEV, POOL_GROUPS * pm_rows, POOL_GROUP_W),
                     w_pool_up=columns_in(dwpu), w_attn_up=columns_in(dwau), w_o=dwo)
    token = scatter(mix_keys, [mix_grads[k] for k in mix_keys], "scatter_mix")
    dh2 = mm_nt(dz, w_in_z, token, out_dtype=BF16, name="mix_dh")
    dx1, dsh2, dsc2, dg_mix, df1, dgate1 = norm_modulate_bwd(x1, g_mix, mod[4], dh2, dx2, (saved1[4], mod[2], 0.5),
                                                           name="mix_norm_bwd")
    dx0, dmod1, dg_ffn1 = _ffn_bwd(dx1, df1, dgate1, x0, g_ffn1, mod[1], wgu1, wd1, saved1, token, scatter, True, "ffn1")

    small = [("b_ada", b_ada, m_b_ada, v_b_ada, jnp.concatenate(list(dmod1 + (dsh2, dsc2, dgate2) + dmod3), axis=1)),
             ("g_ffn1", g_ffn1, m_g_ffn1, v_g_ffn1, dg_ffn1), ("g_mix", g_mix, m_g_mix, v_g_mix, dg_mix),
             ("g_ffn2", g_ffn2, m_g_ffn2, v_g_ffn2, dg_ffn2),
             ("pool_scale", pool_scale, m_pool_scale, v_pool_scale, dpool_scale),
             ("q_gain", q_gain, m_q_gain, v_q_gain, dqg[:, :HEAD_DIM]), ("k_gain", k_gain, m_k_gain, v_k_gain, dkg[:, :HEAD_DIM]),
             ("sinks", sinks, m_sinks, v_sinks, dsinks), ("rel_bias", rel_bias, m_rel_bias, v_rel_bias, drel)]
    n_small = sum(t[1].size for t in small)
    width = n_small + 1 + (-(n_small + 1) % (8 * LANE))

    def flat(arrs):
        row = jnp.concatenate([a.reshape(1, -1) for a in arrs], axis=1)
        return jnp.pad(row, ((0, 0), (0, width - row.shape[1])))

    small_parts, _ = all_gather_small(flat([t[4] for t in small] + [loss_row[:, :1]]).reshape(-1, LANE), "gather_small_grads")
    small_parts = small_parts.reshape(N_DEV, 1, width)
    sg, sd, sm, sv = adamw_from_parts(small_parts, flat([t[1] for t in small]), flat([t[2] for t in small]),
                                      flat([t[3] for t in small]), name="adamw_small")
    loss = sg[0, n_small]

    dmod_all = small_parts[:, 0, :N_MOD * D_MODEL]
    dmod_mine = lax.dynamic_slice(dmod_all, (0, me * n_ada), (N_DEV, n_ada))
    c_t = jnp.pad(c_all.T, ((0, 0), (0, LANE - N_DEV)))
    ada_out = ada_bwd_adamw(c_t, jnp.pad(dmod_mine, ((0, LANE - N_DEV), (0, 0))), w_ada[0], m_w_ada[0], v_w_ada[0],
                            name="ada_bwd_adamw")

    res = {"w_ada": [o[None] for o in ada_out]}
    after = ada_out[0]
    for keys, handle, tag in pending:
        parts = exchange_wait(handle, after, slotted=True, name=f"{tag}_wait")
        for k, part in zip(keys, parts):
            w_, m_, v_ = big[k]
            outs = adamw_from_parts(part, w_.reshape(shard2d[k]), m_.reshape(shard2d[k]), v_.reshape(shard2d[k]),
                                    name=f"adamw_{k}")
            res[k] = [o.reshape(w_.shape) for o in outs]
            after = outs[0]
    off = 0
    for k, w_, _, _, _ in small:
        res[k] = [o[0, off:off + w_.size].reshape(w_.shape) for o in (sg, sd, sm, sv)]
        off += w_.size
    order = ["w_ada", "b_ada", "g_ffn1", "w_ffn1_gu", "w_ffn1_down", "g_mix", "w_in", "pool_mix", "pool_scale",
             "w_pool_up", "q_gain", "k_gain", "sinks", "rel_bias", "w_attn_up", "w_o", "g_ffn2", "w_ffn2_gu", "w_ffn2_down"]
    return (loss, dx0[None], *[res[k][0] for k in order], *[res[k][1] for k in order],
            *[res[k][2] for k in order], *[res[k][3] for k in order])
```

```python
import numpy as np
import jax
import jax.numpy as jnp
from jax import lax
from jax.experimental import pallas as pl
from jax.experimental.pallas import tpu as pltpu

F32, BF16 = jnp.float32, jnp.bfloat16
MESH_ID = pl.DeviceIdType.MESH

N_DEV = 8
D_MODEL = 2048
N_MOD = 9
POOL_WINDOWS = (2, 4, 8, 16)
POOL_GROUPS = 4
POOL_GROUP_W = D_MODEL // 8
POOL_W = POOL_GROUPS * POOL_GROUP_W
POOL_HALO = 16
HEAD_DIM = 64
N_HEADS = 16
N_KV = 2
GQA = N_HEADS // N_KV
BLK = 128
NUM_BUCKETS = 32
MAX_EXACT = 16
REL_MAX_DIST = 128
EPS = 1e-6
NEG_INF = -1e30
ATT_W = N_HEADS * HEAD_DIM
KV_W = N_KV * HEAD_DIM
IN_W = POOL_W + ATT_W + 2 * KV_W + 2 * D_MODEL
GATE_TILE = 512
W_IN_PARTS = dict(u=(0, POOL_W), q=(POOL_W, ATT_W), k=(POOL_W + ATT_W, KV_W), v=(POOL_W + ATT_W + KV_W, KV_W),
                  ga=(POOL_W + ATT_W + 2 * KV_W, D_MODEL), gb=(POOL_W + ATT_W + 2 * KV_W + D_MODEL, D_MODEL))
Z_PIECES = [(W_IN_PARTS[p][0] + j * GATE_TILE, GATE_TILE) for j in range(D_MODEL // GATE_TILE) for p in ("ga", "gb")]
Z_PIECES += [W_IN_PARTS[p] for p in ("u", "q", "k", "v")]
COL_U, COL_Q, COL_K, COL_V = 2 * D_MODEL, 2 * D_MODEL + POOL_W, 2 * D_MODEL + POOL_W + ATT_W, 2 * D_MODEL + POOL_W + ATT_W + KV_W
LANE = 128

ADAM_LR = 0.001
ADAM_B1 = 0.9
ADAM_B2 = 0.999
ADAM_EPS = 1e-08
ADAM_WD = 0.01
ADAM_STEP = 10

NN = ((1,), (0,))
NT = ((1,), (1,))
TN = ((0,), (0,))


def _dot(a, b, dims, precision=None):
    return lax.dot_general(a, b, (dims, ((), ())), preferred_element_type=F32, precision=precision)


def _tile(n, pref, unit):
    t = (min(pref, n) // unit) * unit
    while t >= unit:
        if n % t == 0:
            return t
        t -= unit
    return n


def _params(*sem):
    return pltpu.CompilerParams(dimension_semantics=sem)


def _sigmoid(x):
    return 1.0 / (1.0 + jnp.exp(-x))


def _mesh_pos():
    return lax.axis_index("x"), lax.axis_index("y"), lax.axis_index("c")


def _slot(p):
    return 4 * p[0] + 2 * p[1] + p[2]


def all_gather_small(x_shard, name):
    m_per, n = x_shard.shape

    def body(x_ref, out_ref, token, send_sems, recv_sems, local_sem):
        x, y, c = _mesh_pos()
        me, sibling = (x, y, c), (x, y, 1 - c)
        chips = [(1 - x, y), (x, 1 - y), (1 - x, 1 - y)]
        token[...] = jnp.zeros_like(token)

        def rows(p):
            return out_ref.at[pl.ds(_slot(p) * m_per, m_per), :]

        def copy(k, block, to, src=None):
            return pltpu.make_async_remote_copy(
                src_ref=rows(block) if src is None else src, dst_ref=rows(block),
                send_sem=send_sems.at[k], recv_sem=recv_sems.at[k], device_id=to, device_id_type=MESH_ID)

        mine = pltpu.make_async_copy(x_ref, rows(me), local_sem)
        mine.start()
        first = [copy(0, me, sibling, src=x_ref)]
        first += [copy(1 + j, me, (*chip, c), src=x_ref) for j, chip in enumerate(chips)]
        for cp in first:
            cp.start()
        passed = [copy(4 + j, (*chip, c), sibling) for j, chip in enumerate(chips)]
        for j, chip in enumerate(chips):
            copy(1 + j, (*chip, c), me).wait_recv()
            passed[j].start()
        copy(0, sibling, me).wait_recv()
        for j, chip in enumerate(chips):
            copy(4 + j, (*chip, 1 - c), me).wait_recv()
        for cp in first + passed:
            cp.wait_send()
        mine.wait()

    return pl.pallas_call(
        body, name=name,
        out_shape=[jax.ShapeDtypeStruct((N_DEV * m_per, n), x_shard.dtype), jax.ShapeDtypeStruct((8, LANE), F32)],
        in_specs=[pl.BlockSpec(memory_space=pltpu.VMEM)],
        out_specs=[pl.BlockSpec(memory_space=pltpu.VMEM)] * 2,
        scratch_shapes=[pltpu.SemaphoreType.DMA((7,)), pltpu.SemaphoreType.DMA((7,)), pltpu.SemaphoreType.DMA],
    )(x_shard)


def all_gather_hbm(shard, name):
    rows = shard.shape[0]
    half = rows // 2
    assert rows == 2 * half and half % 16 == 0, shard.shape

    def body(in_ref, out_ref, token, send_sems, recv_sems, local_sem):
        x, y, c = _mesh_pos()
        me, sibling = (x, y, c), (x, y, 1 - c)
        xn, yn, dg = (1 - x, y), (x, 1 - y), (1 - x, 1 - y)
        top, bottom = pl.ds(0, half), pl.ds(half, half)
        token[...] = jnp.zeros_like(token)

        def copy(k, block, to, part=None, src=None):
            dst = out_ref.at[_slot(block)] if part is None else out_ref.at[_slot(block), part]
            return pltpu.make_async_remote_copy(
                src_ref=dst if src is None else src, dst_ref=dst, send_sem=send_sems.at[k], recv_sem=recv_sems.at[k],
                device_id=to, device_id_type=MESH_ID)

        mine = pltpu.make_async_copy(in_ref, out_ref.at[_slot(me)], local_sem)
        mine.start()
        sends = [copy(0, me, sibling, src=in_ref), copy(1, me, (*xn, c), src=in_ref), copy(2, me, (*yn, c), src=in_ref)]
        for cp in sends:
            cp.start()
        copy(1, (*xn, c), me).wait_recv()
        sends += [copy(3, (*xn, c), sibling), copy(4, (*xn, c), (*yn, c), top)]
        sends[-2].start()
        sends[-1].start()
        copy(2, (*yn, c), me).wait_recv()
        sends += [copy(5, (*yn, c), sibling), copy(6, (*yn, c), (*xn, c), bottom)]
        sends[-2].start()
        sends[-1].start()
        copy(4, (*dg, c), me, top).wait_recv()
        copy(6, (*dg, c), me, bottom).wait_recv()
        sends.append(copy(7, (*dg, c), sibling))
        sends[-1].start()
        copy(0, sibling, me).wait_recv()
        copy(3, (*xn, 1 - c), me).wait_recv()
        copy(5, (*yn, 1 - c), me).wait_recv()
        copy(7, (*dg, 1 - c), me).wait_recv()
        for cp in sends:
            cp.wait_send()
        mine.wait()

    any_spec = pl.BlockSpec(memory_space=pl.ANY)
    return pl.pallas_call(
        body, name=name,
        out_shape=[jax.ShapeDtypeStruct((N_DEV,) + shard.shape, shard.dtype), jax.ShapeDtypeStruct((8, LANE), F32)],
        in_specs=[any_spec], out_specs=[any_spec, pl.BlockSpec(memory_space=pltpu.VMEM)],
        scratch_shapes=[pltpu.SemaphoreType.DMA((8,)), pltpu.SemaphoreType.DMA((8,)), pltpu.SemaphoreType.DMA],
    )(shard)


def _peer_list(x, y, c):
    return [((1 - x) if k & 4 else x, (1 - y) if k & 2 else y, (1 - c) if k & 1 else c) for k in range(1, N_DEV)]


def _exchange_copies(srcs, lands, send_sems, recv_sems, slotted, arriving):
    x, y, c = _mesh_pos()
    me = _slot((x, y, c))
    copies = []
    for a in range(len(srcs)):
        for k, peer in enumerate(_peer_list(x, y, c)):
            copies.append(pltpu.make_async_remote_copy(
                src_ref=srcs[a].at[_slot(peer)] if slotted else srcs[a],
                dst_ref=lands[a].at[_slot(peer) if arriving else me],
                send_sem=send_sems.at[7 * a + k], recv_sem=recv_sems.at[7 * a + k],
                device_id=peer, device_id_type=MESH_ID))
    return copies


def unwritten_hbm(shapes, dtype, name):
    def body(*refs):
        pass

    return pl.pallas_call(
        body, name=name, out_shape=[jax.ShapeDtypeStruct(s, dtype) for s in shapes],
        out_specs=[pl.BlockSpec(memory_space=pl.ANY)] * len(shapes),
    )()


HBM_SPEC = pl.BlockSpec(memory_space=pltpu.HBM)
SEM_SPEC = pl.BlockSpec(memory_space=pltpu.SEMAPHORE)
DATAFLOW = pltpu.SideEffectType.DATAFLOW_SIDE_EFFECTING


def exchange_start(srcs, lands, *, slotted, name):
    n = len(srcs)

    def body(*refs):
        ins = refs[:2 * n]
        send_sems, recv_sems = refs[2 * n], refs[2 * n + 1]
        token = refs[-1]
        for cp in _exchange_copies(ins[:n], ins[n:], send_sems, recv_sems, slotted, False):
            cp.start()
        token[...] = jnp.zeros_like(token)

    operands = [pltpu.with_memory_space_constraint(v, pltpu.HBM) for v in list(srcs) + list(lands)]
    out = pl.pallas_call(
        body, name=name,
        out_shape=(pltpu.SemaphoreType.DMA((7 * n,)), pltpu.SemaphoreType.DMA((7 * n,)),
                   *[pltpu.HBM(v.shape, v.dtype) for v in operands], jax.ShapeDtypeStruct((8, LANE), F32)),
        in_specs=[HBM_SPEC] * (2 * n),
        out_specs=(SEM_SPEC, SEM_SPEC, *[HBM_SPEC] * (2 * n), pl.BlockSpec(memory_space=pltpu.VMEM)),
        input_output_aliases={i: 2 + i for i in range(2 * n)},
        compiler_params=pltpu.CompilerParams(has_side_effects=DATAFLOW),
    )(*operands)
    return out[0], out[1], list(out[2:2 + n]), list(out[2 + n:2 + 2 * n]), out[-1]


def exchange_wait(handle, after, *, slotted, name):
    send_sems, recv_sems, srcs, lands, _ = handle
    n = len(srcs)

    def body(*refs):
        ins = refs[:2 * n]
        for cp in _exchange_copies(ins[:n], ins[n:], refs[2 * n], refs[2 * n + 1], slotted, True):
            cp.wait_send()
            cp.wait_recv()

    out = pl.pallas_call(
        body, name=name,
        out_shape=tuple(pltpu.HBM(v.shape, v.dtype) for v in srcs + lands),
        in_specs=[HBM_SPEC] * (2 * n) + [SEM_SPEC, SEM_SPEC, pl.BlockSpec(memory_space=pl.ANY)],
        out_specs=[HBM_SPEC] * (2 * n),
        input_output_aliases={i: i for i in range(2 * n)},
        compiler_params=pltpu.CompilerParams(has_side_effects=DATAFLOW),
    )(*srcs, *lands, send_sems, recv_sems, after)
    return list(out[n:])


VMEM_BLOCK_BUDGET = 46 * 2 ** 20
ROW_TILE, COL_TILE = 1024, 1408
ACC_BUDGET = 12 * 2 ** 20


def _mm_tiles(m, n, row_bytes, col_bytes, elem_bytes):
    tm, tn = _tile(m, ROW_TILE, 16), _tile(n, COL_TILE, LANE)
    while 2 * (tm * row_bytes + tn * col_bytes + tm * tn * elem_bytes) > VMEM_BLOCK_BUDGET:
        narrower = _tile(n, max(tn - LANE, LANE), LANE)
        if tn > 512 and narrower < tn:
            tn = narrower
        else:
            tm //= 2
    return tm, tn


def mm_nn(a, w, *, out_dtype, name):
    m, k = a.shape
    n = w.shape[1]
    tm, tn = _mm_tiles(m, n, 2 * k, 2 * k, jnp.dtype(out_dtype).itemsize)

    def body(a_ref, w_ref, o_ref):
        o_ref[...] = _dot(a_ref[...], w_ref[...], NN).astype(o_ref.dtype)

    return pl.pallas_call(
        body, name=name, grid=(n // tn, m // tm),
        in_specs=[pl.BlockSpec((tm, k), lambda j, i: (i, 0)), pl.BlockSpec((k, tn), lambda j, i: (0, j))],
        out_specs=pl.BlockSpec((tm, tn), lambda j, i: (i, j)),
        out_shape=jax.ShapeDtypeStruct((m, n), out_dtype),
        compiler_params=_params("parallel", "parallel"),
    )(a, w)


def mm_nn_residual(a, w, x_in, gate, coef, *, name):
    m, k = a.shape
    n = w.shape[1]
    tm, tn = _mm_tiles(m, n, 2 * k, 2 * k, 4 + 4 + 2)

    def body(a_ref, w_ref, x_ref, g_ref, o_ref, f_ref):
        f = _dot(a_ref[...], w_ref[...], NN)
        f_ref[...] = f.astype(BF16)
        o_ref[...] = x_ref[...] + (coef * g_ref[...]) * f

    return pl.pallas_call(
        body, name=name, grid=(n // tn, m // tm),
        in_specs=[pl.BlockSpec((tm, k), lambda j, i: (i, 0)), pl.BlockSpec((k, tn), lambda j, i: (0, j)),
                  pl.BlockSpec((tm, tn), lambda j, i: (i, j)), pl.BlockSpec((1, tn), lambda j, i: (0, j))],
        out_specs=[pl.BlockSpec((tm, tn), lambda j, i: (i, j)), pl.BlockSpec((tm, tn), lambda j, i: (i, j))],
        out_shape=[jax.ShapeDtypeStruct((m, n), F32), jax.ShapeDtypeStruct((m, n), BF16)],
        compiler_params=_params("parallel", "parallel"),
    )(a, w, x_in, gate)


def mm_nn_residual_loss(a, w, x_in, gate, coef, target, *, name):
    m, k = a.shape
    n = w.shape[1]
    tm, tn = _mm_tiles(m, n, 2 * k, 2 * k, 4 + 4 + 4 + 2)

    def body(a_ref, w_ref, x_ref, g_ref, t_ref, dy_ref, df_ref, dg_ref, l_ref):
        f = _dot(a_ref[...], w_ref[...], NN)
        err = x_ref[...] + (coef * g_ref[...]) * f - t_ref[...]
        dy = err * (1.0 / n)
        dy_ref[...] = dy
        df_ref[...] = ((coef * g_ref[...]) * dy).astype(BF16)
        dgate = coef * jnp.sum(dy * f, axis=0, keepdims=True)
        part = jnp.sum(jnp.sum(err * err, axis=0, keepdims=True), axis=1, keepdims=True) * (0.5 / n)

        @pl.when(pl.program_id(1) == 0)
        def _():
            dg_ref[...] = jnp.zeros_like(dg_ref)

        @pl.when((pl.program_id(0) == 0) & (pl.program_id(1) == 0))
        def _():
            l_ref[...] = jnp.zeros_like(l_ref)

        dg_ref[...] += dgate
        l_ref[...] += jnp.broadcast_to(part, l_ref.shape)

    blk = pl.BlockSpec((tm, tn), lambda j, i: (i, j))
    vec = pl.BlockSpec((1, tn), lambda j, i: (0, j))
    return pl.pallas_call(
        body, name=name, grid=(n // tn, m // tm),
        in_specs=[pl.BlockSpec((tm, k), lambda j, i: (i, 0)), pl.BlockSpec((k, tn), lambda j, i: (0, j)), blk, vec, blk],
        out_specs=[blk, blk, vec, pl.BlockSpec((1, LANE), lambda j, i: (0, 0))],
        out_shape=[jax.ShapeDtypeStruct((m, n), F32), jax.ShapeDtypeStruct((m, n), BF16),
                   jax.ShapeDtypeStruct((1, n), F32), jax.ShapeDtypeStruct((1, LANE), F32)],
        compiler_params=_params("arbitrary", "arbitrary"),
    )(a, w, x_in, gate, target)


TOKEN_SPEC = pl.BlockSpec((8, LANE), lambda *_: (0, 0))


def mm_nt(a, w, token, *, out_dtype, name):
    m, k = a.shape
    n = w.shape[0]
    tm, tn = _mm_tiles(m, n, 2 * k, 2 * k, jnp.dtype(out_dtype).itemsize)

    def body(a_ref, w_ref, token_ref, o_ref):
        o_ref[...] = _dot(a_ref[...], w_ref[...], NT).astype(o_ref.dtype)

    return pl.pallas_call(
        body, name=name, grid=(n // tn, m // tm),
        in_specs=[pl.BlockSpec((tm, k), lambda j, i: (i, 0)), pl.BlockSpec((tn, k), lambda j, i: (j, 0)), TOKEN_SPEC],
        out_specs=pl.BlockSpec((tm, tn), lambda j, i: (i, j)),
        out_shape=jax.ShapeDtypeStruct((m, n), out_dtype),
        compiler_params=_params("parallel", "parallel"),
    )(a, w, token)


def mm_nt_halves(a_lo, a_hi, w3, token, *, name):
    m = a_lo.shape[0]
    n_blk, n, tn = w3.shape
    half = n_blk // 2
    tm = _tile(m, ROW_TILE, 16)

    def body(lo_ref, hi_ref, w_ref, token_ref, o_ref):
        j = pl.program_id(1)

        @pl.when(j == 0)
        def _():
            o_ref[...] = jnp.zeros_like(o_ref)

        @pl.when(j < half)
        def _():
            o_ref[...] += _dot(lo_ref[...], w_ref[...], NT)

        @pl.when(j >= half)
        def _():
            o_ref[...] += _dot(hi_ref[...], w_ref[...], NT)

    return pl.pallas_call(
        body, name=name, grid=(m // tm, n_blk),
        in_specs=[pl.BlockSpec((tm, tn), lambda i, j: (i, jnp.minimum(j, half - 1))),
                  pl.BlockSpec((tm, tn), lambda i, j: (i, jnp.maximum(j - half, 0))),
                  pl.BlockSpec((None, n, tn), lambda i, j: (j, 0, 0)), TOKEN_SPEC],
        out_specs=pl.BlockSpec((tm, n), lambda i, j: (i, 0)),
        out_shape=jax.ShapeDtypeStruct((m, n), F32),
        compiler_params=_params("parallel", "arbitrary"),
    )(a_lo, a_hi, w3, token)


def mm_tn(a, dy, *, name):
    s, k = a.shape
    n = dy.shape[1]
    ts = _tile(s, ROW_TILE, 16)
    tk = k if k <= 2048 else _tile(k, COL_TILE, LANE)
    tn = _tile(n, ACC_BUDGET // (4 * tk), LANE)
    n_steps = s // ts

    def body(a_ref, dy_ref, o_ref, acc_ref):
        t = pl.program_id(2)

        @pl.when(t == 0)
        def _():
            acc_ref[...] = jnp.zeros_like(acc_ref)

        acc_ref[...] += _dot(a_ref[...], dy_ref[...], TN)

        @pl.when(t == n_steps - 1)
        def _():
            o_ref[...] = acc_ref[...].astype(BF16)

    return pl.pallas_call(
        body, name=name, grid=(k // tk, n // tn, n_steps),
        in_specs=[pl.BlockSpec((ts, tk), lambda kk, j, t: (t, kk)), pl.BlockSpec((ts, tn), lambda kk, j, t: (t, j))],
        out_specs=pl.BlockSpec((tk, tn), lambda kk, j, t: (kk, j)),
        out_shape=jax.ShapeDtypeStruct((k, n), BF16),
        scratch_shapes=[pltpu.VMEM((tk, tn), F32)],
        compiler_params=_params("parallel", "parallel", "arbitrary"),
    )(a, dy)


def mm_tn_halves(a, dy_lo, dy_hi, n_blocks, *, name):
    s, k = a.shape
    half = n_blocks // 2
    tn = dy_lo.shape[1] // half
    ts = _tile(s, ROW_TILE, 16)
    n_steps = s // ts

    def body(a_ref, lo_ref, hi_ref, o_ref, acc_ref):
        j, t = pl.program_id(0), pl.program_id(1)

        @pl.when(t == 0)
        def _():
            acc_ref[...] = jnp.zeros_like(acc_ref)

        @pl.when(j < half)
        def _():
            acc_ref[...] += _dot(a_ref[...], lo_ref[...], TN)

        @pl.when(j >= half)
        def _():
            acc_ref[...] += _dot(a_ref[...], hi_ref[...], TN)

        @pl.when(t == n_steps - 1)
        def _():
            o_ref[...] = acc_ref[...].astype(BF16)

    return pl.pallas_call(
        body, name=name, grid=(n_blocks, n_steps),
        in_specs=[pl.BlockSpec((ts, k), lambda j, t: (t, 0)),
                  pl.BlockSpec((ts, tn), lambda j, t: (jnp.where(j < half, t, n_steps - 1), jnp.minimum(j, half - 1))),
                  pl.BlockSpec((ts, tn), lambda j, t: (jnp.where(j < half, 0, t), jnp.maximum(j - half, 0)))],
        out_specs=pl.BlockSpec((None, k, tn), lambda j, t: (j, 0, 0)),
        out_shape=jax.ShapeDtypeStruct((n_blocks, k, tn), BF16),
        scratch_shapes=[pltpu.VMEM((k, tn), F32)],
        compiler_params=_params("parallel", "arbitrary"),
    )(a, dy_lo, dy_hi)


def ffn_dgu(df, wd, g, u, token, *, name):
    m, k = df.shape
    n = wd.shape[0]
    tm, tn = _mm_tiles(m, n, 2 * k, 2 * k, 4 * 2)

    def body(df_ref, w_ref, g_ref, u_ref, token_ref, dg_ref, du_ref):
        da = _dot(df_ref[...], w_ref[...], NT)
        gg, uu = g_ref[...].astype(F32), u_ref[...].astype(F32)
        sg = _sigmoid(gg)
        dg_ref[...] = (da * uu * (sg * (1 + gg * (1 - sg)))).astype(BF16)
        du_ref[...] = (da * (gg * sg)).astype(BF16)

    blk = pl.BlockSpec((tm, tn), lambda j, i: (i, j))
    out = jax.ShapeDtypeStruct((m, n), BF16)
    return pl.pallas_call(
        body, name=name, grid=(n // tn, m // tm),
        in_specs=[pl.BlockSpec((tm, k), lambda j, i: (i, 0)), pl.BlockSpec((tn, k), lambda j, i: (j, 0)), blk, blk, TOKEN_SPEC],
        out_specs=[blk, blk], out_shape=[out, out],
        compiler_params=_params("parallel", "parallel"),
    )(df, wd, g, u, token)


def ffn_up(h, wgu3, *, name, tm=512):
    s, k = h.shape
    n = wgu3.shape[2]
    half = wgu3.shape[0] // 2
    tm = _tile(s, tm, 16)

    def body(h_ref, wg_ref, wu_ref, g_ref, u_ref, a_ref):
        hh = h_ref[...]
        g = _dot(hh, wg_ref[...], NN)
        u = _dot(hh, wu_ref[...], NN)
        g_ref[...] = g.astype(BF16)
        u_ref[...] = u.astype(BF16)
        a_ref[...] = (g * _sigmoid(g) * u).astype(BF16)

    out = jax.ShapeDtypeStruct((s, half * n), BF16)
    blk = pl.BlockSpec((tm, n), lambda j, i: (i, j))
    return pl.pallas_call(
        body, name=name, grid=(half, s // tm),
        in_specs=[pl.BlockSpec((tm, k), lambda j, i: (i, 0)),
                  pl.BlockSpec((None, k, n), lambda j, i: (j, 0, 0)),
                  pl.BlockSpec((None, k, n), lambda j, i: (j + half, 0, 0))],
        out_specs=[blk, blk, blk], out_shape=[out, out, out],
        compiler_params=_params("parallel", "parallel"),
    )(h, wgu3, wgu3)


def _row_spec(ts, width, col=0):
    return pl.BlockSpec((ts, width), lambda i: (i, col))


def _vec_spec(width):
    return pl.BlockSpec((1, width), lambda i: (0, 0))


def _accumulate(ref, value):
    i = pl.program_id(0)

    @pl.when(i == 0)
    def _():
        ref[...] = value

    @pl.when(i > 0)
    def _():
        ref[...] += value


def norm_modulate(x, g, shift, scale, token, *, name, ts=512):
    s, d = x.shape
    ts = _tile(s, ts, 16)

    def body(x_ref, g_ref, sh_ref, sc_ref, token_ref, h_ref):
        xx = x_ref[...]
        r = lax.rsqrt(jnp.mean(xx * xx, axis=-1, keepdims=True) + EPS)
        h_ref[...] = ((xx * r) * g_ref[...] * (1 + sc_ref[...]) + sh_ref[...]).astype(BF16)

    return pl.pallas_call(
        body, name=name, grid=(s // ts,),
        in_specs=[_row_spec(ts, d), _vec_spec(d), _vec_spec(d), _vec_spec(d), TOKEN_SPEC],
        out_specs=_row_spec(ts, d), out_shape=jax.ShapeDtypeStruct((s, d), BF16),
        compiler_params=_params("parallel"),
    )(x, g, shift, scale, token)


def norm_modulate_bwd(x, g, scale, dh, dx_out, below=None, *, name, ts=256):
    s, d = x.shape
    ts = _tile(s, ts, 16)
    coef = None if below is None else below[2]

    def body(*refs):
        x_ref, g_ref, sc_ref, dh_ref, dxo_ref = refs[:5]
        dx_ref, dsh_ref, dsc_ref, dg_ref = refs[-6:-2] if below else refs[-4:]
        xx, dh_ = x_ref[...], dh_ref[...].astype(F32)
        r = lax.rsqrt(jnp.mean(xx * xx, axis=-1, keepdims=True) + EPS)
        xh = xx * r
        dn = dh_ * (1 + sc_ref[...])
        dxh = dn * g_ref[...]
        dx = dxo_ref[...] + r * (dxh - xh * jnp.mean(dxh * xh, axis=-1, keepdims=True))
        dx_ref[...] = dx
        _accumulate(dsh_ref, jnp.sum(dh_, axis=0, keepdims=True))
        _accumulate(dsc_ref, jnp.sum(dh_ * (xh * g_ref[...]), axis=0, keepdims=True))
        _accumulate(dg_ref, jnp.sum(dn * xh, axis=0, keepdims=True))
        if below:
            f_ref, gate_ref, df_ref, dgate_ref = refs[5], refs[6], refs[-2], refs[-1]
            df_ref[...] = ((coef * gate_ref[...]) * dx).astype(BF16)
            _accumulate(dgate_ref, coef * jnp.sum(dx * f_ref[...].astype(F32), axis=0, keepdims=True))

    vec = jax.ShapeDtypeStruct((1, d), F32)
    extra_in, extra_out, extra_shape, extra_args = [], [], [], []
    if below:
        extra_in, extra_args = [_row_spec(ts, d), _vec_spec(d)], [below[0], below[1]]
        extra_out, extra_shape = [_row_spec(ts, d), _vec_spec(d)], [jax.ShapeDtypeStruct((s, d), BF16), vec]
    return pl.pallas_call(
        body, name=name, grid=(s // ts,),
        in_specs=[_row_spec(ts, d), _vec_spec(d), _vec_spec(d), _row_spec(ts, d), _row_spec(ts, d)] + extra_in,
        out_specs=[_row_spec(ts, d), _vec_spec(d), _vec_spec(d), _vec_spec(d)] + extra_out,
        out_shape=[jax.ShapeDtypeStruct((s, d), F32), vec, vec, vec] + extra_shape,
        compiler_params=_params("arbitrary"),
    )(x, g, scale, dh, dx_out, *extra_args)


def mixers_up_merge(p_act, attn, wpu, wau, z, *, name):
    m, k = p_act.shape
    n = wpu.shape[1]
    tm, tn = _tile(m, ROW_TILE, 16), GATE_TILE

    def body(p_ref, a_ref, wp_ref, wa_ref, gate_ref, yp_ref, ya_ref, o_ref):
        yp = _dot(p_ref[...], wp_ref[...], NN).astype(BF16)
        ya = _dot(a_ref[...], wa_ref[...], NN).astype(BF16)
        yp_ref[...] = yp
        ya_ref[...] = ya
        sa = _sigmoid(gate_ref[:, :tn].astype(F32))
        sb = _sigmoid(gate_ref[:, tn:].astype(F32))
        o_ref[...] = (sa * yp + sb * ya).astype(BF16)

    rows = pl.BlockSpec((tm, k), lambda j, i: (i, 0))
    cols = pl.BlockSpec((k, tn), lambda j, i: (0, j))
    blk = pl.BlockSpec((tm, tn), lambda j, i: (i, j))
    out = jax.ShapeDtypeStruct((m, n), BF16)
    return pl.pallas_call(
        body, name=name, grid=(n // tn, m // tm),
        in_specs=[rows, rows, cols, cols, pl.BlockSpec((tm, 2 * tn), lambda j, i: (i, j))],
        out_specs=[blk, blk, blk], out_shape=[out, out, out],
        compiler_params=_params("parallel", "parallel"),
    )(p_act, attn, wpu, wau, z)


def merge_bwd(d_o, wo, z, y_pool, y_attn, token, *, name):
    m, k = d_o.shape
    n = wo.shape[0]
    tm, tn = _tile(m, ROW_TILE, 16), GATE_TILE

    def body(do_ref, w_ref, gate_ref, yp_ref, ya_ref, token_ref, dyp_ref, dya_ref, dz_ref):
        dm = _dot(do_ref[...], w_ref[...], NT)
        sa = _sigmoid(gate_ref[:, :tn].astype(F32))
        sb = _sigmoid(gate_ref[:, tn:].astype(F32))
        dyp_ref[...] = (dm * sa).astype(BF16)
        dya_ref[...] = (dm * sb).astype(BF16)
        dz_ref[:, :tn] = (dm * yp_ref[...] * (sa * (1 - sa))).astype(BF16)
        dz_ref[:, tn:] = (dm * ya_ref[...] * (sb * (1 - sb))).astype(BF16)

    blk = pl.BlockSpec((tm, tn), lambda j, i: (i, j))
    pair = pl.BlockSpec((tm, 2 * tn), lambda j, i: (i, j))
    out = jax.ShapeDtypeStruct((m, n), BF16)
    return pl.pallas_call(
        body, name=name, grid=(n // tn, m // tm),
        in_specs=[pl.BlockSpec((tm, k), lambda j, i: (i, 0)), pl.BlockSpec((tn, k), lambda j, i: (j, 0)),
                  pair, blk, blk, TOKEN_SPEC],
        out_specs=[blk, blk, pair], out_shape=[out, out, jax.ShapeDtypeStruct(z.shape, BF16)],
        compiler_params=_params("parallel", "parallel"),
    )(d_o, wo, z, y_pool, y_attn, token)


def _window_counts(t0, rows):
    t1 = (t0 + 1 + lax.broadcasted_iota(jnp.int32, (rows, 1), 0)).astype(F32)
    return [jnp.minimum(t1, float(w)) for w in POOL_WINDOWS]


def pool_fwd(z, pool_mix, pool_scale, *, name, ts=256):
    s = z.shape[0]
    ts = _tile(s, ts, 16)
    per = ts // POOL_HALO

    def body(u_ref, halo_ref, pm_ref, ps_ref, pooled_ref, p_ref):
        i = pl.program_id(0)
        u = u_ref[...].astype(F32)
        halo = jnp.where(i > 0, halo_ref[...].astype(F32), 0.0)
        run = jnp.concatenate([halo, u], axis=0)
        sums, width = [], 1
        for w in POOL_WINDOWS:
            while width < w:
                run = run + pltpu.roll(run, width, 0)
                width *= 2
            sums.append(run[POOL_HALO:])
        counts = _window_counts(i * ts, ts)
        for gi in range(POOL_GROUPS):
            cols = slice(gi * POOL_GROUP_W, (gi + 1) * POOL_GROUP_W)
            pooled = (sums[gi][:, cols] / counts[gi] - u[:, cols]).astype(BF16)
            pooled_ref[:, cols] = pooled
            p_ref[:, cols] = (_dot(pooled, pm_ref[gi], NN) * ps_ref[:, cols]).astype(BF16)

    out = jax.ShapeDtypeStruct((s, POOL_W), BF16)
    return pl.pallas_call(
        body, name=name, grid=(s // ts,),
        in_specs=[_row_spec(ts, POOL_W, COL_U // POOL_W),
                  pl.BlockSpec((POOL_HALO, POOL_W), lambda i: (jnp.maximum(i * per - 1, 0), COL_U // POOL_W)),
                  pl.BlockSpec((POOL_GROUPS, POOL_GROUP_W, POOL_GROUP_W), lambda i: (0, 0, 0)),
                  _vec_spec(POOL_W)],
        out_specs=[_row_spec(ts, POOL_W)] * 2, out_shape=[out, out],
        compiler_params=_params("parallel"),
    )(z, z, pool_mix, pool_scale)


def pool_bwd(dp, pooled, pool_mix, pool_scale, dz, *, name, ts=256):
    s = dp.shape[0]
    ts = _tile(s, ts, 16)
    per = ts // POOL_HALO
    n_steps = s // ts
    rows = ts + POOL_HALO

    def body(dp_ref, halo_ref, pooled_ref, pm_ref, ps_ref, dz_in_ref, du_ref, dps_ref, dpm_ref):
        i = pl.program_id(0)
        dp_main = dp_ref[...]
        halo = jnp.where(i < n_steps - 1, halo_ref[...], 0.0)
        dmixed = jnp.concatenate([dp_main, halo], axis=0) * ps_ref[...]
        counts = _window_counts(i * ts, rows)
        dps_parts = []
        for gi, w in enumerate(POOL_WINDOWS):
            cols = slice(gi * POOL_GROUP_W, (gi + 1) * POOL_GROUP_W)
            dmx = dmixed[:, cols].astype(BF16)
            pooled = pooled_ref[:, cols]
            mixed = _dot(pooled, pm_ref[gi], NN)
            dps_parts.append(jnp.sum(dp_main[:, cols] * mixed, axis=0, keepdims=True))
            dpm_g = _dot(pooled, dmx[:ts], TN)

            @pl.when(i == 0)
            def _():
                dpm_ref[gi] = dpm_g

            @pl.when(i > 0)
            def _():
                dpm_ref[gi] += dpm_g

            dpooled = _dot(dmx, pm_ref[gi], NT)
            run, width = dpooled / counts[gi], 1
            while width < w:
                run = run + pltpu.roll(run, rows - width, 0)
                width *= 2
            du_ref[:, cols] = (run[:ts] - dpooled[:ts]).astype(BF16)
        _accumulate(dps_ref, jnp.concatenate(dps_parts, axis=1))

    return pl.pallas_call(
        body, name=name, grid=(n_steps,),
        in_specs=[_row_spec(ts, POOL_W),
                  pl.BlockSpec((POOL_HALO, POOL_W), lambda i: (jnp.minimum((i + 1) * per, s // POOL_HALO - 1), 0)),
                  _row_spec(ts, POOL_W),
                  pl.BlockSpec((POOL_GROUPS, POOL_GROUP_W, POOL_GROUP_W), lambda i: (0, 0, 0)),
                  _vec_spec(POOL_W), pl.BlockSpec(memory_space=pl.ANY)],
        out_specs=[_row_spec(ts, POOL_W, COL_U // POOL_W), _vec_spec(POOL_W),
                   pl.BlockSpec((POOL_GROUPS, POOL_GROUP_W, POOL_GROUP_W), lambda i: (0, 0, 0))],
        out_shape=[jax.ShapeDtypeStruct(dz.shape, BF16), jax.ShapeDtypeStruct((1, POOL_W), F32),
                   jax.ShapeDtypeStruct((POOL_GROUPS, POOL_GROUP_W, POOL_GROUP_W), F32)],
        input_output_aliases={5: 0},
        compiler_params=_params("arbitrary"),
    )(dp, dp, pooled, pool_mix, pool_scale, dz)


def _bucket_one_hot():
    ql = np.arange(BLK)[:, None]
    j = np.arange(2 * BLK)[None, :]
    n = np.clip(BLK + ql - j, 0, None)
    nf = np.maximum(n, 1).astype(np.float32)
    large = MAX_EXACT + (np.log(nf / MAX_EXACT) / np.log(REL_MAX_DIST / MAX_EXACT)
                         * (NUM_BUCKETS - MAX_EXACT)).astype(np.int32)
    large = np.minimum(large, NUM_BUCKETS - 1)
    bucket = np.where(n < MAX_EXACT, n, large).astype(np.int32).reshape(-1)
    return (np.arange(NUM_BUCKETS)[:, None] == bucket[None, :]).astype(np.float32)


def bias_table(rel_bias_t, one_hot, *, name, tc=4096):
    n = one_hot.shape[1]

    def body(rb_ref, oh_ref, o_ref):
        o_ref[...] = _dot(rb_ref[...], oh_ref[...], NN, precision=lax.Precision.HIGHEST)

    return pl.pallas_call(
        body, name=name, grid=(n // tc,),
        in_specs=[pl.BlockSpec((N_HEADS, NUM_BUCKETS), lambda i: (0, 0)), pl.BlockSpec((NUM_BUCKETS, tc), lambda i: (0, i))],
        out_specs=pl.BlockSpec((N_HEADS, tc), lambda i: (0, i)),
        out_shape=jax.ShapeDtypeStruct((N_HEADS, n), F32),
        compiler_params=_params("parallel"),
    )(rel_bias_t, one_hot)


def bias_table_bwd(dbias, one_hot, *, name, tc=4096):
    n = one_hot.shape[1]

    def body(db_ref, oh_ref, o_ref):
        _accumulate(o_ref, _dot(db_ref[...], oh_ref[...], NT, precision=lax.Precision.HIGHEST))

    return pl.pallas_call(
        body, name=name, grid=(n // tc,),
        in_specs=[pl.BlockSpec((N_HEADS, tc), lambda i: (0, i)), pl.BlockSpec((NUM_BUCKETS, tc), lambda i: (0, i))],
        out_specs=pl.BlockSpec((N_HEADS, NUM_BUCKETS), lambda i: (0, 0)),
        out_shape=jax.ShapeDtypeStruct((N_HEADS, NUM_BUCKETS), F32),
        compiler_params=_params("arbitrary"),
    )(dbias, one_hot)


def _lane_half(shape):
    return lax.broadcasted_iota(jnp.int32, shape, len(shape) - 1) < HEAD_DIM


def _half_sums(v, first):
    s0 = jnp.sum(jnp.where(first, v, 0.0), axis=-1, keepdims=True)
    s1 = jnp.sum(jnp.where(first, 0.0, v), axis=-1, keepdims=True)
    return jnp.where(first, s0, s1)


BWD_STACK, FWD_STACK = N_HEADS, N_HEADS


def _band_mask(n, heads):
    ql = lax.broadcasted_iota(jnp.int32, (heads * BLK, 2 * BLK), 0) & (BLK - 1)
    j = lax.broadcasted_iota(jnp.int32, (heads * BLK, 2 * BLK), 1)
    return (j > ql) & (j <= ql + BLK) & ((j >= BLK) | (n > 0))


def _norm_keys(kband):
    first = _lane_half(kband.shape)
    r = lax.rsqrt(_half_sums(kband * kband, first) * (1.0 / HEAD_DIM) + EPS)
    return kband * r


def _kv_lanes(h0, heads):
    shape = (heads * BLK, LANE)
    head = h0 + lax.broadcasted_iota(jnp.int32, shape, 0) // BLK
    return _lane_half(shape) == (head < GQA)


def _stack_heads(ref, h0, heads, dtype=F32):
    parts = []
    for h in range(h0, h0 + heads):
        part = ref[:, (h // 2) * LANE:(h // 2 + 1) * LANE].astype(dtype)
        parts.append(pltpu.roll(part, HEAD_DIM, 1) if h % 2 != h // GQA else part)
    return jnp.where(_kv_lanes(h0, heads), jnp.concatenate(parts, axis=0), 0.0)


def _unstack_heads(ref, h0, heads, stacked):
    for i in range(heads // 2):
        pair = None
        for sub in range(2):
            part = stacked[(2 * i + sub) * BLK:(2 * i + sub + 1) * BLK]
            part = pltpu.roll(part, HEAD_DIM, 1) if sub != (h0 + 2 * i) // GQA else part
            pair = part if pair is None else pair + part
        col = (h0 // 2 + i) * LANE
        ref[:, col:col + LANE] = pair.astype(BF16)


def _stack_logits(q_ref, h0, heads, qg, kn, bias_ref, sink_ref, mask):
    qa = _stack_heads(q_ref, h0, heads)
    r = lax.rsqrt(jnp.sum(qa * qa, axis=-1, keepdims=True) * (1.0 / HEAD_DIM) + EPS)
    xh = qa * r
    qn = (xh * qg).astype(BF16)
    bias = bias_ref[h0:h0 + heads].reshape(heads * BLK, 2 * BLK)
    logits = _dot(qn, kn, NT) * (HEAD_DIM ** -0.5) + bias
    p, p_sink = _softmax_with_sink(jnp.where(mask, logits, NEG_INF), sink_ref[h0 * BLK:(h0 + heads) * BLK])
    return xh, r, qn, p, p_sink


def _softmax_with_sink(logits, sink):
    m = jnp.maximum(jnp.max(logits, axis=-1, keepdims=True), sink)
    e = jnp.exp(logits - m)
    es = jnp.exp(sink - m)
    den = jnp.sum(e, axis=-1, keepdims=True) + es
    return e / den, es / den


def _attn_specs(nb, last):
    cur = lambda n: jnp.minimum(n, last)
    prev = lambda n: jnp.minimum(jnp.maximum(n - 1, 0), last)
    return [pl.BlockSpec((BLK, ATT_W), lambda n: (cur(n), COL_Q // ATT_W)),
            pl.BlockSpec((BLK, KV_W), lambda n: (prev(n), COL_K // KV_W)),
            pl.BlockSpec((BLK, KV_W), lambda n: (cur(n), COL_K // KV_W)),
            pl.BlockSpec((BLK, KV_W), lambda n: (prev(n), COL_V // KV_W)),
            pl.BlockSpec((BLK, KV_W), lambda n: (cur(n), COL_V // KV_W))]


def attn_fwd(z, qg2, kg2, sinks, bias, *, name):
    s = z.shape[0]
    nb = s // BLK

    def body(q_ref, kp_ref, kc_ref, vp_ref, vc_ref, qg_ref, kg_ref, sink_ref, bias_ref, o_ref):
        mask = _band_mask(pl.program_id(0), FWD_STACK)
        kn = (_norm_keys(jnp.concatenate([kp_ref[...], kc_ref[...]], axis=0).astype(F32)) * kg_ref[...]).astype(BF16)
        vb = jnp.concatenate([vp_ref[...], vc_ref[...]], axis=0).astype(BF16)
        for h0 in range(0, N_HEADS, FWD_STACK):
            _, _, _, p, _ = _stack_logits(q_ref, h0, FWD_STACK, qg_ref[...], kn, bias_ref, sink_ref, mask)
            out = _dot(p.astype(BF16), vb, NN)
            _unstack_heads(o_ref, h0, FWD_STACK, jnp.where(_kv_lanes(h0, FWD_STACK), out, 0.0))

    return pl.pallas_call(
        body, name=name, grid=(nb,),
        in_specs=_attn_specs(nb, nb - 1) + [
            _vec_spec(LANE), _vec_spec(LANE), pl.BlockSpec((N_HEADS * BLK, 1), lambda n: (0, 0)),
            pl.BlockSpec((N_HEADS, BLK, 2 * BLK), lambda n: (0, 0, 0))],
        out_specs=pl.BlockSpec((BLK, ATT_W), lambda n: (n, 0)),
        out_shape=jax.ShapeDtypeStruct((s, ATT_W), BF16),
        compiler_params=_params("parallel"),
    )(z, z, z, z, z, qg2, kg2, sinks, bias)


def attn_bwd(z, d_out, qg2, kg2, sinks, bias, dz, *, name):
    s = z.shape[0]
    nb = s // BLK
    scale = HEAD_DIM ** -0.5

    def body(q_ref, kp_ref, kc_ref, vp_ref, vc_ref, do_ref, qg_ref, kg_ref, sink_ref, bias_ref, dz_in_ref,
             dq_ref, dk_ref, dv_ref, dqg_ref, dkg_ref, dsink_ref, dbias_ref, band_k, band_v, carry_k, carry_v, dsink_rows):
        n = pl.program_id(0)

        @pl.when(n == 0)
        def _():
            dqg_ref[...] = jnp.zeros_like(dqg_ref)
            dkg_ref[...] = jnp.zeros_like(dkg_ref)
            dbias_ref[...] = jnp.zeros_like(dbias_ref)
            carry_k[...] = jnp.zeros_like(carry_k)
            carry_v[...] = jnp.zeros_like(carry_v)
            dsink_rows[...] = jnp.zeros_like(dsink_rows)

        @pl.when(n == nb)
        def _():
            band_k[...] = jnp.zeros_like(band_k)
            band_v[...] = jnp.zeros_like(band_v)

        @pl.when(n < nb)
        def _():
            mask = _band_mask(n, BWD_STACK)
            kn = (_norm_keys(jnp.concatenate([kp_ref[...], kc_ref[...]], axis=0).astype(F32)) * kg_ref[...]).astype(BF16)
            vb = jnp.concatenate([vp_ref[...], vc_ref[...]], axis=0).astype(BF16)
            dkn = jnp.zeros((2 * BLK, KV_W), F32)
            dvb = jnp.zeros((2 * BLK, KV_W), F32)
            dqg = jnp.zeros((1, LANE), F32)
            for h0 in range(0, N_HEADS, BWD_STACK):
                rows = slice(h0 * BLK, (h0 + BWD_STACK) * BLK)
                xh, r, qn, p, p_sink = _stack_logits(q_ref, h0, BWD_STACK, qg_ref[...], kn, bias_ref, sink_ref, mask)
                do = _stack_heads(do_ref, h0, BWD_STACK).astype(BF16)
                dp = _dot(do, vb, NT)
                delta = jnp.sum(p * dp, axis=-1, keepdims=True)
                ds = p * (dp - delta)
                dsink_rows[rows] -= p_sink * delta
                dbias_ref[h0:h0 + BWD_STACK] += ds.reshape(BWD_STACK, BLK, 2 * BLK)
                ds16 = ds.astype(BF16)
                dqn = jnp.where(_kv_lanes(h0, BWD_STACK), _dot(ds16, kn, NN) * scale, 0.0)
                dkn = dkn + _dot(ds16, qn, TN) * scale
                dvb = dvb + _dot(p.astype(BF16), do, TN)
                dqg = dqg + jnp.sum(dqn * xh, axis=0, keepdims=True)
                dxh = dqn * qg_ref[...]
                _unstack_heads(dq_ref, h0, BWD_STACK,
                               r * (dxh - xh * (jnp.sum(dxh * xh, axis=-1, keepdims=True) * (1.0 / HEAD_DIM))))
            band_k[...] = dkn
            band_v[...] = dvb
            dqg_ref[...] += dqg

        dkn_prev = carry_k[...] + band_k[:BLK]
        dv_ref[...] = (carry_v[...] + band_v[:BLK]).astype(BF16)
        carry_k[...] = band_k[BLK:]
        carry_v[...] = band_v[BLK:]
        kp = kp_ref[...].astype(F32)
        first = _lane_half(kp.shape)
        r = lax.rsqrt(_half_sums(kp * kp, first) * (1.0 / HEAD_DIM) + EPS)
        xh = kp * r
        dkg_ref[...] += jnp.sum(dkn_prev * xh, axis=0, keepdims=True)
        dxh = dkn_prev * kg_ref[...]
        dk_ref[...] = (r * (dxh - xh * (_half_sums(dxh * xh, first) * (1.0 / HEAD_DIM)))).astype(BF16)

        @pl.when(n == nb)
        def _():
            dqg_ref[...] += pltpu.roll(dqg_ref[...], HEAD_DIM, 1)
            dkg_ref[...] += pltpu.roll(dkg_ref[...], HEAD_DIM, 1)
            lane16 = lax.broadcasted_iota(jnp.int32, (1, N_HEADS), 1)
            dsink = jnp.zeros((1, N_HEADS), F32)
            for h in range(N_HEADS):
                dsink = dsink + jnp.where(lane16 == h, jnp.sum(dsink_rows[h * BLK:(h + 1) * BLK], axis=0, keepdims=True), 0.0)
            dsink_ref[...] = dsink

    last = nb - 1
    cur = lambda n: jnp.minimum(n, last)
    back = lambda n: jnp.maximum(n - 1, 0)
    full3 = pl.BlockSpec((N_HEADS, BLK, 2 * BLK), lambda n: (0, 0, 0))
    return pl.pallas_call(
        body, name=name, grid=(nb + 1,),
        in_specs=_attn_specs(nb, last) + [
            pl.BlockSpec((BLK, ATT_W), lambda n: (cur(n), 0)),
            _vec_spec(LANE), _vec_spec(LANE), pl.BlockSpec((N_HEADS * BLK, 1), lambda n: (0, 0)), full3,
            pl.BlockSpec(memory_space=pl.ANY)],
        out_specs=[pl.BlockSpec((BLK, ATT_W), lambda n: (cur(n), COL_Q // ATT_W)),
                   pl.BlockSpec((BLK, KV_W), lambda n: (back(n), 0)),
                   pl.BlockSpec((BLK, KV_W), lambda n: (back(n), 0)),
                   _vec_spec(LANE), _vec_spec(LANE), _vec_spec(N_HEADS), full3],
        input_output_aliases={10: 0},
        out_shape=[jax.ShapeDtypeStruct(dz.shape, BF16), jax.ShapeDtypeStruct((s, KV_W), BF16),
                   jax.ShapeDtypeStruct((s, KV_W), BF16), jax.ShapeDtypeStruct((1, LANE), F32),
                   jax.ShapeDtypeStruct((1, LANE), F32), jax.ShapeDtypeStruct((1, N_HEADS), F32),
                   jax.ShapeDtypeStruct((N_HEADS, BLK, 2 * BLK), F32)],
        scratch_shapes=[pltpu.VMEM((2 * BLK, KV_W), F32), pltpu.VMEM((2 * BLK, KV_W), F32),
                        pltpu.VMEM((BLK, KV_W), F32), pltpu.VMEM((BLK, KV_W), F32), pltpu.VMEM((N_HEADS * BLK, 1), F32)],
        compiler_params=_params("arbitrary"),
    )(z, z, z, z, z, d_out, qg2, kg2, sinks, bias, dz)


def _adamw(w, g, m, v):
    m = ADAM_B1 * m + (1.0 - ADAM_B1) * g
    v = ADAM_B2 * v + (1.0 - ADAM_B2) * (g * g)
    m_hat = m / (1.0 - ADAM_B1 ** ADAM_STEP)
    v_hat = v / (1.0 - ADAM_B2 ** ADAM_STEP)
    delta = -ADAM_LR * (m_hat / (jnp.sqrt(v_hat) + ADAM_EPS) + ADAM_WD * w)
    return delta, m, v


def ada_fwd(c16, w, b, *, name, tn=768):
    k, n = w.shape
    tn = _tile(n, tn, LANE)

    def body(c_ref, w_ref, b_ref, o_ref):
        cc = c_ref[...]
        o_ref[...] = _dot((cc * _sigmoid(cc)).astype(BF16), w_ref[...].astype(BF16), NN) + b_ref[...]

    return pl.pallas_call(
        body, name=name, grid=(n // tn,),
        in_specs=[pl.BlockSpec((c16.shape[0], k), lambda j: (0, 0)), pl.BlockSpec((k, tn), lambda j: (0, j)),
                  pl.BlockSpec((1, tn), lambda j: (0, j))],
        out_specs=pl.BlockSpec((c16.shape[0], tn), lambda j: (0, j)),
        out_shape=jax.ShapeDtypeStruct((c16.shape[0], n), F32),
        compiler_params=_params("parallel"),
    )(c16, w, b)


def ada_bwd_adamw(c_t, dmod, w, m, v, *, name, tn=256):
    k, n = w.shape
    tn = _tile(n, tn, LANE)

    def body(c_ref, d_ref, w_ref, m_ref, v_ref, g_ref, dl_ref, mo_ref, vo_ref):
        cc = c_ref[...]
        g = _dot((cc * _sigmoid(cc)).astype(BF16), d_ref[...].astype(BF16), NN)
        g_ref[...] = g
        dl_ref[...], mo_ref[...], vo_ref[...] = _adamw(w_ref[...], g, m_ref[...], v_ref[...])

    blk = pl.BlockSpec((k, tn), lambda j: (0, j))
    out = jax.ShapeDtypeStruct((k, n), F32)
    return pl.pallas_call(
        body, name=name, grid=(n // tn,),
        in_specs=[pl.BlockSpec((k, LANE), lambda j: (0, 0)), pl.BlockSpec((LANE, tn), lambda j: (0, j)), blk, blk, blk],
        out_specs=[blk] * 4, out_shape=[out] * 4,
        compiler_params=_params("parallel"),
    )(c_t, dmod, w, m, v)


def adamw_from_parts(parts, w, m, v, *, name):
    r, c = w.shape
    tr = _tile(r, max(16, (256 * 1024) // c), 16)

    def body(p_ref, w_ref, m_ref, v_ref, g_ref, dl_ref, mo_ref, vo_ref):
        g = p_ref[0].astype(F32)
        for d in range(1, N_DEV):
            g = g + p_ref[d].astype(F32)
        g_ref[...] = g
        dl_ref[...], mo_ref[...], vo_ref[...] = _adamw(w_ref[...], g, m_ref[...], v_ref[...])

    blk = pl.BlockSpec((tr, c), lambda i: (i, 0))
    out = jax.ShapeDtypeStruct((r, c), F32)
    return pl.pallas_call(
        body, name=name, grid=(r // tr,),
        in_specs=[pl.BlockSpec((N_DEV, tr, c), lambda i: (0, i, 0)), blk, blk, blk],
        out_specs=[blk] * 4, out_shape=[out] * 4,
        compiler_params=_params("parallel"),
    )(parts, w, m, v)


def _ffn_fwd(x_in, g, shift, scale, gate, wgu3, get_wd, token, tag, target=None):
    h = norm_modulate(x_in, g, shift, scale, token, name=f"{tag}_norm")
    gg, uu, act = ffn_up(h, wgu3, name=f"{tag}_up")
    wd = get_wd(gg)
    if target is None:
        x_out, f = mm_nn_residual(act, wd, x_in, gate, 0.5, name=f"{tag}_down")
    else:
        x_out, f = mm_nn_residual_loss(act, wd, x_in, gate, 0.5, target, name=f"{tag}_down_loss"), None
    return x_out, (h, gg, uu, act, f), wd


def _ffn_bwd(dx_out, df, dgate, x_in, g, scale, wgu3, wd, saved, token, scatter, split, tag, below=None):
    h, gg, uu, act, _ = saved
    dwd = mm_tn(act, df, name=f"{tag}_dwd").reshape(N_DEV, -1, D_MODEL)
    if split:
        token = scatter([f"w_{tag}_down"], [dwd], f"scatter_{tag}_down")
    dgg, duu = ffn_dgu(df, wd, gg, uu, token, name=f"{tag}_dgu")
    dwgu = mm_tn_halves(h, dgg, duu, N_DEV, name=f"{tag}_dwgu")
    if split:
        token = scatter([f"w_{tag}_gu"], [dwgu], f"scatter_{tag}_gu")
    else:
        token = scatter([f"w_{tag}_gu", f"w_{tag}_down"], [dwgu, dwd], f"scatter_{tag}")
    dh = mm_nt_halves(dgg, duu, wgu3, token, name=f"{tag}_dh")
    dx_in, dshift, dscale, dg, *rest = norm_modulate_bwd(x_in, g, scale, dh, dx_out, below, name=f"{tag}_norm_bwd")
    return (dx_in, (dshift, dscale, dgate), dg, *rest)


def kernel(x, c, w_ada, b_ada, g_ffn1, w_ffn1_gu, w_ffn1_down, g_mix, w_in, pool_mix, pool_scale, w_pool_up, q_gain, k_gain, sinks, rel_bias, w_attn_up, w_o, g_ffn2, w_ffn2_gu, w_ffn2_down, loss_target, m_w_ada, m_b_ada, m_g_ffn1, m_w_ffn1_gu, m_w_ffn1_down, m_g_mix, m_w_in, m_pool_mix, m_pool_scale, m_w_pool_up, m_q_gain, m_k_gain, m_sinks, m_rel_bias, m_w_attn_up, m_w_o, m_g_ffn2, m_w_ffn2_gu, m_w_ffn2_down, v_w_ada, v_b_ada, v_g_ffn1, v_w_ffn1_gu, v_w_ffn1_down, v_g_mix, v_w_in, v_pool_mix, v_pool_scale, v_w_pool_up, v_q_gain, v_k_gain, v_sinks, v_rel_bias, v_w_attn_up, v_w_o, v_g_ffn2, v_w_ffn2_gu, v_w_ffn2_down):
    me = _slot(_mesh_pos())
    x0, target = x[0], loss_target[0]
    n_ada = w_ada.shape[2]
    pm_rows = pool_mix.shape[2]

    big = dict(w_ffn1_gu=(w_ffn1_gu, m_w_ffn1_gu, v_w_ffn1_gu), w_ffn1_down=(w_ffn1_down, m_w_ffn1_down, v_w_ffn1_down),
               w_in=(w_in, m_w_in, v_w_in), pool_mix=(pool_mix, m_pool_mix, v_pool_mix),
               w_pool_up=(w_pool_up, m_w_pool_up, v_w_pool_up), w_attn_up=(w_attn_up, m_w_attn_up, v_w_attn_up),
               w_o=(w_o, m_w_o, v_w_o), w_ffn2_gu=(w_ffn2_gu, m_w_ffn2_gu, v_w_ffn2_gu),
               w_ffn2_down=(w_ffn2_down, m_w_ffn2_down, v_w_ffn2_down))
    shard2d = {k: (POOL_GROUPS * pm_rows, POOL_GROUP_W) if k == "pool_mix" else t[0].shape[1:] for k, t in big.items()}
    mix_keys = ["w_in", "pool_mix", "w_pool_up", "w_attn_up", "w_o"]
    ffn2_keys = ["w_ffn2_gu", "w_ffn2_down"]

    def shard_bf16(k, token=None):
        w = big[k][0].reshape(shard2d[k])
        return (w if token is None else w + token[0, 0]).astype(BF16)

    def landing_zones(blocks, tag):
        zones = unwritten_hbm([(N_DEV,) + b.shape for b in blocks], BF16, f"{tag}_zones")
        return [lax.dynamic_update_slice(z, b[None], (me, 0, 0)) for z, b in zip(zones, blocks)]

    def start_gather(keys, token, tag):
        shards = [shard_bf16(k, token) for k in keys]
        return exchange_start(shards, landing_zones(shards, tag), slotted=False, name=f"{tag}_start")

    pending = []

    def scatter(keys, grads, tag):
        lands = landing_zones([lax.dynamic_index_in_dim(g, me, 0, keepdims=False) for g in grads], tag)
        handle = exchange_start(grads, lands, slotted=True, name=f"{tag}_start")
        pending.append((keys, handle, tag))
        return handle[4]

    c_all, _ = all_gather_small(c.reshape(D_MODEL // LANE, LANE), "gather_c")
    c_all = c_all.reshape(N_DEV, D_MODEL)
    c16 = jnp.pad(c_all, ((0, 16 - N_DEV), (0, 0)))
    b_mine = lax.dynamic_slice(b_ada, (0, me * n_ada), (1, n_ada))
    mod_cols = ada_fwd(c16, w_ada[0], b_mine, name="ada_fwd")[:N_DEV]
    mod_all, token = all_gather_small(mod_cols.reshape(-1, LANE), "gather_mod")
    mod = lax.dynamic_index_in_dim(mod_all.reshape(N_DEV, N_DEV, n_ada), me, axis=1, keepdims=False)
    mod = mod.reshape(N_MOD, 1, D_MODEL)

    wgu1, token = all_gather_hbm(shard_bf16("w_ffn1_gu", token), "gather_ffn1_gu")
    gather_wd1 = start_gather(["w_ffn1_down"], token, "gather_ffn1_down")
    gather_mix = start_gather(mix_keys, gather_wd1[4], "gather_mix")
    gather_ffn2 = start_gather(ffn2_keys, gather_mix[4], "gather_ffn2")
    token = gather_ffn2[4]

    def get_wd1(after):
        return exchange_wait(gather_wd1, after, slotted=False, name="gather_ffn1_down_wait")[0].reshape(-1, D_MODEL)

    x1, saved1, wd1 = _ffn_fwd(x0, g_ffn1, mod[0], mod[1], mod[2], wgu1, get_wd1, token, "ffn1")
    gathered = dict(zip(mix_keys, exchange_wait(gather_mix, x1, slotted=False, name="gather_mix_wait")))
    def columns_out(blocks):
        return jnp.transpose(blocks, (1, 0, 2)).reshape(blocks.shape[1], -1)

    def columns_in(full):
        return jnp.transpose(full.reshape(full.shape[0], N_DEV, -1), (1, 0, 2))

    w_in_full = columns_out(gathered["w_in"])
    w_in_z = jnp.concatenate([w_in_full[:, s:s + w] for s, w in Z_PIECES], axis=1)
    pm_full = jnp.transpose(gathered["pool_mix"].reshape(N_DEV, POOL_GROUPS, pm_rows, POOL_GROUP_W),
                            (1, 0, 2, 3)).reshape(POOL_GROUPS, POOL_GROUP_W, POOL_GROUP_W)
    wpu, wau = columns_out(gathered["w_pool_up"]), columns_out(gathered["w_attn_up"])
    wo_full = gathered["w_o"].reshape(D_MODEL, D_MODEL)
    h2 = norm_modulate(x1, g_mix, mod[3], mod[4], token, name="mix_norm")
    z = mm_nn(h2, w_in_z, out_dtype=BF16, name="mix_in")
    pooled, p_act = pool_fwd(z, pm_full, pool_scale, name="pool_fwd")
    one_hot = jnp.asarray(_bucket_one_hot())
    bias = bias_table(rel_bias.T, one_hot, name="bias_table").reshape(N_HEADS, BLK, 2 * BLK)
    qg2, kg2 = jnp.tile(q_gain, (1, 2)), jnp.tile(k_gain, (1, 2))
    sink_rows = jnp.repeat(sinks[0], BLK).reshape(N_HEADS * BLK, 1)
    attn = attn_fwd(z, qg2, kg2, sink_rows, bias, name="attn_fwd")
    y_pool, y_attn, merged = mixers_up_merge(p_act, attn, wpu, wau, z, name="mixers_up")
    x2, o_act = mm_nn_residual(merged, wo_full, x1, mod[5], 1.0, name="mix_out")
    wgu2, wd2 = exchange_wait(gather_ffn2, x2, slotted=False, name="gather_ffn2_wait")
    wd2 = wd2.reshape(-1, D_MODEL)
    (dy, df2, dgate3, loss_row), saved2, _ = _ffn_fwd(x2, g_ffn2, mod[6], mod[7], mod[8], wgu2, lambda after: wd2, token,
                                                     "ffn2", target)

    dx2, dmod3, dg_ffn2, d_o, dgate2 = _ffn_bwd(dy, df2, dgate3, x2, g_ffn2, mod[7], wgu2, wd2, saved2, token, scatter, False,
                                               "ffn2", below=(o_act, mod[5], 1.0))
    dwo = mm_tn(merged, d_o, name="mix_dwo").reshape(N_DEV, -1, D_MODEL)
    dyp, dya, dz = merge_bwd(d_o, wo_full, z, y_pool, y_attn, token, name="merge_bwd")
    dwpu = mm_tn(p_act, dyp, name="pool_dwup")
    dp_act = mm_nt(dyp, wpu, token, out_dtype=BF16, name="pool_dp")
    dz, dpool_scale, dpm = pool_bwd(dp_act, pooled, pm_full, pool_scale, dz, name="pool_bwd")
    dwau = mm_tn(attn, dya, name="attn_dwup")
    dattn = mm_nt(dya, wau, token, out_dtype=BF16, name="attn_dout")
    dz, dk, dv, dqg, dkg, dsinks, dbias = attn_bwd(z, dattn, qg2, kg2, sink_rows, bias, dz, name="attn_bwd")
    dz = lax.dynamic_update_slice(dz, jnp.concatenate([dk, dv], axis=1), (0, COL_K))
    drel = bias_table_bwd(dbias.reshape(N_HEADS, -1), one_hot, name="bias_table_bwd").T
    dwin_z = mm_tn(h2, dz, name="mix_dwin")
    z_start = np.cumsum([0] + [w for _, w in Z_PIECES[:-1]])
    in_w_in_order = sorted(zip(Z_PIECES, z_start))
    dwin = jnp.concatenate([dwin_z[:, int(at):int(at) + w] for (_, w), at in in_w_in_order], axis=1)
    mix_grads = dict(w_in=columns_in(dwin),
                     pool_mix=jnp.transpose(dpm.astype(BF16).reshape(POOL_GROUPS, N_DEV, pm_rows, POOL_GROUP_W),
                                            (1, 0, 2, 3)).reshape(N_DEV, POOL_GROUPS * pm_rows, POOL_GROUP_W),
                     w_pool_up=columns_in(dwpu), w_attn_up=columns_in(dwau), w_o=dwo)
    token = scatter(mix_keys, [mix_grads[k] for k in mix_keys], "scatter_mix")
    dh2 = mm_nt(dz, w_in_z, token, out_dtype=BF16, name="mix_dh")
    dx1, dsh2, dsc2, dg_mix, df1, dgate1 = norm_modulate_bwd(x1, g_mix, mod[4], dh2, dx2, (saved1[4], mod[2], 0.5),
                                                           name="mix_norm_bwd")
    dx0, dmod1, dg_ffn1 = _ffn_bwd(dx1, df1, dgate1, x0, g_ffn1, mod[1], wgu1, wd1, saved1, token, scatter, True, "ffn1")

    small = [("b_ada", b_ada, m_b_ada, v_b_ada, jnp.concatenate(list(dmod1 + (dsh2, dsc2, dgate2) + dmod3), axis=1)),
             ("g_ffn1", g_ffn1, m_g_ffn1, v_g_ffn1, dg_ffn1), ("g_mix", g_mix, m_g_mix, v_g_mix, dg_mix),
             ("g_ffn2", g_ffn2, m_g_ffn2, v_g_ffn2, dg_ffn2),
             ("pool_scale", pool_scale, m_pool_scale, v_pool_scale, dpool_scale),
             ("q_gain", q_gain, m_q_gain, v_q_gain, dqg[:, :HEAD_DIM]), ("k_gain", k_gain, m_k_gain, v_k_gain, dkg[:, :HEAD_DIM]),
             ("sinks", sinks, m_sinks, v_sinks, dsinks), ("rel_bias", rel_bias, m_rel_bias, v_rel_bias, drel)]
    n_small = sum(t[1].size for t in small)
    width = n_small + 1 + (-(n_small + 1) % (8 * LANE))

    def flat(arrs):
        row = jnp.concatenate([a.reshape(1, -1) for a in arrs], axis=1)
        return jnp.pad(row, ((0, 0), (0, width - row.shape[1])))

    small_parts, _ = all_gather_small(flat([t[4] for t in small] + [loss_row[:, :1]]).reshape(-1, LANE), "gather_small_grads")
    small_parts = small_parts.reshape(N_DEV, 1, width)
    sg, sd, sm, sv = adamw_from_parts(small_parts, flat([t[1] for t in small]), flat([t[2] for t in small]),
                                      flat([t[3] for t in small]), name="adamw_small")
    loss = sg[0, n_small]

    dmod_all = small_parts[:, 0, :N_MOD * D_MODEL]
    dmod_mine = lax.dynamic_slice(dmod_all, (0, me * n_ada), (N_DEV, n_ada))
    c_t = jnp.pad(c_all.T, ((0, 0), (0, LANE - N_DEV)))
    ada_out = ada_bwd_adamw(c_t, jnp.pad(dmod_mine, ((0, LANE - N_DEV), (0, 0))), w_ada[0], m_w_ada[0], v_w_ada[0],
                            name="ada_bwd_adamw")

    res = {"w_ada": [o[None] for o in ada_out]}
    after = ada_out[0]
    for keys, handle, tag in pending:
        parts = exchange_wait(handle, after, slotted=True, name=f"{tag}_wait")
        for k, part in zip(keys, parts):
            w_, m_, v_ = big[k]
            outs = adamw_from_parts(part, w_.reshape(shard2d[k]), m_.reshape(shard2d[k]), v_.reshape(shard2d[k]),
                                    name=f"adamw_{k}")
            res[k] = [o.reshape(w_.shape) for o in outs]
            after = outs[0]
    off = 0
    for k, w_, _, _, _ in small:
        res[k] = [o[0, off:off + w_.size].reshape(w_.shape) for o in (sg, sd, sm, sv)]
        off += w_.size
    order = ["w_ada", "b_ada", "g_ffn1", "w_ffn1_gu", "w_ffn1_down", "g_mix", "w_in", "pool_mix", "pool_scale",
             "w_pool_up", "q_gain", "k_gain", "sinks", "rel_bias", "w_attn_up", "w_o", "g_ffn2", "w_ffn2_gu", "w_ffn2_down"]
    return (loss, dx0[None], *[res[k][0] for k in order], *[res[k][1] for k in order],
            *[res[k][2] for k in order], *[res[k][3] for k in order])
```

```python
import numpy as np
import jax
import jax.numpy as jnp
from jax import lax
from jax.experimental import pallas as pl
from jax.experimental.pallas import tpu as pltpu

F32, BF16 = jnp.float32, jnp.bfloat16
MESH_ID = pl.DeviceIdType.MESH

N_DEV = 8
D_MODEL = 2048
N_MOD = 9
POOL_WINDOWS = (2, 4, 8, 16)
POOL_GROUPS = 4
POOL_GROUP_W = D_MODEL // 8
POOL_W = POOL_GROUPS * POOL_GROUP_W
POOL_HALO = 16
HEAD_DIM = 64
N_HEADS = 16
N_KV = 2
GQA = N_HEADS // N_KV
BLK = 128
NUM_BUCKETS = 32
MAX_EXACT = 16
REL_MAX_DIST = 128
EPS = 1e-6
NEG_INF = -1e30
ATT_W = N_HEADS * HEAD_DIM
KV_W = N_KV * HEAD_DIM
IN_W = POOL_W + ATT_W + 2 * KV_W + 2 * D_MODEL
GATE_TILE = 512
W_IN_PARTS = dict(u=(0, POOL_W), q=(POOL_W, ATT_W), k=(POOL_W + ATT_W, KV_W), v=(POOL_W + ATT_W + KV_W, KV_W),
                  ga=(POOL_W + ATT_W + 2 * KV_W, D_MODEL), gb=(POOL_W + ATT_W + 2 * KV_W + D_MODEL, D_MODEL))
Z_PIECES = [(W_IN_PARTS[p][0] + j * GATE_TILE, GATE_TILE) for j in range(D_MODEL // GATE_TILE) for p in ("ga", "gb")]
Z_PIECES += [W_IN_PARTS[p] for p in ("u", "q", "k", "v")]
COL_U, COL_Q, COL_K, COL_V = 2 * D_MODEL, 2 * D_MODEL + POOL_W, 2 * D_MODEL + POOL_W + ATT_W, 2 * D_MODEL + POOL_W + ATT_W + KV_W
LANE = 128

ADAM_LR = 0.001
ADAM_B1 = 0.9
ADAM_B2 = 0.999
ADAM_EPS = 1e-08
ADAM_WD = 0.01
ADAM_STEP = 10

NN = ((1,), (0,))
NT = ((1,), (1,))
TN = ((0,), (0,))


def _dot(a, b, dims, precision=None):
    return lax.dot_general(a, b, (dims, ((), ())), preferred_element_type=F32, precision=precision)


def _tile(n, pref, unit):
    t = (min(pref, n) // unit) * unit
    while t >= unit:
        if n % t == 0:
            return t
        t -= unit
    return n


def _params(*sem):
    return pltpu.CompilerParams(dimension_semantics=sem)


def _sigmoid(x):
    return 1.0 / (1.0 + jnp.exp(-x))


def _mesh_pos():
    return lax.axis_index("x"), lax.axis_index("y"), lax.axis_index("c")


def _slot(p):
    return 4 * p[0] + 2 * p[1] + p[2]


def all_gather_small(x_shard, name):
    m_per, n = x_shard.shape

    def body(x_ref, out_ref, token, send_sems, recv_sems, local_sem):
        x, y, c = _mesh_pos()
        me, sibling = (x, y, c), (x, y, 1 - c)
        chips = [(1 - x, y), (x, 1 - y), (1 - x, 1 - y)]
        token[...] = jnp.zeros_like(token)

        def rows(p):
            return out_ref.at[pl.ds(_slot(p) * m_per, m_per), :]

        def copy(k, block, to, src=None):
            return pltpu.make_async_remote_copy(
                src_ref=rows(block) if src is None else src, dst_ref=rows(block),
                send_sem=send_sems.at[k], recv_sem=recv_sems.at[k], device_id=to, device_id_type=MESH_ID)

        mine = pltpu.make_async_copy(x_ref, rows(me), local_sem)
        mine.start()
        first = [copy(0, me, sibling, src=x_ref)]
        first += [copy(1 + j, me, (*chip, c), src=x_ref) for j, chip in enumerate(chips)]
        for cp in first:
            cp.start()
        passed = [copy(4 + j, (*chip, c), sibling) for j, chip in enumerate(chips)]
        for j, chip in enumerate(chips):
            copy(1 + j, (*chip, c), me).wait_recv()
            passed[j].start()
        copy(0, sibling, me).wait_recv()
        for j, chip in enumerate(chips):
            copy(4 + j, (*chip, 1 - c), me).wait_recv()
        for cp in first + passed:
            cp.wait_send()
        mine.wait()

    return pl.pallas_call(
        body, name=name,
        out_shape=[jax.ShapeDtypeStruct((N_DEV * m_per, n), x_shard.dtype), jax.ShapeDtypeStruct((8, LANE), F32)],
        in_specs=[pl.BlockSpec(memory_space=pltpu.VMEM)],
        out_specs=[pl.BlockSpec(memory_space=pltpu.VMEM)] * 2,
        scratch_shapes=[pltpu.SemaphoreType.DMA((7,)), pltpu.SemaphoreType.DMA((7,)), pltpu.SemaphoreType.DMA],
    )(x_shard)


def all_gather_hbm(shard, name):
    rows = shard.shape[0]
    half = rows // 2
    assert rows == 2 * half and half % 16 == 0, shard.shape

    def body(in_ref, out_ref, token, send_sems, recv_sems, local_sem):
        x, y, c = _mesh_pos()
        me, sibling = (x, y, c), (x, y, 1 - c)
        xn, yn, dg = (1 - x, y), (x, 1 - y), (1 - x, 1 - y)
        top, bottom = pl.ds(0, half), pl.ds(half, half)
        token[...] = jnp.zeros_like(token)

        def copy(k, block, to, part=None, src=None):
            dst = out_ref.at[_slot(block)] if part is None else out_ref.at[_slot(block), part]
            return pltpu.make_async_remote_copy(
                src_ref=dst if src is None else src, dst_ref=dst, send_sem=send_sems.at[k], recv_sem=recv_sems.at[k],
                device_id=to, device_id_type=MESH_ID)

        mine = pltpu.make_async_copy(in_ref, out_ref.at[_slot(me)], local_sem)
        mine.start()
        sends = [copy(0, me, sibling, src=in_ref), copy(1, me, (*xn, c), src=in_ref), copy(2, me, (*yn, c), src=in_ref)]
        for cp in sends:
            cp.start()
        copy(1, (*xn, c), me).wait_recv()
        sends += [copy(3, (*xn, c), sibling), copy(4, (*xn, c), (*yn, c), top)]
        sends[-2].start()
        sends[-1].start()
        copy(2, (*yn, c), me).wait_recv()
        sends += [copy(5, (*yn, c), sibling), copy(6, (*yn, c), (*xn, c), bottom)]
        sends[-2].start()
        sends[-1].start()
        copy(4, (*dg, c), me, top).wait_recv()
        copy(6, (*dg, c), me, bottom).wait_recv()
        sends.append(copy(7, (*dg, c), sibling))
        sends[-1].start()
        copy(0, sibling, me).wait_recv()
        copy(3, (*xn, 1 - c), me).wait_recv()
        copy(5, (*yn, 1 - c), me).wait_recv()
        copy(7, (*dg, 1 - c), me).wait_recv()
        for cp in sends:
            cp.wait_send()
        mine.wait()

    any_spec = pl.BlockSpec(memory_space=pl.ANY)
    return pl.pallas_call(
        body, name=name,
        out_shape=[jax.ShapeDtypeStruct((N_DEV,) + shard.shape, shard.dtype), jax.ShapeDtypeStruct((8, LANE), F32)],
        in_specs=[any_spec], out_specs=[any_spec, pl.BlockSpec(memory_space=pltpu.VMEM)],
        scratch_shapes=[pltpu.SemaphoreType.DMA((8,)), pltpu.SemaphoreType.DMA((8,)), pltpu.SemaphoreType.DMA],
    )(shard)


def _peer_list(x, y, c):
    return [((1 - x) if k & 4 else x, (1 - y) if k & 2 else y, (1 - c) if k & 1 else c) for k in range(1, N_DEV)]


def _exchange_copies(srcs, lands, send_sems, recv_sems, slotted, arriving):
    x, y, c = _mesh_pos()
    me = _slot((x, y, c))
    copies = []
    for a in range(len(srcs)):
        for k, peer in enumerate(_peer_list(x, y, c)):
            copies.append(pltpu.make_async_remote_copy(
                src_ref=srcs[a].at[_slot(peer)] if slotted else srcs[a],
                dst_ref=lands[a].at[_slot(peer) if arriving else me],
                send_sem=send_sems.at[7 * a + k], recv_sem=recv_sems.at[7 * a + k],
                device_id=peer, device_id_type=MESH_ID))
    return copies


def unwritten_hbm(shapes, dtype, name):
    def body(*refs):
        pass

    return pl.pallas_call(
        body, name=name, out_shape=[jax.ShapeDtypeStruct(s, dtype) for s in shapes],
        out_specs=[pl.BlockSpec(memory_space=pl.ANY)] * len(shapes),
    )()


HBM_SPEC = pl.BlockSpec(memory_space=pltpu.HBM)
SEM_SPEC = pl.BlockSpec(memory_space=pltpu.SEMAPHORE)
DATAFLOW = pltpu.SideEffectType.DATAFLOW_SIDE_EFFECTING


def exchange_start(srcs, lands, *, slotted, name):
    n = len(srcs)

    def body(*refs):
        ins = refs[:2 * n]
        send_sems, recv_sems = refs[2 * n], refs[2 * n + 1]
        token = refs[-1]
        for cp in _exchange_copies(ins[:n], ins[n:], send_sems, recv_sems, slotted, False):
            cp.start()
        token[...] = jnp.zeros_like(token)

    operands = [pltpu.with_memory_space_constraint(v, pltpu.HBM) for v in list(srcs) + list(lands)]
    out = pl.pallas_call(
        body, name=name,
        out_shape=(pltpu.SemaphoreType.DMA((7 * n,)), pltpu.SemaphoreType.DMA((7 * n,)),
                   *[pltpu.HBM(v.shape, v.dtype) for v in operands], jax.ShapeDtypeStruct((8, LANE), F32)),
        in_specs=[HBM_SPEC] * (2 * n),
        out_specs=(SEM_SPEC, SEM_SPEC, *[HBM_SPEC] * (2 * n), pl.BlockSpec(memory_space=pltpu.VMEM)),
        input_output_aliases={i: 2 + i for i in range(2 * n)},
        compiler_params=pltpu.CompilerParams(has_side_effects=DATAFLOW),
    )(*operands)
    return out[0], out[1], list(out[2:2 + n]), list(out[2 + n:2 + 2 * n]), out[-1]


def exchange_wait(handle, after, *, slotted, name):
    send_sems, recv_sems, srcs, lands, _ = handle
    n = len(srcs)

    def body(*refs):
        ins = refs[:2 * n]
        for cp in _exchange_copies(ins[:n], ins[n:], refs[2 * n], refs[2 * n + 1], slotted, True):
            cp.wait_send()
            cp.wait_recv()

    out = pl.pallas_call(
        body, name=name,
        out_shape=tuple(pltpu.HBM(v.shape, v.dtype) for v in srcs + lands),
        in_specs=[HBM_SPEC] * (2 * n) + [SEM_SPEC, SEM_SPEC, pl.BlockSpec(memory_space=pl.ANY)],
        out_specs=[HBM_SPEC] * (2 * n),
        input_output_aliases={i: i for i in range(2 * n)},
        compiler_params=pltpu.CompilerParams(has_side_effects=DATAFLOW),
    )(*srcs, *lands, send_sems, recv_sems, after)
    return list(out[n:])


VMEM_BLOCK_BUDGET = 46 * 2 ** 20
ROW_TILE, COL_TILE = 1024, 1408
ACC_BUDGET = 12 * 2 ** 20


def _mm_tiles(m, n, row_bytes, col_bytes, elem_bytes):
    tm, tn = _tile(m, ROW_TILE, 16), _tile(n, COL_TILE, LANE)
    while 2 * (tm * row_bytes + tn * col_bytes + tm * tn * elem_bytes) > VMEM_BLOCK_BUDGET:
        narrower = _tile(n, max(tn - LANE, LANE), LANE)
        if tn > 512 and narrower < tn:
            tn = narrower
        else:
            tm //= 2
    return tm, tn


def mm_nn(a, w, *, out_dtype, name):
    m, k = a.shape
    n = w.shape[1]
    tm, tn = _mm_tiles(m, n, 2 * k, 2 * k, jnp.dtype(out_dtype).itemsize)

    def body(a_ref, w_ref, o_ref):
        o_ref[...] = _dot(a_ref[...], w_ref[...], NN).astype(o_ref.dtype)

    return pl.pallas_call(
        body, name=name, grid=(n // tn, m // tm),
        in_specs=[pl.BlockSpec((tm, k), lambda j, i: (i, 0)), pl.BlockSpec((k, tn), lambda j, i: (0, j))],
        out_specs=pl.BlockSpec((tm, tn), lambda j, i: (i, j)),
        out_shape=jax.ShapeDtypeStruct((m, n), out_dtype),
        compiler_params=_params("parallel", "parallel"),
    )(a, w)


def mm_nn_residual(a, w, x_in, gate, coef, *, name):
    m, k = a.shape
    n = w.shape[1]
    tm, tn = _mm_tiles(m, n, 2 * k, 2 * k, 4 + 4 + 2)

    def body(a_ref, w_ref, x_ref, g_ref, o_ref, f_ref):
        f = _dot(a_ref[...], w_ref[...], NN)
        f_ref[...] = f.astype(BF16)
        o_ref[...] = x_ref[...] + (coef * g_ref[...]) * f

    return pl.pallas_call(
        body, name=name, grid=(n // tn, m // tm),
        in_specs=[pl.BlockSpec((tm, k), lambda j, i: (i, 0)), pl.BlockSpec((k, tn), lambda j, i: (0, j)),
                  pl.BlockSpec((tm, tn), lambda j, i: (i, j)), pl.BlockSpec((1, tn), lambda j, i: (0, j))],
        out_specs=[pl.BlockSpec((tm, tn), lambda j, i: (i, j)), pl.BlockSpec((tm, tn), lambda j, i: (i, j))],
        out_shape=[jax.ShapeDtypeStruct((m, n), F32), jax.ShapeDtypeStruct((m, n), BF16)],
        compiler_params=_params("parallel", "parallel"),
    )(a, w, x_in, gate)


def mm_nn_residual_loss(a, w, x_in, gate, coef, target, *, name):
    m, k = a.shape
    n = w.shape[1]
    tm, tn = _mm_tiles(m, n, 2 * k, 2 * k, 4 + 4 + 4 + 2)

    def body(a_ref, w_ref, x_ref, g_ref, t_ref, dy_ref, df_ref, dg_ref, l_ref):
        f = _dot(a_ref[...], w_ref[...], NN)
        err = x_ref[...] + (coef * g_ref[...]) * f - t_ref[...]
        dy = err * (1.0 / n)
        dy_ref[...] = dy
        df_ref[...] = ((coef * g_ref[...]) * dy).astype(BF16)
        dgate = coef * jnp.sum(dy * f, axis=0, keepdims=True)
        part = jnp.sum(jnp.sum(err * err, axis=0, keepdims=True), axis=1, keepdims=True) * (0.5 / n)

        @pl.when(pl.program_id(1) == 0)
        def _():
            dg_ref[...] = jnp.zeros_like(dg_ref)

        @pl.when((pl.program_id(0) == 0) & (pl.program_id(1) == 0))
        def _():
            l_ref[...] = jnp.zeros_like(l_ref)

        dg_ref[...] += dgate
        l_ref[...] += jnp.broadcast_to(part, l_ref.shape)

    blk = pl.BlockSpec((tm, tn), lambda j, i: (i, j))
    vec = pl.BlockSpec((1, tn), lambda j, i: (0, j))
    return pl.pallas_call(
        body, name=name, grid=(n // tn, m // tm),
        in_specs=[pl.BlockSpec((tm, k), lambda j, i: (i, 0)), pl.BlockSpec((k, tn), lambda j, i: (0, j)), blk, vec, blk],
        out_specs=[blk, blk, vec, pl.BlockSpec((1, LANE), lambda j, i: (0, 0))],
        out_shape=[jax.ShapeDtypeStruct((m, n), F32), jax.ShapeDtypeStruct((m, n), BF16),
                   jax.ShapeDtypeStruct((1, n), F32), jax.ShapeDtypeStruct((1, LANE), F32)],
        compiler_params=_params("arbitrary", "arbitrary"),
    )(a, w, x_in, gate, target)


TOKEN_SPEC = pl.BlockSpec((8, LANE), lambda *_: (0, 0))


def mm_nt(a, w, token, *, out_dtype, name):
    m, k = a.shape
    n = w.shape[0]
    tm, tn = _mm_tiles(m, n, 2 * k, 2 * k, jnp.dtype(out_dtype).itemsize)

    def body(a_ref, w_ref, token_ref, o_ref):
        o_ref[...] = _dot(a_ref[...], w_ref[...], NT).astype(o_ref.dtype)

    return pl.pallas_call(
        body, name=name, grid=(n // tn, m // tm),
        in_specs=[pl.BlockSpec((tm, k), lambda j, i: (i, 0)), pl.BlockSpec((tn, k), lambda j, i: (j, 0)), TOKEN_SPEC],
        out_specs=pl.BlockSpec((tm, tn), lambda j, i: (i, j)),
        out_shape=jax.ShapeDtypeStruct((m, n), out_dtype),
        compiler_params=_params("parallel", "parallel"),
    )(a, w, token)


def mm_nt_halves(a_lo, a_hi, w3, token, *, name):
    m = a_lo.shape[0]
    n_blk, n, tn = w3.shape
    half = n_blk // 2
    tm = _tile(m, ROW_TILE, 16)

    def body(lo_ref, hi_ref, w_ref, token_ref, o_ref):
        j = pl.program_id(1)

        @pl.when(j == 0)
        def _():
            o_ref[...] = jnp.zeros_like(o_ref)

        @pl.when(j < half)
        def _():
            o_ref[...] += _dot(lo_ref[...], w_ref[...], NT)

        @pl.when(j >= half)
        def _():
            o_ref[...] += _dot(hi_ref[...], w_ref[...], NT)

    return pl.pallas_call(
        body, name=name, grid=(m // tm, n_blk),
        in_specs=[pl.BlockSpec((tm, tn), lambda i, j: (i, jnp.minimum(j, half - 1))),
                  pl.BlockSpec((tm, tn), lambda i, j: (i, jnp.maximum(j - half, 0))),
                  pl.BlockSpec((None, n, tn), lambda i, j: (j, 0, 0)), TOKEN_SPEC],
        out_specs=pl.BlockSpec((tm, n), lambda i, j: (i, 0)),
        out_shape=jax.ShapeDtypeStruct((m, n), F32),
        compiler_params=_params("parallel", "arbitrary"),
    )(a_lo, a_hi, w3, token)


def mm_tn(a, dy, *, name):
    s, k = a.shape
    n = dy.shape[1]
    ts = _tile(s, ROW_TILE, 16)
    tk = k if k <= 2048 else _tile(k, COL_TILE, LANE)
    tn = _tile(n, ACC_BUDGET // (4 * tk), LANE)
    n_steps = s // ts

    def body(a_ref, dy_ref, o_ref, acc_ref):
        t = pl.program_id(2)

        @pl.when(t == 0)
        def _():
            acc_ref[...] = jnp.zeros_like(acc_ref)

        acc_ref[...] += _dot(a_ref[...], dy_ref[...], TN)

        @pl.when(t == n_steps - 1)
        def _():
            o_ref[...] = acc_ref[...].astype(BF16)

    return pl.pallas_call(
        body, name=name, grid=(k // tk, n // tn, n_steps),
        in_specs=[pl.BlockSpec((ts, tk), lambda kk, j, t: (t, kk)), pl.BlockSpec((ts, tn), lambda kk, j, t: (t, j))],
        out_specs=pl.BlockSpec((tk, tn), lambda kk, j, t: (kk, j)),
        out_shape=jax.ShapeDtypeStruct((k, n), BF16),
        scratch_shapes=[pltpu.VMEM((tk, tn), F32)],
        compiler_params=_params("parallel", "parallel", "arbitrary"),
    )(a, dy)


def mm_tn_halves(a, dy_lo, dy_hi, n_blocks, *, name):
    s, k = a.shape
    half = n_blocks // 2
    tn = dy_lo.shape[1] // half
    ts = _tile(s, ROW_TILE, 16)
    n_steps = s // ts

    def body(a_ref, lo_ref, hi_ref, o_ref, acc_ref):
        j, t = pl.program_id(0), pl.program_id(1)

        @pl.when(t == 0)
        def _():
            acc_ref[...] = jnp.zeros_like(acc_ref)

        @pl.when(j < half)
        def _():
            acc_ref[...] += _dot(a_ref[...], lo_ref[...], TN)

        @pl.when(j >= half)
        def _():
            acc_ref[...] += _dot(a_ref[...], hi_ref[...], TN)

        @pl.when(t == n_steps - 1)
        def _():
            o_ref[...] = acc_ref[...].astype(BF16)

    return pl.pallas_call(
        body, name=name, grid=(n_blocks, n_steps),
        in_specs=[pl.BlockSpec((ts, k), lambda j, t: (t, 0)),
                  pl.BlockSpec((ts, tn), lambda j, t: (jnp.where(j < half, t, n_steps - 1), jnp.minimum(j, half - 1))),
                  pl.BlockSpec((ts, tn), lambda j, t: (jnp.where(j < half, 0, t), jnp.maximum(j - half, 0)))],
        out_specs=pl.BlockSpec((None, k, tn), lambda j, t: (j, 0, 0)),
        out_shape=jax.ShapeDtypeStruct((n_blocks, k, tn), BF16),
        scratch_shapes=[pltpu.VMEM((k, tn), F32)],
        compiler_params=_params("parallel", "arbitrary"),
    )(a, dy_lo, dy_hi)


def ffn_dgu(df, wd, g, u, token, *, name):
    m, k = df.shape
    n = wd.shape[0]
    tm, tn = _mm_tiles(m, n, 2 * k, 2 * k, 4 * 2)

    def body(df_ref, w_ref, g_ref, u_ref, token_ref, dg_ref, du_ref):
        da = _dot(df_ref[...], w_ref[...], NT)
        gg, uu = g_ref[...].astype(F32), u_ref[...].astype(F32)
        sg = _sigmoid(gg)
        dg_ref[...] = (da * uu * (sg * (1 + gg * (1 - sg)))).astype(BF16)
        du_ref[...] = (da * (gg * sg)).astype(BF16)

    blk = pl.BlockSpec((tm, tn), lambda j, i: (i, j))
    out = jax.ShapeDtypeStruct((m, n), BF16)
    return pl.pallas_call(
        body, name=name, grid=(n // tn, m // tm),
        in_specs=[pl.BlockSpec((tm, k), lambda j, i: (i, 0)), pl.BlockSpec((tn, k), lambda j, i: (j, 0)), blk, blk, TOKEN_SPEC],
        out_specs=[blk, blk], out_shape=[out, out],
        compiler_params=_params("parallel", "parallel"),
    )(df, wd, g, u, token)


def ffn_up(h, wgu3, *, name, tm=512):
    s, k = h.shape
    n = wgu3.shape[2]
    half = wgu3.shape[0] // 2
    tm = _tile(s, tm, 16)

    def body(h_ref, wg_ref, wu_ref, g_ref, u_ref, a_ref, w_pair):
        @pl.when(pl.program_id(1) == 0)
        def _():
            w_pair[:, :n] = wg_ref[...]
            w_pair[:, n:] = wu_ref[...]

        gu = _dot(h_ref[...], w_pair[...], NN)
        g, u = gu[:, :n], gu[:, n:]
        g_ref[...] = g.astype(BF16)
        u_ref[...] = u.astype(BF16)
        a_ref[...] = (g * _sigmoid(g) * u).astype(BF16)

    out = jax.ShapeDtypeStruct((s, half * n), BF16)
    blk = pl.BlockSpec((tm, n), lambda j, i: (i, j))
    return pl.pallas_call(
        body, name=name, grid=(half, s // tm),
        in_specs=[pl.BlockSpec((tm, k), lambda j, i: (i, 0)),
                  pl.BlockSpec((None, k, n), lambda j, i: (j, 0, 0)),
                  pl.BlockSpec((None, k, n), lambda j, i: (j + half, 0, 0))],
        out_specs=[blk, blk, blk], out_shape=[out, out, out],
        scratch_shapes=[pltpu.VMEM((k, 2 * n), BF16)],
        compiler_params=_params("parallel", "arbitrary"),
    )(h, wgu3, wgu3)


def _row_spec(ts, width, col=0):
    return pl.BlockSpec((ts, width), lambda i: (i, col))


def _vec_spec(width):
    return pl.BlockSpec((1, width), lambda i: (0, 0))


def _accumulate(ref, value):
    i = pl.program_id(0)

    @pl.when(i == 0)
    def _():
        ref[...] = value

    @pl.when(i > 0)
    def _():
        ref[...] += value


def norm_modulate(x, g, shift, scale, token, *, name, ts=512):
    s, d = x.shape
    ts = _tile(s, ts, 16)

    def body(x_ref, g_ref, sh_ref, sc_ref, token_ref, h_ref):
        xx = x_ref[...]
        r = lax.rsqrt(jnp.mean(xx * xx, axis=-1, keepdims=True) + EPS)
        h_ref[...] = ((xx * r) * g_ref[...] * (1 + sc_ref[...]) + sh_ref[...]).astype(BF16)

    return pl.pallas_call(
        body, name=name, grid=(s // ts,),
        in_specs=[_row_spec(ts, d), _vec_spec(d), _vec_spec(d), _vec_spec(d), TOKEN_SPEC],
        out_specs=_row_spec(ts, d), out_shape=jax.ShapeDtypeStruct((s, d), BF16),
        compiler_params=_params("parallel"),
    )(x, g, shift, scale, token)


def norm_modulate_bwd(x, g, scale, dh, dx_out, below=None, *, name, ts=256):
    s, d = x.shape
    ts = _tile(s, ts, 16)
    coef = None if below is None else below[2]

    def body(*refs):
        x_ref, g_ref, sc_ref, dh_ref, dxo_ref = refs[:5]
        dx_ref, dsh_ref, dsc_ref, dg_ref = refs[-6:-2] if below else refs[-4:]
        xx, dh_ = x_ref[...], dh_ref[...].astype(F32)
        r = lax.rsqrt(jnp.mean(xx * xx, axis=-1, keepdims=True) + EPS)
        xh = xx * r
        dn = dh_ * (1 + sc_ref[...])
        dxh = dn * g_ref[...]
        dx = dxo_ref[...] + r * (dxh - xh * jnp.mean(dxh * xh, axis=-1, keepdims=True))
        dx_ref[...] = dx
        _accumulate(dsh_ref, jnp.sum(dh_, axis=0, keepdims=True))
        _accumulate(dsc_ref, jnp.sum(dh_ * (xh * g_ref[...]), axis=0, keepdims=True))
        _accumulate(dg_ref, jnp.sum(dn * xh, axis=0, keepdims=True))
        if below:
            f_ref, gate_ref, df_ref, dgate_ref = refs[5], refs[6], refs[-2], refs[-1]
            df_ref[...] = ((coef * gate_ref[...]) * dx).astype(BF16)
            _accumulate(dgate_ref, coef * jnp.sum(dx * f_ref[...].astype(F32), axis=0, keepdims=True))

    vec = jax.ShapeDtypeStruct((1, d), F32)
    extra_in, extra_out, extra_shape, extra_args = [], [], [], []
    if below:
        extra_in, extra_args = [_row_spec(ts, d), _vec_spec(d)], [below[0], below[1]]
        extra_out, extra_shape = [_row_spec(ts, d), _vec_spec(d)], [jax.ShapeDtypeStruct((s, d), BF16), vec]
    return pl.pallas_call(
        body, name=name, grid=(s // ts,),
        in_specs=[_row_spec(ts, d), _vec_spec(d), _vec_spec(d), _row_spec(ts, d), _row_spec(ts, d)] + extra_in,
        out_specs=[_row_spec(ts, d), _vec_spec(d), _vec_spec(d), _vec_spec(d)] + extra_out,
        out_shape=[jax.ShapeDtypeStruct((s, d), F32), vec, vec, vec] + extra_shape,
        compiler_params=_params("arbitrary"),
    )(x, g, scale, dh, dx_out, *extra_args)


def mixers_up_merge(p_act, attn, wpu, wau, z, *, name):
    m, k = p_act.shape
    n = wpu.shape[1]
    tm, tn = _tile(m, ROW_TILE, 16), GATE_TILE

    def body(p_ref, a_ref, wp_ref, wa_ref, gate_ref, yp_ref, ya_ref, o_ref):
        yp = _dot(p_ref[...], wp_ref[...], NN).astype(BF16)
        ya = _dot(a_ref[...], wa_ref[...], NN).astype(BF16)
        yp_ref[...] = yp
        ya_ref[...] = ya
        sa = _sigmoid(gate_ref[:, :tn].astype(F32))
        sb = _sigmoid(gate_ref[:, tn:].astype(F32))
        o_ref[...] = (sa * yp + sb * ya).astype(BF16)

    rows = pl.BlockSpec((tm, k), lambda j, i: (i, 0))
    cols = pl.BlockSpec((k, tn), lambda j, i: (0, j))
    blk = pl.BlockSpec((tm, tn), lambda j, i: (i, j))
    out = jax.ShapeDtypeStruct((m, n), BF16)
    return pl.pallas_call(
        body, name=name, grid=(n // tn, m // tm),
        in_specs=[rows, rows, cols, cols, pl.BlockSpec((tm, 2 * tn), lambda j, i: (i, j))],
        out_specs=[blk, blk, blk], out_shape=[out, out, out],
        compiler_params=_params("parallel", "parallel"),
    )(p_act, attn, wpu, wau, z)


def merge_bwd(d_o, wo, z, y_pool, y_attn, token, *, name):
    m, k = d_o.shape
    n = wo.shape[0]
    tm, tn = _tile(m, ROW_TILE, 16), GATE_TILE

    def body(do_ref, w_ref, gate_ref, yp_ref, ya_ref, token_ref, dyp_ref, dya_ref, dz_ref):
        dm = _dot(do_ref[...], w_ref[...], NT)
        sa = _sigmoid(gate_ref[:, :tn].astype(F32))
        sb = _sigmoid(gate_ref[:, tn:].astype(F32))
        dyp_ref[...] = (dm * sa).astype(BF16)
        dya_ref[...] = (dm * sb).astype(BF16)
        dz_ref[:, :tn] = (dm * yp_ref[...] * (sa * (1 - sa))).astype(BF16)
        dz_ref[:, tn:] = (dm * ya_ref[...] * (sb * (1 - sb))).astype(BF16)

    blk = pl.BlockSpec((tm, tn), lambda j, i: (i, j))
    pair = pl.BlockSpec((tm, 2 * tn), lambda j, i: (i, j))
    out = jax.ShapeDtypeStruct((m, n), BF16)
    return pl.pallas_call(
        body, name=name, grid=(n // tn, m // tm),
        in_specs=[pl.BlockSpec((tm, k), lambda j, i: (i, 0)), pl.BlockSpec((tn, k), lambda j, i: (j, 0)),
                  pair, blk, blk, TOKEN_SPEC],
        out_specs=[blk, blk, pair], out_shape=[out, out, jax.ShapeDtypeStruct(z.shape, BF16)],
        compiler_params=_params("parallel", "parallel"),
    )(d_o, wo, z, y_pool, y_attn, token)


def _window_counts(t0, rows):
    t1 = (t0 + 1 + lax.broadcasted_iota(jnp.int32, (rows, 1), 0)).astype(F32)
    return [jnp.minimum(t1, float(w)) for w in POOL_WINDOWS]


def pool_fwd(z, pool_mix, pool_scale, *, name, ts=256):
    s = z.shape[0]
    ts = _tile(s, ts, 16)
    per = ts // POOL_HALO

    def body(u_ref, halo_ref, pm_ref, ps_ref, pooled_ref, p_ref):
        i = pl.program_id(0)
        u = u_ref[...].astype(F32)
        halo = jnp.where(i > 0, halo_ref[...].astype(F32), 0.0)
        run = jnp.concatenate([halo, u], axis=0)
        sums, width = [], 1
        for w in POOL_WINDOWS:
            while width < w:
                run = run + pltpu.roll(run, width, 0)
                width *= 2
            sums.append(run[POOL_HALO:])
        counts = _window_counts(i * ts, ts)
        for gi in range(POOL_GROUPS):
            cols = slice(gi * POOL_GROUP_W, (gi + 1) * POOL_GROUP_W)
            pooled = (sums[gi][:, cols] / counts[gi] - u[:, cols]).astype(BF16)
            pooled_ref[:, cols] = pooled
            p_ref[:, cols] = (_dot(pooled, pm_ref[gi], NN) * ps_ref[:, cols]).astype(BF16)

    out = jax.ShapeDtypeStruct((s, POOL_W), BF16)
    return pl.pallas_call(
        body, name=name, grid=(s // ts,),
        in_specs=[_row_spec(ts, POOL_W, COL_U // POOL_W),
                  pl.BlockSpec((POOL_HALO, POOL_W), lambda i: (jnp.maximum(i * per - 1, 0), COL_U // POOL_W)),
                  pl.BlockSpec((POOL_GROUPS, POOL_GROUP_W, POOL_GROUP_W), lambda i: (0, 0, 0)),
                  _vec_spec(POOL_W)],
        out_specs=[_row_spec(ts, POOL_W)] * 2, out_shape=[out, out],
        compiler_params=_params("parallel"),
    )(z, z, pool_mix, pool_scale)


def pool_bwd(dp, pooled, pool_mix, pool_scale, dz, *, name, ts=256):
    s = dp.shape[0]
    ts = _tile(s, ts, 16)
    per = ts // POOL_HALO
    n_steps = s // ts
    rows = ts + POOL_HALO

    def body(dp_ref, halo_ref, pooled_ref, pm_ref, ps_ref, dz_in_ref, du_ref, dps_ref, dpm_ref):
        i = pl.program_id(0)
        dp_main = dp_ref[...]
        halo = jnp.where(i < n_steps - 1, halo_ref[...], 0.0)
        dmixed = jnp.concatenate([dp_main, halo], axis=0) * ps_ref[...]
        counts = _window_counts(i * ts, rows)
        dps_parts = []
        for gi, w in enumerate(POOL_WINDOWS):
            cols = slice(gi * POOL_GROUP_W, (gi + 1) * POOL_GROUP_W)
            dmx = dmixed[:, cols].astype(BF16)
            pooled = pooled_ref[:, cols]
            mixed = _dot(pooled, pm_ref[gi], NN)
            dps_parts.append(jnp.sum(dp_main[:, cols] * mixed, axis=0, keepdims=True))
            dpm_g = _dot(pooled, dmx[:ts], TN)

            @pl.when(i == 0)
            def _():
                dpm_ref[gi] = dpm_g

            @pl.when(i > 0)
            def _():
                dpm_ref[gi] += dpm_g

            dpooled = _dot(dmx, pm_ref[gi], NT)
            run, width = dpooled / counts[gi], 1
            while width < w:
                run = run + pltpu.roll(run, rows - width, 0)
                width *= 2
            du_ref[:, cols] = (run[:ts] - dpooled[:ts]).astype(BF16)
        _accumulate(dps_ref, jnp.concatenate(dps_parts, axis=1))

    return pl.pallas_call(
        body, name=name, grid=(n_steps,),
        in_specs=[_row_spec(ts, POOL_W),
                  pl.BlockSpec((POOL_HALO, POOL_W), lambda i: (jnp.minimum((i + 1) * per, s // POOL_HALO - 1), 0)),
                  _row_spec(ts, POOL_W),
                  pl.BlockSpec((POOL_GROUPS, POOL_GROUP_W, POOL_GROUP_W), lambda i: (0, 0, 0)),
                  _vec_spec(POOL_W), pl.BlockSpec(memory_space=pl.ANY)],
        out_specs=[_row_spec(ts, POOL_W, COL_U // POOL_W), _vec_spec(POOL_W),
                   pl.BlockSpec((POOL_GROUPS, POOL_GROUP_W, POOL_GROUP_W), lambda i: (0, 0, 0))],
        out_shape=[jax.ShapeDtypeStruct(dz.shape, BF16), jax.ShapeDtypeStruct((1, POOL_W), F32),
                   jax.ShapeDtypeStruct((POOL_GROUPS, POOL_GROUP_W, POOL_GROUP_W), F32)],
        input_output_aliases={5: 0},
        compiler_params=_params("arbitrary"),
    )(dp, dp, pooled, pool_mix, pool_scale, dz)


def _bucket_one_hot():
    ql = np.arange(BLK)[:, None]
    j = np.arange(2 * BLK)[None, :]
    n = np.clip(BLK + ql - j, 0, None)
    nf = np.maximum(n, 1).astype(np.float32)
    large = MAX_EXACT + (np.log(nf / MAX_EXACT) / np.log(REL_MAX_DIST / MAX_EXACT)
                         * (NUM_BUCKETS - MAX_EXACT)).astype(np.int32)
    large = np.minimum(large, NUM_BUCKETS - 1)
    bucket = np.where(n < MAX_EXACT, n, large).astype(np.int32).reshape(-1)
    return (np.arange(NUM_BUCKETS)[:, None] == bucket[None, :]).astype(np.float32)


def bias_table(rel_bias_t, one_hot, *, name, tc=4096):
    n = one_hot.shape[1]

    def body(rb_ref, oh_ref, o_ref):
        o_ref[...] = _dot(rb_ref[...], oh_ref[...], NN, precision=lax.Precision.HIGHEST)

    return pl.pallas_call(
        body, name=name, grid=(n // tc,),
        in_specs=[pl.BlockSpec((N_HEADS, NUM_BUCKETS), lambda i: (0, 0)), pl.BlockSpec((NUM_BUCKETS, tc), lambda i: (0, i))],
        out_specs=pl.BlockSpec((N_HEADS, tc), lambda i: (0, i)),
        out_shape=jax.ShapeDtypeStruct((N_HEADS, n), F32),
        compiler_params=_params("parallel"),
    )(rel_bias_t, one_hot)


def bias_table_bwd(dbias, one_hot, *, name, tc=4096):
    n = one_hot.shape[1]

    def body(db_ref, oh_ref, o_ref):
        _accumulate(o_ref, _dot(db_ref[...], oh_ref[...], NT, precision=lax.Precision.HIGHEST))

    return pl.pallas_call(
        body, name=name, grid=(n // tc,),
        in_specs=[pl.BlockSpec((N_HEADS, tc), lambda i: (0, i)), pl.BlockSpec((NUM_BUCKETS, tc), lambda i: (0, i))],
        out_specs=pl.BlockSpec((N_HEADS, NUM_BUCKETS), lambda i: (0, 0)),
        out_shape=jax.ShapeDtypeStruct((N_HEADS, NUM_BUCKETS), F32),
        compiler_params=_params("arbitrary"),
    )(dbias, one_hot)


def _lane_half(shape):
    return lax.broadcasted_iota(jnp.int32, shape, len(shape) - 1) < HEAD_DIM


def _half_sums(v, first):
    s0 = jnp.sum(jnp.where(first, v, 0.0), axis=-1, keepdims=True)
    s1 = jnp.sum(jnp.where(first, 0.0, v), axis=-1, keepdims=True)
    return jnp.where(first, s0, s1)


BWD_STACK, FWD_STACK = N_HEADS, N_HEADS


def _band_mask(n, heads):
    ql = lax.broadcasted_iota(jnp.int32, (heads * BLK, 2 * BLK), 0) & (BLK - 1)
    j = lax.broadcasted_iota(jnp.int32, (heads * BLK, 2 * BLK), 1)
    return (j > ql) & (j <= ql + BLK) & ((j >= BLK) | (n > 0))


def _norm_keys(kband):
    first = _lane_half(kband.shape)
    r = lax.rsqrt(_half_sums(kband * kband, first) * (1.0 / HEAD_DIM) + EPS)
    return kband * r


def _kv_lanes(h0, heads):
    shape = (heads * BLK, LANE)
    head = h0 + lax.broadcasted_iota(jnp.int32, shape, 0) // BLK
    return _lane_half(shape) == (head < GQA)


def _stack_heads(ref, h0, heads, dtype=F32):
    parts = []
    for h in range(h0, h0 + heads):
        part = ref[:, (h // 2) * LANE:(h // 2 + 1) * LANE].astype(dtype)
        parts.append(pltpu.roll(part, HEAD_DIM, 1) if h % 2 != h // GQA else part)
    return jnp.where(_kv_lanes(h0, heads), jnp.concatenate(parts, axis=0), 0.0)


def _unstack_heads(ref, h0, heads, stacked):
    for i in range(heads // 2):
        pair = None
        for sub in range(2):
            part = stacked[(2 * i + sub) * BLK:(2 * i + sub + 1) * BLK]
            part = pltpu.roll(part, HEAD_DIM, 1) if sub != (h0 + 2 * i) // GQA else part
            pair = part if pair is None else pair + part
        col = (h0 // 2 + i) * LANE
        ref[:, col:col + LANE] = pair.astype(BF16)


def _stack_logits(q_ref, h0, heads, qg, kn, bias_ref, sink_ref, mask):
    qa = _stack_heads(q_ref, h0, heads)
    r = lax.rsqrt(jnp.sum(qa * qa, axis=-1, keepdims=True) * (1.0 / HEAD_DIM) + EPS)
    xh = qa * r
    qn = (xh * qg).astype(BF16)
    bias = bias_ref[h0:h0 + heads].reshape(heads * BLK, 2 * BLK)
    logits = _dot(qn, kn, NT) * (HEAD_DIM ** -0.5) + bias
    p, p_sink = _softmax_with_sink(jnp.where(mask, logits, NEG_INF), sink_ref[h0 * BLK:(h0 + heads) * BLK])
    return xh, r, qn, p, p_sink


def _softmax_with_sink(logits, sink):
    m = jnp.maximum(jnp.max(logits, axis=-1, keepdims=True), sink)
    e = jnp.exp(logits - m)
    es = jnp.exp(sink - m)
    den = jnp.sum(e, axis=-1, keepdims=True) + es
    return e / den, es / den


def _attn_specs(nb, last):
    cur = lambda n: jnp.minimum(n, last)
    prev = lambda n: jnp.minimum(jnp.maximum(n - 1, 0), last)
    return [pl.BlockSpec((BLK, ATT_W), lambda n: (cur(n), COL_Q // ATT_W)),
            pl.BlockSpec((BLK, KV_W), lambda n: (prev(n), COL_K // KV_W)),
            pl.BlockSpec((BLK, KV_W), lambda n: (cur(n), COL_K // KV_W)),
            pl.BlockSpec((BLK, KV_W), lambda n: (prev(n), COL_V // KV_W)),
            pl.BlockSpec((BLK, KV_W), lambda n: (cur(n), COL_V // KV_W))]


def attn_fwd(z, qg2, kg2, sinks, bias, *, name):
    s = z.shape[0]
    nb = s // BLK

    def body(q_ref, kp_ref, kc_ref, vp_ref, vc_ref, qg_ref, kg_ref, sink_ref, bias_ref, o_ref):
        mask = _band_mask(pl.program_id(0), FWD_STACK)
        kn = (_norm_keys(jnp.concatenate([kp_ref[...], kc_ref[...]], axis=0).astype(F32)) * kg_ref[...]).astype(BF16)
        vb = jnp.concatenate([vp_ref[...], vc_ref[...]], axis=0).astype(BF16)
        for h0 in range(0, N_HEADS, FWD_STACK):
            _, _, _, p, _ = _stack_logits(q_ref, h0, FWD_STACK, qg_ref[...], kn, bias_ref, sink_ref, mask)
            out = _dot(p.astype(BF16), vb, NN)
            _unstack_heads(o_ref, h0, FWD_STACK, jnp.where(_kv_lanes(h0, FWD_STACK), out, 0.0))

    return pl.pallas_call(
        body, name=name, grid=(nb,),
        in_specs=_attn_specs(nb, nb - 1) + [
            _vec_spec(LANE), _vec_spec(LANE), pl.BlockSpec((N_HEADS * BLK, 1), lambda n: (0, 0)),
            pl.BlockSpec((N_HEADS, BLK, 2 * BLK), lambda n: (0, 0, 0))],
        out_specs=pl.BlockSpec((BLK, ATT_W), lambda n: (n, 0)),
        out_shape=jax.ShapeDtypeStruct((s, ATT_W), BF16),
        compiler_params=_params("parallel"),
    )(z, z, z, z, z, qg2, kg2, sinks, bias)


def attn_bwd(z, d_out, qg2, kg2, sinks, bias, dz, *, name):
    s = z.shape[0]
    nb = s // BLK
    scale = HEAD_DIM ** -0.5

    def body(q_ref, kp_ref, kc_ref, vp_ref, vc_ref, do_ref, qg_ref, kg_ref, sink_ref, bias_ref, dz_in_ref,
             dq_ref, dk_ref, dv_ref, dqg_ref, dkg_ref, dsink_ref, dbias_ref, band_k, band_v, carry_k, carry_v, dsink_rows):
        n = pl.program_id(0)

        @pl.when(n == 0)
        def _():
            dqg_ref[...] = jnp.zeros_like(dqg_ref)
            dkg_ref[...] = jnp.zeros_like(dkg_ref)
            dbias_ref[...] = jnp.zeros_like(dbias_ref)
            carry_k[...] = jnp.zeros_like(carry_k)
            carry_v[...] = jnp.zeros_like(carry_v)
            dsink_rows[...] = jnp.zeros_like(dsink_rows)

        @pl.when(n == nb)
        def _():
            band_k[...] = jnp.zeros_like(band_k)
            band_v[...] = jnp.zeros_like(band_v)

        @pl.when(n < nb)
        def _():
            mask = _band_mask(n, BWD_STACK)
            kn = (_norm_keys(jnp.concatenate([kp_ref[...], kc_ref[...]], axis=0).astype(F32)) * kg_ref[...]).astype(BF16)
            vb = jnp.concatenate([vp_ref[...], vc_ref[...]], axis=0).astype(BF16)
            dkn = jnp.zeros((2 * BLK, KV_W), F32)
            dvb = jnp.zeros((2 * BLK, KV_W), F32)
            dqg = jnp.zeros((1, LANE), F32)
            for h0 in range(0, N_HEADS, BWD_STACK):
                rows = slice(h0 * BLK, (h0 + BWD_STACK) * BLK)
                xh, r, qn, p, p_sink = _stack_logits(q_ref, h0, BWD_STACK, qg_ref[...], kn, bias_ref, sink_ref, mask)
                do = _stack_heads(do_ref, h0, BWD_STACK).astype(BF16)
                dp = _dot(do, vb, NT)
                delta = jnp.sum(p * dp, axis=-1, keepdims=True)
                ds = p * (dp - delta)
                dsink_rows[rows] -= p_sink * delta
                dbias_ref[h0:h0 + BWD_STACK] += ds.reshape(BWD_STACK, BLK, 2 * BLK)
                ds16 = ds.astype(BF16)
                dqn = jnp.where(_kv_lanes(h0, BWD_STACK), _dot(ds16, kn, NN) * scale, 0.0)
                dkn = dkn + _dot(ds16, qn, TN) * scale
                dvb = dvb + _dot(p.astype(BF16), do, TN)
                dqg = dqg + jnp.sum(dqn * xh, axis=0, keepdims=True)
                dxh = dqn * qg_ref[...]
                _unstack_heads(dq_ref, h0, BWD_STACK,
                               r * (dxh - xh * (jnp.sum(dxh * xh, axis=-1, keepdims=True) * (1.0 / HEAD_DIM))))
            band_k[...] = dkn
            band_v[...] = dvb
            dqg_ref[...] += dqg

        dkn_prev = carry_k[...] + band_k[:BLK]
        dv_ref[...] = (carry_v[...] + band_v[:BLK]).astype(BF16)
        carry_k[...] = band_k[BLK:]
        carry_v[...] = band_v[BLK:]
        kp = kp_ref[...].astype(F32)
        first = _lane_half(kp.shape)
        r = lax.rsqrt(_half_sums(kp * kp, first) * (1.0 / HEAD_DIM) + EPS)
        xh = kp * r
        dkg_ref[...] += jnp.sum(dkn_prev * xh, axis=0, keepdims=True)
        dxh = dkn_prev * kg_ref[...]
        dk_ref[...] = (r * (dxh - xh * (_half_sums(dxh * xh, first) * (1.0 / HEAD_DIM)))).astype(BF16)

        @pl.when(n == nb)
        def _():
            dqg_ref[...] += pltpu.roll(dqg_ref[...], HEAD_DIM, 1)
            dkg_ref[...] += pltpu.roll(dkg_ref[...], HEAD_DIM, 1)
            lane16 = lax.broadcasted_iota(jnp.int32, (1, N_HEADS), 1)
            dsink = jnp.zeros((1, N_HEADS), F32)
            for h in range(N_HEADS):
                dsink = dsink + jnp.where(lane16 == h, jnp.sum(dsink_rows[h * BLK:(h + 1) * BLK], axis=0, keepdims=True), 0.0)
            dsink_ref[...] = dsink

    last = nb - 1
    cur = lambda n: jnp.minimum(n, last)
    back = lambda n: jnp.maximum(n - 1, 0)
    full3 = pl.BlockSpec((N_HEADS, BLK, 2 * BLK), lambda n: (0, 0, 0))
    return pl.pallas_call(
        body, name=name, grid=(nb + 1,),
        in_specs=_attn_specs(nb, last) + [
            pl.BlockSpec((BLK, ATT_W), lambda n: (cur(n), 0)),
            _vec_spec(LANE), _vec_spec(LANE), pl.BlockSpec((N_HEADS * BLK, 1), lambda n: (0, 0)), full3,
            pl.BlockSpec(memory_space=pl.ANY)],
        out_specs=[pl.BlockSpec((BLK, ATT_W), lambda n: (cur(n), COL_Q // ATT_W)),
                   pl.BlockSpec((BLK, KV_W), lambda n: (back(n), 0)),
                   pl.BlockSpec((BLK, KV_W), lambda n: (back(n), 0)),
                   _vec_spec(LANE), _vec_spec(LANE), _vec_spec(N_HEADS), full3],
        input_output_aliases={10: 0},
        out_shape=[jax.ShapeDtypeStruct(dz.shape, BF16), jax.ShapeDtypeStruct((s, KV_W), BF16),
                   jax.ShapeDtypeStruct((s, KV_W), BF16), jax.ShapeDtypeStruct((1, LANE), F32),
                   jax.ShapeDtypeStruct((1, LANE), F32), jax.ShapeDtypeStruct((1, N_HEADS), F32),
                   jax.ShapeDtypeStruct((N_HEADS, BLK, 2 * BLK), F32)],
        scratch_shapes=[pltpu.VMEM((2 * BLK, KV_W), F32), pltpu.VMEM((2 * BLK, KV_W), F32),
                        pltpu.VMEM((BLK, KV_W), F32), pltpu.VMEM((BLK, KV_W), F32), pltpu.VMEM((N_HEADS * BLK, 1), F32)],
        compiler_params=_params("arbitrary"),
    )(z, z, z, z, z, d_out, qg2, kg2, sinks, bias, dz)


def _adamw(w, g, m, v):
    m = ADAM_B1 * m + (1.0 - ADAM_B1) * g
    v = ADAM_B2 * v + (1.0 - ADAM_B2) * (g * g)
    m_hat = m / (1.0 - ADAM_B1 ** ADAM_STEP)
    v_hat = v / (1.0 - ADAM_B2 ** ADAM_STEP)
    delta = -ADAM_LR * (m_hat / (jnp.sqrt(v_hat) + ADAM_EPS) + ADAM_WD * w)
    return delta, m, v


def ada_fwd(c16, w, b, *, name, tn=768):
    k, n = w.shape
    tn = _tile(n, tn, LANE)

    def body(c_ref, w_ref, b_ref, o_ref):
        cc = c_ref[...]
        o_ref[...] = _dot((cc * _sigmoid(cc)).astype(BF16), w_ref[...].astype(BF16), NN) + b_ref[...]

    return pl.pallas_call(
        body, name=name, grid=(n // tn,),
        in_specs=[pl.BlockSpec((c16.shape[0], k), lambda j: (0, 0)), pl.BlockSpec((k, tn), lambda j: (0, j)),
                  pl.BlockSpec((1, tn), lambda j: (0, j))],
        out_specs=pl.BlockSpec((c16.shape[0], tn), lambda j: (0, j)),
        out_shape=jax.ShapeDtypeStruct((c16.shape[0], n), F32),
        compiler_params=_params("parallel"),
    )(c16, w, b)


def ada_bwd_adamw(c_t, dmod, w, m, v, *, name, tn=256):
    k, n = w.shape
    tn = _tile(n, tn, LANE)

    def body(c_ref, d_ref, w_ref, m_ref, v_ref, g_ref, dl_ref, mo_ref, vo_ref):
        cc = c_ref[...]
        g = _dot((cc * _sigmoid(cc)).astype(BF16), d_ref[...].astype(BF16), NN)
        g_ref[...] = g
        dl_ref[...], mo_ref[...], vo_ref[...] = _adamw(w_ref[...], g, m_ref[...], v_ref[...])

    blk = pl.BlockSpec((k, tn), lambda j: (0, j))
    out = jax.ShapeDtypeStruct((k, n), F32)
    return pl.pallas_call(
        body, name=name, grid=(n // tn,),
        in_specs=[pl.BlockSpec((k, LANE), lambda j: (0, 0)), pl.BlockSpec((LANE, tn), lambda j: (0, j)), blk, blk, blk],
        out_specs=[blk] * 4, out_shape=[out] * 4,
        compiler_params=_params("parallel"),
    )(c_t, dmod, w, m, v)


def adamw_from_parts(parts, w, m, v, *, name):
    r, c = w.shape
    tr = _tile(r, max(16, (256 * 1024) // c), 16)

    def body(p_ref, w_ref, m_ref, v_ref, g_ref, dl_ref, mo_ref, vo_ref):
        g = p_ref[0].astype(F32)
        for d in range(1, N_DEV):
            g = g + p_ref[d].astype(F32)
        g_ref[...] = g
        dl_ref[...], mo_ref[...], vo_ref[...] = _adamw(w_ref[...], g, m_ref[...], v_ref[...])

    blk = pl.BlockSpec((tr, c), lambda i: (i, 0))
    out = jax.ShapeDtypeStruct((r, c), F32)
    return pl.pallas_call(
        body, name=name, grid=(r // tr,),
        in_specs=[pl.BlockSpec((N_DEV, tr, c), lambda i: (0, i, 0)), blk, blk, blk],
        out_specs=[blk] * 4, out_shape=[out] * 4,
        compiler_params=_params("parallel"),
    )(parts, w, m, v)


def _ffn_fwd(x_in, g, shift, scale, gate, wgu3, get_wd, token, tag, target=None):
    h = norm_modulate(x_in, g, shift, scale, token, name=f"{tag}_norm")
    gg, uu, act = ffn_up(h, wgu3, name=f"{tag}_up")
    wd = get_wd(gg)
    if target is None:
        x_out, f = mm_nn_residual(act, wd, x_in, gate, 0.5, name=f"{tag}_down")
    else:
        x_out, f = mm_nn_residual_loss(act, wd, x_in, gate, 0.5, target, name=f"{tag}_down_loss"), None
    return x_out, (h, gg, uu, act, f), wd


def _ffn_bwd(dx_out, df, dgate, x_in, g, scale, wgu3, wd, saved, token, scatter, split, tag, below=None):
    h, gg, uu, act, _ = saved
    dwd = mm_tn(act, df, name=f"{tag}_dwd").reshape(N_DEV, -1, D_MODEL)
    if split:
        token = scatter([f"w_{tag}_down"], [dwd], f"scatter_{tag}_down")
    dgg, duu = ffn_dgu(df, wd, gg, uu, token, name=f"{tag}_dgu")
    dwgu = mm_tn_halves(h, dgg, duu, N_DEV, name=f"{tag}_dwgu")
    if split:
        token = scatter([f"w_{tag}_gu"], [dwgu], f"scatter_{tag}_gu")
    else:
        token = scatter([f"w_{tag}_gu", f"w_{tag}_down"], [dwgu, dwd], f"scatter_{tag}")
    dh = mm_nt_halves(dgg, duu, wgu3, token, name=f"{tag}_dh")
    dx_in, dshift, dscale, dg, *rest = norm_modulate_bwd(x_in, g, scale, dh, dx_out, below, name=f"{tag}_norm_bwd")
    return (dx_in, (dshift, dscale, dgate), dg, *rest)


def kernel(x, c, w_ada, b_ada, g_ffn1, w_ffn1_gu, w_ffn1_down, g_mix, w_in, pool_mix, pool_scale, w_pool_up, q_gain, k_gain, sinks, rel_bias, w_attn_up, w_o, g_ffn2, w_ffn2_gu, w_ffn2_down, loss_target, m_w_ada, m_b_ada, m_g_ffn1, m_w_ffn1_gu, m_w_ffn1_down, m_g_mix, m_w_in, m_pool_mix, m_pool_scale, m_w_pool_up, m_q_gain, m_k_gain, m_sinks, m_rel_bias, m_w_attn_up, m_w_o, m_g_ffn2, m_w_ffn2_gu, m_w_ffn2_down, v_w_ada, v_b_ada, v_g_ffn1, v_w_ffn1_gu, v_w_ffn1_down, v_g_mix, v_w_in, v_pool_mix, v_pool_scale, v_w_pool_up, v_q_gain, v_k_gain, v_sinks, v_rel_bias, v_w_attn_up, v_w_o, v_g_ffn2, v_w_ffn2_gu, v_w_ffn2_down):
    me = _slot(_mesh_pos())
    x0, target = x[0], loss_target[0]
    n_ada = w_ada.shape[2]
    pm_rows = pool_mix.shape[2]

    big = dict(w_ffn1_gu=(w_ffn1_gu, m_w_ffn1_gu, v_w_ffn1_gu), w_ffn1_down=(w_ffn1_down, m_w_ffn1_down, v_w_ffn1_down),
               w_in=(w_in, m_w_in, v_w_in), pool_mix=(pool_mix, m_pool_mix, v_pool_mix),
               w_pool_up=(w_pool_up, m_w_pool_up, v_w_pool_up), w_attn_up=(w_attn_up, m_w_attn_up, v_w_attn_up),
               w_o=(w_o, m_w_o, v_w_o), w_ffn2_gu=(w_ffn2_gu, m_w_ffn2_gu, v_w_ffn2_gu),
               w_ffn2_down=(w_ffn2_down, m_w_ffn2_down, v_w_ffn2_down))
    shard2d = {k: (POOL_GROUPS * pm_rows, POOL_GROUP_W) if k == "pool_mix" else t[0].shape[1:] for k, t in big.items()}
    mix_keys = ["w_in", "pool_mix", "w_pool_up", "w_attn_up", "w_o"]
    ffn2_keys = ["w_ffn2_gu", "w_ffn2_down"]

    def shard_bf16(k, token=None):
        w = big[k][0].reshape(shard2d[k])
        return (w if token is None else w + token[0, 0]).astype(BF16)

    def landing_zones(blocks, tag):
        zones = unwritten_hbm([(N_DEV,) + b.shape for b in blocks], BF16, f"{tag}_zones")
        return [lax.dynamic_update_slice(z, b[None], (me, 0, 0)) for z, b in zip(zones, blocks)]

    def start_gather(keys, token, tag):
        shards = [shard_bf16(k, token) for k in keys]
        return exchange_start(shards, landing_zones(shards, tag), slotted=False, name=f"{tag}_start")

    pending = []

    def scatter(keys, grads, tag):
        lands = landing_zones([lax.dynamic_index_in_dim(g, me, 0, keepdims=False) for g in grads], tag)
        handle = exchange_start(grads, lands, slotted=True, name=f"{tag}_start")
        pending.append((keys, handle, tag))
        return handle[4]

    c_all, _ = all_gather_small(c.reshape(D_MODEL // LANE, LANE), "gather_c")
    c_all = c_all.reshape(N_DEV, D_MODEL)
    c16 = jnp.pad(c_all, ((0, 16 - N_DEV), (0, 0)))
    b_mine = lax.dynamic_slice(b_ada, (0, me * n_ada), (1, n_ada))
    mod_cols = ada_fwd(c16, w_ada[0], b_mine, name="ada_fwd")[:N_DEV]
    mod_all, token = all_gather_small(mod_cols.reshape(-1, LANE), "gather_mod")
    mod = lax.dynamic_index_in_dim(mod_all.reshape(N_DEV, N_DEV, n_ada), me, axis=1, keepdims=False)
    mod = mod.reshape(N_MOD, 1, D_MODEL)

    wgu1, token = all_gather_hbm(shard_bf16("w_ffn1_gu", token), "gather_ffn1_gu")
    gather_wd1 = start_gather(["w_ffn1_down"], token, "gather_ffn1_down")
    gather_mix = start_gather(mix_keys, gather_wd1[4], "gather_mix")
    gather_ffn2 = start_gather(ffn2_keys, gather_mix[4], "gather_ffn2")
    token = gather_ffn2[4]

    def get_wd1(after):
        return exchange_wait(gather_wd1, after, slotted=False, name="gather_ffn1_down_wait")[0].reshape(-1, D_MODEL)

    x1, saved1, wd1 = _ffn_fwd(x0, g_ffn1, mod[0], mod[1], mod[2], wgu1, get_wd1, token, "ffn1")
    gathered = dict(zip(mix_keys, exchange_wait(gather_mix, x1, slotted=False, name="gather_mix_wait")))
    def columns_out(blocks):
        return jnp.transpose(blocks, (1, 0, 2)).reshape(blocks.shape[1], -1)

    def columns_in(full):
        return jnp.transpose(full.reshape(full.shape[0], N_DEV, -1), (1, 0, 2))

    w_in_full = columns_out(gathered["w_in"])
    w_in_z = jnp.concatenate([w_in_full[:, s:s + w] for s, w in Z_PIECES], axis=1)
    pm_full = jnp.transpose(gathered["pool_mix"].reshape(N_DEV, POOL_GROUPS, pm_rows, POOL_GROUP_W),
                            (1, 0, 2, 3)).reshape(POOL_GROUPS, POOL_GROUP_W, POOL_GROUP_W)
    wpu, wau = columns_out(gathered["w_pool_up"]), columns_out(gathered["w_attn_up"])
    wo_full = gathered["w_o"].reshape(D_MODEL, D_MODEL)
    h2 = norm_modulate(x1, g_mix, mod[3], mod[4], token, name="mix_norm")
    z = mm_nn(h2, w_in_z, out_dtype=BF16, name="mix_in")
    pooled, p_act = pool_fwd(z, pm_full, pool_scale, name="pool_fwd")
    one_hot = jnp.asarray(_bucket_one_hot())
    bias = bias_table(rel_bias.T, one_hot, name="bias_table").reshape(N_HEADS, BLK, 2 * BLK)
    qg2, kg2 = jnp.tile(q_gain, (1, 2)), jnp.tile(k_gain, (1, 2))
    sink_rows = jnp.repeat(sinks[0], BLK).reshape(N_HEADS * BLK, 1)
    attn = attn_fwd(z, qg2, kg2, sink_rows, bias, name="attn_fwd")
    y_pool, y_attn, merged = mixers_up_merge(p_act, attn, wpu, wau, z, name="mixers_up")
    x2, o_act = mm_nn_residual(merged, wo_full, x1, mod[5], 1.0, name="mix_out")
    wgu2, wd2 = exchange_wait(gather_ffn2, x2, slotted=False, name="gather_ffn2_wait")
    wd2 = wd2.reshape(-1, D_MODEL)
    (dy, df2, dgate3, loss_row), saved2, _ = _ffn_fwd(x2, g_ffn2, mod[6], mod[7], mod[8], wgu2, lambda after: wd2, token,
                                                     "ffn2", target)

    dx2, dmod3, dg_ffn2, d_o, dgate2 = _ffn_bwd(dy, df2, dgate3, x2, g_ffn2, mod[7], wgu2, wd2, saved2, token, scatter, False,
                                               "ffn2", below=(o_act, mod[5], 1.0))
    dwo = mm_tn(merged, d_o, name="mix_dwo").reshape(N_DEV, -1, D_MODEL)
    dyp, dya, dz = merge_bwd(d_o, wo_full, z, y_pool, y_attn, token, name="merge_bwd")
    dwpu = mm_tn(p_act, dyp, name="pool_dwup")
    dp_act = mm_nt(dyp, wpu, token, out_dtype=BF16, name="pool_dp")
    dz, dpool_scale, dpm = pool_bwd(dp_act, pooled, pm_full, pool_scale, dz, name="pool_bwd")
    dwau = mm_tn(attn, dya, name="attn_dwup")
    dattn = mm_nt(dya, wau, token, out_dtype=BF16, name="attn_dout")
    dz, dk, dv, dqg, dkg, dsinks, dbias = attn_bwd(z, dattn, qg2, kg2, sink_rows, bias, dz, name="attn_bwd")
    dz = lax.dynamic_update_slice(dz, jnp.concatenate([dk, dv], axis=1), (0, COL_K))
    drel = bias_table_bwd(dbias.reshape(N_HEADS, -1), one_hot, name="bias_table_bwd").T
    dwin_z = mm_tn(h2, dz, name="mix_dwin")
    z_start = np.cumsum([0] + [w for _, w in Z_PIECES[:-1]])
    in_w_in_order = sorted(zip(Z_PIECES, z_start))
    dwin = jnp.concatenate([dwin_z[:, int(at):int(at) + w] for (_, w), at in in_w_in_order], axis=1)
    mix_grads = dict(w_in=columns_in(dwin),
                     pool_mix=jnp.transpose(dpm.astype(BF16).reshape(POOL_GROUPS, N_DEV, pm_rows, POOL_GROUP_W),
                                            (1, 0, 2, 3)).reshape(N_DEV, POOL_GROUPS * pm_rows, POOL_GROUP_W),
                     w_pool_up=columns_in(dwpu), w_attn_up=columns_in(dwau), w_o=dwo)
    token = scatter(mix_keys, [mix_grads[k] for k in mix_keys], "scatter_mix")
    dh2 = mm_nt(dz, w_in_z, token, out_dtype=BF16, name="mix_dh")
    dx1, dsh2, dsc2, dg_mix, df1, dgate1 = norm_modulate_bwd(x1, g_mix, mod[4], dh2, dx2, (saved1[4], mod[2], 0.5),
                                                           name="mix_norm_bwd")
    dx0, dmod1, dg_ffn1 = _ffn_bwd(dx1, df1, dgate1, x0, g_ffn1, mod[1], wgu1, wd1, saved1, token, scatter, True, "ffn1")

    small = [("b_ada", b_ada, m_b_ada, v_b_ada, jnp.concatenate(list(dmod1 + (dsh2, dsc2, dgate2) + dmod3), axis=1)),
             ("g_ffn1", g_ffn1, m_g_ffn1, v_g_ffn1, dg_ffn1), ("g_mix", g_mix, m_g_mix, v_g_mix, dg_mix),
             ("g_ffn2", g_ffn2, m_g_ffn2, v_g_ffn2, dg_ffn2),
             ("pool_scale", pool_scale, m_pool_scale, v_pool_scale, dpool_scale),
             ("q_gain", q_gain, m_q_gain, v_q_gain, dqg[:, :HEAD_DIM]), ("k_gain", k_gain, m_k_gain, v_k_gain, dkg[:, :HEAD_DIM]),
             ("sinks", sinks, m_sinks, v_sinks, dsinks), ("rel_bias", rel_bias, m_rel_bias, v_rel_bias, drel)]
    n_small = sum(t[1].size for t in small)
    width = n_small + 1 + (-(n_small + 1) % (8 * LANE))

    def flat(arrs):
        row = jnp.concatenate([a.reshape(1, -1) for a in arrs], axis=1)
        return jnp.pad(row, ((0, 0), (0, width - row.shape[1])))

    small_parts, _ = all_gather_small(flat([t[4] for t in small] + [loss_row[:, :1]]).reshape(-1, LANE), "gather_small_grads")
    small_parts = small_parts.reshape(N_DEV, 1, width)
    sg, sd, sm, sv = adamw_from_parts(small_parts, flat([t[1] for t in small]), flat([t[2] for t in small]),
                                      flat([t[3] for t in small]), name="adamw_small")
    loss = sg[0, n_small]

    dmod_all = small_parts[:, 0, :N_MOD * D_MODEL]
    dmod_mine = lax.dynamic_slice(dmod_all, (0, me * n_ada), (N_DEV, n_ada))
    c_t = jnp.pad(c_all.T, ((0, 0), (0, LANE - N_DEV)))
    ada_out = ada_bwd_adamw(c_t, jnp.pad(dmod_mine, ((0, LANE - N_DEV), (0, 0))), w_ada[0], m_w_ada[0], v_w_ada[0],
                            name="ada_bwd_adamw")

    res = {"w_ada": [o[None] for o in ada_out]}
    after = ada_out[0]
    for keys, handle, tag in pending:
        parts = exchange_wait(handle, after, slotted=True, name=f"{tag}_wait")
        for k, part in zip(keys, parts):
            w_, m_, v_ = big[k]
            outs = adamw_from_parts(part, w_.reshape(shard2d[k]), m_.reshape(shard2d[k]), v_.reshape(shard2d[k]),
                                    name=f"adamw_{k}")
            res[k] = [o.reshape(w_.shape) for o in outs]
            after = outs[0]
    off = 0
    for k, w_, _, _, _ in small:
        res[k] = [o[0, off:off + w_.size].reshape(w_.shape) for o in (sg, sd, sm, sv)]
        off += w_.size
    order = ["w_ada", "b_ada", "g_ffn1", "w_ffn1_gu", "w_ffn1_down", "g_mix", "w_in", "pool_mix", "pool_scale",
             "w_pool_up", "q_gain", "k_gain", "sinks", "rel_bias", "w_attn_up", "w_o", "g_ffn2", "w_ffn2_gu", "w_ffn2_down"]
    return (loss, dx0[None], *[res[k][0] for k in order], *[res[k][1] for k in order],
            *[res[k][2] for k in order], *[res[k][3] for k in order])
```

```python
import numpy as np
import jax
import jax.numpy as jnp
from jax import lax
from jax.experimental import pallas as pl
from jax.experimental.pallas import tpu as pltpu

F32, BF16 = jnp.float32, jnp.bfloat16
MESH_ID = pl.DeviceIdType.MESH

N_DEV = 8
D_MODEL = 2048
N_MOD = 9
POOL_WINDOWS = (2, 4, 8, 16)
POOL_GROUPS = 4
POOL_GROUP_W = D_MODEL // 8
POOL_W = POOL_GROUPS * POOL_GROUP_W
POOL_HALO = 16
HEAD_DIM = 64
N_HEADS = 16
N_KV = 2
GQA = N_HEADS // N_KV
BLK = 128
NUM_BUCKETS = 32
MAX_EXACT = 16
REL_MAX_DIST = 128
EPS = 1e-6
NEG_INF = -1e30
ATT_W = N_HEADS * HEAD_DIM
KV_W = N_KV * HEAD_DIM
IN_W = POOL_W + ATT_W + 2 * KV_W + 2 * D_MODEL
GATE_TILE = 512
W_IN_PARTS = dict(u=(0, POOL_W), q=(POOL_W, ATT_W), k=(POOL_W + ATT_W, KV_W), v=(POOL_W + ATT_W + KV_W, KV_W),
                  ga=(POOL_W + ATT_W + 2 * KV_W, D_MODEL), gb=(POOL_W + ATT_W + 2 * KV_W + D_MODEL, D_MODEL))
Z_PIECES = [(W_IN_PARTS[p][0] + j * GATE_TILE, GATE_TILE) for j in range(D_MODEL // GATE_TILE) for p in ("ga", "gb")]
Z_PIECES += [W_IN_PARTS[p] for p in ("u", "q", "k", "v")]
COL_U, COL_Q, COL_K, COL_V = 2 * D_MODEL, 2 * D_MODEL + POOL_W, 2 * D_MODEL + POOL_W + ATT_W, 2 * D_MODEL + POOL_W + ATT_W + KV_W
LANE = 128

ADAM_LR = 0.001
ADAM_B1 = 0.9
ADAM_B2 = 0.999
ADAM_EPS = 1e-08
ADAM_WD = 0.01
ADAM_STEP = 10

NN = ((1,), (0,))
NT = ((1,), (1,))
TN = ((0,), (0,))


def _dot(a, b, dims, precision=None):
    return lax.dot_general(a, b, (dims, ((), ())), preferred_element_type=F32, precision=precision)


def _tile(n, pref, unit):
    t = (min(pref, n) // unit) * unit
    while t >= unit:
        if n % t == 0:
            return t
        t -= unit
    return n


def _params(*sem):
    return pltpu.CompilerParams(dimension_semantics=sem)


def _sigmoid(x):
    return 1.0 / (1.0 + jnp.exp(-x))


def _mesh_pos():
    return lax.axis_index("x"), lax.axis_index("y"), lax.axis_index("c")


def _slot(p):
    return 4 * p[0] + 2 * p[1] + p[2]


def all_gather_small(x_shard, name):
    m_per, n = x_shard.shape

    def body(x_ref, out_ref, token, send_sems, recv_sems, local_sem):
        x, y, c = _mesh_pos()
        me, sibling = (x, y, c), (x, y, 1 - c)
        chips = [(1 - x, y), (x, 1 - y), (1 - x, 1 - y)]
        token[...] = jnp.zeros_like(token)

        def rows(p):
            return out_ref.at[pl.ds(_slot(p) * m_per, m_per), :]

        def copy(k, block, to, src=None):
            return pltpu.make_async_remote_copy(
                src_ref=rows(block) if src is None else src, dst_ref=rows(block),
                send_sem=send_sems.at[k], recv_sem=recv_sems.at[k], device_id=to, device_id_type=MESH_ID)

        mine = pltpu.make_async_copy(x_ref, rows(me), local_sem)
        mine.start()
        first = [copy(0, me, sibling, src=x_ref)]
        first += [copy(1 + j, me, (*chip, c), src=x_ref) for j, chip in enumerate(chips)]
        for cp in first:
            cp.start()
        passed = [copy(4 + j, (*chip, c), sibling) for j, chip in enumerate(chips)]
        for j, chip in enumerate(chips):
            copy(1 + j, (*chip, c), me).wait_recv()
            passed[j].start()
        copy(0, sibling, me).wait_recv()
        for j, chip in enumerate(chips):
            copy(4 + j, (*chip, 1 - c), me).wait_recv()
        for cp in first + passed:
            cp.wait_send()
        mine.wait()

    return pl.pallas_call(
        body, name=name,
        out_shape=[jax.ShapeDtypeStruct((N_DEV * m_per, n), x_shard.dtype), jax.ShapeDtypeStruct((8, LANE), F32)],
        in_specs=[pl.BlockSpec(memory_space=pltpu.VMEM)],
        out_specs=[pl.BlockSpec(memory_space=pltpu.VMEM)] * 2,
        scratch_shapes=[pltpu.SemaphoreType.DMA((7,)), pltpu.SemaphoreType.DMA((7,)), pltpu.SemaphoreType.DMA],
    )(x_shard)


def all_gather_hbm(shard, name):
    rows = shard.shape[0]
    half = rows // 2
    assert rows == 2 * half and half % 16 == 0, shard.shape

    def body(in_ref, out_ref, token, send_sems, recv_sems, local_sem):
        x, y, c = _mesh_pos()
        me, sibling = (x, y, c), (x, y, 1 - c)
        xn, yn, dg = (1 - x, y), (x, 1 - y), (1 - x, 1 - y)
        top, bottom = pl.ds(0, half), pl.ds(half, half)
        token[...] = jnp.zeros_like(token)

        def copy(k, block, to, part=None, src=None):
            dst = out_ref.at[_slot(block)] if part is None else out_ref.at[_slot(block), part]
            return pltpu.make_async_remote_copy(
                src_ref=dst if src is None else src, dst_ref=dst, send_sem=send_sems.at[k], recv_sem=recv_sems.at[k],
                device_id=to, device_id_type=MESH_ID)

        mine = pltpu.make_async_copy(in_ref, out_ref.at[_slot(me)], local_sem)
        mine.start()
        sends = [copy(0, me, sibling, src=in_ref), copy(1, me, (*xn, c), src=in_ref), copy(2, me, (*yn, c), src=in_ref)]
        for cp in sends:
            cp.start()
        copy(1, (*xn, c), me).wait_recv()
        sends += [copy(3, (*xn, c), sibling), copy(4, (*xn, c), (*yn, c), top)]
        sends[-2].start()
        sends[-1].start()
        copy(2, (*yn, c), me).wait_recv()
        sends += [copy(5, (*yn, c), sibling), copy(6, (*yn, c), (*xn, c), bottom)]
        sends[-2].start()
        sends[-1].start()
        copy(4, (*dg, c), me, top).wait_recv()
        copy(6, (*dg, c), me, bottom).wait_recv()
        sends.append(copy(7, (*dg, c), sibling))
        sends[-1].start()
        copy(0, sibling, me).wait_recv()
        copy(3, (*xn, 1 - c), me).wait_recv()
        copy(5, (*yn, 1 - c), me).wait_recv()
        copy(7, (*dg, 1 - c), me).wait_recv()
        for cp in sends:
            cp.wait_send()
        mine.wait()

    any_spec = pl.BlockSpec(memory_space=pl.ANY)
    return pl.pallas_call(
        body, name=name,
        out_shape=[jax.ShapeDtypeStruct((N_DEV,) + shard.shape, shard.dtype), jax.ShapeDtypeStruct((8, LANE), F32)],
        in_specs=[any_spec], out_specs=[any_spec, pl.BlockSpec(memory_space=pltpu.VMEM)],
        scratch_shapes=[pltpu.SemaphoreType.DMA((8,)), pltpu.SemaphoreType.DMA((8,)), pltpu.SemaphoreType.DMA],
    )(shard)


def _peer_list(x, y, c):
    return [((1 - x) if k & 4 else x, (1 - y) if k & 2 else y, (1 - c) if k & 1 else c) for k in range(1, N_DEV)]


def _exchange_copies(srcs, lands, send_sems, recv_sems, slotted, arriving):
    x, y, c = _mesh_pos()
    me = _slot((x, y, c))
    copies = []
    for a in range(len(srcs)):
        for k, peer in enumerate(_peer_list(x, y, c)):
            copies.append(pltpu.make_async_remote_copy(
                src_ref=srcs[a].at[_slot(peer)] if slotted else srcs[a],
                dst_ref=lands[a].at[_slot(peer) if arriving else me],
                send_sem=send_sems.at[7 * a + k], recv_sem=recv_sems.at[7 * a + k],
                device_id=peer, device_id_type=MESH_ID))
    return copies


def unwritten_hbm(shapes, dtype, name):
    def body(*refs):
        pass

    return pl.pallas_call(
        body, name=name, out_shape=[jax.ShapeDtypeStruct(s, dtype) for s in shapes],
        out_specs=[pl.BlockSpec(memory_space=pl.ANY)] * len(shapes),
    )()


HBM_SPEC = pl.BlockSpec(memory_space=pltpu.HBM)
SEM_SPEC = pl.BlockSpec(memory_space=pltpu.SEMAPHORE)
DATAFLOW = pltpu.SideEffectType.DATAFLOW_SIDE_EFFECTING


def exchange_start(srcs, lands, *, slotted, name):
    n = len(srcs)

    def body(*refs):
        ins = refs[:2 * n]
        send_sems, recv_sems = refs[2 * n], refs[2 * n + 1]
        token = refs[-1]
        for cp in _exchange_copies(ins[:n], ins[n:], send_sems, recv_sems, slotted, False):
            cp.start()
        token[...] = jnp.zeros_like(token)

    operands = [pltpu.with_memory_space_constraint(v, pltpu.HBM) for v in list(srcs) + list(lands)]
    out = pl.pallas_call(
        body, name=name,
        out_shape=(pltpu.SemaphoreType.DMA((7 * n,)), pltpu.SemaphoreType.DMA((7 * n,)),
                   *[pltpu.HBM(v.shape, v.dtype) for v in operands], jax.ShapeDtypeStruct((8, LANE), F32)),
        in_specs=[HBM_SPEC] * (2 * n),
        out_specs=(SEM_SPEC, SEM_SPEC, *[HBM_SPEC] * (2 * n), pl.BlockSpec(memory_space=pltpu.VMEM)),
        input_output_aliases={i: 2 + i for i in range(2 * n)},
        compiler_params=pltpu.CompilerParams(has_side_effects=DATAFLOW),
    )(*operands)
    return out[0], out[1], list(out[2:2 + n]), list(out[2 + n:2 + 2 * n]), out[-1]


def exchange_wait(handle, after, *, slotted, name):
    send_sems, recv_sems, srcs, lands, _ = handle
    n = len(srcs)

    def body(*refs):
        ins = refs[:2 * n]
        for cp in _exchange_copies(ins[:n], ins[n:], refs[2 * n], refs[2 * n + 1], slotted, True):
            cp.wait_send()
            cp.wait_recv()

    out = pl.pallas_call(
        body, name=name,
        out_shape=tuple(pltpu.HBM(v.shape, v.dtype) for v in srcs + lands),
        in_specs=[HBM_SPEC] * (2 * n) + [SEM_SPEC, SEM_SPEC, pl.BlockSpec(memory_space=pl.ANY)],
        out_specs=[HBM_SPEC] * (2 * n),
        input_output_aliases={i: i for i in range(2 * n)},
        compiler_params=pltpu.CompilerParams(has_side_effects=DATAFLOW),
    )(*srcs, *lands, send_sems, recv_sems, after)
    return list(out[n:])


VMEM_BLOCK_BUDGET = 46 * 2 ** 20
ROW_TILE, COL_TILE = 1024, 1408
ACC_BUDGET = 12 * 2 ** 20


def _mm_tiles(m, n, row_bytes, col_bytes, elem_bytes):
    tm, tn = _tile(m, ROW_TILE, 16), _tile(n, COL_TILE, LANE)
    while 2 * (tm * row_bytes + tn * col_bytes + tm * tn * elem_bytes) > VMEM_BLOCK_BUDGET:
        narrower = _tile(n, max(tn - LANE, LANE), LANE)
        if tn > 512 and narrower < tn:
            tn = narrower
        else:
            tm //= 2
    return tm, tn


def mm_nn(a, w, *, out_dtype, name):
    m, k = a.shape
    n = w.shape[1]
    tm, tn = _mm_tiles(m, n, 2 * k, 2 * k, jnp.dtype(out_dtype).itemsize)

    def body(a_ref, w_ref, o_ref):
        o_ref[...] = _dot(a_ref[...], w_ref[...], NN).astype(o_ref.dtype)

    return pl.pallas_call(
        body, name=name, grid=(n // tn, m // tm),
        in_specs=[pl.BlockSpec((tm, k), lambda j, i: (i, 0)), pl.BlockSpec((k, tn), lambda j, i: (0, j))],
        out_specs=pl.BlockSpec((tm, tn), lambda j, i: (i, j)),
        out_shape=jax.ShapeDtypeStruct((m, n), out_dtype),
        compiler_params=_params("parallel", "parallel"),
    )(a, w)


def mm_nn_residual(a, w, x_in, gate, coef, *, name):
    m, k = a.shape
    n = w.shape[1]
    tm, tn = _mm_tiles(m, n, 2 * k, 2 * k, 4 + 4 + 2)

    def body(a_ref, w_ref, x_ref, g_ref, o_ref, f_ref):
        f = _dot(a_ref[...], w_ref[...], NN)
        f_ref[...] = f.astype(BF16)
        o_ref[...] = x_ref[...] + (coef * g_ref[...]) * f

    return pl.pallas_call(
        body, name=name, grid=(n // tn, m // tm),
        in_specs=[pl.BlockSpec((tm, k), lambda j, i: (i, 0)), pl.BlockSpec((k, tn), lambda j, i: (0, j)),
                  pl.BlockSpec((tm, tn), lambda j, i: (i, j)), pl.BlockSpec((1, tn), lambda j, i: (0, j))],
        out_specs=[pl.BlockSpec((tm, tn), lambda j, i: (i, j)), pl.BlockSpec((tm, tn), lambda j, i: (i, j))],
        out_shape=[jax.ShapeDtypeStruct((m, n), F32), jax.ShapeDtypeStruct((m, n), BF16)],
        compiler_params=_params("parallel", "parallel"),
    )(a, w, x_in, gate)


def mm_nn_residual_loss(a, w, x_in, gate, coef, target, *, name):
    m, k = a.shape
    n = w.shape[1]
    tm, tn = _mm_tiles(m, n, 2 * k, 2 * k, 4 + 4 + 4 + 2)

    def body(a_ref, w_ref, x_ref, g_ref, t_ref, dy_ref, df_ref, dg_ref, l_ref):
        f = _dot(a_ref[...], w_ref[...], NN)
        err = x_ref[...] + (coef * g_ref[...]) * f - t_ref[...]
        dy = err * (1.0 / n)
        dy_ref[...] = dy
        df_ref[...] = ((coef * g_ref[...]) * dy).astype(BF16)
        dgate = coef * jnp.sum(dy * f, axis=0, keepdims=True)
        part = jnp.sum(jnp.sum(err * err, axis=0, keepdims=True), axis=1, keepdims=True) * (0.5 / n)

        @pl.when(pl.program_id(1) == 0)
        def _():
            dg_ref[...] = jnp.zeros_like(dg_ref)

        @pl.when((pl.program_id(0) == 0) & (pl.program_id(1) == 0))
        def _():
            l_ref[...] = jnp.zeros_like(l_ref)

        dg_ref[...] += dgate
        l_ref[...] += jnp.broadcast_to(part, l_ref.shape)

    blk = pl.BlockSpec((tm, tn), lambda j, i: (i, j))
    vec = pl.BlockSpec((1, tn), lambda j, i: (0, j))
    return pl.pallas_call(
        body, name=name, grid=(n // tn, m // tm),
        in_specs=[pl.BlockSpec((tm, k), lambda j, i: (i, 0)), pl.BlockSpec((k, tn), lambda j, i: (0, j)), blk, vec, blk],
        out_specs=[blk, blk, vec, pl.BlockSpec((1, LANE), lambda j, i: (0, 0))],
        out_shape=[jax.ShapeDtypeStruct((m, n), F32), jax.ShapeDtypeStruct((m, n), BF16),
                   jax.ShapeDtypeStruct((1, n), F32), jax.ShapeDtypeStruct((1, LANE), F32)],
        compiler_params=_params("arbitrary", "arbitrary"),
    )(a, w, x_in, gate, target)


TOKEN_SPEC = pl.BlockSpec((8, LANE), lambda *_: (0, 0))


def mm_nt(a, w, token, *, out_dtype, name):
    m, k = a.shape
    n = w.shape[0]
    tm, tn = _mm_tiles(m, n, 2 * k, 2 * k, jnp.dtype(out_dtype).itemsize)

    def body(a_ref, w_ref, token_ref, o_ref):
        o_ref[...] = _dot(a_ref[...], w_ref[...], NT).astype(o_ref.dtype)

    return pl.pallas_call(
        body, name=name, grid=(n // tn, m // tm),
        in_specs=[pl.BlockSpec((tm, k), lambda j, i: (i, 0)), pl.BlockSpec((tn, k), lambda j, i: (j, 0)), TOKEN_SPEC],
        out_specs=pl.BlockSpec((tm, tn), lambda j, i: (i, j)),
        out_shape=jax.ShapeDtypeStruct((m, n), out_dtype),
        compiler_params=_params("parallel", "parallel"),
    )(a, w, token)


def mm_nt_halves(a_lo, a_hi, w3, token, *, name):
    m = a_lo.shape[0]
    n_blk, n, tn = w3.shape
    half = n_blk // 2
    tm = _tile(m, ROW_TILE, 16)

    def body(lo_ref, hi_ref, w_ref, token_ref, o_ref):
        j = pl.program_id(1)

        @pl.when(j == 0)
        def _():
            o_ref[...] = jnp.zeros_like(o_ref)

        @pl.when(j < half)
        def _():
            o_ref[...] += _dot(lo_ref[...], w_ref[...], NT)

        @pl.when(j >= half)
        def _():
            o_ref[...] += _dot(hi_ref[...], w_ref[...], NT)

    return pl.pallas_call(
        body, name=name, grid=(m // tm, n_blk),
        in_specs=[pl.BlockSpec((tm, tn), lambda i, j: (i, jnp.minimum(j, half - 1))),
                  pl.BlockSpec((tm, tn), lambda i, j: (i, jnp.maximum(j - half, 0))),
                  pl.BlockSpec((None, n, tn), lambda i, j: (j, 0, 0)), TOKEN_SPEC],
        out_specs=pl.BlockSpec((tm, n), lambda i, j: (i, 0)),
        out_shape=jax.ShapeDtypeStruct((m, n), F32),
        compiler_params=_params("parallel", "arbitrary"),
    )(a_lo, a_hi, w3, token)


def mm_tn(a, dy, *, name):
    s, k = a.shape
    n = dy.shape[1]
    ts = _tile(s, ROW_TILE, 16)
    tk = k if k <= 2048 else _tile(k, COL_TILE, LANE)
    tn = _tile(n, ACC_BUDGET // (4 * tk), LANE)
    n_steps = s // ts

    def body(a_ref, dy_ref, o_ref, acc_ref):
        t = pl.program_id(2)

        @pl.when(t == 0)
        def _():
            acc_ref[...] = jnp.zeros_like(acc_ref)

        acc_ref[...] += _dot(a_ref[...], dy_ref[...], TN)

        @pl.when(t == n_steps - 1)
        def _():
            o_ref[...] = acc_ref[...].astype(BF16)

    return pl.pallas_call(
        body, name=name, grid=(k // tk, n // tn, n_steps),
        in_specs=[pl.BlockSpec((ts, tk), lambda kk, j, t: (t, kk)), pl.BlockSpec((ts, tn), lambda kk, j, t: (t, j))],
        out_specs=pl.BlockSpec((tk, tn), lambda kk, j, t: (kk, j)),
        out_shape=jax.ShapeDtypeStruct((k, n), BF16),
        scratch_shapes=[pltpu.VMEM((tk, tn), F32)],
        compiler_params=_params("parallel", "parallel", "arbitrary"),
    )(a, dy)


def mm_tn_halves(a, dy_lo, dy_hi, n_blocks, *, name):
    s, k = a.shape
    half = n_blocks // 2
    tn = dy_lo.shape[1] // half
    ts = _tile(s, ROW_TILE, 16)
    n_steps = s // ts

    def body(a_ref, lo_ref, hi_ref, o_ref, acc_ref):
        j, t = pl.program_id(0), pl.program_id(1)

        @pl.when(t == 0)
        def _():
            acc_ref[...] = jnp.zeros_like(acc_ref)

        @pl.when(j < half)
        def _():
            acc_ref[...] += _dot(lo_ref[...], a_ref[...], TN)

        @pl.when(j >= half)
        def _():
            acc_ref[...] += _dot(hi_ref[...], a_ref[...], TN)

        @pl.when(t == n_steps - 1)
        def _():
            o_ref[...] = acc_ref[...].astype(BF16)

    return pl.pallas_call(
        body, name=name, grid=(n_blocks, n_steps),
        in_specs=[pl.BlockSpec((ts, k), lambda j, t: (t, 0)),
                  pl.BlockSpec((ts, tn), lambda j, t: (jnp.where(j < half, t, n_steps - 1), jnp.minimum(j, half - 1))),
                  pl.BlockSpec((ts, tn), lambda j, t: (jnp.where(j < half, 0, t), jnp.maximum(j - half, 0)))],
        out_specs=pl.BlockSpec((None, tn, k), lambda j, t: (j, 0, 0)),
        out_shape=jax.ShapeDtypeStruct((n_blocks, tn, k), BF16),
        scratch_shapes=[pltpu.VMEM((tn, k), F32)],
        compiler_params=_params("parallel", "arbitrary"),
    )(a, dy_lo, dy_hi)


def ffn_dgu(df, wd, g, u, token, *, name):
    m, k = df.shape
    n = wd.shape[0]
    tm, tn = _mm_tiles(m, n, 2 * k, 2 * k, 4 * 2)

    def body(df_ref, w_ref, g_ref, u_ref, token_ref, dg_ref, du_ref):
        da = _dot(df_ref[...], w_ref[...], NT)
        gg, uu = g_ref[...].astype(F32), u_ref[...].astype(F32)
        sg = _sigmoid(gg)
        dg_ref[...] = (da * uu * (sg * (1 + gg * (1 - sg)))).astype(BF16)
        du_ref[...] = (da * (gg * sg)).astype(BF16)

    blk = pl.BlockSpec((tm, tn), lambda j, i: (i, j))
    out = jax.ShapeDtypeStruct((m, n), BF16)
    return pl.pallas_call(
        body, name=name, grid=(n // tn, m // tm),
        in_specs=[pl.BlockSpec((tm, k), lambda j, i: (i, 0)), pl.BlockSpec((tn, k), lambda j, i: (j, 0)), blk, blk, TOKEN_SPEC],
        out_specs=[blk, blk], out_shape=[out, out],
        compiler_params=_params("parallel", "parallel"),
    )(df, wd, g, u, token)


def ffn_up(h, wgu3, *, name, tm=512):
    s, k = h.shape
    n = wgu3.shape[2]
    half = wgu3.shape[0] // 2
    tm = _tile(s, tm, 16)

    def body(h_ref, wg_ref, wu_ref, g_ref, u_ref, a_ref, w_pair):
        @pl.when(pl.program_id(1) == 0)
        def _():
            w_pair[:, :n] = wg_ref[...]
            w_pair[:, n:] = wu_ref[...]

        gu = _dot(h_ref[...], w_pair[...], NN)
        g, u = gu[:, :n], gu[:, n:]
        g_ref[...] = g.astype(BF16)
        u_ref[...] = u.astype(BF16)
        a_ref[...] = (g * _sigmoid(g) * u).astype(BF16)

    out = jax.ShapeDtypeStruct((s, half * n), BF16)
    blk = pl.BlockSpec((tm, n), lambda j, i: (i, j))
    return pl.pallas_call(
        body, name=name, grid=(half, s // tm),
        in_specs=[pl.BlockSpec((tm, k), lambda j, i: (i, 0)),
                  pl.BlockSpec((None, k, n), lambda j, i: (j, 0, 0)),
                  pl.BlockSpec((None, k, n), lambda j, i: (j + half, 0, 0))],
        out_specs=[blk, blk, blk], out_shape=[out, out, out],
        scratch_shapes=[pltpu.VMEM((k, 2 * n), BF16)],
        compiler_params=_params("parallel", "arbitrary"),
    )(h, wgu3, wgu3)


def _row_spec(ts, width, col=0):
    return pl.BlockSpec((ts, width), lambda i: (i, col))


def _vec_spec(width):
    return pl.BlockSpec((1, width), lambda i: (0, 0))


def _accumulate(ref, value):
    i = pl.program_id(0)

    @pl.when(i == 0)
    def _():
        ref[...] = value

    @pl.when(i > 0)
    def _():
        ref[...] += value


def norm_modulate(x, g, shift, scale, token, *, name, ts=512):
    s, d = x.shape
    ts = _tile(s, ts, 16)

    def body(x_ref, g_ref, sh_ref, sc_ref, token_ref, h_ref):
        xx = x_ref[...]
        r = lax.rsqrt(jnp.mean(xx * xx, axis=-1, keepdims=True) + EPS)
        h_ref[...] = ((xx * r) * g_ref[...] * (1 + sc_ref[...]) + sh_ref[...]).astype(BF16)

    return pl.pallas_call(
        body, name=name, grid=(s // ts,),
        in_specs=[_row_spec(ts, d), _vec_spec(d), _vec_spec(d), _vec_spec(d), TOKEN_SPEC],
        out_specs=_row_spec(ts, d), out_shape=jax.ShapeDtypeStruct((s, d), BF16),
        compiler_params=_params("parallel"),
    )(x, g, shift, scale, token)


def norm_modulate_bwd(x, g, scale, dh, dx_out, below=None, *, name, ts=256):
    s, d = x.shape
    ts = _tile(s, ts, 16)
    coef = None if below is None else below[2]

    def body(*refs):
        x_ref, g_ref, sc_ref, dh_ref, dxo_ref = refs[:5]
        dx_ref, dsh_ref, dsc_ref, dg_ref = refs[-6:-2] if below else refs[-4:]
        xx, dh_ = x_ref[...], dh_ref[...].astype(F32)
        r = lax.rsqrt(jnp.mean(xx * xx, axis=-1, keepdims=True) + EPS)
        xh = xx * r
        dn = dh_ * (1 + sc_ref[...])
        dxh = dn * g_ref[...]
        dx = dxo_ref[...] + r * (dxh - xh * jnp.mean(dxh * xh, axis=-1, keepdims=True))
        dx_ref[...] = dx
        _accumulate(dsh_ref, jnp.sum(dh_, axis=0, keepdims=True))
        _accumulate(dsc_ref, jnp.sum(dh_ * (xh * g_ref[...]), axis=0, keepdims=True))
        _accumulate(dg_ref, jnp.sum(dn * xh, axis=0, keepdims=True))
        if below:
            f_ref, gate_ref, df_ref, dgate_ref = refs[5], refs[6], refs[-2], refs[-1]
            df_ref[...] = ((coef * gate_ref[...]) * dx).astype(BF16)
            _accumulate(dgate_ref, coef * jnp.sum(dx * f_ref[...].astype(F32), axis=0, keepdims=True))

    vec = jax.ShapeDtypeStruct((1, d), F32)
    extra_in, extra_out, extra_shape, extra_args = [], [], [], []
    if below:
        extra_in, extra_args = [_row_spec(ts, d), _vec_spec(d)], [below[0], below[1]]
        extra_out, extra_shape = [_row_spec(ts, d), _vec_spec(d)], [jax.ShapeDtypeStruct((s, d), BF16), vec]
    return pl.pallas_call(
        body, name=name, grid=(s // ts,),
        in_specs=[_row_spec(ts, d), _vec_spec(d), _vec_spec(d), _row_spec(ts, d), _row_spec(ts, d)] + extra_in,
        out_specs=[_row_spec(ts, d), _vec_spec(d), _vec_spec(d), _vec_spec(d)] + extra_out,
        out_shape=[jax.ShapeDtypeStruct((s, d), F32), vec, vec, vec] + extra_shape,
        compiler_params=_params("arbitrary"),
    )(x, g, scale, dh, dx_out, *extra_args)


def mixers_up_merge(p_act, attn, wpu, wau, z, *, name):
    m, k = p_act.shape
    n = wpu.shape[1]
    tm, tn = _tile(m, ROW_TILE, 16), GATE_TILE

    def body(p_ref, a_ref, wp_ref, wa_ref, gate_ref, yp_ref, ya_ref, o_ref):
        yp = _dot(p_ref[...], wp_ref[...], NN).astype(BF16)
        ya = _dot(a_ref[...], wa_ref[...], NN).astype(BF16)
        yp_ref[...] = yp
        ya_ref[...] = ya
        sa = _sigmoid(gate_ref[:, :tn].astype(F32))
        sb = _sigmoid(gate_ref[:, tn:].astype(F32))
        o_ref[...] = (sa * yp + sb * ya).astype(BF16)

    rows = pl.BlockSpec((tm, k), lambda j, i: (i, 0))
    cols = pl.BlockSpec((k, tn), lambda j, i: (0, j))
    blk = pl.BlockSpec((tm, tn), lambda j, i: (i, j))
    out = jax.ShapeDtypeStruct((m, n), BF16)
    return pl.pallas_call(
        body, name=name, grid=(n // tn, m // tm),
        in_specs=[rows, rows, cols, cols, pl.BlockSpec((tm, 2 * tn), lambda j, i: (i, j))],
        out_specs=[blk, blk, blk], out_shape=[out, out, out],
        compiler_params=_params("parallel", "parallel"),
    )(p_act, attn, wpu, wau, z)


def merge_bwd(d_o, wo, z, y_pool, y_attn, token, *, name):
    m, k = d_o.shape
    n = wo.shape[0]
    tm, tn = _tile(m, ROW_TILE, 16), GATE_TILE

    def body(do_ref, w_ref, gate_ref, yp_ref, ya_ref, token_ref, dyp_ref, dya_ref, dz_ref):
        dm = _dot(do_ref[...], w_ref[...], NT)
        sa = _sigmoid(gate_ref[:, :tn].astype(F32))
        sb = _sigmoid(gate_ref[:, tn:].astype(F32))
        dyp_ref[...] = (dm * sa).astype(BF16)
        dya_ref[...] = (dm * sb).astype(BF16)
        dz_ref[:, :tn] = (dm * yp_ref[...] * (sa * (1 - sa))).astype(BF16)
        dz_ref[:, tn:] = (dm * ya_ref[...] * (sb * (1 - sb))).astype(BF16)

    blk = pl.BlockSpec((tm, tn), lambda j, i: (i, j))
    pair = pl.BlockSpec((tm, 2 * tn), lambda j, i: (i, j))
    out = jax.ShapeDtypeStruct((m, n), BF16)
    return pl.pallas_call(
        body, name=name, grid=(n // tn, m // tm),
        in_specs=[pl.BlockSpec((tm, k), lambda j, i: (i, 0)), pl.BlockSpec((tn, k), lambda j, i: (j, 0)),
                  pair, blk, blk, TOKEN_SPEC],
        out_specs=[blk, blk, pair], out_shape=[out, out, jax.ShapeDtypeStruct(z.shape, BF16)],
        compiler_params=_params("parallel", "parallel"),
    )(d_o, wo, z, y_pool, y_attn, token)


def _window_counts(t0, rows):
    t1 = (t0 + 1 + lax.broadcasted_iota(jnp.int32, (rows, 1), 0)).astype(F32)
    return [jnp.minimum(t1, float(w)) for w in POOL_WINDOWS]


def pool_fwd(z, pool_mix, pool_scale, *, name, ts=256):
    s = z.shape[0]
    ts = _tile(s, ts, 16)
    per = ts // POOL_HALO

    def body(u_ref, halo_ref, pm_ref, ps_ref, pooled_ref, p_ref):
        i = pl.program_id(0)
        u = u_ref[...].astype(F32)
        halo = jnp.where(i > 0, halo_ref[...].astype(F32), 0.0)
        run = jnp.concatenate([halo, u], axis=0)
        sums, width = [], 1
        for w in POOL_WINDOWS:
            while width < w:
                run = run + pltpu.roll(run, width, 0)
                width *= 2
            sums.append(run[POOL_HALO:])
        counts = _window_counts(i * ts, ts)
        for gi in range(POOL_GROUPS):
            cols = slice(gi * POOL_GROUP_W, (gi + 1) * POOL_GROUP_W)
            pooled = (sums[gi][:, cols] / counts[gi] - u[:, cols]).astype(BF16)
            pooled_ref[:, cols] = pooled
            p_ref[:, cols] = (_dot(pooled, pm_ref[gi], NN) * ps_ref[:, cols]).astype(BF16)

    out = jax.ShapeDtypeStruct((s, POOL_W), BF16)
    return pl.pallas_call(
        body, name=name, grid=(s // ts,),
        in_specs=[_row_spec(ts, POOL_W, COL_U // POOL_W),
                  pl.BlockSpec((POOL_HALO, POOL_W), lambda i: (jnp.maximum(i * per - 1, 0), COL_U // POOL_W)),
                  pl.BlockSpec((POOL_GROUPS, POOL_GROUP_W, POOL_GROUP_W), lambda i: (0, 0, 0)),
                  _vec_spec(POOL_W)],
        out_specs=[_row_spec(ts, POOL_W)] * 2, out_shape=[out, out],
        compiler_params=_params("parallel"),
    )(z, z, pool_mix, pool_scale)


def pool_bwd(dp, pooled, pool_mix, pool_scale, dz, *, name, ts=256):
    s = dp.shape[0]
    ts = _tile(s, ts, 16)
    per = ts // POOL_HALO
    n_steps = s // ts
    rows = ts + POOL_HALO

    def body(dp_ref, halo_ref, pooled_ref, pm_ref, ps_ref, dz_in_ref, du_ref, dps_ref, dpm_ref):
        i = pl.program_id(0)
        dp_main = dp_ref[...]
        halo = jnp.where(i < n_steps - 1, halo_ref[...], 0.0)
        dmixed = jnp.concatenate([dp_main, halo], axis=0) * ps_ref[...]
        counts = _window_counts(i * ts, rows)
        dps_parts = []
        for gi, w in enumerate(POOL_WINDOWS):
            cols = slice(gi * POOL_GROUP_W, (gi + 1) * POOL_GROUP_W)
            dmx = dmixed[:, cols].astype(BF16)
            pooled = pooled_ref[:, cols]
            mixed = _dot(pooled, pm_ref[gi], NN)
            dps_parts.append(jnp.sum(dp_main[:, cols] * mixed, axis=0, keepdims=True))
            dpm_g = _dot(pooled, dmx[:ts], TN)

            @pl.when(i == 0)
            def _():
                dpm_ref[gi] = dpm_g

            @pl.when(i > 0)
            def _():
                dpm_ref[gi] += dpm_g

            dpooled = _dot(dmx, pm_ref[gi], NT)
            run, width = dpooled / counts[gi], 1
            while width < w:
                run = run + pltpu.roll(run, rows - width, 0)
                width *= 2
            du_ref[:, cols] = (run[:ts] - dpooled[:ts]).astype(BF16)
        _accumulate(dps_ref, jnp.concatenate(dps_parts, axis=1))

    return pl.pallas_call(
        body, name=name, grid=(n_steps,),
        in_specs=[_row_spec(ts, POOL_W),
                  pl.BlockSpec((POOL_HALO, POOL_W), lambda i: (jnp.minimum((i + 1) * per, s // POOL_HALO - 1), 0)),
                  _row_spec(ts, POOL_W),
                  pl.BlockSpec((POOL_GROUPS, POOL_GROUP_W, POOL_GROUP_W), lambda i: (0, 0, 0)),
                  _vec_spec(POOL_W), pl.BlockSpec(memory_space=pl.ANY)],
        out_specs=[_row_spec(ts, POOL_W, COL_U // POOL_W), _vec_spec(POOL_W),
                   pl.BlockSpec((POOL_GROUPS, POOL_GROUP_W, POOL_GROUP_W), lambda i: (0, 0, 0))],
        out_shape=[jax.ShapeDtypeStruct(dz.shape, BF16), jax.ShapeDtypeStruct((1, POOL_W), F32),
                   jax.ShapeDtypeStruct((POOL_GROUPS, POOL_GROUP_W, POOL_GROUP_W), F32)],
        input_output_aliases={5: 0},
        compiler_params=_params("arbitrary"),
    )(dp, dp, pooled, pool_mix, pool_scale, dz)


def _bucket_one_hot():
    ql = np.arange(BLK)[:, None]
    j = np.arange(2 * BLK)[None, :]
    n = np.clip(BLK + ql - j, 0, None)
    nf = np.maximum(n, 1).astype(np.float32)
    large = MAX_EXACT + (np.log(nf / MAX_EXACT) / np.log(REL_MAX_DIST / MAX_EXACT)
                         * (NUM_BUCKETS - MAX_EXACT)).astype(np.int32)
    large = np.minimum(large, NUM_BUCKETS - 1)
    bucket = np.where(n < MAX_EXACT, n, large).astype(np.int32).reshape(-1)
    return (np.arange(NUM_BUCKETS)[:, None] == bucket[None, :]).astype(np.float32)


def bias_table(rel_bias_t, one_hot, *, name, tc=4096):
    n = one_hot.shape[1]

    def body(rb_ref, oh_ref, o_ref):
        o_ref[...] = _dot(rb_ref[...], oh_ref[...], NN, precision=lax.Precision.HIGHEST)

    return pl.pallas_call(
        body, name=name, grid=(n // tc,),
        in_specs=[pl.BlockSpec((N_HEADS, NUM_BUCKETS), lambda i: (0, 0)), pl.BlockSpec((NUM_BUCKETS, tc), lambda i: (0, i))],
        out_specs=pl.BlockSpec((N_HEADS, tc), lambda i: (0, i)),
        out_shape=jax.ShapeDtypeStruct((N_HEADS, n), F32),
        compiler_params=_params("parallel"),
    )(rel_bias_t, one_hot)


def bias_table_bwd(dbias, one_hot, *, name, tc=4096):
    n = one_hot.shape[1]

    def body(db_ref, oh_ref, o_ref):
        _accumulate(o_ref, _dot(db_ref[...], oh_ref[...], NT, precision=lax.Precision.HIGHEST))

    return pl.pallas_call(
        body, name=name, grid=(n // tc,),
        in_specs=[pl.BlockSpec((N_HEADS, tc), lambda i: (0, i)), pl.BlockSpec((NUM_BUCKETS, tc), lambda i: (0, i))],
        out_specs=pl.BlockSpec((N_HEADS, NUM_BUCKETS), lambda i: (0, 0)),
        out_shape=jax.ShapeDtypeStruct((N_HEADS, NUM_BUCKETS), F32),
        compiler_params=_params("arbitrary"),
    )(dbias, one_hot)


def _lane_half(shape):
    return lax.broadcasted_iota(jnp.int32, shape, len(shape) - 1) < HEAD_DIM


def _half_sums(v, first):
    s0 = jnp.sum(jnp.where(first, v, 0.0), axis=-1, keepdims=True)
    s1 = jnp.sum(jnp.where(first, 0.0, v), axis=-1, keepdims=True)
    return jnp.where(first, s0, s1)


BWD_STACK, FWD_STACK = N_HEADS, N_HEADS


def _band_mask(n, heads):
    ql = lax.broadcasted_iota(jnp.int32, (heads * BLK, 2 * BLK), 0) & (BLK - 1)
    j = lax.broadcasted_iota(jnp.int32, (heads * BLK, 2 * BLK), 1)
    return (j > ql) & (j <= ql + BLK) & ((j >= BLK) | (n > 0))


def _norm_keys(kband):
    first = _lane_half(kband.shape)
    r = lax.rsqrt(_half_sums(kband * kband, first) * (1.0 / HEAD_DIM) + EPS)
    return kband * r


def _kv_lanes(h0, heads):
    shape = (heads * BLK, LANE)
    head = h0 + lax.broadcasted_iota(jnp.int32, shape, 0) // BLK
    return _lane_half(shape) == (head < GQA)


def _stack_heads(ref, h0, heads, dtype=F32):
    parts = []
    for h in range(h0, h0 + heads):
        part = ref[:, (h // 2) * LANE:(h // 2 + 1) * LANE].astype(dtype)
        parts.append(pltpu.roll(part, HEAD_DIM, 1) if h % 2 != h // GQA else part)
    return jnp.where(_kv_lanes(h0, heads), jnp.concatenate(parts, axis=0), 0.0)


def _unstack_heads(ref, h0, heads, stacked):
    for i in range(heads // 2):
        pair = None
        for sub in range(2):
            part = stacked[(2 * i + sub) * BLK:(2 * i + sub + 1) * BLK]
            part = pltpu.roll(part, HEAD_DIM, 1) if sub != (h0 + 2 * i) // GQA else part
            pair = part if pair is None else pair + part
        col = (h0 // 2 + i) * LANE
        ref[:, col:col + LANE] = pair.astype(BF16)


def _stack_logits(q_ref, h0, heads, qg, kn, bias_ref, sink_ref, mask):
    qa = _stack_heads(q_ref, h0, heads)
    r = lax.rsqrt(jnp.sum(qa * qa, axis=-1, keepdims=True) * (1.0 / HEAD_DIM) + EPS)
    xh = qa * r
    qn = (xh * qg).astype(BF16)
    bias = bias_ref[h0:h0 + heads].reshape(heads * BLK, 2 * BLK)
    logits = _dot(qn, kn, NT) * (HEAD_DIM ** -0.5) + bias
    p, p_sink = _softmax_with_sink(jnp.where(mask, logits, NEG_INF), sink_ref[h0 * BLK:(h0 + heads) * BLK])
    return xh, r, qn, p, p_sink


def _softmax_with_sink(logits, sink):
    m = jnp.maximum(jnp.max(logits, axis=-1, keepdims=True), sink)
    e = jnp.exp(logits - m)
    es = jnp.exp(sink - m)
    den = jnp.sum(e, axis=-1, keepdims=True) + es
    return e / den, es / den


def _attn_specs(nb, last):
    cur = lambda n: jnp.minimum(n, last)
    prev = lambda n: jnp.minimum(jnp.maximum(n - 1, 0), last)
    return [pl.BlockSpec((BLK, ATT_W), lambda n: (cur(n), COL_Q // ATT_W)),
            pl.BlockSpec((BLK, KV_W), lambda n: (prev(n), COL_K // KV_W)),
            pl.BlockSpec((BLK, KV_W), lambda n: (cur(n), COL_K // KV_W)),
            pl.BlockSpec((BLK, KV_W), lambda n: (prev(n), COL_V // KV_W)),
            pl.BlockSpec((BLK, KV_W), lambda n: (cur(n), COL_V // KV_W))]


def attn_fwd(z, qg2, kg2, sinks, bias, *, name):
    s = z.shape[0]
    nb = s // BLK

    def body(q_ref, kp_ref, kc_ref, vp_ref, vc_ref, qg_ref, kg_ref, sink_ref, bias_ref, o_ref):
        mask = _band_mask(pl.program_id(0), FWD_STACK)
        kn = (_norm_keys(jnp.concatenate([kp_ref[...], kc_ref[...]], axis=0).astype(F32)) * kg_ref[...]).astype(BF16)
        vb = jnp.concatenate([vp_ref[...], vc_ref[...]], axis=0).astype(BF16)
        for h0 in range(0, N_HEADS, FWD_STACK):
            _, _, _, p, _ = _stack_logits(q_ref, h0, FWD_STACK, qg_ref[...], kn, bias_ref, sink_ref, mask)
            out = _dot(p.astype(BF16), vb, NN)
            _unstack_heads(o_ref, h0, FWD_STACK, jnp.where(_kv_lanes(h0, FWD_STACK), out, 0.0))

    return pl.pallas_call(
        body, name=name, grid=(nb,),
        in_specs=_attn_specs(nb, nb - 1) + [
            _vec_spec(LANE), _vec_spec(LANE), pl.BlockSpec((N_HEADS * BLK, 1), lambda n: (0, 0)),
            pl.BlockSpec((N_HEADS, BLK, 2 * BLK), lambda n: (0, 0, 0))],
        out_specs=pl.BlockSpec((BLK, ATT_W), lambda n: (n, 0)),
        out_shape=jax.ShapeDtypeStruct((s, ATT_W), BF16),
        compiler_params=_params("parallel"),
    )(z, z, z, z, z, qg2, kg2, sinks, bias)


def attn_bwd(z, d_out, qg2, kg2, sinks, bias, dz, *, name):
    s = z.shape[0]
    nb = s // BLK
    scale = HEAD_DIM ** -0.5

    def body(q_ref, kp_ref, kc_ref, vp_ref, vc_ref, do_ref, qg_ref, kg_ref, sink_ref, bias_ref, dz_in_ref,
             dq_ref, dk_ref, dv_ref, dqg_ref, dkg_ref, dsink_ref, dbias_ref, band_k, band_v, carry_k, carry_v, dsink_rows):
        n = pl.program_id(0)

        @pl.when(n == 0)
        def _():
            dqg_ref[...] = jnp.zeros_like(dqg_ref)
            dkg_ref[...] = jnp.zeros_like(dkg_ref)
            dbias_ref[...] = jnp.zeros_like(dbias_ref)
            carry_k[...] = jnp.zeros_like(carry_k)
            carry_v[...] = jnp.zeros_like(carry_v)
            dsink_rows[...] = jnp.zeros_like(dsink_rows)

        @pl.when(n == nb)
        def _():
            band_k[...] = jnp.zeros_like(band_k)
            band_v[...] = jnp.zeros_like(band_v)

        @pl.when(n < nb)
        def _():
            mask = _band_mask(n, BWD_STACK)
            kn = (_norm_keys(jnp.concatenate([kp_ref[...], kc_ref[...]], axis=0).astype(F32)) * kg_ref[...]).astype(BF16)
            vb = jnp.concatenate([vp_ref[...], vc_ref[...]], axis=0).astype(BF16)
            dkn = jnp.zeros((2 * BLK, KV_W), F32)
            dvb = jnp.zeros((2 * BLK, KV_W), F32)
            dqg = jnp.zeros((1, LANE), F32)
            for h0 in range(0, N_HEADS, BWD_STACK):
                rows = slice(h0 * BLK, (h0 + BWD_STACK) * BLK)
                xh, r, qn, p, p_sink = _stack_logits(q_ref, h0, BWD_STACK, qg_ref[...], kn, bias_ref, sink_ref, mask)
                do = _stack_heads(do_ref, h0, BWD_STACK).astype(BF16)
                dp = _dot(do, vb, NT)
                delta = jnp.sum(p * dp, axis=-1, keepdims=True)
                ds = p * (dp - delta)
                dsink_rows[rows] -= p_sink * delta
                dbias_ref[h0:h0 + BWD_STACK] += ds.reshape(BWD_STACK, BLK, 2 * BLK)
                ds16 = ds.astype(BF16)
                dqn = jnp.where(_kv_lanes(h0, BWD_STACK), _dot(ds16, kn, NN) * scale, 0.0)
                dkn = dkn + _dot(ds16, qn, TN) * scale
                dvb = dvb + _dot(p.astype(BF16), do, TN)
                dqg = dqg + jnp.sum(dqn * xh, axis=0, keepdims=True)
                dxh = dqn * qg_ref[...]
                _unstack_heads(dq_ref, h0, BWD_STACK,
                               r * (dxh - xh * (jnp.sum(dxh * xh, axis=-1, keepdims=True) * (1.0 / HEAD_DIM))))
            band_k[...] = dkn
            band_v[...] = dvb
            dqg_ref[...] += dqg

        dkn_prev = carry_k[...] + band_k[:BLK]
        dv_ref[...] = (carry_v[...] + band_v[:BLK]).astype(BF16)
        carry_k[...] = band_k[BLK:]
        carry_v[...] = band_v[BLK:]
        kp = kp_ref[...].astype(F32)
        first = _lane_half(kp.shape)
        r = lax.rsqrt(_half_sums(kp * kp, first) * (1.0 / HEAD_DIM) + EPS)
        xh = kp * r
        dkg_ref[...] += jnp.sum(dkn_prev * xh, axis=0, keepdims=True)
        dxh = dkn_prev * kg_ref[...]
        dk_ref[...] = (r * (dxh - xh * (_half_sums(dxh * xh, first) * (1.0 / HEAD_DIM)))).astype(BF16)

        @pl.when(n == nb)
        def _():
            dqg_ref[...] += pltpu.roll(dqg_ref[...], HEAD_DIM, 1)
            dkg_ref[...] += pltpu.roll(dkg_ref[...], HEAD_DIM, 1)
            lane16 = lax.broadcasted_iota(jnp.int32, (1, N_HEADS), 1)
            dsink = jnp.zeros((1, N_HEADS), F32)
            for h in range(N_HEADS):
                dsink = dsink + jnp.where(lane16 == h, jnp.sum(dsink_rows[h * BLK:(h + 1) * BLK], axis=0, keepdims=True), 0.0)
            dsink_ref[...] = dsink

    last = nb - 1
    cur = lambda n: jnp.minimum(n, last)
    back = lambda n: jnp.maximum(n - 1, 0)
    full3 = pl.BlockSpec((N_HEADS, BLK, 2 * BLK), lambda n: (0, 0, 0))
    return pl.pallas_call(
        body, name=name, grid=(nb + 1,),
        in_specs=_attn_specs(nb, last) + [
            pl.BlockSpec((BLK, ATT_W), lambda n: (cur(n), 0)),
            _vec_spec(LANE), _vec_spec(LANE), pl.BlockSpec((N_HEADS * BLK, 1), lambda n: (0, 0)), full3,
            pl.BlockSpec(memory_space=pl.ANY)],
        out_specs=[pl.BlockSpec((BLK, ATT_W), lambda n: (cur(n), COL_Q // ATT_W)),
                   pl.BlockSpec((BLK, KV_W), lambda n: (back(n), 0)),
                   pl.BlockSpec((BLK, KV_W), lambda n: (back(n), 0)),
                   _vec_spec(LANE), _vec_spec(LANE), _vec_spec(N_HEADS), full3],
        input_output_aliases={10: 0},
        out_shape=[jax.ShapeDtypeStruct(dz.shape, BF16), jax.ShapeDtypeStruct((s, KV_W), BF16),
                   jax.ShapeDtypeStruct((s, KV_W), BF16), jax.ShapeDtypeStruct((1, LANE), F32),
                   jax.ShapeDtypeStruct((1, LANE), F32), jax.ShapeDtypeStruct((1, N_HEADS), F32),
                   jax.ShapeDtypeStruct((N_HEADS, BLK, 2 * BLK), F32)],
        scratch_shapes=[pltpu.VMEM((2 * BLK, KV_W), F32), pltpu.VMEM((2 * BLK, KV_W), F32),
                        pltpu.VMEM((BLK, KV_W), F32), pltpu.VMEM((BLK, KV_W), F32), pltpu.VMEM((N_HEADS * BLK, 1), F32)],
        compiler_params=_params("arbitrary"),
    )(z, z, z, z, z, d_out, qg2, kg2, sinks, bias, dz)


def _adamw(w, g, m, v):
    m = ADAM_B1 * m + (1.0 - ADAM_B1) * g
    v = ADAM_B2 * v + (1.0 - ADAM_B2) * (g * g)
    m_hat = m / (1.0 - ADAM_B1 ** ADAM_STEP)
    v_hat = v / (1.0 - ADAM_B2 ** ADAM_STEP)
    delta = -ADAM_LR * (m_hat / (jnp.sqrt(v_hat) + ADAM_EPS) + ADAM_WD * w)
    return delta, m, v


def ada_fwd(c16, w, b, *, name, tn=768):
    k, n = w.shape
    tn = _tile(n, tn, LANE)

    def body(c_ref, w_ref, b_ref, o_ref):
        cc = c_ref[...]
        o_ref[...] = _dot((cc * _sigmoid(cc)).astype(BF16), w_ref[...].astype(BF16), NN) + b_ref[...]

    return pl.pallas_call(
        body, name=name, grid=(n // tn,),
        in_specs=[pl.BlockSpec((c16.shape[0], k), lambda j: (0, 0)), pl.BlockSpec((k, tn), lambda j: (0, j)),
                  pl.BlockSpec((1, tn), lambda j: (0, j))],
        out_specs=pl.BlockSpec((c16.shape[0], tn), lambda j: (0, j)),
        out_shape=jax.ShapeDtypeStruct((c16.shape[0], n), F32),
        compiler_params=_params("parallel"),
    )(c16, w, b)


def ada_bwd_adamw(c_t, dmod, w, m, v, *, name, tn=256):
    k, n = w.shape
    tn = _tile(n, tn, LANE)

    def body(c_ref, d_ref, w_ref, m_ref, v_ref, g_ref, dl_ref, mo_ref, vo_ref):
        cc = c_ref[...]
        g = _dot((cc * _sigmoid(cc)).astype(BF16), d_ref[...].astype(BF16), NN)
        g_ref[...] = g
        dl_ref[...], mo_ref[...], vo_ref[...] = _adamw(w_ref[...], g, m_ref[...], v_ref[...])

    blk = pl.BlockSpec((k, tn), lambda j: (0, j))
    out = jax.ShapeDtypeStruct((k, n), F32)
    return pl.pallas_call(
        body, name=name, grid=(n // tn,),
        in_specs=[pl.BlockSpec((k, LANE), lambda j: (0, 0)), pl.BlockSpec((LANE, tn), lambda j: (0, j)), blk, blk, blk],
        out_specs=[blk] * 4, out_shape=[out] * 4,
        compiler_params=_params("parallel"),
    )(c_t, dmod, w, m, v)


def adamw_from_transposed_parts(parts, w, m, v, *, name):
    r, c = w.shape
    tc = _tile(c, LANE, LANE)

    def body(p_ref, w_ref, m_ref, v_ref, g_ref, dl_ref, mo_ref, vo_ref):
        g_t = p_ref[0].astype(F32)
        for d in range(1, N_DEV):
            g_t = g_t + p_ref[d].astype(F32)
        g = g_t.T
        g_ref[...] = g
        dl_ref[...], mo_ref[...], vo_ref[...] = _adamw(w_ref[...], g, m_ref[...], v_ref[...])

    blk = pl.BlockSpec((r, tc), lambda i: (0, i))
    out = jax.ShapeDtypeStruct((r, c), F32)
    return pl.pallas_call(
        body, name=name, grid=(c // tc,),
        in_specs=[pl.BlockSpec((N_DEV, tc, r), lambda i: (0, i, 0)), blk, blk, blk],
        out_specs=[blk] * 4, out_shape=[out] * 4,
        compiler_params=_params("parallel"),
    )(parts, w, m, v)


def adamw_from_parts(parts, w, m, v, *, name):
    r, c = w.shape
    tr = _tile(r, max(16, (256 * 1024) // c), 16)

    def body(p_ref, w_ref, m_ref, v_ref, g_ref, dl_ref, mo_ref, vo_ref):
        g = p_ref[0].astype(F32)
        for d in range(1, N_DEV):
            g = g + p_ref[d].astype(F32)
        g_ref[...] = g
        dl_ref[...], mo_ref[...], vo_ref[...] = _adamw(w_ref[...], g, m_ref[...], v_ref[...])

    blk = pl.BlockSpec((tr, c), lambda i: (i, 0))
    out = jax.ShapeDtypeStruct((r, c), F32)
    return pl.pallas_call(
        body, name=name, grid=(r // tr,),
        in_specs=[pl.BlockSpec((N_DEV, tr, c), lambda i: (0, i, 0)), blk, blk, blk],
        out_specs=[blk] * 4, out_shape=[out] * 4,
        compiler_params=_params("parallel"),
    )(parts, w, m, v)


def _ffn_fwd(x_in, g, shift, scale, gate, wgu3, get_wd, token, tag, target=None):
    h = norm_modulate(x_in, g, shift, scale, token, name=f"{tag}_norm")
    gg, uu, act = ffn_up(h, wgu3, name=f"{tag}_up")
    wd = get_wd(gg)
    if target is None:
        x_out, f = mm_nn_residual(act, wd, x_in, gate, 0.5, name=f"{tag}_down")
    else:
        x_out, f = mm_nn_residual_loss(act, wd, x_in, gate, 0.5, target, name=f"{tag}_down_loss"), None
    return x_out, (h, gg, uu, act, f), wd


def _ffn_bwd(dx_out, df, dgate, x_in, g, scale, wgu3, wd, saved, token, scatter, split, tag, below=None):
    h, gg, uu, act, _ = saved
    dwd = mm_tn(act, df, name=f"{tag}_dwd").reshape(N_DEV, -1, D_MODEL)
    if split:
        token = scatter([f"w_{tag}_down"], [dwd], f"scatter_{tag}_down")
    dgg, duu = ffn_dgu(df, wd, gg, uu, token, name=f"{tag}_dgu")
    dwgu = mm_tn_halves(h, dgg, duu, N_DEV, name=f"{tag}_dwgu")
    if split:
        token = scatter([f"w_{tag}_gu"], [dwgu], f"scatter_{tag}_gu")
    else:
        token = scatter([f"w_{tag}_gu", f"w_{tag}_down"], [dwgu, dwd], f"scatter_{tag}")
    dh = mm_nt_halves(dgg, duu, wgu3, token, name=f"{tag}_dh")
    dx_in, dshift, dscale, dg, *rest = norm_modulate_bwd(x_in, g, scale, dh, dx_out, below, name=f"{tag}_norm_bwd")
    return (dx_in, (dshift, dscale, dgate), dg, *rest)


def kernel(x, c, w_ada, b_ada, g_ffn1, w_ffn1_gu, w_ffn1_down, g_mix, w_in, pool_mix, pool_scale, w_pool_up, q_gain, k_gain, sinks, rel_bias, w_attn_up, w_o, g_ffn2, w_ffn2_gu, w_ffn2_down, loss_target, m_w_ada, m_b_ada, m_g_ffn1, m_w_ffn1_gu, m_w_ffn1_down, m_g_mix, m_w_in, m_pool_mix, m_pool_scale, m_w_pool_up, m_q_gain, m_k_gain, m_sinks, m_rel_bias, m_w_attn_up, m_w_o, m_g_ffn2, m_w_ffn2_gu, m_w_ffn2_down, v_w_ada, v_b_ada, v_g_ffn1, v_w_ffn1_gu, v_w_ffn1_down, v_g_mix, v_w_in, v_pool_mix, v_pool_scale, v_w_pool_up, v_q_gain, v_k_gain, v_sinks, v_rel_bias, v_w_attn_up, v_w_o, v_g_ffn2, v_w_ffn2_gu, v_w_ffn2_down):
    me = _slot(_mesh_pos())
    x0, target = x[0], loss_target[0]
    n_ada = w_ada.shape[2]
    pm_rows = pool_mix.shape[2]

    big = dict(w_ffn1_gu=(w_ffn1_gu, m_w_ffn1_gu, v_w_ffn1_gu), w_ffn1_down=(w_ffn1_down, m_w_ffn1_down, v_w_ffn1_down),
               w_in=(w_in, m_w_in, v_w_in), pool_mix=(pool_mix, m_pool_mix, v_pool_mix),
               w_pool_up=(w_pool_up, m_w_pool_up, v_w_pool_up), w_attn_up=(w_attn_up, m_w_attn_up, v_w_attn_up),
               w_o=(w_o, m_w_o, v_w_o), w_ffn2_gu=(w_ffn2_gu, m_w_ffn2_gu, v_w_ffn2_gu),
               w_ffn2_down=(w_ffn2_down, m_w_ffn2_down, v_w_ffn2_down))
    shard2d = {k: (POOL_GROUPS * pm_rows, POOL_GROUP_W) if k == "pool_mix" else t[0].shape[1:] for k, t in big.items()}
    mix_keys = ["w_in", "pool_mix", "w_pool_up", "w_attn_up", "w_o"]
    ffn2_keys = ["w_ffn2_gu", "w_ffn2_down"]

    def shard_bf16(k, token=None):
        w = big[k][0].reshape(shard2d[k])
        return (w if token is None else w + token[0, 0]).astype(BF16)

    def landing_zones(blocks, tag):
        zones = unwritten_hbm([(N_DEV,) + b.shape for b in blocks], BF16, f"{tag}_zones")
        return [lax.dynamic_update_slice(z, b[None], (me, 0, 0)) for z, b in zip(zones, blocks)]

    def start_gather(keys, token, tag):
        shards = [shard_bf16(k, token) for k in keys]
        return exchange_start(shards, landing_zones(shards, tag), slotted=False, name=f"{tag}_start")

    pending = []

    def scatter(keys, grads, tag):
        lands = landing_zones([lax.dynamic_index_in_dim(g, me, 0, keepdims=False) for g in grads], tag)
        handle = exchange_start(grads, lands, slotted=True, name=f"{tag}_start")
        pending.append((keys, handle, tag))
        return handle[4]

    c_all, _ = all_gather_small(c.reshape(D_MODEL // LANE, LANE), "gather_c")
    c_all = c_all.reshape(N_DEV, D_MODEL)
    c16 = jnp.pad(c_all, ((0, 16 - N_DEV), (0, 0)))
    b_mine = lax.dynamic_slice(b_ada, (0, me * n_ada), (1, n_ada))
    mod_cols = ada_fwd(c16, w_ada[0], b_mine, name="ada_fwd")[:N_DEV]
    mod_all, token = all_gather_small(mod_cols.reshape(-1, LANE), "gather_mod")
    mod = lax.dynamic_index_in_dim(mod_all.reshape(N_DEV, N_DEV, n_ada), me, axis=1, keepdims=False)
    mod = mod.reshape(N_MOD, 1, D_MODEL)

    wgu1, token = all_gather_hbm(shard_bf16("w_ffn1_gu", token), "gather_ffn1_gu")
    gather_wd1 = start_gather(["w_ffn1_down"], token, "gather_ffn1_down")
    gather_mix = start_gather(mix_keys, gather_wd1[4], "gather_mix")
    gather_ffn2 = start_gather(ffn2_keys, gather_mix[4], "gather_ffn2")
    token = gather_ffn2[4]

    def get_wd1(after):
        return exchange_wait(gather_wd1, after, slotted=False, name="gather_ffn1_down_wait")[0].reshape(-1, D_MODEL)

    x1, saved1, wd1 = _ffn_fwd(x0, g_ffn1, mod[0], mod[1], mod[2], wgu1, get_wd1, token, "ffn1")
    gathered = dict(zip(mix_keys, exchange_wait(gather_mix, x1, slotted=False, name="gather_mix_wait")))
    def columns_out(blocks):
        return jnp.transpose(blocks, (1, 0, 2)).reshape(blocks.shape[1], -1)

    def columns_in(full):
        return jnp.transpose(full.reshape(full.shape[0], N_DEV, -1), (1, 0, 2))

    w_in_full = columns_out(gathered["w_in"])
    w_in_z = jnp.concatenate([w_in_full[:, s:s + w] for s, w in Z_PIECES], axis=1)
    pm_full = jnp.transpose(gathered["pool_mix"].reshape(N_DEV, POOL_GROUPS, pm_rows, POOL_GROUP_W),
                            (1, 0, 2, 3)).reshape(POOL_GROUPS, POOL_GROUP_W, POOL_GROUP_W)
    wpu, wau = columns_out(gathered["w_pool_up"]), columns_out(gathered["w_attn_up"])
    wo_full = gathered["w_o"].reshape(D_MODEL, D_MODEL)
    h2 = norm_modulate(x1, g_mix, mod[3], mod[4], token, name="mix_norm")
    z = mm_nn(h2, w_in_z, out_dtype=BF16, name="mix_in")
    pooled, p_act = pool_fwd(z, pm_full, pool_scale, name="pool_fwd")
    one_hot = jnp.asarray(_bucket_one_hot())
    bias = bias_table(rel_bias.T, one_hot, name="bias_table").reshape(N_HEADS, BLK, 2 * BLK)
    qg2, kg2 = jnp.tile(q_gain, (1, 2)), jnp.tile(k_gain, (1, 2))
    sink_rows = jnp.repeat(sinks[0], BLK).reshape(N_HEADS * BLK, 1)
    attn = attn_fwd(z, qg2, kg2, sink_rows, bias, name="attn_fwd")
    y_pool, y_attn, merged = mixers_up_merge(p_act, attn, wpu, wau, z, name="mixers_up")
    x2, o_act = mm_nn_residual(merged, wo_full, x1, mod[5], 1.0, name="mix_out")
    wgu2, wd2 = exchange_wait(gather_ffn2, x2, slotted=False, name="gather_ffn2_wait")
    wd2 = wd2.reshape(-1, D_MODEL)
    (dy, df2, dgate3, loss_row), saved2, _ = _ffn_fwd(x2, g_ffn2, mod[6], mod[7], mod[8], wgu2, lambda after: wd2, token,
                                                     "ffn2", target)

    dx2, dmod3, dg_ffn2, d_o, dgate2 = _ffn_bwd(dy, df2, dgate3, x2, g_ffn2, mod[7], wgu2, wd2, saved2, token, scatter, False,
                                               "ffn2", below=(o_act, mod[5], 1.0))
    dwo = mm_tn(merged, d_o, name="mix_dwo").reshape(N_DEV, -1, D_MODEL)
    dyp, dya, dz = merge_bwd(d_o, wo_full, z, y_pool, y_attn, token, name="merge_bwd")
    dwpu = mm_tn(p_act, dyp, name="pool_dwup")
    dp_act = mm_nt(dyp, wpu, token, out_dtype=BF16, name="pool_dp")
    dz, dpool_scale, dpm = pool_bwd(dp_act, pooled, pm_full, pool_scale, dz, name="pool_bwd")
    dwau = mm_tn(attn, dya, name="attn_dwup")
    dattn = mm_nt(dya, wau, token, out_dtype=BF16, name="attn_dout")
    dz, dk, dv, dqg, dkg, dsinks, dbias = attn_bwd(z, dattn, qg2, kg2, sink_rows, bias, dz, name="attn_bwd")
    dz = lax.dynamic_update_slice(dz, jnp.concatenate([dk, dv], axis=1), (0, COL_K))
    drel = bias_table_bwd(dbias.reshape(N_HEADS, -1), one_hot, name="bias_table_bwd").T
    dwin_z = mm_tn(h2, dz, name="mix_dwin")
    z_start = np.cumsum([0] + [w for _, w in Z_PIECES[:-1]])
    in_w_in_order = sorted(zip(Z_PIECES, z_start))
    dwin = jnp.concatenate([dwin_z[:, int(at):int(at) + w] for (_, w), at in in_w_in_order], axis=1)
    mix_grads = dict(w_in=columns_in(dwin),
                     pool_mix=jnp.transpose(dpm.astype(BF16).reshape(POOL_GROUPS, N_DEV, pm_rows, POOL_GROUP_W),
                                            (1, 0, 2, 3)).reshape(N_DEV, POOL_GROUPS * pm_rows, POOL_GROUP_W),
                     w_pool_up=columns_in(dwpu), w_attn_up=columns_in(dwau), w_o=dwo)
    token = scatter(mix_keys, [mix_grads[k] for k in mix_keys], "scatter_mix")
    dh2 = mm_nt(dz, w_in_z, token, out_dtype=BF16, name="mix_dh")
    dx1, dsh2, dsc2, dg_mix, df1, dgate1 = norm_modulate_bwd(x1, g_mix, mod[4], dh2, dx2, (saved1[4], mod[2], 0.5),
                                                           name="mix_norm_bwd")
    dx0, dmod1, dg_ffn1 = _ffn_bwd(dx1, df1, dgate1, x0, g_ffn1, mod[1], wgu1, wd1, saved1, token, scatter, True, "ffn1")

    small = [("b_ada", b_ada, m_b_ada, v_b_ada, jnp.concatenate(list(dmod1 + (dsh2, dsc2, dgate2) + dmod3), axis=1)),
             ("g_ffn1", g_ffn1, m_g_ffn1, v_g_ffn1, dg_ffn1), ("g_mix", g_mix, m_g_mix, v_g_mix, dg_mix),
             ("g_ffn2", g_ffn2, m_g_ffn2, v_g_ffn2, dg_ffn2),
             ("pool_scale", pool_scale, m_pool_scale, v_pool_scale, dpool_scale),
             ("q_gain", q_gain, m_q_gain, v_q_gain, dqg[:, :HEAD_DIM]), ("k_gain", k_gain, m_k_gain, v_k_gain, dkg[:, :HEAD_DIM]),
             ("sinks", sinks, m_sinks, v_sinks, dsinks), ("rel_bias", rel_bias, m_rel_bias, v_rel_bias, drel)]
    n_small = sum(t[1].size for t in small)
    width = n_small + 1 + (-(n_small + 1) % (8 * LANE))

    def flat(arrs):
        row = jnp.concatenate([a.reshape(1, -1) for a in arrs], axis=1)
        return jnp.pad(row, ((0, 0), (0, width - row.shape[1])))

    small_parts, _ = all_gather_small(flat([t[4] for t in small] + [loss_row[:, :1]]).reshape(-1, LANE), "gather_small_grads")
    small_parts = small_parts.reshape(N_DEV, 1, width)
    sg, sd, sm, sv = adamw_from_parts(small_parts, flat([t[1] for t in small]), flat([t[2] for t in small]),
                                      flat([t[3] for t in small]), name="adamw_small")
    loss = sg[0, n_small]

    dmod_all = small_parts[:, 0, :N_MOD * D_MODEL]
    dmod_mine = lax.dynamic_slice(dmod_all, (0, me * n_ada), (N_DEV, n_ada))
    c_t = jnp.pad(c_all.T, ((0, 0), (0, LANE - N_DEV)))
    ada_out = ada_bwd_adamw(c_t, jnp.pad(dmod_mine, ((0, LANE - N_DEV), (0, 0))), w_ada[0], m_w_ada[0], v_w_ada[0],
                            name="ada_bwd_adamw")

    res = {"w_ada": [o[None] for o in ada_out]}
    after = ada_out[0]
    for keys, handle, tag in pending:
        parts = exchange_wait(handle, after, slotted=True, name=f"{tag}_wait")
        for k, part in zip(keys, parts):
            w_, m_, v_ = big[k]
            update = adamw_from_transposed_parts if part.shape[1:] != shard2d[k] else adamw_from_parts
            outs = update(part, w_.reshape(shard2d[k]), m_.reshape(shard2d[k]), v_.reshape(shard2d[k]), name=f"adamw_{k}")
            res[k] = [o.reshape(w_.shape) for o in outs]
            after = outs[0]
    off = 0
    for k, w_, _, _, _ in small:
        res[k] = [o[0, off:off + w_.size].reshape(w_.shape) for o in (sg, sd, sm, sv)]
        off += w_.size
    order = ["w_ada", "b_ada", "g_ffn1", "w_ffn1_gu", "w_ffn1_down", "g_mix", "w_in", "pool_mix", "pool_scale",
             "w_pool_up", "q_gain", "k_gain", "sinks", "rel_bias", "w_attn_up", "w_o", "g_ffn2", "w_ffn2_gu", "w_ffn2_down"]
    return (loss, dx0[None], *[res[k][0] for k in order], *[res[k][1] for k in order],
            *[res[k][2] for k in order], *[res[k][3] for k in order])
```

```python
import numpy as np
import jax
import jax.numpy as jnp
from jax import lax
from jax.experimental import pallas as pl
from jax.experimental.pallas import tpu as pltpu

F32, BF16 = jnp.float32, jnp.bfloat16
MESH_ID = pl.DeviceIdType.MESH

N_DEV = 8
D_MODEL = 2048
N_MOD = 9
POOL_WINDOWS = (2, 4, 8, 16)
POOL_GROUPS = 4
POOL_GROUP_W = D_MODEL // 8
POOL_W = POOL_GROUPS * POOL_GROUP_W
POOL_HALO = 16
HEAD_DIM = 64
N_HEADS = 16
N_KV = 2
GQA = N_HEADS // N_KV
BLK = 128
NUM_BUCKETS = 32
MAX_EXACT = 16
REL_MAX_DIST = 128
EPS = 1e-6
NEG_INF = -1e30
ATT_W = N_HEADS * HEAD_DIM
KV_W = N_KV * HEAD_DIM
IN_W = POOL_W + ATT_W + 2 * KV_W + 2 * D_MODEL
GATE_TILE = 512
W_IN_PARTS = dict(u=(0, POOL_W), q=(POOL_W, ATT_W), k=(POOL_W + ATT_W, KV_W), v=(POOL_W + ATT_W + KV_W, KV_W),
                  ga=(POOL_W + ATT_W + 2 * KV_W, D_MODEL), gb=(POOL_W + ATT_W + 2 * KV_W + D_MODEL, D_MODEL))
Z_PIECES = [(W_IN_PARTS[p][0] + j * GATE_TILE, GATE_TILE) for j in range(D_MODEL // GATE_TILE) for p in ("ga", "gb")]
Z_PIECES += [W_IN_PARTS[p] for p in ("u", "q", "k", "v")]
COL_U, COL_Q, COL_K, COL_V = 2 * D_MODEL, 2 * D_MODEL + POOL_W, 2 * D_MODEL + POOL_W + ATT_W, 2 * D_MODEL + POOL_W + ATT_W + KV_W
LANE = 128

ADAM_LR = 0.001
ADAM_B1 = 0.9
ADAM_B2 = 0.999
ADAM_EPS = 1e-08
ADAM_WD = 0.01
ADAM_STEP = 10

NN = ((1,), (0,))
NT = ((1,), (1,))
TN = ((0,), (0,))


def _dot(a, b, dims, precision=None):
    return lax.dot_general(a, b, (dims, ((), ())), preferred_element_type=F32, precision=precision)


def _tile(n, pref, unit):
    t = (min(pref, n) // unit) * unit
    while t >= unit:
        if n % t == 0:
            return t
        t -= unit
    return n


def _params(*sem):
    return pltpu.CompilerParams(dimension_semantics=sem)


def _sigmoid(x):
    return 1.0 / (1.0 + jnp.exp(-x))


def _mesh_pos():
    return lax.axis_index("x"), lax.axis_index("y"), lax.axis_index("c")


def _slot(p):
    return 4 * p[0] + 2 * p[1] + p[2]


def all_gather_small(x_shard, name):
    m_per, n = x_shard.shape

    def body(x_ref, out_ref, token, send_sems, recv_sems, local_sem):
        x, y, c = _mesh_pos()
        me, sibling = (x, y, c), (x, y, 1 - c)
        chips = [(1 - x, y), (x, 1 - y), (1 - x, 1 - y)]
        token[...] = jnp.zeros_like(token)

        def rows(p):
            return out_ref.at[pl.ds(_slot(p) * m_per, m_per), :]

        def copy(k, block, to, src=None):
            return pltpu.make_async_remote_copy(
                src_ref=rows(block) if src is None else src, dst_ref=rows(block),
                send_sem=send_sems.at[k], recv_sem=recv_sems.at[k], device_id=to, device_id_type=MESH_ID)

        mine = pltpu.make_async_copy(x_ref, rows(me), local_sem)
        mine.start()
        first = [copy(0, me, sibling, src=x_ref)]
        first += [copy(1 + j, me, (*chip, c), src=x_ref) for j, chip in enumerate(chips)]
        for cp in first:
            cp.start()
        passed = [copy(4 + j, (*chip, c), sibling) for j, chip in enumerate(chips)]
        for j, chip in enumerate(chips):
            copy(1 + j, (*chip, c), me).wait_recv()
            passed[j].start()
        copy(0, sibling, me).wait_recv()
        for j, chip in enumerate(chips):
            copy(4 + j, (*chip, 1 - c), me).wait_recv()
        for cp in first + passed:
            cp.wait_send()
        mine.wait()

    return pl.pallas_call(
        body, name=name,
        out_shape=[jax.ShapeDtypeStruct((N_DEV * m_per, n), x_shard.dtype), jax.ShapeDtypeStruct((8, LANE), F32)],
        in_specs=[pl.BlockSpec(memory_space=pltpu.VMEM)],
        out_specs=[pl.BlockSpec(memory_space=pltpu.VMEM)] * 2,
        scratch_shapes=[pltpu.SemaphoreType.DMA((7,)), pltpu.SemaphoreType.DMA((7,)), pltpu.SemaphoreType.DMA],
    )(x_shard)


def all_gather_hbm(shard, name):
    rows = shard.shape[0]
    half = rows // 2
    assert rows == 2 * half and half % 16 == 0, shard.shape

    def body(in_ref, out_ref, token, send_sems, recv_sems, local_sem):
        x, y, c = _mesh_pos()
        me, sibling = (x, y, c), (x, y, 1 - c)
        xn, yn, dg = (1 - x, y), (x, 1 - y), (1 - x, 1 - y)
        top, bottom = pl.ds(0, half), pl.ds(half, half)
        token[...] = jnp.zeros_like(token)

        def copy(k, block, to, part=None, src=None):
            dst = out_ref.at[_slot(block)] if part is None else out_ref.at[_slot(block), part]
            return pltpu.make_async_remote_copy(
                src_ref=dst if src is None else src, dst_ref=dst, send_sem=send_sems.at[k], recv_sem=recv_sems.at[k],
                device_id=to, device_id_type=MESH_ID)

        mine = pltpu.make_async_copy(in_ref, out_ref.at[_slot(me)], local_sem)
        mine.start()
        sends = [copy(0, me, sibling, src=in_ref), copy(1, me, (*xn, c), src=in_ref), copy(2, me, (*yn, c), src=in_ref)]
        for cp in sends:
            cp.start()
        copy(1, (*xn, c), me).wait_recv()
        sends += [copy(3, (*xn, c), sibling), copy(4, (*xn, c), (*yn, c), top)]
        sends[-2].start()
        sends[-1].start()
        copy(2, (*yn, c), me).wait_recv()
        sends += [copy(5, (*yn, c), sibling), copy(6, (*yn, c), (*xn, c), bottom)]
        sends[-2].start()
        sends[-1].start()
        copy(4, (*dg, c), me, top).wait_recv()
        copy(6, (*dg, c), me, bottom).wait_recv()
        sends.append(copy(7, (*dg, c), sibling))
        sends[-1].start()
        copy(0, sibling, me).wait_recv()
        copy(3, (*xn, 1 - c), me).wait_recv()
        copy(5, (*yn, 1 - c), me).wait_recv()
        copy(7, (*dg, 1 - c), me).wait_recv()
        for cp in sends:
            cp.wait_send()
        mine.wait()

    any_spec = pl.BlockSpec(memory_space=pl.ANY)
    return pl.pallas_call(
        body, name=name,
        out_shape=[jax.ShapeDtypeStruct((N_DEV,) + shard.shape, shard.dtype), jax.ShapeDtypeStruct((8, LANE), F32)],
        in_specs=[any_spec], out_specs=[any_spec, pl.BlockSpec(memory_space=pltpu.VMEM)],
        scratch_shapes=[pltpu.SemaphoreType.DMA((8,)), pltpu.SemaphoreType.DMA((8,)), pltpu.SemaphoreType.DMA],
    )(shard)


def _peer_list(x, y, c):
    return [((1 - x) if k & 4 else x, (1 - y) if k & 2 else y, (1 - c) if k & 1 else c) for k in range(1, N_DEV)]


def _exchange_copies(srcs, lands, send_sems, recv_sems, slotted, arriving):
    x, y, c = _mesh_pos()
    me = _slot((x, y, c))
    copies = []
    for a in range(len(srcs)):
        for k, peer in enumerate(_peer_list(x, y, c)):
            copies.append(pltpu.make_async_remote_copy(
                src_ref=srcs[a].at[_slot(peer)] if slotted else srcs[a],
                dst_ref=lands[a].at[_slot(peer) if arriving else me],
                send_sem=send_sems.at[7 * a + k], recv_sem=recv_sems.at[7 * a + k],
                device_id=peer, device_id_type=MESH_ID))
    return copies


def unwritten_hbm(shapes, dtype, name):
    def body(*refs):
        pass

    return pl.pallas_call(
        body, name=name, out_shape=[jax.ShapeDtypeStruct(s, dtype) for s in shapes],
        out_specs=[pl.BlockSpec(memory_space=pl.ANY)] * len(shapes),
    )()


HBM_SPEC = pl.BlockSpec(memory_space=pltpu.HBM)
SEM_SPEC = pl.BlockSpec(memory_space=pltpu.SEMAPHORE)
DATAFLOW = pltpu.SideEffectType.DATAFLOW_SIDE_EFFECTING


def exchange_start(srcs, lands, *, slotted, name):
    n = len(srcs)

    def body(*refs):
        ins = refs[:2 * n]
        send_sems, recv_sems = refs[2 * n], refs[2 * n + 1]
        token = refs[-1]
        for cp in _exchange_copies(ins[:n], ins[n:], send_sems, recv_sems, slotted, False):
            cp.start()
        token[...] = jnp.zeros_like(token)

    operands = [pltpu.with_memory_space_constraint(v, pltpu.HBM) for v in list(srcs) + list(lands)]
    out = pl.pallas_call(
        body, name=name,
        out_shape=(pltpu.SemaphoreType.DMA((7 * n,)), pltpu.SemaphoreType.DMA((7 * n,)),
                   *[pltpu.HBM(v.shape, v.dtype) for v in operands], jax.ShapeDtypeStruct((8, LANE), F32)),
        in_specs=[HBM_SPEC] * (2 * n),
        out_specs=(SEM_SPEC, SEM_SPEC, *[HBM_SPEC] * (2 * n), pl.BlockSpec(memory_space=pltpu.VMEM)),
        input_output_aliases={i: 2 + i for i in range(2 * n)},
        compiler_params=pltpu.CompilerParams(has_side_effects=DATAFLOW),
    )(*operands)
    return out[0], out[1], list(out[2:2 + n]), list(out[2 + n:2 + 2 * n]), out[-1]


def exchange_wait(handle, after, *, slotted, name):
    send_sems, recv_sems, srcs, lands, _ = handle
    n = len(srcs)

    def body(*refs):
        ins = refs[:2 * n]
        for cp in _exchange_copies(ins[:n], ins[n:], refs[2 * n], refs[2 * n + 1], slotted, True):
            cp.wait_send()
            cp.wait_recv()

    out = pl.pallas_call(
        body, name=name,
        out_shape=tuple(pltpu.HBM(v.shape, v.dtype) for v in srcs + lands),
        in_specs=[HBM_SPEC] * (2 * n) + [SEM_SPEC, SEM_SPEC, pl.BlockSpec(memory_space=pl.ANY)],
        out_specs=[HBM_SPEC] * (2 * n),
        input_output_aliases={i: i for i in range(2 * n)},
        compiler_params=pltpu.CompilerParams(has_side_effects=DATAFLOW),
    )(*srcs, *lands, send_sems, recv_sems, after)
    return list(out[n:])


VMEM_BLOCK_BUDGET = 46 * 2 ** 20
ROW_TILE, COL_TILE = 1024, 1408
ACC_BUDGET = 12 * 2 ** 20


def _mm_tiles(m, n, row_bytes, col_bytes, elem_bytes):
    tm, tn = _tile(m, ROW_TILE, 16), _tile(n, COL_TILE, LANE)
    while 2 * (tm * row_bytes + tn * col_bytes + tm * tn * elem_bytes) > VMEM_BLOCK_BUDGET:
        narrower = _tile(n, max(tn - LANE, LANE), LANE)
        if tn > 512 and narrower < tn:
            tn = narrower
        else:
            tm //= 2
    return tm, tn


def mm_nn(a, w, *, out_dtype, name):
    m, k = a.shape
    n = w.shape[1]
    tm, tn = _mm_tiles(m, n, 2 * k, 2 * k, jnp.dtype(out_dtype).itemsize)

    def body(a_ref, w_ref, o_ref):
        o_ref[...] = _dot(a_ref[...], w_ref[...], NN).astype(o_ref.dtype)

    return pl.pallas_call(
        body, name=name, grid=(n // tn, m // tm),
        in_specs=[pl.BlockSpec((tm, k), lambda j, i: (i, 0)), pl.BlockSpec((k, tn), lambda j, i: (0, j))],
        out_specs=pl.BlockSpec((tm, tn), lambda j, i: (i, j)),
        out_shape=jax.ShapeDtypeStruct((m, n), out_dtype),
        compiler_params=_params("parallel", "parallel"),
    )(a, w)


def mm_nn_residual(a, w, x_in, gate, coef, *, name):
    m, k = a.shape
    n = w.shape[1]
    tm, tn = _mm_tiles(m, n, 2 * k, 2 * k, 4 + 4 + 2)

    def body(a_ref, w_ref, x_ref, g_ref, o_ref, f_ref):
        f = _dot(a_ref[...], w_ref[...], NN)
        f_ref[...] = f.astype(BF16)
        o_ref[...] = x_ref[...] + (coef * g_ref[...]) * f

    return pl.pallas_call(
        body, name=name, grid=(n // tn, m // tm),
        in_specs=[pl.BlockSpec((tm, k), lambda j, i: (i, 0)), pl.BlockSpec((k, tn), lambda j, i: (0, j)),
                  pl.BlockSpec((tm, tn), lambda j, i: (i, j)), pl.BlockSpec((1, tn), lambda j, i: (0, j))],
        out_specs=[pl.BlockSpec((tm, tn), lambda j, i: (i, j)), pl.BlockSpec((tm, tn), lambda j, i: (i, j))],
        out_shape=[jax.ShapeDtypeStruct((m, n), F32), jax.ShapeDtypeStruct((m, n), BF16)],
        compiler_params=_params("parallel", "parallel"),
    )(a, w, x_in, gate)


def mm_nn_residual_loss(a, w, x_in, gate, coef, target, *, name):
    m, k = a.shape
    n = w.shape[1]
    tm, tn = _mm_tiles(m, n, 2 * k, 2 * k, 4 + 4 + 4 + 2)

    def body(a_ref, w_ref, x_ref, g_ref, t_ref, dy_ref, df_ref, dg_ref, l_ref):
        f = _dot(a_ref[...], w_ref[...], NN)
        err = x_ref[...] + (coef * g_ref[...]) * f - t_ref[...]
        dy = err * (1.0 / n)
        dy_ref[...] = dy
        df_ref[...] = ((coef * g_ref[...]) * dy).astype(BF16)
        dgate = coef * jnp.sum(dy * f, axis=0, keepdims=True)
        part = jnp.sum(jnp.sum(err * err, axis=0, keepdims=True), axis=1, keepdims=True) * (0.5 / n)

        @pl.when(pl.program_id(1) == 0)
        def _():
            dg_ref[...] = jnp.zeros_like(dg_ref)

        @pl.when((pl.program_id(0) == 0) & (pl.program_id(1) == 0))
        def _():
            l_ref[...] = jnp.zeros_like(l_ref)

        dg_ref[...] += dgate
        l_ref[...] += jnp.broadcast_to(part, l_ref.shape)

    blk = pl.BlockSpec((tm, tn), lambda j, i: (i, j))
    vec = pl.BlockSpec((1, tn), lambda j, i: (0, j))
    return pl.pallas_call(
        body, name=name, grid=(n // tn, m // tm),
        in_specs=[pl.BlockSpec((tm, k), lambda j, i: (i, 0)), pl.BlockSpec((k, tn), lambda j, i: (0, j)), blk, vec, blk],
        out_specs=[blk, blk, vec, pl.BlockSpec((1, LANE), lambda j, i: (0, 0))],
        out_shape=[jax.ShapeDtypeStruct((m, n), F32), jax.ShapeDtypeStruct((m, n), BF16),
                   jax.ShapeDtypeStruct((1, n), F32), jax.ShapeDtypeStruct((1, LANE), F32)],
        compiler_params=_params("arbitrary", "arbitrary"),
    )(a, w, x_in, gate, target)


TOKEN_SPEC = pl.BlockSpec((8, LANE), lambda *_: (0, 0))


def mm_nt(a, w, token, *, out_dtype, name):
    m, k = a.shape
    n = w.shape[0]
    tm, tn = _mm_tiles(m, n, 2 * k, 2 * k, jnp.dtype(out_dtype).itemsize)

    def body(a_ref, w_ref, token_ref, o_ref):
        o_ref[...] = _dot(a_ref[...], w_ref[...], NT).astype(o_ref.dtype)

    return pl.pallas_call(
        body, name=name, grid=(n // tn, m // tm),
        in_specs=[pl.BlockSpec((tm, k), lambda j, i: (i, 0)), pl.BlockSpec((tn, k), lambda j, i: (j, 0)), TOKEN_SPEC],
        out_specs=pl.BlockSpec((tm, tn), lambda j, i: (i, j)),
        out_shape=jax.ShapeDtypeStruct((m, n), out_dtype),
        compiler_params=_params("parallel", "parallel"),
    )(a, w, token)


def mm_nt_halves(a_lo, a_hi, w3, token, *, name):
    m = a_lo.shape[0]
    n_blk, n, tn = w3.shape
    half = n_blk // 2
    tm = _tile(m, ROW_TILE, 16)
    deep = (tn // (2 * LANE)) * 2 * LANE
    per = tn // LANE

    def body(lo_ref, hi_ref, lo_rem, hi_rem, w_ref, w_rem, token_ref, o_ref):
        j = pl.program_id(1)

        @pl.when(j == 0)
        def _():
            o_ref[...] = jnp.zeros_like(o_ref)

        def step(a_ref, a_rem):
            o_ref[...] += _dot(a_ref[:, :deep], w_ref[:, :deep], NT)

            @pl.when(j % 2 == 1)
            def _():
                a2 = jnp.concatenate([a_rem[...], a_ref[:, deep:]], axis=1)
                w2 = jnp.concatenate([w_rem[...], w_ref[:, deep:]], axis=1)
                o_ref[...] += _dot(a2, w2, NT)

        @pl.when(j < half)
        def _():
            step(lo_ref, lo_rem)

        @pl.when(j >= half)
        def _():
            step(hi_ref, hi_rem)

    even = lambda j: j - j % 2
    return pl.pallas_call(
        body, name=name, grid=(m // tm, n_blk),
        in_specs=[pl.BlockSpec((tm, tn), lambda i, j: (i, jnp.minimum(j, half - 1))),
                  pl.BlockSpec((tm, tn), lambda i, j: (i, jnp.maximum(j - half, 0))),
                  pl.BlockSpec((tm, LANE), lambda i, j: (i, jnp.minimum(even(j), half - 2) * per + per - 1)),
                  pl.BlockSpec((tm, LANE), lambda i, j: (i, jnp.maximum(even(j) - half, 0) * per + per - 1)),
                  pl.BlockSpec((None, n, tn), lambda i, j: (j, 0, 0)),
                  pl.BlockSpec((None, n, LANE), lambda i, j: (even(j), 0, per - 1)), TOKEN_SPEC],
        out_specs=pl.BlockSpec((tm, n), lambda i, j: (i, 0)),
        out_shape=jax.ShapeDtypeStruct((m, n), F32),
        compiler_params=_params("parallel", "arbitrary"),
    )(a_lo, a_hi, a_lo, a_hi, w3, w3, token)


def mm_tn(a, dy, *, name):
    s, k = a.shape
    n = dy.shape[1]
    ts = _tile(s, ROW_TILE, 16)
    tk = k if k <= 2048 else _tile(k, COL_TILE, LANE)
    tn = _tile(n, ACC_BUDGET // (4 * tk), LANE)
    n_steps = s // ts

    def body(a_ref, dy_ref, o_ref, acc_ref):
        t = pl.program_id(2)

        @pl.when(t == 0)
        def _():
            acc_ref[...] = jnp.zeros_like(acc_ref)

        acc_ref[...] += _dot(a_ref[...], dy_ref[...], TN)

        @pl.when(t == n_steps - 1)
        def _():
            o_ref[...] = acc_ref[...].astype(BF16)

    return pl.pallas_call(
        body, name=name, grid=(k // tk, n // tn, n_steps),
        in_specs=[pl.BlockSpec((ts, tk), lambda kk, j, t: (t, kk)), pl.BlockSpec((ts, tn), lambda kk, j, t: (t, j))],
        out_specs=pl.BlockSpec((tk, tn), lambda kk, j, t: (kk, j)),
        out_shape=jax.ShapeDtypeStruct((k, n), BF16),
        scratch_shapes=[pltpu.VMEM((tk, tn), F32)],
        compiler_params=_params("parallel", "parallel", "arbitrary"),
    )(a, dy)


def mm_tn_halves(a, dy_lo, dy_hi, n_blocks, *, name):
    s, k = a.shape
    half = n_blocks // 2
    tn = dy_lo.shape[1] // half
    ts = _tile(s, ROW_TILE, 16)
    n_steps = s // ts

    def body(a_ref, lo_ref, hi_ref, o_ref, acc_ref):
        j, t = pl.program_id(0), pl.program_id(1)

        @pl.when(t == 0)
        def _():
            acc_ref[...] = jnp.zeros_like(acc_ref)

        @pl.when(j < half)
        def _():
            acc_ref[...] += _dot(lo_ref[...], a_ref[...], TN)

        @pl.when(j >= half)
        def _():
            acc_ref[...] += _dot(hi_ref[...], a_ref[...], TN)

        @pl.when(t == n_steps - 1)
        def _():
            o_ref[...] = acc_ref[...].astype(BF16)

    return pl.pallas_call(
        body, name=name, grid=(n_blocks, n_steps),
        in_specs=[pl.BlockSpec((ts, k), lambda j, t: (t, 0)),
                  pl.BlockSpec((ts, tn), lambda j, t: (jnp.where(j < half, t, n_steps - 1), jnp.minimum(j, half - 1))),
                  pl.BlockSpec((ts, tn), lambda j, t: (jnp.where(j < half, 0, t), jnp.maximum(j - half, 0)))],
        out_specs=pl.BlockSpec((None, tn, k), lambda j, t: (j, 0, 0)),
        out_shape=jax.ShapeDtypeStruct((n_blocks, tn, k), BF16),
        scratch_shapes=[pltpu.VMEM((tn, k), F32)],
        compiler_params=_params("parallel", "arbitrary"),
    )(a, dy_lo, dy_hi)


def ffn_dgu(df, wd, g, u, token, *, name):
    m, k = df.shape
    n = wd.shape[0]
    tm, tn = _mm_tiles(m, n, 2 * k, 2 * k, 4 * 2)

    def body(df_ref, w_ref, g_ref, u_ref, token_ref, dg_ref, du_ref):
        da = _dot(df_ref[...], w_ref[...], NT)
        gg, uu = g_ref[...].astype(F32), u_ref[...].astype(F32)
        sg = _sigmoid(gg)
        dg_ref[...] = (da * uu * (sg * (1 + gg * (1 - sg)))).astype(BF16)
        du_ref[...] = (da * (gg * sg)).astype(BF16)

    blk = pl.BlockSpec((tm, tn), lambda j, i: (i, j))
    out = jax.ShapeDtypeStruct((m, n), BF16)
    return pl.pallas_call(
        body, name=name, grid=(n // tn, m // tm),
        in_specs=[pl.BlockSpec((tm, k), lambda j, i: (i, 0)), pl.BlockSpec((tn, k), lambda j, i: (j, 0)), blk, blk, TOKEN_SPEC],
        out_specs=[blk, blk], out_shape=[out, out],
        compiler_params=_params("parallel", "parallel"),
    )(df, wd, g, u, token)


def ffn_up(h, wgu3, *, name, tm=512):
    s, k = h.shape
    n = wgu3.shape[2]
    half = wgu3.shape[0] // 2
    tm = _tile(s, tm, 16)

    def body(h_ref, wg_ref, wu_ref, g_ref, u_ref, a_ref, w_pair):
        @pl.when(pl.program_id(1) == 0)
        def _():
            w_pair[:, :n] = wg_ref[...]
            w_pair[:, n:] = wu_ref[...]

        gu = _dot(h_ref[...], w_pair[...], NN)
        g, u = gu[:, :n], gu[:, n:]
        g_ref[...] = g.astype(BF16)
        u_ref[...] = u.astype(BF16)
        a_ref[...] = (g * _sigmoid(g) * u).astype(BF16)

    out = jax.ShapeDtypeStruct((s, half * n), BF16)
    blk = pl.BlockSpec((tm, n), lambda j, i: (i, j))
    return pl.pallas_call(
        body, name=name, grid=(half, s // tm),
        in_specs=[pl.BlockSpec((tm, k), lambda j, i: (i, 0)),
                  pl.BlockSpec((None, k, n), lambda j, i: (j, 0, 0)),
                  pl.BlockSpec((None, k, n), lambda j, i: (j + half, 0, 0))],
        out_specs=[blk, blk, blk], out_shape=[out, out, out],
        scratch_shapes=[pltpu.VMEM((k, 2 * n), BF16)],
        compiler_params=_params("parallel", "arbitrary"),
    )(h, wgu3, wgu3)


def _row_spec(ts, width, col=0):
    return pl.BlockSpec((ts, width), lambda i: (i, col))


def _vec_spec(width):
    return pl.BlockSpec((1, width), lambda i: (0, 0))


def _accumulate(ref, value):
    i = pl.program_id(0)

    @pl.when(i == 0)
    def _():
        ref[...] = value

    @pl.when(i > 0)
    def _():
        ref[...] += value


def norm_modulate(x, g, shift, scale, token, *, name, ts=512):
    s, d = x.shape
    ts = _tile(s, ts, 16)

    def body(x_ref, g_ref, sh_ref, sc_ref, token_ref, h_ref):
        xx = x_ref[...]
        r = lax.rsqrt(jnp.mean(xx * xx, axis=-1, keepdims=True) + EPS)
        h_ref[...] = ((xx * r) * g_ref[...] * (1 + sc_ref[...]) + sh_ref[...]).astype(BF16)

    return pl.pallas_call(
        body, name=name, grid=(s // ts,),
        in_specs=[_row_spec(ts, d), _vec_spec(d), _vec_spec(d), _vec_spec(d), TOKEN_SPEC],
        out_specs=_row_spec(ts, d), out_shape=jax.ShapeDtypeStruct((s, d), BF16),
        compiler_params=_params("parallel"),
    )(x, g, shift, scale, token)


def norm_modulate_bwd(x, g, scale, dh, dx_out, below=None, *, name, ts=256):
    s, d = x.shape
    ts = _tile(s, ts, 16)
    coef = None if below is None else below[2]

    def body(*refs):
        x_ref, g_ref, sc_ref, dh_ref, dxo_ref = refs[:5]
        dx_ref, dsh_ref, dsc_ref, dg_ref = refs[-6:-2] if below else refs[-4:]
        xx, dh_ = x_ref[...], dh_ref[...].astype(F32)
        r = lax.rsqrt(jnp.mean(xx * xx, axis=-1, keepdims=True) + EPS)
        xh = xx * r
        dn = dh_ * (1 + sc_ref[...])
        dxh = dn * g_ref[...]
        dx = dxo_ref[...] + r * (dxh - xh * jnp.mean(dxh * xh, axis=-1, keepdims=True))
        dx_ref[...] = dx
        _accumulate(dsh_ref, jnp.sum(dh_, axis=0, keepdims=True))
        _accumulate(dsc_ref, jnp.sum(dh_ * (xh * g_ref[...]), axis=0, keepdims=True))
        _accumulate(dg_ref, jnp.sum(dn * xh, axis=0, keepdims=True))
        if below:
            f_ref, gate_ref, df_ref, dgate_ref = refs[5], refs[6], refs[-2], refs[-1]
            df_ref[...] = ((coef * gate_ref[...]) * dx).astype(BF16)
            _accumulate(dgate_ref, coef * jnp.sum(dx * f_ref[...].astype(F32), axis=0, keepdims=True))

    vec = jax.ShapeDtypeStruct((1, d), F32)
    extra_in, extra_out, extra_shape, extra_args = [], [], [], []
    if below:
        extra_in, extra_args = [_row_spec(ts, d), _vec_spec(d)], [below[0], below[1]]
        extra_out, extra_shape = [_row_spec(ts, d), _vec_spec(d)], [jax.ShapeDtypeStruct((s, d), BF16), vec]
    return pl.pallas_call(
        body, name=name, grid=(s // ts,),
        in_specs=[_row_spec(ts, d), _vec_spec(d), _vec_spec(d), _row_spec(ts, d), _row_spec(ts, d)] + extra_in,
        out_specs=[_row_spec(ts, d), _vec_spec(d), _vec_spec(d), _vec_spec(d)] + extra_out,
        out_shape=[jax.ShapeDtypeStruct((s, d), F32), vec, vec, vec] + extra_shape,
        compiler_params=_params("arbitrary"),
    )(x, g, scale, dh, dx_out, *extra_args)


def mixers_up_merge(p_act, attn, wpu, wau, z, *, name):
    m, k = p_act.shape
    n = wpu.shape[1]
    tm, tn = _tile(m, ROW_TILE, 16), GATE_TILE

    def body(p_ref, a_ref, wp_ref, wa_ref, gate_ref, yp_ref, ya_ref, o_ref):
        yp = _dot(p_ref[...], wp_ref[...], NN).astype(BF16)
        ya = _dot(a_ref[...], wa_ref[...], NN).astype(BF16)
        yp_ref[...] = yp
        ya_ref[...] = ya
        sa = _sigmoid(gate_ref[:, :tn].astype(F32))
        sb = _sigmoid(gate_ref[:, tn:].astype(F32))
        o_ref[...] = (sa * yp + sb * ya).astype(BF16)

    rows = pl.BlockSpec((tm, k), lambda j, i: (i, 0))
    cols = pl.BlockSpec((k, tn), lambda j, i: (0, j))
    blk = pl.BlockSpec((tm, tn), lambda j, i: (i, j))
    out = jax.ShapeDtypeStruct((m, n), BF16)
    return pl.pallas_call(
        body, name=name, grid=(n // tn, m // tm),
        in_specs=[rows, rows, cols, cols, pl.BlockSpec((tm, 2 * tn), lambda j, i: (i, j))],
        out_specs=[blk, blk, blk], out_shape=[out, out, out],
        compiler_params=_params("parallel", "parallel"),
    )(p_act, attn, wpu, wau, z)


def merge_bwd(d_o, wo, z, y_pool, y_attn, token, *, name):
    m, k = d_o.shape
    n = wo.shape[0]
    tm, tn = _tile(m, ROW_TILE, 16), GATE_TILE

    def body(do_ref, w_ref, gate_ref, yp_ref, ya_ref, token_ref, dyp_ref, dya_ref, dz_ref):
        dm = _dot(do_ref[...], w_ref[...], NT)
        sa = _sigmoid(gate_ref[:, :tn].astype(F32))
        sb = _sigmoid(gate_ref[:, tn:].astype(F32))
        dyp_ref[...] = (dm * sa).astype(BF16)
        dya_ref[...] = (dm * sb).astype(BF16)
        dz_ref[:, :tn] = (dm * yp_ref[...] * (sa * (1 - sa))).astype(BF16)
        dz_ref[:, tn:] = (dm * ya_ref[...] * (sb * (1 - sb))).astype(BF16)

    blk = pl.BlockSpec((tm, tn), lambda j, i: (i, j))
    pair = pl.BlockSpec((tm, 2 * tn), lambda j, i: (i, j))
    out = jax.ShapeDtypeStruct((m, n), BF16)
    return pl.pallas_call(
        body, name=name, grid=(n // tn, m // tm),
        in_specs=[pl.BlockSpec((tm, k), lambda j, i: (i, 0)), pl.BlockSpec((tn, k), lambda j, i: (j, 0)),
                  pair, blk, blk, TOKEN_SPEC],
        out_specs=[blk, blk, pair], out_shape=[out, out, jax.ShapeDtypeStruct(z.shape, BF16)],
        compiler_params=_params("parallel", "parallel"),
    )(d_o, wo, z, y_pool, y_attn, token)


def _window_counts(t0, rows):
    t1 = (t0 + 1 + lax.broadcasted_iota(jnp.int32, (rows, 1), 0)).astype(F32)
    return [jnp.minimum(t1, float(w)) for w in POOL_WINDOWS]


def pool_fwd(z, pool_mix, pool_scale, *, name, ts=256):
    s = z.shape[0]
    ts = _tile(s, ts, 16)
    per = ts // POOL_HALO

    def body(u_ref, halo_ref, pm_ref, ps_ref, pooled_ref, p_ref):
        i = pl.program_id(0)
        u = u_ref[...].astype(F32)
        halo = jnp.where(i > 0, halo_ref[...].astype(F32), 0.0)
        run = jnp.concatenate([halo, u], axis=0)
        sums, width = [], 1
        for w in POOL_WINDOWS:
            while width < w:
                run = run + pltpu.roll(run, width, 0)
                width *= 2
            sums.append(run[POOL_HALO:])
        counts = _window_counts(i * ts, ts)
        for gi in range(POOL_GROUPS):
            cols = slice(gi * POOL_GROUP_W, (gi + 1) * POOL_GROUP_W)
            pooled = (sums[gi][:, cols] / counts[gi] - u[:, cols]).astype(BF16)
            pooled_ref[:, cols] = pooled
            p_ref[:, cols] = (_dot(pooled, pm_ref[gi], NN) * ps_ref[:, cols]).astype(BF16)

    out = jax.ShapeDtypeStruct((s, POOL_W), BF16)
    return pl.pallas_call(
        body, name=name, grid=(s // ts,),
        in_specs=[_row_spec(ts, POOL_W, COL_U // POOL_W),
                  pl.BlockSpec((POOL_HALO, POOL_W), lambda i: (jnp.maximum(i * per - 1, 0), COL_U // POOL_W)),
                  pl.BlockSpec((POOL_GROUPS, POOL_GROUP_W, POOL_GROUP_W), lambda i: (0, 0, 0)),
                  _vec_spec(POOL_W)],
        out_specs=[_row_spec(ts, POOL_W)] * 2, out_shape=[out, out],
        compiler_params=_params("parallel"),
    )(z, z, pool_mix, pool_scale)


def pool_bwd(dp, pooled, pool_mix, pool_scale, dz, *, name, ts=256):
    s = dp.shape[0]
    ts = _tile(s, ts, 16)
    per = ts // POOL_HALO
    n_steps = s // ts
    rows = ts + POOL_HALO

    def body(dp_ref, halo_ref, pooled_ref, pm_ref, ps_ref, dz_in_ref, du_ref, dps_ref, dpm_ref):
        i = pl.program_id(0)
        dp_main = dp_ref[...]
        halo = jnp.where(i < n_steps - 1, halo_ref[...], 0.0)
        dmixed = jnp.concatenate([dp_main, halo], axis=0) * ps_ref[...]
        counts = _window_counts(i * ts, rows)
        dps_parts = []
        for gi, w in enumerate(POOL_WINDOWS):
            cols = slice(gi * POOL_GROUP_W, (gi + 1) * POOL_GROUP_W)
            dmx = dmixed[:, cols].astype(BF16)
            pooled = pooled_ref[:, cols]
            mixed = _dot(pooled, pm_ref[gi], NN)
            dps_parts.append(jnp.sum(dp_main[:, cols] * mixed, axis=0, keepdims=True))
            dpm_g = _dot(pooled, dmx[:ts], TN)

            @pl.when(i == 0)
            def _():
                dpm_ref[gi] = dpm_g

            @pl.when(i > 0)
            def _():
                dpm_ref[gi] += dpm_g

            dpooled = _dot(dmx, pm_ref[gi], NT)
            run, width = dpooled / counts[gi], 1
            while width < w:
                run = run + pltpu.roll(run, rows - width, 0)
                width *= 2
            du_ref[:, cols] = (run[:ts] - dpooled[:ts]).astype(BF16)
        _accumulate(dps_ref, jnp.concatenate(dps_parts, axis=1))

    return pl.pallas_call(
        body, name=name, grid=(n_steps,),
        in_specs=[_row_spec(ts, POOL_W),
                  pl.BlockSpec((POOL_HALO, POOL_W), lambda i: (jnp.minimum((i + 1) * per, s // POOL_HALO - 1), 0)),
                  _row_spec(ts, POOL_W),
                  pl.BlockSpec((POOL_GROUPS, POOL_GROUP_W, POOL_GROUP_W), lambda i: (0, 0, 0)),
                  _vec_spec(POOL_W), pl.BlockSpec(memory_space=pl.ANY)],
        out_specs=[_row_spec(ts, POOL_W, COL_U // POOL_W), _vec_spec(POOL_W),
                   pl.BlockSpec((POOL_GROUPS, POOL_GROUP_W, POOL_GROUP_W), lambda i: (0, 0, 0))],
        out_shape=[jax.ShapeDtypeStruct(dz.shape, BF16), jax.ShapeDtypeStruct((1, POOL_W), F32),
                   jax.ShapeDtypeStruct((POOL_GROUPS, POOL_GROUP_W, POOL_GROUP_W), F32)],
        input_output_aliases={5: 0},
        compiler_params=_params("arbitrary"),
    )(dp, dp, pooled, pool_mix, pool_scale, dz)


def _bucket_one_hot():
    ql = np.arange(BLK)[:, None]
    j = np.arange(2 * BLK)[None, :]
    n = np.clip(BLK + ql - j, 0, None)
    nf = np.maximum(n, 1).astype(np.float32)
    large = MAX_EXACT + (np.log(nf / MAX_EXACT) / np.log(REL_MAX_DIST / MAX_EXACT)
                         * (NUM_BUCKETS - MAX_EXACT)).astype(np.int32)
    large = np.minimum(large, NUM_BUCKETS - 1)
    bucket = np.where(n < MAX_EXACT, n, large).astype(np.int32).reshape(-1)
    return (np.arange(NUM_BUCKETS)[:, None] == bucket[None, :]).astype(np.float32)


def bias_table(rel_bias_t, one_hot, *, name, tc=4096):
    n = one_hot.shape[1]

    def body(rb_ref, oh_ref, o_ref):
        o_ref[...] = _dot(rb_ref[...], oh_ref[...], NN, precision=lax.Precision.HIGHEST)

    return pl.pallas_call(
        body, name=name, grid=(n // tc,),
        in_specs=[pl.BlockSpec((N_HEADS, NUM_BUCKETS), lambda i: (0, 0)), pl.BlockSpec((NUM_BUCKETS, tc), lambda i: (0, i))],
        out_specs=pl.BlockSpec((N_HEADS, tc), lambda i: (0, i)),
        out_shape=jax.ShapeDtypeStruct((N_HEADS, n), F32),
        compiler_params=_params("parallel"),
    )(rel_bias_t, one_hot)


def bias_table_bwd(dbias, one_hot, *, name, tc=4096):
    n = one_hot.shape[1]

    def body(db_ref, oh_ref, o_ref):
        _accumulate(o_ref, _dot(db_ref[...], oh_ref[...], NT, precision=lax.Precision.HIGHEST))

    return pl.pallas_call(
        body, name=name, grid=(n // tc,),
        in_specs=[pl.BlockSpec((N_HEADS, tc), lambda i: (0, i)), pl.BlockSpec((NUM_BUCKETS, tc), lambda i: (0, i))],
        out_specs=pl.BlockSpec((N_HEADS, NUM_BUCKETS), lambda i: (0, 0)),
        out_shape=jax.ShapeDtypeStruct((N_HEADS, NUM_BUCKETS), F32),
        compiler_params=_params("arbitrary"),
    )(dbias, one_hot)


def _lane_half(shape):
    return lax.broadcasted_iota(jnp.int32, shape, len(shape) - 1) < HEAD_DIM


def _half_sums(v, first):
    s0 = jnp.sum(jnp.where(first, v, 0.0), axis=-1, keepdims=True)
    s1 = jnp.sum(jnp.where(first, 0.0, v), axis=-1, keepdims=True)
    return jnp.where(first, s0, s1)


BWD_STACK, FWD_STACK = N_HEADS, N_HEADS


def _band_mask(n, heads):
    ql = lax.broadcasted_iota(jnp.int32, (heads * BLK, 2 * BLK), 0) & (BLK - 1)
    j = lax.broadcasted_iota(jnp.int32, (heads * BLK, 2 * BLK), 1)
    return (j > ql) & (j <= ql + BLK) & ((j >= BLK) | (n > 0))


def _norm_keys(kband):
    first = _lane_half(kband.shape)
    r = lax.rsqrt(_half_sums(kband * kband, first) * (1.0 / HEAD_DIM) + EPS)
    return kband * r


def _kv_lanes(h0, heads):
    shape = (heads * BLK, LANE)
    head = h0 + lax.broadcasted_iota(jnp.int32, shape, 0) // BLK
    return _lane_half(shape) == (head < GQA)


def _stack_heads(ref, h0, heads, dtype=F32):
    parts = []
    for h in range(h0, h0 + heads):
        part = ref[:, (h // 2) * LANE:(h // 2 + 1) * LANE].astype(dtype)
        parts.append(pltpu.roll(part, HEAD_DIM, 1) if h % 2 != h // GQA else part)
    return jnp.where(_kv_lanes(h0, heads), jnp.concatenate(parts, axis=0), 0.0)


def _unstack_heads(ref, h0, heads, stacked):
    for i in range(heads // 2):
        pair = None
        for sub in range(2):
            part = stacked[(2 * i + sub) * BLK:(2 * i + sub + 1) * BLK]
            part = pltpu.roll(part, HEAD_DIM, 1) if sub != (h0 + 2 * i) // GQA else part
            pair = part if pair is None else pair + part
        col = (h0 // 2 + i) * LANE
        ref[:, col:col + LANE] = pair.astype(BF16)


def _stack_logits(q_ref, h0, heads, qg, kn, bias_ref, sink_ref, mask):
    qa = _stack_heads(q_ref, h0, heads)
    r = lax.rsqrt(jnp.sum(qa * qa, axis=-1, keepdims=True) * (1.0 / HEAD_DIM) + EPS)
    xh = qa * r
    qn = (xh * qg).astype(BF16)
    bias = bias_ref[h0:h0 + heads].reshape(heads * BLK, 2 * BLK)
    logits = _dot(qn, kn, NT) * (HEAD_DIM ** -0.5) + bias
    p, p_sink = _softmax_with_sink(jnp.where(mask, logits, NEG_INF), sink_ref[h0 * BLK:(h0 + heads) * BLK])
    return xh, r, qn, p, p_sink


def _softmax_with_sink(logits, sink):
    m = jnp.maximum(jnp.max(logits, axis=-1, keepdims=True), sink)
    e = jnp.exp(logits - m)
    es = jnp.exp(sink - m)
    den = jnp.sum(e, axis=-1, keepdims=True) + es
    return e / den, es / den


def _attn_specs(nb, last):
    cur = lambda n: jnp.minimum(n, last)
    prev = lambda n: jnp.minimum(jnp.maximum(n - 1, 0), last)
    return [pl.BlockSpec((BLK, ATT_W), lambda n: (cur(n), COL_Q // ATT_W)),
            pl.BlockSpec((BLK, KV_W), lambda n: (prev(n), COL_K // KV_W)),
            pl.BlockSpec((BLK, KV_W), lambda n: (cur(n), COL_K // KV_W)),
            pl.BlockSpec((BLK, KV_W), lambda n: (prev(n), COL_V // KV_W)),
            pl.BlockSpec((BLK, KV_W), lambda n: (cur(n), COL_V // KV_W))]


def attn_fwd(z, qg2, kg2, sinks, bias, *, name):
    s = z.shape[0]
    nb = s // BLK

    def body(q_ref, kp_ref, kc_ref, vp_ref, vc_ref, qg_ref, kg_ref, sink_ref, bias_ref, o_ref):
        mask = _band_mask(pl.program_id(0), FWD_STACK)
        kn = (_norm_keys(jnp.concatenate([kp_ref[...], kc_ref[...]], axis=0).astype(F32)) * kg_ref[...]).astype(BF16)
        vb = jnp.concatenate([vp_ref[...], vc_ref[...]], axis=0).astype(BF16)
        for h0 in range(0, N_HEADS, FWD_STACK):
            _, _, _, p, _ = _stack_logits(q_ref, h0, FWD_STACK, qg_ref[...], kn, bias_ref, sink_ref, mask)
            out = _dot(p.astype(BF16), vb, NN)
            _unstack_heads(o_ref, h0, FWD_STACK, jnp.where(_kv_lanes(h0, FWD_STACK), out, 0.0))

    return pl.pallas_call(
        body, name=name, grid=(nb,),
        in_specs=_attn_specs(nb, nb - 1) + [
            _vec_spec(LANE), _vec_spec(LANE), pl.BlockSpec((N_HEADS * BLK, 1), lambda n: (0, 0)),
            pl.BlockSpec((N_HEADS, BLK, 2 * BLK), lambda n: (0, 0, 0))],
        out_specs=pl.BlockSpec((BLK, ATT_W), lambda n: (n, 0)),
        out_shape=jax.ShapeDtypeStruct((s, ATT_W), BF16),
        compiler_params=_params("parallel"),
    )(z, z, z, z, z, qg2, kg2, sinks, bias)


def attn_bwd(z, d_out, qg2, kg2, sinks, bias, dz, *, name):
    s = z.shape[0]
    nb = s // BLK
    scale = HEAD_DIM ** -0.5

    def body(q_ref, kp_ref, kc_ref, vp_ref, vc_ref, do_ref, qg_ref, kg_ref, sink_ref, bias_ref, dz_in_ref,
             dq_ref, dk_ref, dv_ref, dqg_ref, dkg_ref, dsink_ref, dbias_ref, band_k, band_v, carry_k, carry_v, dsink_rows):
        n = pl.program_id(0)

        @pl.when(n == 0)
        def _():
            dqg_ref[...] = jnp.zeros_like(dqg_ref)
            dkg_ref[...] = jnp.zeros_like(dkg_ref)
            dbias_ref[...] = jnp.zeros_like(dbias_ref)
            carry_k[...] = jnp.zeros_like(carry_k)
            carry_v[...] = jnp.zeros_like(carry_v)
            dsink_rows[...] = jnp.zeros_like(dsink_rows)

        @pl.when(n == nb)
        def _():
            band_k[...] = jnp.zeros_like(band_k)
            band_v[...] = jnp.zeros_like(band_v)

        @pl.when(n < nb)
        def _():
            mask = _band_mask(n, BWD_STACK)
            kn = (_norm_keys(jnp.concatenate([kp_ref[...], kc_ref[...]], axis=0).astype(F32)) * kg_ref[...]).astype(BF16)
            vb = jnp.concatenate([vp_ref[...], vc_ref[...]], axis=0).astype(BF16)
            dkn = jnp.zeros((2 * BLK, KV_W), F32)
            dvb = jnp.zeros((2 * BLK, KV_W), F32)
            dqg = jnp.zeros((1, LANE), F32)
            for h0 in range(0, N_HEADS, BWD_STACK):
                rows = slice(h0 * BLK, (h0 + BWD_STACK) * BLK)
                xh, r, qn, p, p_sink = _stack_logits(q_ref, h0, BWD_STACK, qg_ref[...], kn, bias_ref, sink_ref, mask)
                do = _stack_heads(do_ref, h0, BWD_STACK).astype(BF16)
                dp = _dot(do, vb, NT)
                delta = jnp.sum(p * dp, axis=-1, keepdims=True)
                ds = p * (dp - delta)
                dsink_rows[rows] -= p_sink * delta
                dbias_ref[h0:h0 + BWD_STACK] += ds.reshape(BWD_STACK, BLK, 2 * BLK)
                ds16 = ds.astype(BF16)
                dqn = jnp.where(_kv_lanes(h0, BWD_STACK), _dot(ds16, kn, NN) * scale, 0.0)
                dkn = dkn + _dot(ds16, qn, TN) * scale
                dvb = dvb + _dot(p.astype(BF16), do, TN)
                dqg = dqg + jnp.sum(dqn * xh, axis=0, keepdims=True)
                dxh = dqn * qg_ref[...]
                _unstack_heads(dq_ref, h0, BWD_STACK,
                               r * (dxh - xh * (jnp.sum(dxh * xh, axis=-1, keepdims=True) * (1.0 / HEAD_DIM))))
            band_k[...] = dkn
            band_v[...] = dvb
            dqg_ref[...] += dqg

        dkn_prev = carry_k[...] + band_k[:BLK]
        dv_ref[...] = (carry_v[...] + band_v[:BLK]).astype(BF16)
        carry_k[...] = band_k[BLK:]
        carry_v[...] = band_v[BLK:]
        kp = kp_ref[...].astype(F32)
        first = _lane_half(kp.shape)
        r = lax.rsqrt(_half_sums(kp * kp, first) * (1.0 / HEAD_DIM) + EPS)
        xh = kp * r
        dkg_ref[...] += jnp.sum(dkn_prev * xh, axis=0, keepdims=True)
        dxh = dkn_prev * kg_ref[...]
        dk_ref[...] = (r * (dxh - xh * (_half_sums(dxh * xh, first) * (1.0 / HEAD_DIM)))).astype(BF16)

        @pl.when(n == nb)
        def _():
            dqg_ref[...] += pltpu.roll(dqg_ref[...], HEAD_DIM, 1)
            dkg_ref[...] += pltpu.roll(dkg_ref[...], HEAD_DIM, 1)
            lane16 = lax.broadcasted_iota(jnp.int32, (1, N_HEADS), 1)
            dsink = jnp.zeros((1, N_HEADS), F32)
            for h in range(N_HEADS):
                dsink = dsink + jnp.where(lane16 == h, jnp.sum(dsink_rows[h * BLK:(h + 1) * BLK], axis=0, keepdims=True), 0.0)
            dsink_ref[...] = dsink

    last = nb - 1
    cur = lambda n: jnp.minimum(n, last)
    back = lambda n: jnp.maximum(n - 1, 0)
    full3 = pl.BlockSpec((N_HEADS, BLK, 2 * BLK), lambda n: (0, 0, 0))
    return pl.pallas_call(
        body, name=name, grid=(nb + 1,),
        in_specs=_attn_specs(nb, last) + [
            pl.BlockSpec((BLK, ATT_W), lambda n: (cur(n), 0)),
            _vec_spec(LANE), _vec_spec(LANE), pl.BlockSpec((N_HEADS * BLK, 1), lambda n: (0, 0)), full3,
            pl.BlockSpec(memory_space=pl.ANY)],
        out_specs=[pl.BlockSpec((BLK, ATT_W), lambda n: (cur(n), COL_Q // ATT_W)),
                   pl.BlockSpec((BLK, KV_W), lambda n: (back(n), 0)),
                   pl.BlockSpec((BLK, KV_W), lambda n: (back(n), 0)),
                   _vec_spec(LANE), _vec_spec(LANE), _vec_spec(N_HEADS), full3],
        input_output_aliases={10: 0},
        out_shape=[jax.ShapeDtypeStruct(dz.shape, BF16), jax.ShapeDtypeStruct((s, KV_W), BF16),
                   jax.ShapeDtypeStruct((s, KV_W), BF16), jax.ShapeDtypeStruct((1, LANE), F32),
                   jax.ShapeDtypeStruct((1, LANE), F32), jax.ShapeDtypeStruct((1, N_HEADS), F32),
                   jax.ShapeDtypeStruct((N_HEADS, BLK, 2 * BLK), F32)],
        scratch_shapes=[pltpu.VMEM((2 * BLK, KV_W), F32), pltpu.VMEM((2 * BLK, KV_W), F32),
                        pltpu.VMEM((BLK, KV_W), F32), pltpu.VMEM((BLK, KV_W), F32), pltpu.VMEM((N_HEADS * BLK, 1), F32)],
        compiler_params=_params("arbitrary"),
    )(z, z, z, z, z, d_out, qg2, kg2, sinks, bias, dz)


def _adamw(w, g, m, v):
    m = ADAM_B1 * m + (1.0 - ADAM_B1) * g
    v = ADAM_B2 * v + (1.0 - ADAM_B2) * (g * g)
    m_hat = m / (1.0 - ADAM_B1 ** ADAM_STEP)
    v_hat = v / (1.0 - ADAM_B2 ** ADAM_STEP)
    delta = -ADAM_LR * (m_hat / (jnp.sqrt(v_hat) + ADAM_EPS) + ADAM_WD * w)
    return delta, m, v


def ada_fwd(c16, w, b, *, name, tn=768):
    k, n = w.shape
    tn = _tile(n, tn, LANE)

    def body(c_ref, w_ref, b_ref, o_ref):
        cc = c_ref[...]
        o_ref[...] = _dot((cc * _sigmoid(cc)).astype(BF16), w_ref[...].astype(BF16), NN) + b_ref[...]

    return pl.pallas_call(
        body, name=name, grid=(n // tn,),
        in_specs=[pl.BlockSpec((c16.shape[0], k), lambda j: (0, 0)), pl.BlockSpec((k, tn), lambda j: (0, j)),
                  pl.BlockSpec((1, tn), lambda j: (0, j))],
        out_specs=pl.BlockSpec((c16.shape[0], tn), lambda j: (0, j)),
        out_shape=jax.ShapeDtypeStruct((c16.shape[0], n), F32),
        compiler_params=_params("parallel"),
    )(c16, w, b)


def ada_bwd_adamw(c_t, dmod, w, m, v, *, name, tn=256):
    k, n = w.shape
    tn = _tile(n, tn, LANE)

    def body(c_ref, d_ref, w_ref, m_ref, v_ref, g_ref, dl_ref, mo_ref, vo_ref):
        cc = c_ref[...]
        g = _dot((cc * _sigmoid(cc)).astype(BF16), d_ref[...].astype(BF16), NN)
        g_ref[...] = g
        dl_ref[...], mo_ref[...], vo_ref[...] = _adamw(w_ref[...], g, m_ref[...], v_ref[...])

    blk = pl.BlockSpec((k, tn), lambda j: (0, j))
    out = jax.ShapeDtypeStruct((k, n), F32)
    return pl.pallas_call(
        body, name=name, grid=(n // tn,),
        in_specs=[pl.BlockSpec((k, LANE), lambda j: (0, 0)), pl.BlockSpec((LANE, tn), lambda j: (0, j)), blk, blk, blk],
        out_specs=[blk] * 4, out_shape=[out] * 4,
        compiler_params=_params("parallel"),
    )(c_t, dmod, w, m, v)


def adamw_from_transposed_parts(parts, w, m, v, *, name):
    r, c = w.shape
    tc = _tile(c, LANE, LANE)

    def body(p_ref, w_ref, m_ref, v_ref, g_ref, dl_ref, mo_ref, vo_ref):
        g_t = p_ref[0].astype(F32)
        for d in range(1, N_DEV):
            g_t = g_t + p_ref[d].astype(F32)
        g = g_t.T
        g_ref[...] = g
        dl_ref[...], mo_ref[...], vo_ref[...] = _adamw(w_ref[...], g, m_ref[...], v_ref[...])

    blk = pl.BlockSpec((r, tc), lambda i: (0, i))
    out = jax.ShapeDtypeStruct((r, c), F32)
    return pl.pallas_call(
        body, name=name, grid=(c // tc,),
        in_specs=[pl.BlockSpec((N_DEV, tc, r), lambda i: (0, i, 0)), blk, blk, blk],
        out_specs=[blk] * 4, out_shape=[out] * 4,
        compiler_params=_params("parallel"),
    )(parts, w, m, v)


def adamw_from_parts(parts, w, m, v, *, name):
    r, c = w.shape
    tr = _tile(r, max(16, (256 * 1024) // c), 16)

    def body(p_ref, w_ref, m_ref, v_ref, g_ref, dl_ref, mo_ref, vo_ref):
        g = p_ref[0].astype(F32)
        for d in range(1, N_DEV):
            g = g + p_ref[d].astype(F32)
        g_ref[...] = g
        dl_ref[...], mo_ref[...], vo_ref[...] = _adamw(w_ref[...], g, m_ref[...], v_ref[...])

    blk = pl.BlockSpec((tr, c), lambda i: (i, 0))
    out = jax.ShapeDtypeStruct((r, c), F32)
    return pl.pallas_call(
        body, name=name, grid=(r // tr,),
        in_specs=[pl.BlockSpec((N_DEV, tr, c), lambda i: (0, i, 0)), blk, blk, blk],
        out_specs=[blk] * 4, out_shape=[out] * 4,
        compiler_params=_params("parallel"),
    )(parts, w, m, v)


def _ffn_fwd(x_in, g, shift, scale, gate, wgu3, get_wd, token, tag, target=None):
    h = norm_modulate(x_in, g, shift, scale, token, name=f"{tag}_norm")
    gg, uu, act = ffn_up(h, wgu3, name=f"{tag}_up")
    wd = get_wd(gg)
    if target is None:
        x_out, f = mm_nn_residual(act, wd, x_in, gate, 0.5, name=f"{tag}_down")
    else:
        x_out, f = mm_nn_residual_loss(act, wd, x_in, gate, 0.5, target, name=f"{tag}_down_loss"), None
    return x_out, (h, gg, uu, act, f), wd


def _ffn_bwd(dx_out, df, dgate, x_in, g, scale, wgu3, wd, saved, token, scatter, split, tag, below=None):
    h, gg, uu, act, _ = saved
    dwd = mm_tn(act, df, name=f"{tag}_dwd").reshape(N_DEV, -1, D_MODEL)
    if split:
        token = scatter([f"w_{tag}_down"], [dwd], f"scatter_{tag}_down")
    dgg, duu = ffn_dgu(df, wd, gg, uu, token, name=f"{tag}_dgu")
    dwgu = mm_tn_halves(h, dgg, duu, N_DEV, name=f"{tag}_dwgu")
    if split:
        token = scatter([f"w_{tag}_gu"], [dwgu], f"scatter_{tag}_gu")
    else:
        token = scatter([f"w_{tag}_gu", f"w_{tag}_down"], [dwgu, dwd], f"scatter_{tag}")
    dh = mm_nt_halves(dgg, duu, wgu3, token, name=f"{tag}_dh")
    dx_in, dshift, dscale, dg, *rest = norm_modulate_bwd(x_in, g, scale, dh, dx_out, below, name=f"{tag}_norm_bwd")
    return (dx_in, (dshift, dscale, dgate), dg, *rest)


def kernel(x, c, w_ada, b_ada, g_ffn1, w_ffn1_gu, w_ffn1_down, g_mix, w_in, pool_mix, pool_scale, w_pool_up, q_gain, k_gain, sinks, rel_bias, w_attn_up, w_o, g_ffn2, w_ffn2_gu, w_ffn2_down, loss_target, m_w_ada, m_b_ada, m_g_ffn1, m_w_ffn1_gu, m_w_ffn1_down, m_g_mix, m_w_in, m_pool_mix, m_pool_scale, m_w_pool_up, m_q_gain, m_k_gain, m_sinks, m_rel_bias, m_w_attn_up, m_w_o, m_g_ffn2, m_w_ffn2_gu, m_w_ffn2_down, v_w_ada, v_b_ada, v_g_ffn1, v_w_ffn1_gu, v_w_ffn1_down, v_g_mix, v_w_in, v_pool_mix, v_pool_scale, v_w_pool_up, v_q_gain, v_k_gain, v_sinks, v_rel_bias, v_w_attn_up, v_w_o, v_g_ffn2, v_w_ffn2_gu, v_w_ffn2_down):
    me = _slot(_mesh_pos())
    x0, target = x[0], loss_target[0]
    n_ada = w_ada.shape[2]
    pm_rows = pool_mix.shape[2]

    big = dict(w_ffn1_gu=(w_ffn1_gu, m_w_ffn1_gu, v_w_ffn1_gu), w_ffn1_down=(w_ffn1_down, m_w_ffn1_down, v_w_ffn1_down),
               w_in=(w_in, m_w_in, v_w_in), pool_mix=(pool_mix, m_pool_mix, v_pool_mix),
               w_pool_up=(w_pool_up, m_w_pool_up, v_w_pool_up), w_attn_up=(w_attn_up, m_w_attn_up, v_w_attn_up),
               w_o=(w_o, m_w_o, v_w_o), w_ffn2_gu=(w_ffn2_gu, m_w_ffn2_gu, v_w_ffn2_gu),
               w_ffn2_down=(w_ffn2_down, m_w_ffn2_down, v_w_ffn2_down))
    shard2d = {k: (POOL_GROUPS * pm_rows, POOL_GROUP_W) if k == "pool_mix" else t[0].shape[1:] for k, t in big.items()}
    mix_keys = ["w_in", "pool_mix", "w_pool_up", "w_attn_up", "w_o"]
    ffn2_keys = ["w_ffn2_gu", "w_ffn2_down"]

    def shard_bf16(k, token=None):
        w = big[k][0].reshape(shard2d[k])
        return (w if token is None else w + token[0, 0]).astype(BF16)

    def landing_zones(blocks, tag):
        zones = unwritten_hbm([(N_DEV,) + b.shape for b in blocks], BF16, f"{tag}_zones")
        return [lax.dynamic_update_slice(z, b[None], (me, 0, 0)) for z, b in zip(zones, blocks)]

    def start_gather(keys, token, tag):
        shards = [shard_bf16(k, token) for k in keys]
        return exchange_start(shards, landing_zones(shards, tag), slotted=False, name=f"{tag}_start")

    pending = []

    def scatter(keys, grads, tag):
        lands = landing_zones([lax.dynamic_index_in_dim(g, me, 0, keepdims=False) for g in grads], tag)
        handle = exchange_start(grads, lands, slotted=True, name=f"{tag}_start")
        pending.append((keys, handle, tag))
        return handle[4]

    c_all, _ = all_gather_small(c.reshape(D_MODEL // LANE, LANE), "gather_c")
    c_all = c_all.reshape(N_DEV, D_MODEL)
    c16 = jnp.pad(c_all, ((0, 16 - N_DEV), (0, 0)))
    b_mine = lax.dynamic_slice(b_ada, (0, me * n_ada), (1, n_ada))
    mod_cols = ada_fwd(c16, w_ada[0], b_mine, name="ada_fwd")[:N_DEV]
    mod_all, token = all_gather_small(mod_cols.reshape(-1, LANE), "gather_mod")
    mod = lax.dynamic_index_in_dim(mod_all.reshape(N_DEV, N_DEV, n_ada), me, axis=1, keepdims=False)
    mod = mod.reshape(N_MOD, 1, D_MODEL)

    wgu1, token = all_gather_hbm(shard_bf16("w_ffn1_gu", token), "gather_ffn1_gu")
    gather_wd1 = start_gather(["w_ffn1_down"], token, "gather_ffn1_down")
    gather_mix = start_gather(mix_keys, gather_wd1[4], "gather_mix")
    gather_ffn2 = start_gather(ffn2_keys, gather_mix[4], "gather_ffn2")
    token = gather_ffn2[4]

    def get_wd1(after):
        return exchange_wait(gather_wd1, after, slotted=False, name="gather_ffn1_down_wait")[0].reshape(-1, D_MODEL)

    x1, saved1, wd1 = _ffn_fwd(x0, g_ffn1, mod[0], mod[1], mod[2], wgu1, get_wd1, token, "ffn1")
    gathered = dict(zip(mix_keys, exchange_wait(gather_mix, x1, slotted=False, name="gather_mix_wait")))
    def columns_out(blocks):
        return jnp.transpose(blocks, (1, 0, 2)).reshape(blocks.shape[1], -1)

    def columns_in(full):
        return jnp.transpose(full.reshape(full.shape[0], N_DEV, -1), (1, 0, 2))

    w_in_full = columns_out(gathered["w_in"])
    w_in_z = jnp.concatenate([w_in_full[:, s:s + w] for s, w in Z_PIECES], axis=1)
    pm_full = jnp.transpose(gathered["pool_mix"].reshape(N_DEV, POOL_GROUPS, pm_rows, POOL_GROUP_W),
                            (1, 0, 2, 3)).reshape(POOL_GROUPS, POOL_GROUP_W, POOL_GROUP_W)
    wpu, wau = columns_out(gathered["w_pool_up"]), columns_out(gathered["w_attn_up"])
    wo_full = gathered["w_o"].reshape(D_MODEL, D_MODEL)
    h2 = norm_modulate(x1, g_mix, mod[3], mod[4], token, name="mix_norm")
    z = mm_nn(h2, w_in_z, out_dtype=BF16, name="mix_in")
    pooled, p_act = pool_fwd(z, pm_full, pool_scale, name="pool_fwd")
    one_hot = jnp.asarray(_bucket_one_hot())
    bias = bias_table(rel_bias.T, one_hot, name="bias_table").reshape(N_HEADS, BLK, 2 * BLK)
    qg2, kg2 = jnp.tile(q_gain, (1, 2)), jnp.tile(k_gain, (1, 2))
    sink_rows = jnp.repeat(sinks[0], BLK).reshape(N_HEADS * BLK, 1)
    attn = attn_fwd(z, qg2, kg2, sink_rows, bias, name="attn_fwd")
    y_pool, y_attn, merged = mixers_up_merge(p_act, attn, wpu, wau, z, name="mixers_up")
    x2, o_act = mm_nn_residual(merged, wo_full, x1, mod[5], 1.0, name="mix_out")
    wgu2, wd2 = exchange_wait(gather_ffn2, x2, slotted=False, name="gather_ffn2_wait")
    wd2 = wd2.reshape(-1, D_MODEL)
    (dy, df2, dgate3, loss_row), saved2, _ = _ffn_fwd(x2, g_ffn2, mod[6], mod[7], mod[8], wgu2, lambda after: wd2, token,
                                                     "ffn2", target)

    dx2, dmod3, dg_ffn2, d_o, dgate2 = _ffn_bwd(dy, df2, dgate3, x2, g_ffn2, mod[7], wgu2, wd2, saved2, token, scatter, False,
                                               "ffn2", below=(o_act, mod[5], 1.0))
    dwo = mm_tn(merged, d_o, name="mix_dwo").reshape(N_DEV, -1, D_MODEL)
    dyp, dya, dz = merge_bwd(d_o, wo_full, z, y_pool, y_attn, token, name="merge_bwd")
    dwpu = mm_tn(p_act, dyp, name="pool_dwup")
    dp_act = mm_nt(dyp, wpu, token, out_dtype=BF16, name="pool_dp")
    dz, dpool_scale, dpm = pool_bwd(dp_act, pooled, pm_full, pool_scale, dz, name="pool_bwd")
    dwau = mm_tn(attn, dya, name="attn_dwup")
    dattn = mm_nt(dya, wau, token, out_dtype=BF16, name="attn_dout")
    dz, dk, dv, dqg, dkg, dsinks, dbias = attn_bwd(z, dattn, qg2, kg2, sink_rows, bias, dz, name="attn_bwd")
    dz = lax.dynamic_update_slice(dz, jnp.concatenate([dk, dv], axis=1), (0, COL_K))
    drel = bias_table_bwd(dbias.reshape(N_HEADS, -1), one_hot, name="bias_table_bwd").T
    dwin_z = mm_tn(h2, dz, name="mix_dwin")
    z_start = np.cumsum([0] + [w for _, w in Z_PIECES[:-1]])
    in_w_in_order = sorted(zip(Z_PIECES, z_start))
    dwin = jnp.concatenate([dwin_z[:, int(at):int(at) + w] for (_, w), at in in_w_in_order], axis=1)
    mix_grads = dict(w_in=columns_in(dwin),
                     pool_mix=jnp.transpose(dpm.astype(BF16).reshape(POOL_GROUPS, N_DEV, pm_rows, POOL_GROUP_W),
                                            (1, 0, 2, 3)).reshape(N_DEV, POOL_GROUPS * pm_rows, POOL_GROUP_W),
                     w_pool_up=columns_in(dwpu), w_attn_up=columns_in(dwau), w_o=dwo)
    token = scatter(mix_keys, [mix_grads[k] for k in mix_keys], "scatter_mix")
    dh2 = mm_nt(dz, w_in_z, token, out_dtype=BF16, name="mix_dh")
    dx1, dsh2, dsc2, dg_mix, df1, dgate1 = norm_modulate_bwd(x1, g_mix, mod[4], dh2, dx2, (saved1[4], mod[2], 0.5),
                                                           name="mix_norm_bwd")
    dx0, dmod1, dg_ffn1 = _ffn_bwd(dx1, df1, dgate1, x0, g_ffn1, mod[1], wgu1, wd1, saved1, token, scatter, True, "ffn1")

    small = [("b_ada", b_ada, m_b_ada, v_b_ada, jnp.concatenate(list(dmod1 + (dsh2, dsc2, dgate2) + dmod3), axis=1)),
             ("g_ffn1", g_ffn1, m_g_ffn1, v_g_ffn1, dg_ffn1), ("g_mix", g_mix, m_g_mix, v_g_mix, dg_mix),
             ("g_ffn2", g_ffn2, m_g_ffn2, v_g_ffn2, dg_ffn2),
             ("pool_scale", pool_scale, m_pool_scale, v_pool_scale, dpool_scale),
             ("q_gain", q_gain, m_q_gain, v_q_gain, dqg[:, :HEAD_DIM]), ("k_gain", k_gain, m_k_gain, v_k_gain, dkg[:, :HEAD_DIM]),
             ("sinks", sinks, m_sinks, v_sinks, dsinks), ("rel_bias", rel_bias, m_rel_bias, v_rel_bias, drel)]
    n_small = sum(t[1].size for t in small)
    width = n_small + 1 + (-(n_small + 1) % (8 * LANE))

    def flat(arrs):
        row = jnp.concatenate([a.reshape(1, -1) for a in arrs], axis=1)
        return jnp.pad(row, ((0, 0), (0, width - row.shape[1])))

    small_parts, _ = all_gather_small(flat([t[4] for t in small] + [loss_row[:, :1]]).reshape(-1, LANE), "gather_small_grads")
    small_parts = small_parts.reshape(N_DEV, 1, width)
    sg, sd, sm, sv = adamw_from_parts(small_parts, flat([t[1] for t in small]), flat([t[2] for t in small]),
                                      flat([t[3] for t in small]), name="adamw_small")
    loss = sg[0, n_small]

    dmod_all = small_parts[:, 0, :N_MOD * D_MODEL]
    dmod_mine = lax.dynamic_slice(dmod_all, (0, me * n_ada), (N_DEV, n_ada))
    c_t = jnp.pad(c_all.T, ((0, 0), (0, LANE - N_DEV)))
    ada_out = ada_bwd_adamw(c_t, jnp.pad(dmod_mine, ((0, LANE - N_DEV), (0, 0))), w_ada[0], m_w_ada[0], v_w_ada[0],
                            name="ada_bwd_adamw")

    res = {"w_ada": [o[None] for o in ada_out]}
    after = ada_out[0]
    for keys, handle, tag in pending:
        parts = exchange_wait(handle, after, slotted=True, name=f"{tag}_wait")
        for k, part in zip(keys, parts):
            w_, m_, v_ = big[k]
            update = adamw_from_transposed_parts if part.shape[1:] != shard2d[k] else adamw_from_parts
            outs = update(part, w_.reshape(shard2d[k]), m_.reshape(shard2d[k]), v_.reshape(shard2d[k]), name=f"adamw_{k}")
            res[k] = [o.reshape(w_.shape) for o in outs]
            after = outs[0]
    off = 0
    for k, w_, _, _, _ in small:
        res[k] = [o[0, off:off + w_.size].reshape(w_.shape) for o in (sg, sd, sm, sv)]
        off += w_.size
    order = ["w_ada", "b_ada", "g_ffn1", "w_ffn1_gu", "w_ffn1_down", "g_mix", "w_in", "pool_mix", "pool_scale",
             "w_pool_up", "q_gain", "k_gain", "sinks", "rel_bias", "w_attn_up", "w_o", "g_ffn2", "w_ffn2_gu", "w_ffn2_down"]
    return (loss, dx0[None], *[res[k][0] for k in order], *[res[k][1] for k in order],
            *[res[k][2] for k in order], *[res[k][3] for k in order])
```
